```python
import jax, jax.numpy as jnp
from jax import lax
import numpy as np


D_MODEL = 1024
BATCH = 8
SEQ = 2048
DEPTH = 2

N_EVEN = (DEPTH + 1) // 2
N_ODD = DEPTH // 2
D_FF = 2816
EPS = 1e-6

RWKV_HEADS = 8
RWKV_HD = 64
RWKV_DIM = RWKV_HEADS * RWKV_HD
LORA_W = 64
LORA_A = 64
LORA_G = 128
RWKV_COLS = 3 * RWKV_DIM + LORA_W + LORA_A + LORA_G
RWKV_SPLITS = (RWKV_DIM, 2 * RWKV_DIM, 3 * RWKV_DIM, 3 * RWKV_DIM + LORA_W, 3 * RWKV_DIM + LORA_W + LORA_A)
LNX_EPS = 64e-5

MOBA_HEADS = 8
MOBA_HD = 64
MOBA_DIM = MOBA_HEADS * MOBA_HD
MOBA_BLOCK = 256
MOBA_TOPK = 3
Q_CHUNK = 16
MOBA_COLS = 3 * MOBA_DIM

EVEN_IN = RWKV_COLS + MOBA_COLS
EVEN_MIX = RWKV_DIM + MOBA_DIM

HG_HEADS = 8
HG_DK = 128
HG_DV = 128
HG_FDIM = HG_HEADS * HG_DK
HG_VDIM = HG_HEADS * HG_DV
HG_CHUNK = 64
ODD_IN = 2 * HG_FDIM + 2 * HG_VDIM

kernel_name = 'hybrid_rwkv7_moba_hgrn2_macaron'


def rmsnorm(x, g):
    xf = x.astype(jnp.float32)
    y = xf * lax.rsqrt(jnp.mean(xf * xf, axis=-1, keepdims=True) + EPS)
    return (y * g).astype(x.dtype)


def swiglu(h, w_gate, w_up, w_down):
    return (jax.nn.silu(h @ w_gate) * (h @ w_up)) @ w_down


def _shift_prev(t):
    return jnp.pad(t, ((0, 0), (1, 0), (0, 0)))[:, :-1]


def _heads(t, n_heads):
    return t.reshape(t.shape[0], t.shape[1], n_heads, -1)


def rwkv7_mix(p, mu, w0, w2, a0, a2, g2, k_k, k_a, r_k, lnx_w, lnx_b):
    bsz, seq, _ = p.shape
    f32 = jnp.float32
    p = p + (_shift_prev(p) - p) * mu
    r, k, v, w_lr, a_lr, g_lr = jnp.split(p, RWKV_SPLITS, axis=-1)
    w_raw = -jax.nn.softplus(-(w0 + jnp.tanh(w_lr) @ w2)) - 0.5
    decay = jnp.exp(-jnp.exp(w_raw.astype(f32)))
    a = jax.nn.sigmoid(a0 + a_lr @ a2)
    g = jax.nn.sigmoid(g_lr) @ g2
    kk = _heads(k * k_k, RWKV_HEADS).astype(f32)
    kk = kk * lax.rsqrt(jnp.maximum(jnp.sum(kk * kk, axis=-1, keepdims=True), 1e-24))
    k = k * (1.0 + (a - 1.0) * k_a)
    rh, kh, vh, ah, wh = [_heads(t, RWKV_HEADS).astype(f32) for t in (r, k, v, a, decay)]

    def step(state, inp):
        r_t, w_t, k_t, v_t, kk_t, a_t = inp
        sa = jnp.einsum('bhvk,bhk->bhv', state, -kk_t)
        state = (state * w_t[:, :, None, :]
                 + jnp.einsum('bhv,bhk->bhvk', sa, kk_t * a_t)
                 + jnp.einsum('bhv,bhk->bhvk', v_t, k_t))
        return state, jnp.einsum('bhvk,bhk->bhv', state, r_t)

    tm = lambda t: jnp.moveaxis(t, 1, 0)
    s0 = jnp.zeros((bsz, RWKV_HEADS, RWKV_HD, RWKV_HD), f32)
    _, y = lax.scan(step, s0, (tm(rh), tm(wh), tm(kh), tm(vh), tm(kk), tm(ah)))
    y = jnp.moveaxis(y, 0, 1)
    mean = jnp.mean(y, axis=-1, keepdims=True)
    var = jnp.mean(jnp.square(y - mean), axis=-1, keepdims=True)
    y = ((y - mean) * lax.rsqrt(var + LNX_EPS)).reshape(bsz, seq, RWKV_DIM) * lnx_w + lnx_b
    bonus = jnp.sum(rh * kh * r_k, axis=-1, keepdims=True) * vh
    y = y + bonus.reshape(bsz, seq, RWKV_DIM)
    return (y * g).astype(p.dtype)


def moba_mix(p, slopes):
    bsz, seq, _ = p.shape
    f32 = jnp.float32
    q, k, v = [jnp.moveaxis(_heads(t, MOBA_HEADS), 2, 1) for t in jnp.split(p, 3, axis=-1)]
    q = q * (MOBA_HD ** -0.5)
    n_blk = -(-seq // MOBA_BLOCK)
    pad = n_blk * MOBA_BLOCK - seq
    padding = ((0, 0), (0, 0), (0, pad), (0, 0))
    kb = jnp.pad(k, padding).reshape(bsz, MOBA_HEADS, n_blk, MOBA_BLOCK, MOBA_HD)
    vb = jnp.pad(v, padding).reshape(bsz, MOBA_HEADS, n_blk, MOBA_BLOCK, MOBA_HD)
    pos = jnp.arange(seq, dtype=jnp.int32)
    own = pos // MOBA_BLOCK
    own_idx = jnp.broadcast_to(own[:, None], (bsz, MOBA_HEADS, seq, 1))
    n_sel = min(MOBA_TOPK, n_blk - 1)
    if n_sel > 0:
        k_mean = jnp.mean(kb, axis=3)
        gate = jnp.einsum('bhsd,bhnd->bhsn', q, k_mean).astype(f32)
        fully_past = jnp.arange(n_blk, dtype=jnp.int32)[None, :] < own[:, None]
        gate = jnp.where(fully_past, gate, -jnp.inf)
        _, top_idx = lax.top_k(gate, n_sel)
        top_idx = top_idx.astype(jnp.int32)
        idx = jnp.concatenate([top_idx, own_idx], axis=-1)
        valid = jnp.concatenate([top_idx < own[:, None], jnp.ones(own_idx.shape, bool)], axis=-1)
    else:
        idx = own_idx
        valid = jnp.ones(own_idx.shape, bool)

    n_chunk = seq // Q_CHUNK

    def to_chunks(t):
        return jnp.moveaxis(t.reshape(t.shape[:2] + (n_chunk, Q_CHUNK) + t.shape[3:]), 2, 0)

    b_i = jnp.arange(bsz)[:, None, None, None]
    h_i = jnp.arange(MOBA_HEADS)[None, :, None, None]
    offs = jnp.arange(MOBA_BLOCK, dtype=jnp.int32)

    def attend(args):
        q_c, idx_c, valid_c, pos_c = args
        k_g = kb[b_i, h_i, idx_c]
        v_g = vb[b_i, h_i, idx_c]
        s = jnp.einsum('bhqd,bhqnkd->bhqnk', q_c, k_g).astype(f32)
        dist = pos_c[:, None, None] - (idx_c[..., None] * MOBA_BLOCK + offs)
        s = s - slopes[:, None, None, None] * dist.astype(f32)
        s = jnp.where(valid_c[..., None] & (dist >= 0), s, -jnp.inf)
        pr = jax.nn.softmax(s.reshape(s.shape[:3] + (-1,)), axis=-1).reshape(s.shape)
        return jnp.einsum('bhqnk,bhqnkd->bhqd', pr.astype(v_g.dtype), v_g)

    o = lax.map(attend, (to_chunks(q), to_chunks(idx), to_chunks(valid), pos.reshape(n_chunk, Q_CHUNK)))
    o = jnp.moveaxis(o, 0, 2).reshape(bsz, MOBA_HEADS, seq, MOBA_HD)
    return jnp.moveaxis(o, 1, 2).reshape(bsz, seq, MOBA_DIM).astype(p.dtype)


def hgrn2_mix(p, lb, norm_w):
    bsz, seq, _ = p.shape
    f32 = jnp.float32
    q, f_raw, i, g = jnp.split(p, 4, axis=-1)
    f_raw = f_raw.astype(f32)
    log_f = jnp.log(lb + (1.0 - lb) * jax.nn.sigmoid(f_raw))
    k = (1.0 - lb) * jax.nn.sigmoid(-f_raw)
    n_chunk = seq // HG_CHUNK

    def chunks(t):
        t = t.astype(f32).reshape(bsz, n_chunk, HG_CHUNK, HG_HEADS, -1)
        return jnp.transpose(t, (1, 0, 3, 2, 4))

    causal = jnp.tril(jnp.ones((HG_CHUNK, HG_CHUNK), bool))[:, :, None]

    def step(state, inp):
        q_c, k_c, v_c, lf_c = inp
        b = jnp.cumsum(lf_c, axis=2)
        decay = jnp.exp(jnp.where(causal, b[:, :, :, None, :] - b[:, :, None, :, :], -jnp.inf))
        scores = jnp.einsum('bhtk,bhtsk,bhsk->bhts', q_c, decay, k_c)
        o = (jnp.einsum('bhts,bhsv->bhtv', scores, v_c)
             + jnp.einsum('bhtk,bhkv->bhtv', q_c * jnp.exp(b), state))
        b_last = b[:, :, -1:, :]
        state = (jnp.exp(b_last[:, :, 0, :])[..., None] * state
                 + jnp.einsum('bhsk,bhsv->bhkv', k_c * jnp.exp(b_last - b), v_c))
        return state, o

    s0 = jnp.zeros((bsz, HG_HEADS, HG_DK, HG_DV), f32)
    _, o = lax.scan(step, s0, (chunks(q), chunks(k), chunks(i), chunks(log_f)))
    o = jnp.transpose(o, (1, 0, 3, 2, 4)).reshape(bsz, seq, HG_HEADS, HG_DV)
    o = o * lax.rsqrt(jnp.mean(o * o, axis=-1, keepdims=True) + EPS)
    o = o.reshape(bsz, seq, HG_VDIM) * norm_w * jax.nn.sigmoid(g.astype(f32))
    return o.astype(p.dtype)


def setup_inputs(seed: int = 0) -> dict:
    key = jax.random.key(seed)
    ks = jax.random.split(key, 32)
    f32 = jnp.float32
    nrm = lambda k, shape, scale: scale * jax.random.normal(k, shape, f32)
    D, F = D_MODEL, D_FF
    return {
        'x': jax.random.normal(ks[0], (BATCH, SEQ, D), f32),
        'norm_g': 1.0 + nrm(ks[1], (DEPTH, 3, D), 0.02),
        'ffn1_wg': nrm(ks[2], (DEPTH, D, F), D ** -0.5),
        'ffn1_wu': nrm(ks[3], (DEPTH, D, F), D ** -0.5),
        'ffn1_wd': nrm(ks[4], (DEPTH, F, D), F ** -0.5),
        'ffn2_wg': nrm(ks[5], (DEPTH, D, F), D ** -0.5),
        'ffn2_wu': nrm(ks[6], (DEPTH, D, F), D ** -0.5),
        'ffn2_wd': nrm(ks[7], (DEPTH, F, D), F ** -0.5),
        'ev_w_in': nrm(ks[8], (N_EVEN, D, EVEN_IN), D ** -0.5),
        'ev_w_out': nrm(ks[9], (N_EVEN, EVEN_MIX, D), EVEN_MIX ** -0.5),
        'rw_mu': jax.random.uniform(ks[10], (N_EVEN, RWKV_COLS), f32),
        'rw_w0': jax.random.uniform(ks[11], (N_EVEN, RWKV_DIM), f32, -5.0, 1.0),
        'rw_w2': nrm(ks[12], (N_EVEN, LORA_W, RWKV_DIM), 0.1),
        'rw_a0': nrm(ks[13], (N_EVEN, RWKV_DIM), 0.1),
        'rw_a2': nrm(ks[14], (N_EVEN, LORA_A, RWKV_DIM), LORA_A ** -0.5),
        'rw_g2': nrm(ks[15], (N_EVEN, LORA_G, RWKV_DIM), LORA_G ** -0.5),
        'rw_k_k': 0.85 + nrm(ks[16], (N_EVEN, RWKV_DIM), 0.05),
        'rw_k_a': 1.0 + nrm(ks[17], (N_EVEN, RWKV_DIM), 0.05),
        'rw_r_k': nrm(ks[18], (N_EVEN, RWKV_HEADS, RWKV_HD), 0.1),
        'rw_lnx_w': 1.0 + nrm(ks[19], (N_EVEN, RWKV_DIM), 0.02),
        'rw_lnx_b': nrm(ks[20], (N_EVEN, RWKV_DIM), 0.02),
        'od_w_in': nrm(ks[21], (N_ODD, D, ODD_IN), D ** -0.5),
        'od_w_out': nrm(ks[22], (N_ODD, HG_VDIM, D), HG_VDIM ** -0.5),
        'hg_norm_w': 1.0 + nrm(ks[23], (N_ODD, HG_VDIM), 0.02),
        'hg_lb_logits': nrm(ks[24], (DEPTH, HG_FDIM), 0.1),
        'final_g': 1.0 + nrm(ks[25], (D,), 0.02),
    }


def reference(x, norm_g, ffn1_wg, ffn1_wu, ffn1_wd, ffn2_wg, ffn2_wu, ffn2_wd,
              ev_w_in, ev_w_out, rw_mu, rw_w0, rw_w2, rw_a0, rw_a2, rw_g2, rw_k_k, rw_k_a,
              rw_r_k, rw_lnx_w, rw_lnx_b, od_w_in, od_w_out, hg_norm_w, hg_lb_logits, final_g):
    f32 = jnp.float32
    slopes = jnp.exp2(-8.0 * jnp.arange(1, MOBA_HEADS + 1, dtype=f32) / MOBA_HEADS)
    lb_sm = jax.nn.softmax(hg_lb_logits.astype(f32), axis=0)
    lb_table = jnp.cumsum(lb_sm, axis=0) - lb_sm[0]
    for l in range(DEPTH):
        x = x + 0.5 * swiglu(rmsnorm(x, norm_g[l, 0]), ffn1_wg[l], ffn1_wu[l], ffn1_wd[l])
        h = rmsnorm(x, norm_g[l, 1])
        if l % 2 == 0:
            e = l // 2
            p = h @ ev_w_in[e]
            y_a = rwkv7_mix(p[..., :RWKV_COLS], rw_mu[e], rw_w0[e], rw_w2[e], rw_a0[e], rw_a2[e],
                            rw_g2[e], rw_k_k[e], rw_k_a[e], rw_r_k[e], rw_lnx_w[e], rw_lnx_b[e])
            y_b = moba_mix(p[..., RWKV_COLS:], slopes)
            y = jnp.concatenate([y_a, y_b], axis=-1) @ ev_w_out[e]
        else:
            o = l // 2
            p = h @ od_w_in[o]
            y = hgrn2_mix(p, lb_table[l], hg_norm_w[o]) @ od_w_out[o]
        x = x + y.astype(x.dtype)
        x = x + 0.5 * swiglu(rmsnorm(x, norm_g[l, 2]), ffn2_wg[l], ffn2_wu[l], ffn2_wd[l])
    return rmsnorm(x, final_g)
```

```python
import functools

import jax
import jax.numpy as jnp
from jax import lax
from jax.experimental import pallas as pl
from jax.experimental.pallas import tpu as pltpu

F32 = jnp.float32
BF16 = jnp.bfloat16
HIGHEST = lax.Precision.HIGHEST

EPS = 1e-6
LNX_EPS = 64e-5
RWKV_HEADS = 8
RWKV_HD = 64
LORA_W = 64
LORA_A = 64
LORA_G = 128
MOBA_HEADS = 8
MOBA_HD = 64
MOBA_BLOCK = 256
MOBA_TOPK = 3
HG_HEADS = 8
HG_DK = 128
RWKV_CHUNK = 64
HG_CHUNK = 64
VMEM_LIMIT_BYTES = 56 * 1024 * 1024


def _params(*semantics):
    return pltpu.CompilerParams(dimension_semantics=semantics, vmem_limit_bytes=VMEM_LIMIT_BYTES)


def _dot(a, b, precision=None):
    return jnp.dot(a, b, preferred_element_type=F32, precision=precision)


def _dot_nt(a, b, precision=None):
    return lax.dot_general(a, b, (((1,), (1,)), ((), ())), preferred_element_type=F32, precision=precision)


def _dot_tn(a, b, precision=None):
    return lax.dot_general(a, b, (((0,), (0,)), ((), ())), preferred_element_type=F32, precision=precision)


def _rmsnorm(x, g):
    return x * lax.rsqrt(jnp.mean(x * x, axis=-1, keepdims=True) + EPS) * g


def _iota2(shape, dim):
    return lax.broadcasted_iota(jnp.int32, shape, dim)


def _block_ones(n, width):
    return (_iota2((n, n), 0) // width == _iota2((n, n), 1) // width).astype(F32)


def _ffn_kernel(x_ref, g_ref, wg_ref, wu_ref, wd_ref, fg_ref, o_ref, h_ref, acc_ref, *, final_norm):
    f = pl.program_id(1)

    @pl.when(f == 0)
    def _():
        h_ref[...] = _rmsnorm(x_ref[...], g_ref[...]).astype(BF16)
        acc_ref[...] = jnp.zeros_like(acc_ref)

    h = h_ref[...]
    gate = _dot(h, wg_ref[...])
    up = _dot(h, wu_ref[...])
    act = (gate * jax.nn.sigmoid(gate) * up).astype(BF16)
    acc_ref[...] += _dot(act, wd_ref[...])

    @pl.when(f == pl.num_programs(1) - 1)
    def _():
        out = x_ref[...] + 0.5 * acc_ref[...]
        if final_norm:
            out = _rmsnorm(out, fg_ref[...])
        o_ref[...] = out


def _ffn(x, g, wg, wu, wd, final_g, *, final_norm, tm=1024, tf=256):
    m, d = x.shape
    f = wg.shape[1]
    tm = min(tm, m)
    return pl.pallas_call(
        functools.partial(_ffn_kernel, final_norm=final_norm),
        grid=(m // tm, f // tf),
        in_specs=[
            pl.BlockSpec((tm, d), lambda i, j: (i, 0)),
            pl.BlockSpec((1, d), lambda i, j: (0, 0)),
            pl.BlockSpec((d, tf), lambda i, j: (0, j)),
            pl.BlockSpec((d, tf), lambda i, j: (0, j)),
            pl.BlockSpec((tf, d), lambda i, j: (j, 0)),
            pl.BlockSpec((1, d), lambda i, j: (0, 0)),
        ],
        out_specs=pl.BlockSpec((tm, d), lambda i, j: (i, 0)),
        out_shape=jax.ShapeDtypeStruct((m, d), F32),
        scratch_shapes=[pltpu.VMEM((tm, d), BF16), pltpu.VMEM((tm, d), F32)],
        compiler_params=_params("parallel", "arbitrary"),
        name="ffn",
    )(x, g.reshape(1, d), wg.astype(BF16), wu.astype(BF16), wd.astype(BF16), final_g.reshape(1, d))


def _inproj_kernel(x_ref, g_ref, *refs, n_w, tn):
    w_refs, o_refs = refs[:n_w], refs[n_w:]
    h = _rmsnorm(x_ref[...], g_ref[...]).astype(BF16)
    for w_ref, o_ref in zip(w_refs, o_refs):
        n = w_ref.shape[1]
        for c0 in range(0, n, tn):
            c1 = min(c0 + tn, n)
            o_ref[:, c0:c1] = _dot(h, w_ref[:, c0:c1])


def _inproj(x, g, ws, *, tm=256, tn=512):
    m, d = x.shape
    tm = min(tm, m)
    return pl.pallas_call(
        functools.partial(_inproj_kernel, n_w=len(ws), tn=tn),
        grid=(m // tm,),
        in_specs=[pl.BlockSpec((tm, d), lambda i: (i, 0)), pl.BlockSpec((1, d), lambda i: (0, 0))]
        + [pl.BlockSpec(w.shape, lambda i: (0, 0)) for w in ws],
        out_specs=[pl.BlockSpec((tm, w.shape[1]), lambda i: (i, 0)) for w in ws],
        out_shape=[jax.ShapeDtypeStruct((m, w.shape[1]), F32) for w in ws],
        compiler_params=_params("parallel"),
        name="inproj",
    )(x, g.reshape(1, d), *[w.astype(BF16) for w in ws])


def _outproj_kernel(x_ref, *refs, n_y):
    y_refs, w_refs, o_ref = refs[:n_y], refs[n_y:2 * n_y], refs[2 * n_y]
    acc = x_ref[...]
    for y_ref, w_ref in zip(y_refs, w_refs):
        acc = acc + _dot(y_ref[...].astype(BF16), w_ref[...])
    o_ref[...] = acc


def _outproj(x, ys, ws, *, tm=512):
    m, d = x.shape
    tm = min(tm, m)
    return pl.pallas_call(
        functools.partial(_outproj_kernel, n_y=len(ys)),
        grid=(m // tm,),
        in_specs=[pl.BlockSpec((tm, d), lambda i: (i, 0))]
        + [pl.BlockSpec((tm, y.shape[1]), lambda i: (i, 0)) for y in ys]
        + [pl.BlockSpec(w.shape, lambda i: (0, 0)) for w in ws],
        out_specs=pl.BlockSpec((tm, d), lambda i: (i, 0)),
        out_shape=jax.ShapeDtypeStruct((m, d), F32),
        compiler_params=_params("parallel"),
        name="outproj",
    )(x, *ys, *[w.astype(BF16) for w in ws])


def _unit_lower_inverse(a_strict, c):
    row = _iota2((c, c), 0)
    col = _iota2((c, c), 1)
    eye = (row == col).astype(F32)
    t = None
    m = 1
    while m < c:
        mask = ((row // (2 * m)) == (col // (2 * m))) & ((row & m) != 0) & ((col & m) == 0)
        lm = jnp.where(mask, a_strict, 0.0)
        if t is None:
            t = eye - lm
        else:
            t = t - _dot(_dot(t, lm, HIGHEST), t, HIGHEST)
        m *= 2
    return t


def _rwkv_kernel(p_ref, mu_ref, w0_ref, w2_ref, a0_ref, a2_ref, g2_ref, kk_ref, ka_ref, rk_ref,
                 lnw_ref, lnb_ref, o_ref, carry_ref, st_ref, *, chunk, heads, hd):
    c = chunk
    dim = heads * hd

    @pl.when(pl.program_id(1) == 0)
    def _():
        carry_ref[...] = jnp.zeros_like(carry_ref)
        st_ref[...] = jnp.zeros_like(st_ref)

    p = p_ref[0]
    row = _iota2((c, 1), 0)
    prev = jnp.where(row == 0, carry_ref[...], pltpu.roll(p, 1, axis=0))
    carry_ref[...] = p[c - 1:c, :]
    xs = p + (prev - p) * mu_ref[...]

    r = xs[:, 0:dim]
    k = xs[:, dim:2 * dim]
    v = xs[:, 2 * dim:3 * dim]
    o1 = 3 * dim
    w_lr = xs[:, o1:o1 + LORA_W]
    a_lr = xs[:, o1 + LORA_W:o1 + LORA_W + LORA_A]
    g_lr = xs[:, o1 + LORA_W + LORA_A:o1 + LORA_W + LORA_A + LORA_G]

    z = w0_ref[...] + _dot(jnp.tanh(w_lr), w2_ref[...], HIGHEST)
    softplus = jnp.maximum(-z, 0.0) + jnp.log1p(jnp.exp(-jnp.abs(z)))
    w_raw = -softplus - 0.5
    lw = -jnp.exp(w_raw)
    a = jax.nn.sigmoid(a0_ref[...] + _dot(a_lr, a2_ref[...], HIGHEST))
    g = _dot(jax.nn.sigmoid(g_lr), g2_ref[...], HIGHEST)

    head_sum = _block_ones(dim, hd)
    kk = k * kk_ref[...]
    kk = kk * lax.rsqrt(jnp.maximum(_dot(kk * kk, head_sum, HIGHEST), 1e-24))
    k2 = k * (1.0 + (a - 1.0) * ka_ref[...])
    bb = kk * a

    tri_incl = (_iota2((c, c), 0) >= _iota2((c, c), 1))
    tri_strict = (_iota2((c, c), 0) > _iota2((c, c), 1))
    cum = _dot(tri_incl.astype(F32), lw, HIGHEST)
    cum_last = cum[c - 1:c, :]
    w_incl = jnp.exp(cum)
    w_excl = jnp.exp(cum - lw)
    w_inv = jnp.exp(-cum)
    w_tail = jnp.exp(cum_last - cum)
    w_all = jnp.exp(cum_last)

    kt = kk * w_excl
    rt = r * w_incl
    bt = bb * w_inv
    kd = k2 * w_inv
    bw = bb * w_tail
    kw = k2 * w_tail

    ys = []
    for h in range(heads):
        sl = slice(h * hd, (h + 1) * hd)
        kt_h, rt_h, bt_h, kd_h, bw_h, kw_h, v_h = (t[:, sl] for t in (kt, rt, bt, kd, bw, kw, v))
        kr = jnp.concatenate([kt_h, rt_h], axis=0)
        gb = _dot_nt(kr, bt_h, HIGHEST)
        gk = _dot_nt(kr, kd_h, HIGHEST)
        a_b = jnp.where(tri_strict, gb[:c], 0.0)
        a_k = jnp.where(tri_strict, gk[:c], 0.0)
        p_b = jnp.where(tri_incl, gb[c:], 0.0)
        p_k = jnp.where(tri_incl, gk[c:], 0.0)
        t_inv = _unit_lower_inverse(a_b, c)
        kt1 = _dot(t_inv, kt_h, HIGHEST)
        v1 = _dot(t_inv, _dot(a_k, v_h, HIGHEST), HIGHEST)
        q_mat = rt_h - _dot(p_b, kt1, HIGHEST)
        z_mat = _dot(p_k, v_h, HIGHEST) - _dot(p_b, v1, HIGHEST)
        x_mat = _dot_tn(bw_h, kt1, HIGHEST)
        n_mat = _dot_tn(v_h, kw_h, HIGHEST) - _dot_tn(v1, bw_h, HIGHEST)
        s = st_ref[h]
        ys.append(_dot_nt(q_mat, s, HIGHEST) + z_mat)
        st_ref[h] = s * w_all[:, sl] - _dot_nt(s, x_mat, HIGHEST) + n_mat
    y = jnp.concatenate(ys, axis=1)

    head_mean = head_sum * (1.0 / hd)
    mean = _dot(y, head_mean, HIGHEST)
    yc = y - mean
    var = _dot(yc * yc, head_mean, HIGHEST)
    yn = yc * lax.rsqrt(var + LNX_EPS) * lnw_ref[...] + lnb_ref[...]
    bonus = _dot(r * k2 * rk_ref[...], head_sum, HIGHEST) * v
    o_ref[0] = (yn + bonus) * g


def _rwkv(p, mu, w0, w2, a0, a2, g2, k_k, k_a, r_k, lnx_w, lnx_b, *, chunk=RWKV_CHUNK):
    b, s, cols = p.shape
    heads, hd = RWKV_HEADS, RWKV_HD
    dim = heads * hd
    chunk = min(chunk, s)
    row = lambda t: t.reshape(1, -1)
    vecs = [row(mu), row(w0), w2, row(a0), a2, g2, row(k_k), row(k_a), row(r_k), row(lnx_w), row(lnx_b)]
    return pl.pallas_call(
        functools.partial(_rwkv_kernel, chunk=chunk, heads=heads, hd=hd),
        grid=(b, s // chunk),
        in_specs=[pl.BlockSpec((1, chunk, cols), lambda i, j: (i, j, 0))]
        + [pl.BlockSpec(t.shape, lambda i, j: (0, 0)) for t in vecs],
        out_specs=pl.BlockSpec((1, chunk, dim), lambda i, j: (i, j, 0)),
        out_shape=jax.ShapeDtypeStruct((b, s, dim), F32),
        scratch_shapes=[pltpu.VMEM((1, cols), F32), pltpu.VMEM((heads, hd, hd), F32)],
        compiler_params=_params("parallel", "arbitrary"),
        name="rwkv",
    )(p, *vecs)


def _moba_kernel(q_ref, k_ref, v_ref, o_ref, km_ref, *, nb, blk, n_sel, heads, hd):
    j = pl.program_id(1)
    scale = hd ** -0.5
    neg_inf = float("-inf")

    @pl.when(j == 0)
    def _():
        km_ref[...] = jnp.zeros_like(km_ref)
        for n in range(nb):
            km_ref[n:n + 1, :] = jnp.mean(k_ref[0, n * blk:(n + 1) * blk, :], axis=0, keepdims=True)

    nbp = km_ref.shape[0]
    lane = _iota2((blk, nbp), 1)
    dist0 = _iota2((blk, blk), 0) - _iota2((blk, blk), 1)
    own_start = pl.multiple_of(j * blk, blk)

    for h in range(heads):
        sl = slice(h * hd, (h + 1) * hd)
        slope = 2.0 ** (-8.0 * (h + 1) / heads)
        qh = q_ref[0, :, sl] * scale
        qb = qh.astype(BF16)

        gate = _dot_nt(qh, km_ref[:, sl], HIGHEST)
        gate = jnp.where(lane < j, gate, neg_inf)
        cnt = jnp.zeros((blk, nbp), jnp.int32)
        for m in range(nb):
            col = gate[:, m:m + 1]
            beats = (col > gate) | ((col == gate) & (m < lane))
            cnt = cnt + beats.astype(jnp.int32)
        sel = ((lane < j) & (cnt < n_sel)).astype(F32)

        kj = k_ref[0, pl.ds(own_start, blk), sl].astype(BF16)
        vj = v_ref[0, pl.ds(own_start, blk), sl].astype(BF16)
        s = _dot_nt(qb, kj) - slope * dist0.astype(F32)
        s = jnp.where(dist0 >= 0, s, neg_inf)
        m0 = jnp.max(s, axis=-1, keepdims=True)
        p0 = jnp.exp(s - m0)
        l0 = jnp.sum(p0, axis=-1, keepdims=True)
        acc0 = _dot(p0.astype(BF16), vj)

        def past_block(n, carry, qb=qb, sel=sel, sl=sl, slope=slope):
            m_run, l_run, acc = carry
            start = pl.multiple_of(n * blk, blk)
            kn = k_ref[0, pl.ds(start, blk), sl].astype(BF16)
            vn = v_ref[0, pl.ds(start, blk), sl].astype(BF16)
            dist = dist0 + (j - n) * blk
            sc = _dot_nt(qb, kn) - slope * dist.astype(F32)
            chosen = jnp.max(jnp.where(lane == n, sel, 0.0), axis=-1, keepdims=True)
            sc = jnp.where(chosen > 0.0, sc, neg_inf)
            m_new = jnp.maximum(m_run, jnp.max(sc, axis=-1, keepdims=True))
            alpha = jnp.exp(m_run - m_new)
            pr = jnp.exp(sc - m_new)
            l_new = alpha * l_run + jnp.sum(pr, axis=-1, keepdims=True)
            acc_new = alpha * acc + _dot(pr.astype(BF16), vn)
            return m_new, l_new, acc_new

        _, l_fin, acc_fin = lax.fori_loop(0, j, past_block, (m0, l0, acc0))
        o_ref[0, :, sl] = acc_fin / l_fin


def _moba(q, k, v):
    b, s, dim = q.shape
    heads, hd, blk = MOBA_HEADS, MOBA_HD, MOBA_BLOCK
    assert s % blk == 0
    nb = s // blk
    n_sel = min(MOBA_TOPK, nb - 1)
    return pl.pallas_call(
        functools.partial(_moba_kernel, nb=nb, blk=blk, n_sel=n_sel, heads=heads, hd=hd),
        grid=(b, nb),
        in_specs=[
            pl.BlockSpec((1, blk, dim), lambda i, j: (i, j, 0)),
            pl.BlockSpec((1, s, dim), lambda i, j: (i, 0, 0)),
            pl.BlockSpec((1, s, dim), lambda i, j: (i, 0, 0)),
        ],
        out_specs=pl.BlockSpec((1, blk, dim), lambda i, j: (i, j, 0)),
        out_shape=jax.ShapeDtypeStruct((b, s, dim), F32),
        scratch_shapes=[pltpu.VMEM((max(nb, 8), dim), F32)],
        compiler_params=_params("parallel", "arbitrary"),
        name="moba",
    )(q, k, v)


def _hgrn_kernel(q_ref, f_ref, i_ref, g_ref, lbl_ref, nw_ref, o_ref, st_ref, *, chunk, heads, dk, layer):
    c = chunk

    @pl.when(pl.program_id(1) == 0)
    def _():
        st_ref[...] = jnp.zeros_like(st_ref)

    logits = lbl_ref[...]
    e = jnp.exp(logits - jnp.max(logits, axis=0, keepdims=True))
    sm = e / jnp.sum(e, axis=0, keepdims=True)
    lb = jnp.sum(sm[0:layer + 1, :], axis=0, keepdims=True) - sm[0:1, :]

    fr = f_ref[0]
    lf = jnp.log(lb + (1.0 - lb) * jax.nn.sigmoid(fr))
    kf = (1.0 - lb) * jax.nn.sigmoid(-fr)
    tri_incl = (_iota2((c, c), 0) >= _iota2((c, c), 1))
    b = _dot(tri_incl.astype(F32), lf, HIGHEST)
    b_last = b[c - 1:c, :]
    qd = q_ref[0] * jnp.exp(b)
    kd = kf * jnp.exp(-b)
    kw = kf * jnp.exp(b_last - b)
    w_all = jnp.exp(b_last)
    v = i_ref[0]

    outs = []
    for h in range(heads):
        sl = slice(h * dk, (h + 1) * dk)
        qd_h, kd_h, kw_h, v_h = qd[:, sl], kd[:, sl], kw[:, sl], v[:, sl]
        sc = jnp.where(tri_incl, _dot_nt(qd_h, kd_h, HIGHEST), 0.0)
        s = st_ref[h]
        o_h = _dot(sc, v_h, HIGHEST) + _dot_nt(qd_h, s, HIGHEST)
        st_ref[h] = s * w_all[:, sl] + _dot_tn(v_h, kw_h, HIGHEST)
        outs.append(o_h * lax.rsqrt(jnp.mean(o_h * o_h, axis=-1, keepdims=True) + EPS))
    o = jnp.concatenate(outs, axis=1)
    o_ref[0] = o * nw_ref[...] * jax.nn.sigmoid(g_ref[0])


def _hgrn(p, lb_logits, norm_w, *, layer, chunk=HG_CHUNK):
    b, s, cols = p.shape
    heads, dk = HG_HEADS, HG_DK
    dim = heads * dk
    assert cols == 4 * dim
    chunk = min(chunk, s)
    col_block = lambda n: pl.BlockSpec((1, chunk, dim), lambda i, j, n=n: (i, j, n))
    return pl.pallas_call(
        functools.partial(_hgrn_kernel, chunk=chunk, heads=heads, dk=dk, layer=layer),
        grid=(b, s // chunk),
        in_specs=[col_block(0), col_block(1), col_block(2), col_block(3),
                  pl.BlockSpec(lb_logits.shape, lambda i, j: (0, 0)),
                  pl.BlockSpec((1, dim), lambda i, j: (0, 0))],
        out_specs=pl.BlockSpec((1, chunk, dim), lambda i, j: (i, j, 0)),
        out_shape=jax.ShapeDtypeStruct((b, s, dim), F32),
        scratch_shapes=[pltpu.VMEM((heads, dk, dk), F32)],
        compiler_params=_params("parallel", "arbitrary"),
        name="hgrn",
    )(p, p, p, p, lb_logits, norm_w.reshape(1, dim))


def kernel(x, norm_g, ffn1_wg, ffn1_wu, ffn1_wd, ffn2_wg, ffn2_wu, ffn2_wd, ev_w_in, ev_w_out, rw_mu, rw_w0, rw_w2, rw_a0, rw_a2, rw_g2, rw_k_k, rw_k_a, rw_r_k, rw_lnx_w, rw_lnx_b, od_w_in, od_w_out, hg_norm_w, hg_lb_logits, final_g):
    bsz, seq, d = x.shape
    depth = norm_g.shape[0]
    rwkv_dim = RWKV_HEADS * RWKV_HD
    rwkv_cols = 3 * rwkv_dim + LORA_W + LORA_A + LORA_G
    moba_dim = MOBA_HEADS * MOBA_HD
    xf = x.reshape(bsz * seq, d)
    for l in range(depth):
        xf = _ffn(xf, norm_g[l, 0], ffn1_wg[l], ffn1_wu[l], ffn1_wd[l], final_g, final_norm=False)
        if l % 2 == 0:
            e = l // 2
            w_in = ev_w_in[e]
            splits = [0, rwkv_cols, rwkv_cols + moba_dim, rwkv_cols + 2 * moba_dim, rwkv_cols + 3 * moba_dim]
            p_r, q, k, v = _inproj(xf, norm_g[l, 1], [w_in[:, a:b] for a, b in zip(splits[:-1], splits[1:])])
            y_a = _rwkv(p_r.reshape(bsz, seq, rwkv_cols), rw_mu[e], rw_w0[e], rw_w2[e], rw_a0[e], rw_a2[e],
                        rw_g2[e], rw_k_k[e], rw_k_a[e], rw_r_k[e], rw_lnx_w[e], rw_lnx_b[e])
            y_b = _moba(q.reshape(bsz, seq, moba_dim), k.reshape(bsz, seq, moba_dim),
                        v.reshape(bsz, seq, moba_dim))
            w_out = ev_w_out[e]
            xf = _outproj(xf, [y_a.reshape(-1, rwkv_dim), y_b.reshape(-1, moba_dim)],
                          [w_out[:rwkv_dim], w_out[rwkv_dim:]])
        else:
            o = l // 2
            (p,) = _inproj(xf, norm_g[l, 1], [od_w_in[o]])
            y = _hgrn(p.reshape(bsz, seq, -1), hg_lb_logits, hg_norm_w[o], layer=l)
            xf = _outproj(xf, [y.reshape(bsz * seq, -1)], [od_w_out[o]])
        xf = _ffn(xf, norm_g[l, 2], ffn2_wg[l], ffn2_wu[l], ffn2_wd[l], final_g, final_norm=(l == depth - 1))
    return xf.reshape(bsz, seq, d)
```

```python
import functools

import jax
import jax.numpy as jnp
from jax import lax
from jax.experimental import pallas as pl
from jax.experimental.pallas import tpu as pltpu

F32 = jnp.float32
BF16 = jnp.bfloat16
NN = (((1,), (0,)), ((), ()))
NT = (((1,), (1,)), ((), ()))
TN = (((0,), (0,)), ((), ()))
BNN = (((2,), (1,)), ((0,), (0,)))
BNT = (((2,), (2,)), ((0,), (0,)))
BTN = (((1,), (1,)), ((0,), (0,)))
MIX_PIECES = 2

EPS = 1e-6
LNX_EPS = 64e-5
RWKV_HEADS = 8
RWKV_HD = 64
LORA_W = 64
LORA_A = 64
LORA_G = 128
MOBA_HEADS = 8
MOBA_HD = 64
MOBA_BLOCK = 256
MOBA_TOPK = 3
HG_HEADS = 8
HG_DK = 128
RWKV_CHUNK = 64
HG_CHUNK = 64
VMEM_LIMIT_BYTES = 56 * 1024 * 1024


def _params(*semantics):
    return pltpu.CompilerParams(dimension_semantics=semantics, vmem_limit_bytes=VMEM_LIMIT_BYTES)


def _dot(a, b):
    return jnp.dot(a, b, preferred_element_type=F32)


def _dot_nt(a, b):
    return lax.dot_general(a, b, NT, preferred_element_type=F32)


def _pieces(a, n):
    if isinstance(a, tuple):
        return a
    out = []
    for i in range(n):
        hi = a.astype(BF16)
        out.append(hi)
        if i + 1 < n:
            a = a - hi.astype(F32)
    return tuple(out)


def _mm(a, b, dims=NN, n=MIX_PIECES):
    a = _pieces(a, n)
    b = _pieces(b, n)
    order = max(len(a), len(b)) - 1
    out = None
    for i, ai in enumerate(a):
        for j, bj in enumerate(b):
            if i + j <= order:
                t = lax.dot_general(ai, bj, dims, preferred_element_type=F32)
                out = t if out is None else out + t
    return out


def _rmsnorm(x, g):
    return x * lax.rsqrt(jnp.mean(x * x, axis=-1, keepdims=True) + EPS) * g


def _iota2(shape, dim):
    return lax.broadcasted_iota(jnp.int32, shape, dim)


def _block_ones(n, width):
    return (_iota2((n, n), 0) // width == _iota2((n, n), 1) // width).astype(F32)


def _ffn_kernel(x_ref, g_ref, wg_ref, wu_ref, wd_ref, fg_ref, o_ref, h_ref, acc_ref, *, final_norm):
    f = pl.program_id(1)

    @pl.when(f == 0)
    def _():
        h_ref[...] = _rmsnorm(x_ref[...], g_ref[...]).astype(BF16)
        acc_ref[...] = jnp.zeros_like(acc_ref)

    h = h_ref[...]
    gate = _dot(h, wg_ref[...])
    up = _dot(h, wu_ref[...])
    act = (gate * jax.nn.sigmoid(gate) * up).astype(BF16)
    acc_ref[...] += _dot(act, wd_ref[...])

    @pl.when(f == pl.num_programs(1) - 1)
    def _():
        out = x_ref[...] + 0.5 * acc_ref[...]
        if final_norm:
            out = _rmsnorm(out, fg_ref[...])
        o_ref[...] = out


def _ffn(x, g, wg, wu, wd, final_g, *, final_norm, tm=1024, tf=256):
    m, d = x.shape
    f = wg.shape[1]
    tm = min(tm, m)
    return pl.pallas_call(
        functools.partial(_ffn_kernel, final_norm=final_norm),
        grid=(m // tm, f // tf),
        in_specs=[
            pl.BlockSpec((tm, d), lambda i, j: (i, 0)),
            pl.BlockSpec((1, d), lambda i, j: (0, 0)),
            pl.BlockSpec((d, tf), lambda i, j: (0, j)),
            pl.BlockSpec((d, tf), lambda i, j: (0, j)),
            pl.BlockSpec((tf, d), lambda i, j: (j, 0)),
            pl.BlockSpec((1, d), lambda i, j: (0, 0)),
        ],
        out_specs=pl.BlockSpec((tm, d), lambda i, j: (i, 0)),
        out_shape=jax.ShapeDtypeStruct((m, d), F32),
        scratch_shapes=[pltpu.VMEM((tm, d), BF16), pltpu.VMEM((tm, d), F32)],
        compiler_params=_params("parallel", "arbitrary"),
        name="ffn",
    )(x, g.reshape(1, d), wg.astype(BF16), wu.astype(BF16), wd.astype(BF16), final_g.reshape(1, d))


def _inproj_kernel(x_ref, g_ref, *refs, n_w, tn):
    w_refs, o_refs = refs[:n_w], refs[n_w:]
    h = _rmsnorm(x_ref[...], g_ref[...]).astype(BF16)
    for w_ref, o_ref in zip(w_refs, o_refs):
        n = w_ref.shape[1]
        for c0 in range(0, n, tn):
            c1 = min(c0 + tn, n)
            o_ref[:, c0:c1] = _dot(h, w_ref[:, c0:c1])


def _inproj(x, g, ws, *, tm=256, tn=512):
    m, d = x.shape
    tm = min(tm, m)
    return pl.pallas_call(
        functools.partial(_inproj_kernel, n_w=len(ws), tn=tn),
        grid=(m // tm,),
        in_specs=[pl.BlockSpec((tm, d), lambda i: (i, 0)), pl.BlockSpec((1, d), lambda i: (0, 0))]
        + [pl.BlockSpec(w.shape, lambda i: (0, 0)) for w in ws],
        out_specs=[pl.BlockSpec((tm, w.shape[1]), lambda i: (i, 0)) for w in ws],
        out_shape=[jax.ShapeDtypeStruct((m, w.shape[1]), F32) for w in ws],
        compiler_params=_params("parallel"),
        name="inproj",
    )(x, g.reshape(1, d), *[w.astype(BF16) for w in ws])


def _outproj_kernel(x_ref, *refs, n_y):
    y_refs, w_refs, o_ref = refs[:n_y], refs[n_y:2 * n_y], refs[2 * n_y]
    acc = x_ref[...]
    for y_ref, w_ref in zip(y_refs, w_refs):
        acc = acc + _dot(y_ref[...].astype(BF16), w_ref[...])
    o_ref[...] = acc


def _outproj(x, ys, ws, *, tm=512):
    m, d = x.shape
    tm = min(tm, m)
    return pl.pallas_call(
        functools.partial(_outproj_kernel, n_y=len(ys)),
        grid=(m // tm,),
        in_specs=[pl.BlockSpec((tm, d), lambda i: (i, 0))]
        + [pl.BlockSpec((tm, y.shape[1]), lambda i: (i, 0)) for y in ys]
        + [pl.BlockSpec(w.shape, lambda i: (0, 0)) for w in ws],
        out_specs=pl.BlockSpec((tm, d), lambda i: (i, 0)),
        out_shape=jax.ShapeDtypeStruct((m, d), F32),
        compiler_params=_params("parallel"),
        name="outproj",
    )(x, *ys, *[w.astype(BF16) for w in ws])


def _unit_lower_inverse(a_strict, c):
    row = _iota2((1, c, c), 1)
    col = _iota2((1, c, c), 2)
    eye = (row == col).astype(F32)
    t = None
    m = 1
    while m < c:
        mask = ((row // (2 * m)) == (col // (2 * m))) & ((row & m) != 0) & ((col & m) == 0)
        lm = jnp.where(mask, a_strict, 0.0)
        if t is None:
            t = eye - lm
        else:
            tp = _pieces(t, MIX_PIECES)
            t = t - _mm(_mm(tp, lm, BNN), tp, BNN)
        m *= 2
    return t


def _rwkv_kernel(p_ref, mu_ref, w0_ref, w2_ref, a0_ref, a2_ref, g2_ref, kk_ref, ka_ref, rk_ref,
                 lnw_ref, lnb_ref, o_ref, carry_ref, st_ref, *, chunk, heads, hd):
    c = chunk
    dim = heads * hd

    @pl.when(pl.program_id(1) == 0)
    def _():
        carry_ref[...] = jnp.zeros_like(carry_ref)
        st_ref[...] = jnp.zeros_like(st_ref)

    p = p_ref[0]
    row = _iota2((c, 1), 0)
    prev = jnp.where(row == 0, carry_ref[...], pltpu.roll(p, 1, axis=0))
    carry_ref[...] = p[c - 1:c, :]
    xs = p + (prev - p) * mu_ref[...]

    r = xs[:, 0:dim]
    k = xs[:, dim:2 * dim]
    v = xs[:, 2 * dim:3 * dim]
    o1 = 3 * dim
    w_lr = xs[:, o1:o1 + LORA_W]
    a_lr = xs[:, o1 + LORA_W:o1 + LORA_W + LORA_A]
    g_lr = xs[:, o1 + LORA_W + LORA_A:o1 + LORA_W + LORA_A + LORA_G]

    z = w0_ref[...] + _mm(jnp.tanh(w_lr), w2_ref[...])
    softplus = jnp.maximum(-z, 0.0) + jnp.log1p(jnp.exp(-jnp.abs(z)))
    w_raw = -softplus - 0.5
    lw = -jnp.exp(w_raw)
    a = jax.nn.sigmoid(a0_ref[...] + _mm(a_lr, a2_ref[...]))
    g = _mm(jax.nn.sigmoid(g_lr), g2_ref[...])

    head_sum = (_block_ones(dim, hd).astype(BF16),)
    kk = k * kk_ref[...]
    kk = kk * lax.rsqrt(jnp.maximum(_mm(kk * kk, head_sum), 1e-24))
    k2 = k * (1.0 + (a - 1.0) * ka_ref[...])
    bb = kk * a

    tri_incl = (_iota2((c, c), 0) >= _iota2((c, c), 1))
    tri_strict = (_iota2((c, c), 0) > _iota2((c, c), 1))
    cum = _mm((tri_incl.astype(BF16),), _pieces(lw, 3))
    cum_last = cum[c - 1:c, :]
    w_incl = jnp.exp(cum)
    w_excl = jnp.exp(cum - lw)
    w_inv = jnp.exp(-cum)
    w_tail = jnp.exp(cum_last - cum)
    w_all = jnp.exp(cum_last)

    kt = kk * w_excl
    rt = r * w_incl
    bt = bb * w_inv
    kd = k2 * w_inv
    bw = bb * w_tail
    kw = k2 * w_tail

    by_head = lambda t: jnp.stack([t[:, h * hd:(h + 1) * hd] for h in range(heads)], axis=0)
    split = lambda t: _pieces(t, MIX_PIECES)
    kt_h, rt_h, bt_h, kd_h, kw_h = (by_head(t) for t in (kt, rt, bt, kd, kw))
    v_h, bw_h = split(by_head(v)), split(by_head(bw))
    w_all_h = by_head(w_all)
    kr = split(jnp.concatenate([kt_h, rt_h], axis=1))
    gb = _mm(kr, bt_h, BNT)
    gk = _mm(kr, kd_h, BNT)
    a_b = jnp.where(tri_strict[None], gb[:, :c], 0.0)
    a_k = jnp.where(tri_strict[None], gk[:, :c], 0.0)
    p_b = split(jnp.where(tri_incl[None], gb[:, c:], 0.0))
    p_k = jnp.where(tri_incl[None], gk[:, c:], 0.0)
    t_inv = split(_unit_lower_inverse(a_b, c))
    kt1 = split(_mm(t_inv, kt_h, BNN))
    v1 = split(_mm(t_inv, _mm(a_k, v_h, BNN), BNN))
    q_mat = rt_h - _mm(p_b, kt1, BNN)
    z_mat = _mm(p_k, v_h, BNN) - _mm(p_b, v1, BNN)
    x_mat = _mm(bw_h, kt1, BTN)
    n_mat = _mm(v_h, kw_h, BTN) - _mm(v1, bw_h, BTN)
    s = st_ref[...]
    sp = split(s)
    y_h = _mm(q_mat, sp, BNT) + z_mat
    st_ref[...] = s * w_all_h - _mm(sp, x_mat, BNT) + n_mat
    y = jnp.concatenate([y_h[h] for h in range(heads)], axis=1)

    inv_hd = 1.0 / hd
    mean = _mm(y, head_sum) * inv_hd
    yc = y - mean
    var = _mm(yc * yc, head_sum) * inv_hd
    yn = yc * lax.rsqrt(var + LNX_EPS) * lnw_ref[...] + lnb_ref[...]
    bonus = _mm(r * k2 * rk_ref[...], head_sum) * v
    o_ref[0] = (yn + bonus) * g


def _rwkv(p, mu, w0, w2, a0, a2, g2, k_k, k_a, r_k, lnx_w, lnx_b, *, chunk=RWKV_CHUNK):
    b, s, cols = p.shape
    heads, hd = RWKV_HEADS, RWKV_HD
    dim = heads * hd
    chunk = min(chunk, s)
    row = lambda t: t.reshape(1, -1)
    vecs = [row(mu), row(w0), w2, row(a0), a2, g2, row(k_k), row(k_a), row(r_k), row(lnx_w), row(lnx_b)]
    return pl.pallas_call(
        functools.partial(_rwkv_kernel, chunk=chunk, heads=heads, hd=hd),
        grid=(b, s // chunk),
        in_specs=[pl.BlockSpec((1, chunk, cols), lambda i, j: (i, j, 0))]
        + [pl.BlockSpec(t.shape, lambda i, j: (0, 0)) for t in vecs],
        out_specs=pl.BlockSpec((1, chunk, dim), lambda i, j: (i, j, 0)),
        out_shape=jax.ShapeDtypeStruct((b, s, dim), F32),
        scratch_shapes=[pltpu.VMEM((1, cols), F32), pltpu.VMEM((heads, hd, hd), F32)],
        compiler_params=_params("parallel", "arbitrary"),
        name="rwkv",
    )(p, *vecs)


def _moba_kernel(q_ref, k_ref, v_ref, o_ref, km_ref, *, nb, blk, n_sel, heads, hd):
    j = pl.program_id(1)
    scale = hd ** -0.5
    neg_inf = float("-inf")

    @pl.when(j == 0)
    def _():
        km_ref[...] = jnp.zeros_like(km_ref)
        for n in range(nb):
            km_ref[n:n + 1, :] = jnp.mean(k_ref[0, n * blk:(n + 1) * blk, :], axis=0, keepdims=True)

    nbp = km_ref.shape[0]
    lane = _iota2((blk, nbp), 1)
    dist0 = _iota2((blk, blk), 0) - _iota2((blk, blk), 1)
    own_start = pl.multiple_of(j * blk, blk)

    for h in range(heads):
        sl = slice(h * hd, (h + 1) * hd)
        slope = 2.0 ** (-8.0 * (h + 1) / heads)
        qh = q_ref[0, :, sl] * scale
        qb = qh.astype(BF16)

        gate = _mm(_pieces(qh, 3), _pieces(km_ref[:, sl], 3), NT)
        gate = jnp.where(lane < j, gate, neg_inf)
        cnt = jnp.zeros((blk, nbp), jnp.int32)
        for m in range(nb):
            col = gate[:, m:m + 1]
            beats = (col > gate) | ((col == gate) & (m < lane))
            cnt = cnt + beats.astype(jnp.int32)
        sel = ((lane < j) & (cnt < n_sel)).astype(F32)

        kj = k_ref[0, pl.ds(own_start, blk), sl].astype(BF16)
        vj = v_ref[0, pl.ds(own_start, blk), sl].astype(BF16)
        s = _dot_nt(qb, kj) - slope * dist0.astype(F32)
        s = jnp.where(dist0 >= 0, s, neg_inf)
        m0 = jnp.max(s, axis=-1, keepdims=True)
        p0 = jnp.exp(s - m0)
        l0 = jnp.sum(p0, axis=-1, keepdims=True)
        acc0 = _dot(p0.astype(BF16), vj)

        def past_block(n, carry, qb=qb, sel=sel, sl=sl, slope=slope):
            m_run, l_run, acc = carry
            start = pl.multiple_of(n * blk, blk)
            kn = k_ref[0, pl.ds(start, blk), sl].astype(BF16)
            vn = v_ref[0, pl.ds(start, blk), sl].astype(BF16)
            dist = dist0 + (j - n) * blk
            sc = _dot_nt(qb, kn) - slope * dist.astype(F32)
            chosen = jnp.max(jnp.where(lane == n, sel, 0.0), axis=-1, keepdims=True)
            sc = jnp.where(chosen > 0.0, sc, neg_inf)
            m_new = jnp.maximum(m_run, jnp.max(sc, axis=-1, keepdims=True))
            alpha = jnp.exp(m_run - m_new)
            pr = jnp.exp(sc - m_new)
            l_new = alpha * l_run + jnp.sum(pr, axis=-1, keepdims=True)
            acc_new = alpha * acc + _dot(pr.astype(BF16), vn)
            return m_new, l_new, acc_new

        _, l_fin, acc_fin = lax.fori_loop(0, j, past_block, (m0, l0, acc0))
        o_ref[0, :, sl] = acc_fin / l_fin


def _moba(q, k, v):
    b, s, dim = q.shape
    heads, hd, blk = MOBA_HEADS, MOBA_HD, MOBA_BLOCK
    assert s % blk == 0
    nb = s // blk
    n_sel = min(MOBA_TOPK, nb - 1)
    return pl.pallas_call(
        functools.partial(_moba_kernel, nb=nb, blk=blk, n_sel=n_sel, heads=heads, hd=hd),
        grid=(b, nb),
        in_specs=[
            pl.BlockSpec((1, blk, dim), lambda i, j: (i, j, 0)),
            pl.BlockSpec((1, s, dim), lambda i, j: (i, 0, 0)),
            pl.BlockSpec((1, s, dim), lambda i, j: (i, 0, 0)),
        ],
        out_specs=pl.BlockSpec((1, blk, dim), lambda i, j: (i, j, 0)),
        out_shape=jax.ShapeDtypeStruct((b, s, dim), F32),
        scratch_shapes=[pltpu.VMEM((max(nb, 8), dim), F32)],
        compiler_params=_params("parallel", "arbitrary"),
        name="moba",
    )(q, k, v)


def _hgrn_kernel(q_ref, f_ref, i_ref, g_ref, lbl_ref, nw_ref, o_ref, st_ref, *, chunk, heads, dk, layer):
    c = chunk

    @pl.when(pl.program_id(1) == 0)
    def _():
        st_ref[...] = jnp.zeros_like(st_ref)

    logits = lbl_ref[...]
    e = jnp.exp(logits - jnp.max(logits, axis=0, keepdims=True))
    sm = e / jnp.sum(e, axis=0, keepdims=True)
    lb = jnp.sum(sm[0:layer + 1, :], axis=0, keepdims=True) - sm[0:1, :]

    fr = f_ref[0]
    lf = jnp.log(lb + (1.0 - lb) * jax.nn.sigmoid(fr))
    kf = (1.0 - lb) * jax.nn.sigmoid(-fr)
    tri_incl = (_iota2((c, c), 0) >= _iota2((c, c), 1))
    b = _mm((tri_incl.astype(BF16),), _pieces(lf, 3))
    b_last = b[c - 1:c, :]
    qd = q_ref[0] * jnp.exp(b)
    kd = kf * jnp.exp(-b)
    kw = kf * jnp.exp(b_last - b)
    w_all = jnp.exp(b_last)
    v = i_ref[0]

    by_head = lambda t: jnp.stack([t[:, h * dk:(h + 1) * dk] for h in range(heads)], axis=0)
    qd_h, v_h = _pieces(by_head(qd), MIX_PIECES), _pieces(by_head(v), MIX_PIECES)
    sc = jnp.where(tri_incl[None], _mm(qd_h, by_head(kd), BNT), 0.0)
    s = st_ref[...]
    o_h = _mm(sc, v_h, BNN) + _mm(qd_h, s, BNT)
    st_ref[...] = s * by_head(w_all) + _mm(v_h, by_head(kw), BTN)
    o_h = o_h * lax.rsqrt(jnp.mean(o_h * o_h, axis=-1, keepdims=True) + EPS)
    o = jnp.concatenate([o_h[h] for h in range(heads)], axis=1)
    o_ref[0] = o * nw_ref[...] * jax.nn.sigmoid(g_ref[0])


def _hgrn(p, lb_logits, norm_w, *, layer, chunk=HG_CHUNK):
    b, s, cols = p.shape
    heads, dk = HG_HEADS, HG_DK
    dim = heads * dk
    assert cols == 4 * dim
    chunk = min(chunk, s)
    col_block = lambda n: pl.BlockSpec((1, chunk, dim), lambda i, j, n=n: (i, j, n))
    return pl.pallas_call(
        functools.partial(_hgrn_kernel, chunk=chunk, heads=heads, dk=dk, layer=layer),
        grid=(b, s // chunk),
        in_specs=[col_block(0), col_block(1), col_block(2), col_block(3),
                  pl.BlockSpec(lb_logits.shape, lambda i, j: (0, 0)),
                  pl.BlockSpec((1, dim), lambda i, j: (0, 0))],
        out_specs=pl.BlockSpec((1, chunk, dim), lambda i, j: (i, j, 0)),
        out_shape=jax.ShapeDtypeStruct((b, s, dim), F32),
        scratch_shapes=[pltpu.VMEM((heads, dk, dk), F32)],
        compiler_params=_params("parallel", "arbitrary"),
        name="hgrn",
    )(p, p, p, p, lb_logits, norm_w.reshape(1, dim))


def kernel(x, norm_g, ffn1_wg, ffn1_wu, ffn1_wd, ffn2_wg, ffn2_wu, ffn2_wd, ev_w_in, ev_w_out, rw_mu, rw_w0, rw_w2, rw_a0, rw_a2, rw_g2, rw_k_k, rw_k_a, rw_r_k, rw_lnx_w, rw_lnx_b, od_w_in, od_w_out, hg_norm_w, hg_lb_logits, final_g):
    bsz, seq, d = x.shape
    depth = norm_g.shape[0]
    rwkv_dim = RWKV_HEADS * RWKV_HD
    rwkv_cols = 3 * rwkv_dim + LORA_W + LORA_A + LORA_G
    moba_dim = MOBA_HEADS * MOBA_HD
    xf = x.reshape(bsz * seq, d)
    for l in range(depth):
        xf = _ffn(xf, norm_g[l, 0], ffn1_wg[l], ffn1_wu[l], ffn1_wd[l], final_g, final_norm=False)
        if l % 2 == 0:
            e = l // 2
            w_in = ev_w_in[e]
            splits = [0, rwkv_cols, rwkv_cols + moba_dim, rwkv_cols + 2 * moba_dim, rwkv_cols + 3 * moba_dim]
            p_r, q, k, v = _inproj(xf, norm_g[l, 1], [w_in[:, a:b] for a, b in zip(splits[:-1], splits[1:])])
            y_a = _rwkv(p_r.reshape(bsz, seq, rwkv_cols), rw_mu[e], rw_w0[e], rw_w2[e], rw_a0[e], rw_a2[e],
                        rw_g2[e], rw_k_k[e], rw_k_a[e], rw_r_k[e], rw_lnx_w[e], rw_lnx_b[e])
            y_b = _moba(q.reshape(bsz, seq, moba_dim), k.reshape(bsz, seq, moba_dim),
                        v.reshape(bsz, seq, moba_dim))
            w_out = ev_w_out[e]
            xf = _outproj(xf, [y_a.reshape(-1, rwkv_dim), y_b.reshape(-1, moba_dim)],
                          [w_out[:rwkv_dim], w_out[rwkv_dim:]])
        else:
            o = l // 2
            (p,) = _inproj(xf, norm_g[l, 1], [od_w_in[o]])
            y = _hgrn(p.reshape(bsz, seq, -1), hg_lb_logits, hg_norm_w[o], layer=l)
            xf = _outproj(xf, [y.reshape(bsz * seq, -1)], [od_w_out[o]])
        xf = _ffn(xf, norm_g[l, 2], ffn2_wg[l], ffn2_wu[l], ffn2_wd[l], final_g, final_norm=(l == depth - 1))
    return xf.reshape(bsz, seq, d)
```

```python
import functools

import jax
import jax.numpy as jnp
from jax import lax
from jax.experimental import pallas as pl
from jax.experimental.pallas import tpu as pltpu

F32 = jnp.float32
BF16 = jnp.bfloat16
NN = (((1,), (0,)), ((), ()))
NT = (((1,), (1,)), ((), ()))
TN = (((0,), (0,)), ((), ()))
BNN = (((2,), (1,)), ((0,), (0,)))
BNT = (((2,), (2,)), ((0,), (0,)))
BTN = (((1,), (1,)), ((0,), (0,)))
MIX_PIECES = 2

EPS = 1e-6
LNX_EPS = 64e-5
RWKV_HEADS = 8
RWKV_HD = 64
LORA_W = 64
LORA_A = 64
LORA_G = 128
MOBA_HEADS = 8
MOBA_HD = 64
MOBA_BLOCK = 256
MOBA_TOPK = 3
HG_HEADS = 8
HG_DK = 128
RWKV_CHUNK = 64
HG_CHUNK = 64
VMEM_LIMIT_BYTES = 56 * 1024 * 1024


def _params(*semantics):
    return pltpu.CompilerParams(dimension_semantics=semantics, vmem_limit_bytes=VMEM_LIMIT_BYTES)


def _dot(a, b):
    return jnp.dot(a, b, preferred_element_type=F32)


def _dot_nt(a, b):
    return lax.dot_general(a, b, NT, preferred_element_type=F32)


def _pieces(a, n):
    if isinstance(a, tuple):
        return a
    out = []
    for i in range(n):
        hi = a.astype(BF16)
        out.append(hi)
        if i + 1 < n:
            a = a - hi.astype(F32)
    return tuple(out)


def _mm(a, b, dims=NN, n=MIX_PIECES):
    a = _pieces(a, n)
    b = _pieces(b, n)
    order = max(len(a), len(b)) - 1
    out = None
    for i, ai in enumerate(a):
        for j, bj in enumerate(b):
            if i + j <= order:
                t = lax.dot_general(ai, bj, dims, preferred_element_type=F32)
                out = t if out is None else out + t
    return out


def _rmsnorm(x, g):
    return x * lax.rsqrt(jnp.mean(x * x, axis=-1, keepdims=True) + EPS) * g


def _iota2(shape, dim):
    return lax.broadcasted_iota(jnp.int32, shape, dim)


def _block_ones(n, width):
    return (_iota2((n, n), 0) // width == _iota2((n, n), 1) // width).astype(F32)


def _ffn_kernel(x_ref, g_ref, wg_ref, wu_ref, wd_ref, fg_ref, o_ref, h_ref, acc_ref, *, final_norm):
    f = pl.program_id(1)

    @pl.when(f == 0)
    def _():
        h_ref[...] = _rmsnorm(x_ref[...], g_ref[...]).astype(BF16)
        acc_ref[...] = jnp.zeros_like(acc_ref)

    h = h_ref[...]
    gate = _dot(h, wg_ref[...])
    up = _dot(h, wu_ref[...])
    act = (gate * jax.nn.sigmoid(gate) * up).astype(BF16)
    acc_ref[...] += _dot(act, wd_ref[...])

    @pl.when(f == pl.num_programs(1) - 1)
    def _():
        out = x_ref[...] + 0.5 * acc_ref[...]
        if final_norm:
            out = _rmsnorm(out, fg_ref[...])
        o_ref[...] = out


def _ffn(x, g, wg, wu, wd, final_g, *, final_norm, tm=1024, tf=256):
    m, d = x.shape
    f = wg.shape[1]
    tm = min(tm, m)
    return pl.pallas_call(
        functools.partial(_ffn_kernel, final_norm=final_norm),
        grid=(m // tm, f // tf),
        in_specs=[
            pl.BlockSpec((tm, d), lambda i, j: (i, 0)),
            pl.BlockSpec((1, d), lambda i, j: (0, 0)),
            pl.BlockSpec((d, tf), lambda i, j: (0, j)),
            pl.BlockSpec((d, tf), lambda i, j: (0, j)),
            pl.BlockSpec((tf, d), lambda i, j: (j, 0)),
            pl.BlockSpec((1, d), lambda i, j: (0, 0)),
        ],
        out_specs=pl.BlockSpec((tm, d), lambda i, j: (i, 0)),
        out_shape=jax.ShapeDtypeStruct((m, d), F32),
        scratch_shapes=[pltpu.VMEM((tm, d), BF16), pltpu.VMEM((tm, d), F32)],
        compiler_params=_params("parallel", "arbitrary"),
        name="ffn",
    )(x, g.reshape(1, d), wg.astype(BF16), wu.astype(BF16), wd.astype(BF16), final_g.reshape(1, d))


def _inproj_kernel(x_ref, g_ref, *refs, n_w, tn):
    w_refs, o_refs = refs[:n_w], refs[n_w:]
    h = _rmsnorm(x_ref[...], g_ref[...]).astype(BF16)
    for w_ref, o_ref in zip(w_refs, o_refs):
        n = w_ref.shape[1]
        for c0 in range(0, n, tn):
            c1 = min(c0 + tn, n)
            o_ref[:, c0:c1] = _dot(h, w_ref[:, c0:c1])


def _inproj(x, g, ws, *, tm=256, tn=512):
    m, d = x.shape
    tm = min(tm, m)
    return pl.pallas_call(
        functools.partial(_inproj_kernel, n_w=len(ws), tn=tn),
        grid=(m // tm,),
        in_specs=[pl.BlockSpec((tm, d), lambda i: (i, 0)), pl.BlockSpec((1, d), lambda i: (0, 0))]
        + [pl.BlockSpec(w.shape, lambda i: (0, 0)) for w in ws],
        out_specs=[pl.BlockSpec((tm, w.shape[1]), lambda i: (i, 0)) for w in ws],
        out_shape=[jax.ShapeDtypeStruct((m, w.shape[1]), F32) for w in ws],
        compiler_params=_params("parallel"),
        name="inproj",
    )(x, g.reshape(1, d), *[w.astype(BF16) for w in ws])


def _outproj_kernel(x_ref, *refs, n_y):
    y_refs, w_refs, o_ref = refs[:n_y], refs[n_y:2 * n_y], refs[2 * n_y]
    acc = x_ref[...]
    for y_ref, w_ref in zip(y_refs, w_refs):
        acc = acc + _dot(y_ref[...].astype(BF16), w_ref[...])
    o_ref[...] = acc


def _outproj(x, ys, ws, *, tm=512):
    m, d = x.shape
    tm = min(tm, m)
    return pl.pallas_call(
        functools.partial(_outproj_kernel, n_y=len(ys)),
        grid=(m // tm,),
        in_specs=[pl.BlockSpec((tm, d), lambda i: (i, 0))]
        + [pl.BlockSpec((tm, y.shape[1]), lambda i: (i, 0)) for y in ys]
        + [pl.BlockSpec(w.shape, lambda i: (0, 0)) for w in ws],
        out_specs=pl.BlockSpec((tm, d), lambda i: (i, 0)),
        out_shape=jax.ShapeDtypeStruct((m, d), F32),
        compiler_params=_params("parallel"),
        name="outproj",
    )(x, *ys, *[w.astype(BF16) for w in ws])


def _unit_lower_inverse(a_strict, c):
    row = _iota2((1, c, c), 1)
    col = _iota2((1, c, c), 2)
    eye = (row == col).astype(F32)
    t = None
    m = 1
    while m < c:
        mask = ((row // (2 * m)) == (col // (2 * m))) & ((row & m) != 0) & ((col & m) == 0)
        lm = jnp.where(mask, a_strict, 0.0)
        if t is None:
            t = eye - lm
        else:
            tp = _pieces(t, MIX_PIECES)
            t = t - _mm(_mm(tp, lm, BNN), tp, BNN)
        m *= 2
    return t


def _rwkv_kernel(p_ref, mu_ref, w0_ref, w2_ref, a0_ref, a2_ref, g2_ref, kk_ref, ka_ref, rk_ref,
                 lnw_ref, lnb_ref, o_ref, carry_ref, st_ref, *, chunk, heads, hd):
    c = chunk
    dim = heads * hd

    @pl.when(pl.program_id(1) == 0)
    def _():
        carry_ref[...] = jnp.zeros_like(carry_ref)
        st_ref[...] = jnp.zeros_like(st_ref)

    p = p_ref[0]
    row = _iota2((c, 1), 0)
    prev = jnp.where(row == 0, carry_ref[...], pltpu.roll(p, 1, axis=0))
    carry_ref[...] = p[c - 1:c, :]
    xs = p + (prev - p) * mu_ref[...]

    r = xs[:, 0:dim]
    k = xs[:, dim:2 * dim]
    v = xs[:, 2 * dim:3 * dim]
    o1 = 3 * dim
    w_lr = xs[:, o1:o1 + LORA_W]
    a_lr = xs[:, o1 + LORA_W:o1 + LORA_W + LORA_A]
    g_lr = xs[:, o1 + LORA_W + LORA_A:o1 + LORA_W + LORA_A + LORA_G]

    z = w0_ref[...] + _mm(jnp.tanh(w_lr), w2_ref[...])
    softplus = jnp.maximum(-z, 0.0) + jnp.log1p(jnp.exp(-jnp.abs(z)))
    w_raw = -softplus - 0.5
    lw = -jnp.exp(w_raw)
    a = jax.nn.sigmoid(a0_ref[...] + _mm(a_lr, a2_ref[...]))
    g = _mm(jax.nn.sigmoid(g_lr), g2_ref[...])

    head_sum = (_block_ones(dim, hd).astype(BF16),)
    kk = k * kk_ref[...]
    kk = kk * lax.rsqrt(jnp.maximum(_mm(kk * kk, head_sum), 1e-24))
    k2 = k * (1.0 + (a - 1.0) * ka_ref[...])
    bb = kk * a

    tri_incl = (_iota2((c, c), 0) >= _iota2((c, c), 1))
    tri_strict = (_iota2((c, c), 0) > _iota2((c, c), 1))
    cum = _mm((tri_incl.astype(BF16),), _pieces(lw, 3))
    cum_last = cum[c - 1:c, :]
    w_incl = jnp.exp(cum)
    w_excl = jnp.exp(cum - lw)
    w_inv = jnp.exp(-cum)
    w_tail = jnp.exp(cum_last - cum)
    w_all = jnp.exp(cum_last)

    kt = kk * w_excl
    rt = r * w_incl
    bt = bb * w_inv
    kd = k2 * w_inv
    bw = bb * w_tail
    kw = k2 * w_tail

    by_head = lambda t: jnp.stack([t[:, h * hd:(h + 1) * hd] for h in range(heads)], axis=0)
    split = lambda t: _pieces(t, MIX_PIECES)
    kt_h, rt_h, bt_h, kd_h, kw_h = (by_head(t) for t in (kt, rt, bt, kd, kw))
    v_h, bw_h = split(by_head(v)), split(by_head(bw))
    w_all_h = by_head(w_all)
    kr = split(jnp.concatenate([kt_h, rt_h], axis=1))
    gb = _mm(kr, bt_h, BNT)
    gk = _mm(kr, kd_h, BNT)
    a_b = jnp.where(tri_strict[None], gb[:, :c], 0.0)
    a_k = jnp.where(tri_strict[None], gk[:, :c], 0.0)
    p_b = split(jnp.where(tri_incl[None], gb[:, c:], 0.0))
    p_k = jnp.where(tri_incl[None], gk[:, c:], 0.0)
    t_inv = split(_unit_lower_inverse(a_b, c))
    kt1 = split(_mm(t_inv, kt_h, BNN))
    v1 = split(_mm(t_inv, _mm(a_k, v_h, BNN), BNN))
    q_mat = rt_h - _mm(p_b, kt1, BNN)
    z_mat = _mm(p_k, v_h, BNN) - _mm(p_b, v1, BNN)
    x_mat = _mm(bw_h, kt1, BTN)
    n_mat = _mm(v_h, kw_h, BTN) - _mm(v1, bw_h, BTN)
    s = st_ref[...]
    sp = split(s)
    y_h = _mm(q_mat, sp, BNT) + z_mat
    st_ref[...] = s * w_all_h - _mm(sp, x_mat, BNT) + n_mat
    y = jnp.concatenate([y_h[h] for h in range(heads)], axis=1)

    inv_hd = 1.0 / hd
    mean = _mm(y, head_sum) * inv_hd
    yc = y - mean
    var = _mm(yc * yc, head_sum) * inv_hd
    yn = yc * lax.rsqrt(var + LNX_EPS) * lnw_ref[...] + lnb_ref[...]
    bonus = _mm(r * k2 * rk_ref[...], head_sum) * v
    o_ref[0] = (yn + bonus) * g


def _rwkv(p, mu, w0, w2, a0, a2, g2, k_k, k_a, r_k, lnx_w, lnx_b, *, chunk=RWKV_CHUNK):
    b, s, cols = p.shape
    heads, hd = RWKV_HEADS, RWKV_HD
    dim = heads * hd
    chunk = min(chunk, s)
    row = lambda t: t.reshape(1, -1)
    vecs = [row(mu), row(w0), w2, row(a0), a2, g2, row(k_k), row(k_a), row(r_k), row(lnx_w), row(lnx_b)]
    return pl.pallas_call(
        functools.partial(_rwkv_kernel, chunk=chunk, heads=heads, hd=hd),
        grid=(b, s // chunk),
        in_specs=[pl.BlockSpec((1, chunk, cols), lambda i, j: (i, j, 0))]
        + [pl.BlockSpec(t.shape, lambda i, j: (0, 0)) for t in vecs],
        out_specs=pl.BlockSpec((1, chunk, dim), lambda i, j: (i, j, 0)),
        out_shape=jax.ShapeDtypeStruct((b, s, dim), F32),
        scratch_shapes=[pltpu.VMEM((1, cols), F32), pltpu.VMEM((heads, hd, hd), F32)],
        compiler_params=_params("parallel", "arbitrary"),
        name="rwkv",
    )(p, *vecs)


def _moba_kernel(q_ref, k_ref, v_ref, o_ref, km_ref, ka_ref, vat_ref, *, nb, blk, n_sel, heads, hd):
    j = pl.program_id(1)
    s_len = nb * blk
    nbp = km_ref.shape[0]
    scale = hd ** -0.5
    masked = -1e30
    neg_inf = float("-inf")
    slopes = [2.0 ** (-8.0 * (h + 1) / heads) for h in range(heads)]

    @pl.when(j == 0)
    def _():
        km_ref[...] = jnp.zeros_like(km_ref)
        for n in range(nb):
            km_ref[n:n + 1, :] = jnp.mean(k_ref[0, n * blk:(n + 1) * blk, :], axis=0, keepdims=True)
        row = _iota2((s_len, hd), 0)
        lane = _iota2((s_len, hd), 1)
        blk_id = row // blk
        col = (row - blk_id * blk).astype(F32)
        one_hot = jnp.where(lane == blk_id, 1.0, 0.0)
        for h in range(heads):
            sl = slice(h * hd, (h + 1) * hd)
            k_feat = (one_hot + jnp.where(lane == nbp, slopes[h] * col, 0.0)
                      + jnp.where(lane == nbp + 1, (slopes[h] * blk) * blk_id.astype(F32), 0.0))
            ka_ref[h] = jnp.concatenate([k_ref[0, :, sl], k_feat], axis=1).astype(BF16)
        ones_row = jnp.where(_iota2((hd, blk), 0) == 0, 1.0, 0.0)
        for n in range(nb):
            vt = v_ref[0, n * blk:(n + 1) * blk, :].T
            for h in range(heads):
                vat_ref[n, h] = jnp.concatenate([vt[h * hd:(h + 1) * hd], ones_row], axis=0).astype(BF16)

    qt = (q_ref[0] * scale).T
    sub = _iota2((nbp, blk), 0)
    const_rows = jnp.where(_iota2((hd - nbp, blk), 0) < 2, 1.0, 0.0)
    q_aug = []
    for h in range(heads):
        sl = slice(h * hd, (h + 1) * hd)
        qh = qt[sl]
        gate = _mm(_pieces(km_ref[:, sl], 3), _pieces(qh, 3))
        gate = jnp.where(sub < j, gate, neg_inf)
        cnt = jnp.zeros((nbp, blk), jnp.int32)
        for m in range(nb):
            other = gate[m:m + 1, :]
            beats = (other > gate) | ((other == gate) & (m < sub))
            cnt = cnt + beats.astype(jnp.int32)
        keep = ((sub < j) & (cnt < n_sel)) | (sub == j) | (sub >= nb)
        bias = jnp.where(keep, 0.0, masked)
        q_aug.append(jnp.concatenate([qh, bias, const_rows], axis=0).astype(BF16))
    q_aug = jnp.stack(q_aug, axis=0)

    def scores(n):
        start = pl.multiple_of(n * blk, blk)
        kn = ka_ref[:, pl.ds(start, blk), :]
        return lax.dot_general(kn, q_aug, BNN, preferred_element_type=F32), vat_ref[n]

    s, vj = scores(j)
    causal = _iota2((1, blk, blk), 1) <= _iota2((1, blk, blk), 2)
    s = jnp.where(causal, s, masked)
    m0 = jnp.max(s, axis=1, keepdims=True)
    acc0 = lax.dot_general(vj, jnp.exp(s - m0).astype(BF16), BNN, preferred_element_type=F32)

    def past_block(n, carry):
        m_run, acc = carry
        sc, vn = scores(n)
        m_new = jnp.maximum(m_run, jnp.max(sc, axis=1, keepdims=True))
        pr = jnp.exp(sc - m_new).astype(BF16)
        acc_new = jnp.exp(m_run - m_new) * acc + lax.dot_general(vn, pr, BNN, preferred_element_type=F32)
        return m_new, acc_new

    _, acc = lax.fori_loop(0, j, past_block, (m0, acc0))
    out_t = jnp.concatenate([acc[h, :hd] / acc[h, hd:hd + 1] for h in range(heads)], axis=0)
    o_ref[0] = out_t.T


def _moba(q, k, v):
    b, s, dim = q.shape
    heads, hd, blk = MOBA_HEADS, MOBA_HD, MOBA_BLOCK
    assert s % blk == 0
    nb = s // blk
    nbp = -(-nb // 8) * 8
    assert nbp + 2 <= hd
    n_sel = min(MOBA_TOPK, nb - 1)
    return pl.pallas_call(
        functools.partial(_moba_kernel, nb=nb, blk=blk, n_sel=n_sel, heads=heads, hd=hd),
        grid=(b, nb),
        in_specs=[
            pl.BlockSpec((1, blk, dim), lambda i, j: (i, j, 0)),
            pl.BlockSpec((1, s, dim), lambda i, j: (i, 0, 0)),
            pl.BlockSpec((1, s, dim), lambda i, j: (i, 0, 0)),
        ],
        out_specs=pl.BlockSpec((1, blk, dim), lambda i, j: (i, j, 0)),
        out_shape=jax.ShapeDtypeStruct((b, s, dim), F32),
        scratch_shapes=[pltpu.VMEM((nbp, dim), F32),
                        pltpu.VMEM((heads, s, 2 * hd), BF16),
                        pltpu.VMEM((nb, heads, 2 * hd, blk), BF16)],
        compiler_params=_params("parallel", "arbitrary"),
        name="moba",
    )(q, k, v)


def _hgrn_kernel(q_ref, f_ref, i_ref, g_ref, lbl_ref, nw_ref, o_ref, st_ref, *, chunk, heads, dk, layer):
    c = chunk

    @pl.when(pl.program_id(1) == 0)
    def _():
        st_ref[...] = jnp.zeros_like(st_ref)

    logits = lbl_ref[...]
    e = jnp.exp(logits - jnp.max(logits, axis=0, keepdims=True))
    sm = e / jnp.sum(e, axis=0, keepdims=True)
    lb = jnp.sum(sm[0:layer + 1, :], axis=0, keepdims=True) - sm[0:1, :]

    fr = f_ref[0]
    lf = jnp.log(lb + (1.0 - lb) * jax.nn.sigmoid(fr))
    kf = (1.0 - lb) * jax.nn.sigmoid(-fr)
    tri_incl = (_iota2((c, c), 0) >= _iota2((c, c), 1))
    b = _mm((tri_incl.astype(BF16),), _pieces(lf, 3))
    b_last = b[c - 1:c, :]
    qd = q_ref[0] * jnp.exp(b)
    kd = kf * jnp.exp(-b)
    kw = kf * jnp.exp(b_last - b)
    w_all = jnp.exp(b_last)
    v = i_ref[0]

    by_head = lambda t: jnp.stack([t[:, h * dk:(h + 1) * dk] for h in range(heads)], axis=0)
    qd_h, v_h = _pieces(by_head(qd), MIX_PIECES), _pieces(by_head(v), MIX_PIECES)
    sc = jnp.where(tri_incl[None], _mm(qd_h, by_head(kd), BNT), 0.0)
    s = st_ref[...]
    o_h = _mm(sc, v_h, BNN) + _mm(qd_h, s, BNT)
    st_ref[...] = s * by_head(w_all) + _mm(v_h, by_head(kw), BTN)
    o_h = o_h * lax.rsqrt(jnp.mean(o_h * o_h, axis=-1, keepdims=True) + EPS)
    o = jnp.concatenate([o_h[h] for h in range(heads)], axis=1)
    o_ref[0] = o * nw_ref[...] * jax.nn.sigmoid(g_ref[0])


def _hgrn(p, lb_logits, norm_w, *, layer, chunk=HG_CHUNK):
    b, s, cols = p.shape
    heads, dk = HG_HEADS, HG_DK
    dim = heads * dk
    assert cols == 4 * dim
    chunk = min(chunk, s)
    col_block = lambda n: pl.BlockSpec((1, chunk, dim), lambda i, j, n=n: (i, j, n))
    return pl.pallas_call(
        functools.partial(_hgrn_kernel, chunk=chunk, heads=heads, dk=dk, layer=layer),
        grid=(b, s // chunk),
        in_specs=[col_block(0), col_block(1), col_block(2), col_block(3),
                  pl.BlockSpec(lb_logits.shape, lambda i, j: (0, 0)),
                  pl.BlockSpec((1, dim), lambda i, j: (0, 0))],
        out_specs=pl.BlockSpec((1, chunk, dim), lambda i, j: (i, j, 0)),
        out_shape=jax.ShapeDtypeStruct((b, s, dim), F32),
        scratch_shapes=[pltpu.VMEM((heads, dk, dk), F32)],
        compiler_params=_params("parallel", "arbitrary"),
        name="hgrn",
    )(p, p, p, p, lb_logits, norm_w.reshape(1, dim))


def kernel(x, norm_g, ffn1_wg, ffn1_wu, ffn1_wd, ffn2_wg, ffn2_wu, ffn2_wd, ev_w_in, ev_w_out, rw_mu, rw_w0, rw_w2, rw_a0, rw_a2, rw_g2, rw_k_k, rw_k_a, rw_r_k, rw_lnx_w, rw_lnx_b, od_w_in, od_w_out, hg_norm_w, hg_lb_logits, final_g):
    bsz, seq, d = x.shape
    depth = norm_g.shape[0]
    rwkv_dim = RWKV_HEADS * RWKV_HD
    rwkv_cols = 3 * rwkv_dim + LORA_W + LORA_A + LORA_G
    moba_dim = MOBA_HEADS * MOBA_HD
    xf = x.reshape(bsz * seq, d)
    for l in range(depth):
        xf = _ffn(xf, norm_g[l, 0], ffn1_wg[l], ffn1_wu[l], ffn1_wd[l], final_g, final_norm=False)
        if l % 2 == 0:
            e = l // 2
            w_in = ev_w_in[e]
            splits = [0, rwkv_cols, rwkv_cols + moba_dim, rwkv_cols + 2 * moba_dim, rwkv_cols + 3 * moba_dim]
            p_r, q, k, v = _inproj(xf, norm_g[l, 1], [w_in[:, a:b] for a, b in zip(splits[:-1], splits[1:])])
            y_a = _rwkv(p_r.reshape(bsz, seq, rwkv_cols), rw_mu[e], rw_w0[e], rw_w2[e], rw_a0[e], rw_a2[e],
                        rw_g2[e], rw_k_k[e], rw_k_a[e], rw_r_k[e], rw_lnx_w[e], rw_lnx_b[e])
            y_b = _moba(q.reshape(bsz, seq, moba_dim), k.reshape(bsz, seq, moba_dim),
                        v.reshape(bsz, seq, moba_dim))
            w_out = ev_w_out[e]
            xf = _outproj(xf, [y_a.reshape(-1, rwkv_dim), y_b.reshape(-1, moba_dim)],
                          [w_out[:rwkv_dim], w_out[rwkv_dim:]])
        else:
            o = l // 2
            (p,) = _inproj(xf, norm_g[l, 1], [od_w_in[o]])
            y = _hgrn(p.reshape(bsz, seq, -1), hg_lb_logits, hg_norm_w[o], layer=l)
            xf = _outproj(xf, [y.reshape(bsz * seq, -1)], [od_w_out[o]])
        xf = _ffn(xf, norm_g[l, 2], ffn2_wg[l], ffn2_wu[l], ffn2_wd[l], final_g, final_norm=(l == depth - 1))
    return xf.reshape(bsz, seq, d)
```

```python
import functools

import jax
import jax.numpy as jnp
from jax import lax
from jax.experimental import pallas as pl
from jax.experimental.pallas import tpu as pltpu

F32 = jnp.float32
BF16 = jnp.bfloat16
NN = (((1,), (0,)), ((), ()))
NT = (((1,), (1,)), ((), ()))
TN = (((0,), (0,)), ((), ()))
BNN = (((2,), (1,)), ((0,), (0,)))
BNT = (((2,), (2,)), ((0,), (0,)))
BTN = (((1,), (1,)), ((0,), (0,)))
MIX_PIECES = 2

EPS = 1e-6
LNX_EPS = 64e-5
RWKV_HEADS = 8
RWKV_HD = 64
LORA_W = 64
LORA_A = 64
LORA_G = 128
MOBA_HEADS = 8
MOBA_HD = 64
MOBA_BLOCK = 256
MOBA_TOPK = 3
HG_HEADS = 8
HG_DK = 128
RWKV_CHUNK = 64
RWKV_CHUNKS_PER_STEP = 4
HG_CHUNK = 64
HG_CHUNKS_PER_STEP = 4
VMEM_LIMIT_BYTES = 56 * 1024 * 1024


def _params(*semantics):
    return pltpu.CompilerParams(dimension_semantics=semantics, vmem_limit_bytes=VMEM_LIMIT_BYTES)


def _dot(a, b):
    return jnp.dot(a, b, preferred_element_type=F32)


def _dot_nt(a, b):
    return lax.dot_general(a, b, NT, preferred_element_type=F32)


def _pieces(a, n):
    if isinstance(a, tuple):
        return a
    out = []
    for i in range(n):
        hi = a.astype(BF16)
        out.append(hi)
        if i + 1 < n:
            a = a - hi.astype(F32)
    return tuple(out)


def _mm(a, b, dims=NN, n=MIX_PIECES):
    a = _pieces(a, n)
    b = _pieces(b, n)
    order = max(len(a), len(b)) - 1
    out = None
    for i, ai in enumerate(a):
        for j, bj in enumerate(b):
            if i + j <= order:
                t = lax.dot_general(ai, bj, dims, preferred_element_type=F32)
                out = t if out is None else out + t
    return out


def _rmsnorm(x, g):
    return x * lax.rsqrt(jnp.mean(x * x, axis=-1, keepdims=True) + EPS) * g


def _iota2(shape, dim):
    return lax.broadcasted_iota(jnp.int32, shape, dim)


def _block_ones(n, width):
    return (_iota2((n, n), 0) // width == _iota2((n, n), 1) // width).astype(F32)


def _ffn_kernel(x_ref, g_ref, wg_ref, wu_ref, wd_ref, fg_ref, o_ref, h_ref, acc_ref, *, final_norm):
    f = pl.program_id(1)

    @pl.when(f == 0)
    def _():
        h_ref[...] = _rmsnorm(x_ref[...], g_ref[...]).astype(BF16)
        acc_ref[...] = jnp.zeros_like(acc_ref)

    h = h_ref[...]
    gate = _dot(h, wg_ref[...])
    up = _dot(h, wu_ref[...])
    act = (gate * jax.nn.sigmoid(gate) * up).astype(BF16)
    acc_ref[...] += _dot(act, wd_ref[...])

    @pl.when(f == pl.num_programs(1) - 1)
    def _():
        out = x_ref[...] + 0.5 * acc_ref[...]
        if final_norm:
            out = _rmsnorm(out, fg_ref[...])
        o_ref[...] = out


def _ffn(x, g, wg, wu, wd, final_g, *, final_norm, tm=1024, tf=256):
    m, d = x.shape
    f = wg.shape[1]
    tm = min(tm, m)
    return pl.pallas_call(
        functools.partial(_ffn_kernel, final_norm=final_norm),
        grid=(m // tm, f // tf),
        in_specs=[
            pl.BlockSpec((tm, d), lambda i, j: (i, 0)),
            pl.BlockSpec((1, d), lambda i, j: (0, 0)),
            pl.BlockSpec((d, tf), lambda i, j: (0, j)),
            pl.BlockSpec((d, tf), lambda i, j: (0, j)),
            pl.BlockSpec((tf, d), lambda i, j: (j, 0)),
            pl.BlockSpec((1, d), lambda i, j: (0, 0)),
        ],
        out_specs=pl.BlockSpec((tm, d), lambda i, j: (i, 0)),
        out_shape=jax.ShapeDtypeStruct((m, d), F32),
        scratch_shapes=[pltpu.VMEM((tm, d), BF16), pltpu.VMEM((tm, d), F32)],
        compiler_params=_params("parallel", "arbitrary"),
        name="ffn",
    )(x, g.reshape(1, d), wg.astype(BF16), wu.astype(BF16), wd.astype(BF16), final_g.reshape(1, d))


def _inproj_kernel(x_ref, g_ref, *refs, n_w, tn):
    w_refs, o_refs = refs[:n_w], refs[n_w:]
    h = _rmsnorm(x_ref[...], g_ref[...]).astype(BF16)
    for w_ref, o_ref in zip(w_refs, o_refs):
        n = w_ref.shape[1]
        for c0 in range(0, n, tn):
            c1 = min(c0 + tn, n)
            o_ref[:, c0:c1] = _dot(h, w_ref[:, c0:c1])


def _inproj(x, g, ws, *, tm=256, tn=512):
    m, d = x.shape
    tm = min(tm, m)
    return pl.pallas_call(
        functools.partial(_inproj_kernel, n_w=len(ws), tn=tn),
        grid=(m // tm,),
        in_specs=[pl.BlockSpec((tm, d), lambda i: (i, 0)), pl.BlockSpec((1, d), lambda i: (0, 0))]
        + [pl.BlockSpec(w.shape, lambda i: (0, 0)) for w in ws],
        out_specs=[pl.BlockSpec((tm, w.shape[1]), lambda i: (i, 0)) for w in ws],
        out_shape=[jax.ShapeDtypeStruct((m, w.shape[1]), F32) for w in ws],
        compiler_params=_params("parallel"),
        name="inproj",
    )(x, g.reshape(1, d), *[w.astype(BF16) for w in ws])


def _outproj_kernel(x_ref, *refs, n_y):
    y_refs, w_refs, o_ref = refs[:n_y], refs[n_y:2 * n_y], refs[2 * n_y]
    acc = x_ref[...]
    for y_ref, w_ref in zip(y_refs, w_refs):
        acc = acc + _dot(y_ref[...].astype(BF16), w_ref[...])
    o_ref[...] = acc


def _outproj(x, ys, ws, *, tm=512):
    m, d = x.shape
    tm = min(tm, m)
    return pl.pallas_call(
        functools.partial(_outproj_kernel, n_y=len(ys)),
        grid=(m // tm,),
        in_specs=[pl.BlockSpec((tm, d), lambda i: (i, 0))]
        + [pl.BlockSpec((tm, y.shape[1]), lambda i: (i, 0)) for y in ys]
        + [pl.BlockSpec(w.shape, lambda i: (0, 0)) for w in ws],
        out_specs=pl.BlockSpec((tm, d), lambda i: (i, 0)),
        out_shape=jax.ShapeDtypeStruct((m, d), F32),
        compiler_params=_params("parallel"),
        name="outproj",
    )(x, *ys, *[w.astype(BF16) for w in ws])


def _bdot(a, b, dims):
    return lax.dot_general(a.astype(BF16), b.astype(BF16), dims, preferred_element_type=F32)


def _block_diag(x, half):
    lo = _iota2((1, 1, 2 * half), 2) < half
    return jnp.concatenate([jnp.where(lo, x, 0.0), jnp.where(lo, 0.0, x)], axis=1)


def _unit_lower_inverse(a_strict, c):
    row = _iota2((1, c, 2 * c), 1)
    col = _iota2((1, c, 2 * c), 2) & (c - 1)
    eye = (row == col).astype(F32)
    t = None
    m = 1
    while m < c:
        mask = ((row // (2 * m)) == (col // (2 * m))) & ((row & m) != 0) & ((col & m) == 0)
        lm = jnp.where(mask, a_strict, 0.0)
        if t is None:
            t = eye - lm
        else:
            t = t - _bdot(_bdot(t, _block_diag(lm, c), BNN), _block_diag(t, c), BNN)
        m *= 2
    return t


def _rwkv_kernel(p_ref, mu_ref, w0_ref, w2_ref, a0_ref, a2_ref, g2_ref, kk_ref, ka_ref, rk_ref,
                 lnw_ref, lnb_ref, o_ref, carry_ref, st_ref, *, chunk, n_chunks, heads, hd):
    c = chunk
    rows = n_chunks * c
    dim = heads * hd

    @pl.when(pl.program_id(1) == 0)
    def _():
        carry_ref[...] = jnp.zeros_like(carry_ref)
        st_ref[...] = jnp.zeros_like(st_ref)

    p = p_ref[0]
    row = _iota2((rows, 1), 0)
    prev = jnp.where(row == 0, carry_ref[...], pltpu.roll(p, 1, axis=0))
    carry_ref[...] = p[rows - 1:rows, :]
    xs = p + (prev - p) * mu_ref[...]

    r = xs[:, 0:dim]
    k = xs[:, dim:2 * dim]
    v = xs[:, 2 * dim:3 * dim]
    o1 = 3 * dim
    w_lr = xs[:, o1:o1 + LORA_W]
    a_lr = xs[:, o1 + LORA_W:o1 + LORA_W + LORA_A]
    g_lr = xs[:, o1 + LORA_W + LORA_A:o1 + LORA_W + LORA_A + LORA_G]

    z = w0_ref[...] + _mm(jnp.tanh(w_lr), w2_ref[...])
    softplus = jnp.maximum(-z, 0.0) + jnp.log1p(jnp.exp(-jnp.abs(z)))
    w_raw = -softplus - 0.5
    lw = -jnp.exp(w_raw)
    a = jax.nn.sigmoid(a0_ref[...] + _mm(a_lr, a2_ref[...]))
    g = _mm(jax.nn.sigmoid(g_lr), g2_ref[...])

    head_sum = (_block_ones(dim, hd).astype(BF16),)
    kk = k * kk_ref[...]
    kk = kk * lax.rsqrt(jnp.maximum(_mm(kk * kk, head_sum), 1e-24))
    k2 = k * (1.0 + (a - 1.0) * ka_ref[...])
    bb = kk * a

    r_i, c_i = _iota2((rows, rows), 0), _iota2((rows, rows), 1)
    tri_incl = (r_i >= c_i) & (r_i // c == c_i // c)
    cum = _mm((tri_incl.astype(BF16),), _pieces(lw, 3))
    ends = [cum[(i + 1) * c - 1:(i + 1) * c, :] for i in range(n_chunks)]
    cum_last = jnp.concatenate([jnp.broadcast_to(e, (c, dim)) for e in ends], axis=0)
    w_incl = jnp.exp(cum)
    w_excl = jnp.exp(cum - lw)
    w_inv = jnp.exp(-cum)
    w_tail = jnp.exp(cum_last - cum)
    w_all = jnp.exp(jnp.concatenate(ends, axis=0))

    kt = kk * w_excl
    rt = r * w_incl
    bt = bb * w_inv
    kd = k2 * w_inv
    bw = bb * w_tail
    kw = k2 * w_tail

    pw = 2 * hd
    pairs = heads // 2
    def by_pair(t, n_rows=c):
        return jnp.stack([t[i * n_rows:(i + 1) * n_rows, j * pw:(j + 1) * pw]
                          for i in range(n_chunks) for j in range(pairs)], axis=0)
    bd_f = lambda t: _block_diag(t, hd)
    kt_p, rt_p, bt_p, kd_p, kw_p, bw_p, v_p = (by_pair(t) for t in (kt, rt, bt, kd, kw, bw, v))
    w_all_p = by_pair(w_all, 1)
    row_t = _iota2((1, c, 2 * c), 1)
    col_t = _iota2((1, c, 2 * c), 2) & (c - 1)
    strict_p, incl_p = row_t > col_t, row_t >= col_t
    same_head = (_iota2((1, pw, pw), 1) // hd) == (_iota2((1, pw, pw), 2) // hd)

    kr = jnp.concatenate([kt_p, rt_p], axis=1)
    g_all = _bdot(kr, jnp.concatenate([bd_f(bt_p), bd_f(kd_p)], axis=1), BNT)
    gb, gk = g_all[:, :, :2 * c], g_all[:, :, 2 * c:]
    a_b = jnp.where(strict_p, gb[:, :c], 0.0)
    a_k = jnp.where(strict_p, gk[:, :c], 0.0)
    p_b = jnp.where(incl_p, gb[:, c:], 0.0)
    p_k = jnp.where(incl_p, gk[:, c:], 0.0)
    t_inv = _unit_lower_inverse(a_b, c)
    akpk = _bdot(jnp.concatenate([a_k, p_k], axis=1), bd_f(v_p), BNN)
    kv1 = _bdot(t_inv, jnp.concatenate([bd_f(kt_p), bd_f(akpk[:, :c])], axis=2), BNN)
    kt1, v1 = kv1[:, :, :pw], kv1[:, :, pw:]
    pbk = _bdot(p_b, jnp.concatenate([bd_f(kt1), bd_f(v1)], axis=2), BNN)
    q_mat = rt_p - pbk[:, :, :pw]
    z_mat = akpk[:, c:] - pbk[:, :, pw:]
    x_mat = jnp.where(same_head, _bdot(bw_p, kt1, BTN), 0.0)
    n_mat = jnp.where(same_head, _bdot(jnp.concatenate([v_p, v1], axis=1),
                                       jnp.concatenate([kw_p, -bw_p], axis=1), BTN), 0.0)
    s = st_ref[...]
    y_rows = []
    for i in range(n_chunks):
        sl = slice(i * pairs, (i + 1) * pairs)
        y_p = _bdot(q_mat[sl], s, BNT) + z_mat[sl]
        s = s * w_all_p[sl] - _bdot(s, x_mat[sl], BNT) + n_mat[sl]
        y_rows.append(jnp.concatenate([y_p[j] for j in range(pairs)], axis=1))
    st_ref[...] = s
    y = jnp.concatenate(y_rows, axis=0)

    inv_hd = 1.0 / hd
    mean = _mm(y, head_sum) * inv_hd
    yc = y - mean
    var = _mm(yc * yc, head_sum) * inv_hd
    yn = yc * lax.rsqrt(var + LNX_EPS) * lnw_ref[...] + lnb_ref[...]
    bonus = _mm(r * k2 * rk_ref[...], head_sum) * v
    o_ref[0] = (yn + bonus) * g


def _rwkv(p, mu, w0, w2, a0, a2, g2, k_k, k_a, r_k, lnx_w, lnx_b, *, chunk=RWKV_CHUNK, n_chunks=RWKV_CHUNKS_PER_STEP):
    b, s, cols = p.shape
    heads, hd = RWKV_HEADS, RWKV_HD
    dim = heads * hd
    chunk = min(chunk, s)
    n_chunks = min(n_chunks, s // chunk)
    rows = chunk * n_chunks
    row = lambda t: t.reshape(1, -1)
    vecs = [row(mu), row(w0), w2, row(a0), a2, g2, row(k_k), row(k_a), row(r_k), row(lnx_w), row(lnx_b)]
    return pl.pallas_call(
        functools.partial(_rwkv_kernel, chunk=chunk, n_chunks=n_chunks, heads=heads, hd=hd),
        grid=(b, s // rows),
        in_specs=[pl.BlockSpec((1, rows, cols), lambda i, j: (i, j, 0))]
        + [pl.BlockSpec(t.shape, lambda i, j: (0, 0)) for t in vecs],
        out_specs=pl.BlockSpec((1, rows, dim), lambda i, j: (i, j, 0)),
        out_shape=jax.ShapeDtypeStruct((b, s, dim), F32),
        scratch_shapes=[pltpu.VMEM((1, cols), F32), pltpu.VMEM((heads // 2, 2 * hd, 2 * hd), F32)],
        compiler_params=_params("parallel", "arbitrary"),
        name="rwkv",
    )(p, *vecs)


def _moba_kernel(q_ref, k_ref, v_ref, o_ref, km_ref, ka_ref, vat_ref, *, nb, blk, n_sel, heads, hd):
    j = pl.program_id(1)
    s_len = nb * blk
    nbp = km_ref.shape[0]
    scale = hd ** -0.5
    masked = -1e30
    neg_inf = float("-inf")
    slopes = [2.0 ** (-8.0 * (h + 1) / heads) for h in range(heads)]

    @pl.when(j == 0)
    def _():
        km_ref[...] = jnp.zeros_like(km_ref)
        for n in range(nb):
            km_ref[n:n + 1, :] = jnp.mean(k_ref[0, n * blk:(n + 1) * blk, :], axis=0, keepdims=True)
        row = _iota2((s_len, hd), 0)
        lane = _iota2((s_len, hd), 1)
        blk_id = row // blk
        col = (row - blk_id * blk).astype(F32)
        one_hot = jnp.where(lane == blk_id, 1.0, 0.0)
        for h in range(heads):
            sl = slice(h * hd, (h + 1) * hd)
            k_feat = (one_hot + jnp.where(lane == nbp, slopes[h] * col, 0.0)
                      + jnp.where(lane == nbp + 1, (slopes[h] * blk) * blk_id.astype(F32), 0.0))
            ka_ref[h] = jnp.concatenate([k_ref[0, :, sl], k_feat], axis=1).astype(BF16)
        ones_row = jnp.where(_iota2((hd, blk), 0) == 0, 1.0, 0.0)
        for n in range(nb):
            vt = v_ref[0, n * blk:(n + 1) * blk, :].T
            for h in range(heads):
                vat_ref[n, h] = jnp.concatenate([vt[h * hd:(h + 1) * hd], ones_row], axis=0).astype(BF16)

    qt = (q_ref[0] * scale).T
    sub = _iota2((nbp, blk), 0)
    const_rows = jnp.where(_iota2((hd - nbp, blk), 0) < 2, 1.0, 0.0)
    q_aug = []
    for h in range(heads):
        sl = slice(h * hd, (h + 1) * hd)
        qh = qt[sl]
        gate = _mm(_pieces(km_ref[:, sl], 3), _pieces(qh, 3))
        gate = jnp.where(sub < j, gate, neg_inf)
        cnt = jnp.zeros((nbp, blk), jnp.int32)
        for m in range(nb):
            other = gate[m:m + 1, :]
            beats = (other > gate) | ((other == gate) & (m < sub))
            cnt = cnt + beats.astype(jnp.int32)
        keep = ((sub < j) & (cnt < n_sel)) | (sub == j) | (sub >= nb)
        bias = jnp.where(keep, 0.0, masked)
        q_aug.append(jnp.concatenate([qh, bias, const_rows], axis=0).astype(BF16))
    q_aug = jnp.stack(q_aug, axis=0)

    def scores(n):
        start = pl.multiple_of(n * blk, blk)
        kn = ka_ref[:, pl.ds(start, blk), :]
        return lax.dot_general(kn, q_aug, BNN, preferred_element_type=F32), vat_ref[n]

    s, vj = scores(j)
    causal = _iota2((1, blk, blk), 1) <= _iota2((1, blk, blk), 2)
    s = jnp.where(causal, s, masked)
    m0 = jnp.max(s, axis=1, keepdims=True)
    acc0 = lax.dot_general(vj, jnp.exp(s - m0).astype(BF16), BNN, preferred_element_type=F32)

    def past_block(n, carry):
        m_run, acc = carry
        sc, vn = scores(n)
        m_new = jnp.maximum(m_run, jnp.max(sc, axis=1, keepdims=True))
        pr = jnp.exp(sc - m_new).astype(BF16)
        acc_new = jnp.exp(m_run - m_new) * acc + lax.dot_general(vn, pr, BNN, preferred_element_type=F32)
        return m_new, acc_new

    _, acc = lax.fori_loop(0, j, past_block, (m0, acc0))
    out_t = jnp.concatenate([acc[h, :hd] / acc[h, hd:hd + 1] for h in range(heads)], axis=0)
    o_ref[0] = out_t.T


def _moba(q, k, v):
    b, s, dim = q.shape
    heads, hd, blk = MOBA_HEADS, MOBA_HD, MOBA_BLOCK
    assert s % blk == 0
    nb = s // blk
    nbp = -(-nb // 8) * 8
    assert nbp + 2 <= hd
    n_sel = min(MOBA_TOPK, nb - 1)
    return pl.pallas_call(
        functools.partial(_moba_kernel, nb=nb, blk=blk, n_sel=n_sel, heads=heads, hd=hd),
        grid=(b, nb),
        in_specs=[
            pl.BlockSpec((1, blk, dim), lambda i, j: (i, j, 0)),
            pl.BlockSpec((1, s, dim), lambda i, j: (i, 0, 0)),
            pl.BlockSpec((1, s, dim), lambda i, j: (i, 0, 0)),
        ],
        out_specs=pl.BlockSpec((1, blk, dim), lambda i, j: (i, j, 0)),
        out_shape=jax.ShapeDtypeStruct((b, s, dim), F32),
        scratch_shapes=[pltpu.VMEM((nbp, dim), F32),
                        pltpu.VMEM((heads, s, 2 * hd), BF16),
                        pltpu.VMEM((nb, heads, 2 * hd, blk), BF16)],
        compiler_params=_params("parallel", "arbitrary"),
        name="moba",
    )(q, k, v)


def _hgrn_kernel(q_ref, f_ref, i_ref, g_ref, lbl_ref, nw_ref, o_ref, st_ref, *, chunk, n_chunks, heads, dk,
                 layer):
    c = chunk

    @pl.when(pl.program_id(1) == 0)
    def _():
        st_ref[...] = jnp.zeros_like(st_ref)

    logits = lbl_ref[...]
    e = jnp.exp(logits - jnp.max(logits, axis=0, keepdims=True))
    sm = e / jnp.sum(e, axis=0, keepdims=True)
    lb = jnp.sum(sm[0:layer + 1, :], axis=0, keepdims=True) - sm[0:1, :]

    rows = n_chunks * c
    dim = heads * dk
    fr = f_ref[0]
    lf = jnp.log(lb + (1.0 - lb) * jax.nn.sigmoid(fr))
    kf = (1.0 - lb) * jax.nn.sigmoid(-fr)
    r_i, c_i = _iota2((rows, rows), 0), _iota2((rows, rows), 1)
    tri_all = (r_i >= c_i) & (r_i // c == c_i // c)
    b = _mm((tri_all.astype(BF16),), _pieces(lf, 3))
    ends = [b[(i + 1) * c - 1:(i + 1) * c, :] for i in range(n_chunks)]
    b_last = jnp.concatenate([jnp.broadcast_to(e, (c, dim)) for e in ends], axis=0)
    qd = q_ref[0] * jnp.exp(b)
    kd = kf * jnp.exp(-b)
    kw = kf * jnp.exp(b_last - b)
    w_all = jnp.exp(jnp.concatenate(ends, axis=0))
    v = i_ref[0]

    def by_head(t, n_rows=c):
        return jnp.stack([t[i * n_rows:(i + 1) * n_rows, h * dk:(h + 1) * dk]
                          for i in range(n_chunks) for h in range(heads)], axis=0)
    tri_incl = (_iota2((1, c, c), 1) >= _iota2((1, c, c), 2))
    qd_h, v_h = _pieces(by_head(qd), MIX_PIECES), _pieces(by_head(v), MIX_PIECES)
    sc = jnp.where(tri_incl, _mm(qd_h, by_head(kd), BNT), 0.0)
    o_intra = _mm(sc, v_h, BNN)
    s_add = _mm(v_h, by_head(kw), BTN)
    w_all_h = by_head(w_all, 1)
    s = st_ref[...]
    o_rows = []
    for i in range(n_chunks):
        sl = slice(i * heads, (i + 1) * heads)
        o_h = o_intra[sl] + _mm(tuple(t[sl] for t in qd_h), s, BNT)
        s = s * w_all_h[sl] + s_add[sl]
        o_h = o_h * lax.rsqrt(jnp.mean(o_h * o_h, axis=-1, keepdims=True) + EPS)
        o_rows.append(jnp.concatenate([o_h[h] for h in range(heads)], axis=1))
    st_ref[...] = s
    o = jnp.concatenate(o_rows, axis=0)
    o_ref[0] = o * nw_ref[...] * jax.nn.sigmoid(g_ref[0])


def _hgrn(p, lb_logits, norm_w, *, layer, chunk=HG_CHUNK, n_chunks=HG_CHUNKS_PER_STEP):
    b, s, cols = p.shape
    heads, dk = HG_HEADS, HG_DK
    dim = heads * dk
    assert cols == 4 * dim
    chunk = min(chunk, s)
    n_chunks = min(n_chunks, s // chunk)
    rows = chunk * n_chunks
    col_block = lambda n: pl.BlockSpec((1, rows, dim), lambda i, j, n=n: (i, j, n))
    return pl.pallas_call(
        functools.partial(_hgrn_kernel, chunk=chunk, n_chunks=n_chunks, heads=heads, dk=dk, layer=layer),
        grid=(b, s // rows),
        in_specs=[col_block(0), col_block(1), col_block(2), col_block(3),
                  pl.BlockSpec(lb_logits.shape, lambda i, j: (0, 0)),
                  pl.BlockSpec((1, dim), lambda i, j: (0, 0))],
        out_specs=pl.BlockSpec((1, rows, dim), lambda i, j: (i, j, 0)),
        out_shape=jax.ShapeDtypeStruct((b, s, dim), F32),
        scratch_shapes=[pltpu.VMEM((heads, dk, dk), F32)],
        compiler_params=_params("parallel", "arbitrary"),
        name="hgrn",
    )(p, p, p, p, lb_logits, norm_w.reshape(1, dim))


def kernel(x, norm_g, ffn1_wg, ffn1_wu, ffn1_wd, ffn2_wg, ffn2_wu, ffn2_wd, ev_w_in, ev_w_out, rw_mu, rw_w0, rw_w2, rw_a0, rw_a2, rw_g2, rw_k_k, rw_k_a, rw_r_k, rw_lnx_w, rw_lnx_b, od_w_in, od_w_out, hg_norm_w, hg_lb_logits, final_g):
    bsz, seq, d = x.shape
    depth = norm_g.shape[0]
    rwkv_dim = RWKV_HEADS * RWKV_HD
    rwkv_cols = 3 * rwkv_dim + LORA_W + LORA_A + LORA_G
    moba_dim = MOBA_HEADS * MOBA_HD
    xf = x.reshape(bsz * seq, d)
    for l in range(depth):
        xf = _ffn(xf, norm_g[l, 0], ffn1_wg[l], ffn1_wu[l], ffn1_wd[l], final_g, final_norm=False)
        if l % 2 == 0:
            e = l // 2
            w_in = ev_w_in[e]
            splits = [0, rwkv_cols, rwkv_cols + moba_dim, rwkv_cols + 2 * moba_dim, rwkv_cols + 3 * moba_dim]
            p_r, q, k, v = _inproj(xf, norm_g[l, 1], [w_in[:, a:b] for a, b in zip(splits[:-1], splits[1:])])
            y_a = _rwkv(p_r.reshape(bsz, seq, rwkv_cols), rw_mu[e], rw_w0[e], rw_w2[e], rw_a0[e], rw_a2[e],
                        rw_g2[e], rw_k_k[e], rw_k_a[e], rw_r_k[e], rw_lnx_w[e], rw_lnx_b[e])
            y_b = _moba(q.reshape(bsz, seq, moba_dim), k.reshape(bsz, seq, moba_dim),
                        v.reshape(bsz, seq, moba_dim))
            w_out = ev_w_out[e]
            xf = _outproj(xf, [y_a.reshape(-1, rwkv_dim), y_b.reshape(-1, moba_dim)],
                          [w_out[:rwkv_dim], w_out[rwkv_dim:]])
        else:
            o = l // 2
            (p,) = _inproj(xf, norm_g[l, 1], [od_w_in[o]])
            y = _hgrn(p.reshape(bsz, seq, -1), hg_lb_logits, hg_norm_w[o], layer=l)
            xf = _outproj(xf, [y.reshape(bsz * seq, -1)], [od_w_out[o]])
        xf = _ffn(xf, norm_g[l, 2], ffn2_wg[l], ffn2_wu[l], ffn2_wd[l], final_g, final_norm=(l == depth - 1))
    return xf.reshape(bsz, seq, d)
```

```python
import functools

import jax
import jax.numpy as jnp
from jax import lax
from jax.experimental import pallas as pl
from jax.experimental.pallas import tpu as pltpu

F32 = jnp.float32
BF16 = jnp.bfloat16
NN = (((1,), (0,)), ((), ()))
NT = (((1,), (1,)), ((), ()))
TN = (((0,), (0,)), ((), ()))
BNN = (((2,), (1,)), ((0,), (0,)))
BNT = (((2,), (2,)), ((0,), (0,)))
BTN = (((1,), (1,)), ((0,), (0,)))
MIX_PIECES = 2

EPS = 1e-6
LNX_EPS = 64e-5
RWKV_HEADS = 8
RWKV_HD = 64
LORA_W = 64
LORA_A = 64
LORA_G = 128
MOBA_HEADS = 8
MOBA_HD = 64
MOBA_BLOCK = 256
MOBA_TOPK = 3
HG_HEADS = 8
HG_DK = 128
RWKV_CHUNK = 64
RWKV_CHUNKS_PER_STEP = 4
HG_CHUNK = 64
HG_CHUNKS_PER_STEP = 4
VMEM_LIMIT_BYTES = 56 * 1024 * 1024


def _params(*semantics):
    return pltpu.CompilerParams(dimension_semantics=semantics, vmem_limit_bytes=VMEM_LIMIT_BYTES)


def _dot(a, b):
    return jnp.dot(a, b, preferred_element_type=F32)


def _dot_nt(a, b):
    return lax.dot_general(a, b, NT, preferred_element_type=F32)


def _pieces(a, n):
    if isinstance(a, tuple):
        return a
    out = []
    for i in range(n):
        hi = a.astype(BF16)
        out.append(hi)
        if i + 1 < n:
            a = a - hi.astype(F32)
    return tuple(out)


def _mm(a, b, dims=NN, n=MIX_PIECES):
    a = _pieces(a, n)
    b = _pieces(b, n)
    order = max(len(a), len(b)) - 1
    out = None
    for i, ai in enumerate(a):
        for j, bj in enumerate(b):
            if i + j <= order:
                t = lax.dot_general(ai, bj, dims, preferred_element_type=F32)
                out = t if out is None else out + t
    return out


def _rmsnorm(x, g):
    return x * lax.rsqrt(jnp.mean(x * x, axis=-1, keepdims=True) + EPS) * g


def _iota2(shape, dim):
    return lax.broadcasted_iota(jnp.int32, shape, dim)


def _block_ones(n, width):
    return (_iota2((n, n), 0) // width == _iota2((n, n), 1) // width).astype(F32)


def _ffn_kernel(x_ref, *refs, n_y, final_norm, tf):
    y_refs, wo_refs = refs[:n_y], refs[n_y:2 * n_y]
    g_ref, wg_ref, wu_ref, wd_ref, fg_ref, o_ref = refs[2 * n_y:]
    x = x_ref[...]
    for y_ref, wo_ref in zip(y_refs, wo_refs):
        x = x + _dot(y_ref[...].astype(BF16), wo_ref[...])
    h = _rmsnorm(x, g_ref[...]).astype(BF16)
    acc = None
    for c0 in range(0, wg_ref.shape[1], tf):
        gate = _dot(h, wg_ref[:, c0:c0 + tf])
        up = _dot(h, wu_ref[:, c0:c0 + tf])
        act = (gate * jax.nn.sigmoid(gate) * up).astype(BF16)
        part = _dot(act, wd_ref[c0:c0 + tf, :])
        acc = part if acc is None else acc + part
    out = x + 0.5 * acc
    if final_norm:
        out = _rmsnorm(out, fg_ref[...])
    o_ref[...] = out


def _resident(shape):
    return pl.BlockSpec(shape, lambda i: (0,) * len(shape), pipeline_mode=pl.Buffered(1))


def _ffn(x, ys, wos, g, wg, wu, wd, final_g, *, final_norm, tm=512, tf=256):
    m, d = x.shape
    f = wg.shape[1]
    tm = min(tm, m)
    assert f % tf == 0
    return pl.pallas_call(
        functools.partial(_ffn_kernel, n_y=len(ys), final_norm=final_norm, tf=tf),
        grid=(m // tm,),
        in_specs=[pl.BlockSpec((tm, d), lambda i: (i, 0))]
        + [pl.BlockSpec((tm, y.shape[1]), lambda i: (i, 0)) for y in ys]
        + [_resident(w.shape) for w in wos]
        + [_resident((1, d)), _resident((d, f)), _resident((d, f)), _resident((f, d)), _resident((1, d))],
        out_specs=pl.BlockSpec((tm, d), lambda i: (i, 0)),
        out_shape=jax.ShapeDtypeStruct((m, d), F32),
        compiler_params=_params("parallel"),
        name="ffn",
    )(x, *ys, *[w.astype(BF16) for w in wos], g.reshape(1, d), wg.astype(BF16), wu.astype(BF16),
      wd.astype(BF16), final_g.reshape(1, d))


def _inproj_kernel(x_ref, g_ref, *refs, n_w, tn):
    w_refs, o_refs = refs[:n_w], refs[n_w:]
    h = _rmsnorm(x_ref[...], g_ref[...]).astype(BF16)
    for w_ref, o_ref in zip(w_refs, o_refs):
        n = w_ref.shape[1]
        for c0 in range(0, n, tn):
            c1 = min(c0 + tn, n)
            o_ref[:, c0:c1] = _dot(h, w_ref[:, c0:c1])


def _inproj(x, g, ws, *, tm=512, tn=512):
    m, d = x.shape
    tm = min(tm, m)
    return pl.pallas_call(
        functools.partial(_inproj_kernel, n_w=len(ws), tn=tn),
        grid=(m // tm,),
        in_specs=[pl.BlockSpec((tm, d), lambda i: (i, 0)), _resident((1, d))]
        + [_resident(w.shape) for w in ws],
        out_specs=[pl.BlockSpec((tm, w.shape[1]), lambda i: (i, 0)) for w in ws],
        out_shape=[jax.ShapeDtypeStruct((m, w.shape[1]), F32) for w in ws],
        compiler_params=_params("parallel"),
        name="inproj",
    )(x, g.reshape(1, d), *[w.astype(BF16) for w in ws])


def _bdot(a, b, dims):
    return lax.dot_general(a.astype(BF16), b.astype(BF16), dims, preferred_element_type=F32)


def _block_diag(x, half):
    lo = _iota2((1, 1, 2 * half), 2) < half
    return jnp.concatenate([jnp.where(lo, x, 0.0), jnp.where(lo, 0.0, x)], axis=1)


def _unit_lower_inverse(a_strict, c):
    row = _iota2((1, c, 2 * c), 1)
    col = _iota2((1, c, 2 * c), 2) & (c - 1)
    eye = (row == col).astype(F32)
    t = None
    m = 1
    while m < c:
        mask = ((row // (2 * m)) == (col // (2 * m))) & ((row & m) != 0) & ((col & m) == 0)
        lm = jnp.where(mask, a_strict, 0.0)
        if t is None:
            t = eye - lm
        else:
            t = t - _bdot(_bdot(t, _block_diag(lm, c), BNN), _block_diag(t, c), BNN)
        m *= 2
    return t


def _rwkv_kernel(p_ref, mu_ref, w0_ref, w2_ref, a0_ref, a2_ref, g2_ref, kk_ref, ka_ref, rk_ref,
                 lnw_ref, lnb_ref, o_ref, carry_ref, st_ref, *, chunk, n_chunks, heads, hd):
    c = chunk
    rows = n_chunks * c
    dim = heads * hd

    @pl.when(pl.program_id(1) == 0)
    def _():
        carry_ref[...] = jnp.zeros_like(carry_ref)
        st_ref[...] = jnp.zeros_like(st_ref)

    p = p_ref[0]
    row = _iota2((rows, 1), 0)
    prev = jnp.where(row == 0, carry_ref[...], pltpu.roll(p, 1, axis=0))
    carry_ref[...] = p[rows - 1:rows, :]
    xs = p + (prev - p) * mu_ref[...]

    r = xs[:, 0:dim]
    k = xs[:, dim:2 * dim]
    v = xs[:, 2 * dim:3 * dim]
    o1 = 3 * dim
    w_lr = xs[:, o1:o1 + LORA_W]
    a_lr = xs[:, o1 + LORA_W:o1 + LORA_W + LORA_A]
    g_lr = xs[:, o1 + LORA_W + LORA_A:o1 + LORA_W + LORA_A + LORA_G]

    z = w0_ref[...] + _mm(jnp.tanh(w_lr), w2_ref[...])
    softplus = jnp.maximum(-z, 0.0) + jnp.log1p(jnp.exp(-jnp.abs(z)))
    w_raw = -softplus - 0.5
    lw = -jnp.exp(w_raw)
    a = jax.nn.sigmoid(a0_ref[...] + _mm(a_lr, a2_ref[...]))
    g = _mm(jax.nn.sigmoid(g_lr), g2_ref[...])

    head_sum = (_block_ones(dim, hd).astype(BF16),)
    kk = k * kk_ref[...]
    kk = kk * lax.rsqrt(jnp.maximum(_mm(kk * kk, head_sum), 1e-24))
    k2 = k * (1.0 + (a - 1.0) * ka_ref[...])
    bb = kk * a

    r_i, c_i = _iota2((rows, rows), 0), _iota2((rows, rows), 1)
    tri_incl = (r_i >= c_i) & (r_i // c == c_i // c)
    cum = _mm((tri_incl.astype(BF16),), _pieces(lw, 3))
    ends = [cum[(i + 1) * c - 1:(i + 1) * c, :] for i in range(n_chunks)]
    cum_last = jnp.concatenate([jnp.broadcast_to(e, (c, dim)) for e in ends], axis=0)
    w_incl = jnp.exp(cum)
    w_excl = jnp.exp(cum - lw)
    w_inv = jnp.exp(-cum)
    w_tail = jnp.exp(cum_last - cum)
    w_all = jnp.exp(jnp.concatenate(ends, axis=0))

    kt = kk * w_excl
    rt = r * w_incl
    bt = bb * w_inv
    kd = k2 * w_inv
    bw = bb * w_tail
    kw = k2 * w_tail

    pw = 2 * hd
    pairs = heads // 2
    def by_pair(t, n_rows=c):
        return jnp.stack([t[i * n_rows:(i + 1) * n_rows, j * pw:(j + 1) * pw]
                          for i in range(n_chunks) for j in range(pairs)], axis=0)
    bd_f = lambda t: _block_diag(t, hd)
    kt_p, rt_p, bt_p, kd_p, kw_p, bw_p, v_p = (by_pair(t) for t in (kt, rt, bt, kd, kw, bw, v))
    w_all_p = by_pair(w_all, 1)
    row_t = _iota2((1, c, 2 * c), 1)
    col_t = _iota2((1, c, 2 * c), 2) & (c - 1)
    strict_p, incl_p = row_t > col_t, row_t >= col_t
    same_head = (_iota2((1, pw, pw), 1) // hd) == (_iota2((1, pw, pw), 2) // hd)

    kr = jnp.concatenate([kt_p, rt_p], axis=1)
    g_all = _bdot(kr, jnp.concatenate([bd_f(bt_p), bd_f(kd_p)], axis=1), BNT)
    gb, gk = g_all[:, :, :2 * c], g_all[:, :, 2 * c:]
    a_b = jnp.where(strict_p, gb[:, :c], 0.0)
    a_k = jnp.where(strict_p, gk[:, :c], 0.0)
    p_b = jnp.where(incl_p, gb[:, c:], 0.0)
    p_k = jnp.where(incl_p, gk[:, c:], 0.0)
    t_inv = _unit_lower_inverse(a_b, c)
    akpk = _bdot(jnp.concatenate([a_k, p_k], axis=1), bd_f(v_p), BNN)
    kv1 = _bdot(t_inv, jnp.concatenate([bd_f(kt_p), bd_f(akpk[:, :c])], axis=2), BNN)
    kt1, v1 = kv1[:, :, :pw], kv1[:, :, pw:]
    pbk = _bdot(p_b, jnp.concatenate([bd_f(kt1), bd_f(v1)], axis=2), BNN)
    q_mat = rt_p - pbk[:, :, :pw]
    z_mat = akpk[:, c:] - pbk[:, :, pw:]
    x_mat = jnp.where(same_head, _bdot(bw_p, kt1, BTN), 0.0)
    n_mat = jnp.where(same_head, _bdot(jnp.concatenate([v_p, v1], axis=1),
                                       jnp.concatenate([kw_p, -bw_p], axis=1), BTN), 0.0)
    s = st_ref[...]
    y_rows = []
    for i in range(n_chunks):
        sl = slice(i * pairs, (i + 1) * pairs)
        y_p = _bdot(q_mat[sl], s, BNT) + z_mat[sl]
        s = s * w_all_p[sl] - _bdot(s, x_mat[sl], BNT) + n_mat[sl]
        y_rows.append(jnp.concatenate([y_p[j] for j in range(pairs)], axis=1))
    st_ref[...] = s
    y = jnp.concatenate(y_rows, axis=0)

    inv_hd = 1.0 / hd
    mean = _mm(y, head_sum) * inv_hd
    yc = y - mean
    var = _mm(yc * yc, head_sum) * inv_hd
    yn = yc * lax.rsqrt(var + LNX_EPS) * lnw_ref[...] + lnb_ref[...]
    bonus = _mm(r * k2 * rk_ref[...], head_sum) * v
    o_ref[0] = (yn + bonus) * g


def _rwkv(p, mu, w0, w2, a0, a2, g2, k_k, k_a, r_k, lnx_w, lnx_b, *, chunk=RWKV_CHUNK, n_chunks=RWKV_CHUNKS_PER_STEP):
    b, s, cols = p.shape
    heads, hd = RWKV_HEADS, RWKV_HD
    dim = heads * hd
    chunk = min(chunk, s)
    n_chunks = min(n_chunks, s // chunk)
    rows = chunk * n_chunks
    row = lambda t: t.reshape(1, -1)
    vecs = [row(mu), row(w0), w2, row(a0), a2, g2, row(k_k), row(k_a), row(r_k), row(lnx_w), row(lnx_b)]
    return pl.pallas_call(
        functools.partial(_rwkv_kernel, chunk=chunk, n_chunks=n_chunks, heads=heads, hd=hd),
        grid=(b, s // rows),
        in_specs=[pl.BlockSpec((1, rows, cols), lambda i, j: (i, j, 0))]
        + [pl.BlockSpec(t.shape, lambda i, j: (0, 0)) for t in vecs],
        out_specs=pl.BlockSpec((1, rows, dim), lambda i, j: (i, j, 0)),
        out_shape=jax.ShapeDtypeStruct((b, s, dim), F32),
        scratch_shapes=[pltpu.VMEM((1, cols), F32), pltpu.VMEM((heads // 2, 2 * hd, 2 * hd), F32)],
        compiler_params=_params("parallel", "arbitrary"),
        name="rwkv",
    )(p, *vecs)


def _moba_kernel(q_ref, k_ref, v_ref, o_ref, km_ref, ka_ref, vat_ref, *, nb, blk, n_sel, heads, hd):
    j = pl.program_id(1)
    s_len = nb * blk
    nbp = km_ref.shape[0]
    scale = hd ** -0.5
    masked = -1e30
    neg_inf = float("-inf")
    slopes = [2.0 ** (-8.0 * (h + 1) / heads) for h in range(heads)]

    @pl.when(j == 0)
    def _():
        km_ref[...] = jnp.zeros_like(km_ref)
        for n in range(nb):
            km_ref[n:n + 1, :] = jnp.mean(k_ref[0, n * blk:(n + 1) * blk, :], axis=0, keepdims=True)
        row = _iota2((s_len, hd), 0)
        lane = _iota2((s_len, hd), 1)
        blk_id = row // blk
        col = (row - blk_id * blk).astype(F32)
        one_hot = jnp.where(lane == blk_id, 1.0, 0.0)
        for h in range(heads):
            sl = slice(h * hd, (h + 1) * hd)
            k_feat = (one_hot + jnp.where(lane == nbp, slopes[h] * col, 0.0)
                      + jnp.where(lane == nbp + 1, (slopes[h] * blk) * blk_id.astype(F32), 0.0))
            ka_ref[h] = jnp.concatenate([k_ref[0, :, sl], k_feat], axis=1).astype(BF16)
        ones_row = jnp.where(_iota2((hd, blk), 0) == 0, 1.0, 0.0)
        for n in range(nb):
            vt = v_ref[0, n * blk:(n + 1) * blk, :].T
            for h in range(heads):
                vat_ref[n, h] = jnp.concatenate([vt[h * hd:(h + 1) * hd], ones_row], axis=0).astype(BF16)

    qt = (q_ref[0] * scale).T
    sub = _iota2((nbp, blk), 0)
    const_rows = jnp.where(_iota2((hd - nbp, blk), 0) < 2, 1.0, 0.0)
    q_aug = []
    for h in range(heads):
        sl = slice(h * hd, (h + 1) * hd)
        qh = qt[sl]
        gate = _mm(_pieces(km_ref[:, sl], 3), _pieces(qh, 3))
        gate = jnp.where(sub < j, gate, neg_inf)
        cnt = jnp.zeros((nbp, blk), jnp.int32)
        for m in range(nb):
            other = gate[m:m + 1, :]
            beats = (other > gate) | ((other == gate) & (m < sub))
            cnt = cnt + beats.astype(jnp.int32)
        keep = ((sub < j) & (cnt < n_sel)) | (sub == j) | (sub >= nb)
        bias = jnp.where(keep, 0.0, masked)
        q_aug.append(jnp.concatenate([qh, bias, const_rows], axis=0).astype(BF16))
    q_aug = jnp.stack(q_aug, axis=0)

    def scores(n):
        start = pl.multiple_of(n * blk, blk)
        kn = ka_ref[:, pl.ds(start, blk), :]
        return lax.dot_general(kn, q_aug, BNN, preferred_element_type=F32), vat_ref[n]

    s, vj = scores(j)
    causal = _iota2((1, blk, blk), 1) <= _iota2((1, blk, blk), 2)
    s = jnp.where(causal, s, masked)
    m0 = jnp.max(s, axis=1, keepdims=True)
    acc0 = lax.dot_general(vj, jnp.exp(s - m0).astype(BF16), BNN, preferred_element_type=F32)

    def past_block(n, carry):
        m_run, acc = carry
        sc, vn = scores(n)
        m_new = jnp.maximum(m_run, jnp.max(sc, axis=1, keepdims=True))
        pr = jnp.exp(sc - m_new).astype(BF16)
        acc_new = jnp.exp(m_run - m_new) * acc + lax.dot_general(vn, pr, BNN, preferred_element_type=F32)
        return m_new, acc_new

    _, acc = lax.fori_loop(0, j, past_block, (m0, acc0))
    out_t = jnp.concatenate([acc[h, :hd] / acc[h, hd:hd + 1] for h in range(heads)], axis=0)
    o_ref[0] = out_t.T


def _moba(q, k, v):
    b, s, dim = q.shape
    heads, hd, blk = MOBA_HEADS, MOBA_HD, MOBA_BLOCK
    assert s % blk == 0
    nb = s // blk
    nbp = -(-nb // 8) * 8
    assert nbp + 2 <= hd
    n_sel = min(MOBA_TOPK, nb - 1)
    return pl.pallas_call(
        functools.partial(_moba_kernel, nb=nb, blk=blk, n_sel=n_sel, heads=heads, hd=hd),
        grid=(b, nb),
        in_specs=[
            pl.BlockSpec((1, blk, dim), lambda i, j: (i, j, 0)),
            pl.BlockSpec((1, s, dim), lambda i, j: (i, 0, 0)),
            pl.BlockSpec((1, s, dim), lambda i, j: (i, 0, 0)),
        ],
        out_specs=pl.BlockSpec((1, blk, dim), lambda i, j: (i, j, 0)),
        out_shape=jax.ShapeDtypeStruct((b, s, dim), F32),
        scratch_shapes=[pltpu.VMEM((nbp, dim), F32),
                        pltpu.VMEM((heads, s, 2 * hd), BF16),
                        pltpu.VMEM((nb, heads, 2 * hd, blk), BF16)],
        compiler_params=_params("parallel", "arbitrary"),
        name="moba",
    )(q, k, v)


def _hgrn_kernel(q_ref, f_ref, i_ref, g_ref, lbl_ref, nw_ref, o_ref, st_ref, *, chunk, n_chunks, heads, dk,
                 layer):
    c = chunk

    @pl.when(pl.program_id(1) == 0)
    def _():
        st_ref[...] = jnp.zeros_like(st_ref)

    logits = lbl_ref[...]
    e = jnp.exp(logits - jnp.max(logits, axis=0, keepdims=True))
    sm = e / jnp.sum(e, axis=0, keepdims=True)
    lb = jnp.sum(sm[0:layer + 1, :], axis=0, keepdims=True) - sm[0:1, :]

    rows = n_chunks * c
    dim = heads * dk
    fr = f_ref[0]
    lf = jnp.log(lb + (1.0 - lb) * jax.nn.sigmoid(fr))
    kf = (1.0 - lb) * jax.nn.sigmoid(-fr)
    r_i, c_i = _iota2((rows, rows), 0), _iota2((rows, rows), 1)
    tri_all = (r_i >= c_i) & (r_i // c == c_i // c)
    b = _mm((tri_all.astype(BF16),), _pieces(lf, 3))
    ends = [b[(i + 1) * c - 1:(i + 1) * c, :] for i in range(n_chunks)]
    b_last = jnp.concatenate([jnp.broadcast_to(e, (c, dim)) for e in ends], axis=0)
    qd = q_ref[0] * jnp.exp(b)
    kd = kf * jnp.exp(-b)
    kw = kf * jnp.exp(b_last - b)
    w_all = jnp.exp(jnp.concatenate(ends, axis=0))
    v = i_ref[0]

    def by_head(t, n_rows=c):
        return jnp.stack([t[i * n_rows:(i + 1) * n_rows, h * dk:(h + 1) * dk]
                          for i in range(n_chunks) for h in range(heads)], axis=0)
    tri_incl = (_iota2((1, c, c), 1) >= _iota2((1, c, c), 2))
    qd_h, v_h = _pieces(by_head(qd), MIX_PIECES), _pieces(by_head(v), MIX_PIECES)
    sc = jnp.where(tri_incl, _mm(qd_h, by_head(kd), BNT), 0.0)
    o_intra = _mm(sc, v_h, BNN)
    s_add = _mm(v_h, by_head(kw), BTN)
    w_all_h = by_head(w_all, 1)
    s = st_ref[...]
    o_rows = []
    for i in range(n_chunks):
        sl = slice(i * heads, (i + 1) * heads)
        o_h = o_intra[sl] + _mm(tuple(t[sl] for t in qd_h), s, BNT)
        s = s * w_all_h[sl] + s_add[sl]
        o_h = o_h * lax.rsqrt(jnp.mean(o_h * o_h, axis=-1, keepdims=True) + EPS)
        o_rows.append(jnp.concatenate([o_h[h] for h in range(heads)], axis=1))
    st_ref[...] = s
    o = jnp.concatenate(o_rows, axis=0)
    o_ref[0] = o * nw_ref[...] * jax.nn.sigmoid(g_ref[0])


def _hgrn(p, lb_logits, norm_w, *, layer, chunk=HG_CHUNK, n_chunks=HG_CHUNKS_PER_STEP):
    b, s, cols = p.shape
    heads, dk = HG_HEADS, HG_DK
    dim = heads * dk
    assert cols == 4 * dim
    chunk = min(chunk, s)
    n_chunks = min(n_chunks, s // chunk)
    rows = chunk * n_chunks
    col_block = lambda n: pl.BlockSpec((1, rows, dim), lambda i, j, n=n: (i, j, n))
    return pl.pallas_call(
        functools.partial(_hgrn_kernel, chunk=chunk, n_chunks=n_chunks, heads=heads, dk=dk, layer=layer),
        grid=(b, s // rows),
        in_specs=[col_block(0), col_block(1), col_block(2), col_block(3),
                  pl.BlockSpec(lb_logits.shape, lambda i, j: (0, 0)),
                  pl.BlockSpec((1, dim), lambda i, j: (0, 0))],
        out_specs=pl.BlockSpec((1, rows, dim), lambda i, j: (i, j, 0)),
        out_shape=jax.ShapeDtypeStruct((b, s, dim), F32),
        scratch_shapes=[pltpu.VMEM((heads, dk, dk), F32)],
        compiler_params=_params("parallel", "arbitrary"),
        name="hgrn",
    )(p, p, p, p, lb_logits, norm_w.reshape(1, dim))


def kernel(x, norm_g, ffn1_wg, ffn1_wu, ffn1_wd, ffn2_wg, ffn2_wu, ffn2_wd, ev_w_in, ev_w_out, rw_mu, rw_w0, rw_w2, rw_a0, rw_a2, rw_g2, rw_k_k, rw_k_a, rw_r_k, rw_lnx_w, rw_lnx_b, od_w_in, od_w_out, hg_norm_w, hg_lb_logits, final_g):
    bsz, seq, d = x.shape
    depth = norm_g.shape[0]
    rwkv_dim = RWKV_HEADS * RWKV_HD
    rwkv_cols = 3 * rwkv_dim + LORA_W + LORA_A + LORA_G
    moba_dim = MOBA_HEADS * MOBA_HD
    xf = x.reshape(bsz * seq, d)
    for l in range(depth):
        xf = _ffn(xf, [], [], norm_g[l, 0], ffn1_wg[l], ffn1_wu[l], ffn1_wd[l], final_g, final_norm=False)
        if l % 2 == 0:
            e = l // 2
            w_in = ev_w_in[e]
            splits = [0, rwkv_cols, rwkv_cols + moba_dim, rwkv_cols + 2 * moba_dim, rwkv_cols + 3 * moba_dim]
            p_r, q, k, v = _inproj(xf, norm_g[l, 1], [w_in[:, a:b] for a, b in zip(splits[:-1], splits[1:])])
            y_a = _rwkv(p_r.reshape(bsz, seq, rwkv_cols), rw_mu[e], rw_w0[e], rw_w2[e], rw_a0[e], rw_a2[e],
                        rw_g2[e], rw_k_k[e], rw_k_a[e], rw_r_k[e], rw_lnx_w[e], rw_lnx_b[e])
            y_b = _moba(q.reshape(bsz, seq, moba_dim), k.reshape(bsz, seq, moba_dim),
                        v.reshape(bsz, seq, moba_dim))
            w_out = ev_w_out[e]
            ys = [y_a.reshape(-1, rwkv_dim), y_b.reshape(-1, moba_dim)]
            wos = [w_out[:rwkv_dim], w_out[rwkv_dim:]]
        else:
            o = l // 2
            (p,) = _inproj(xf, norm_g[l, 1], [od_w_in[o]])
            y = _hgrn(p.reshape(bsz, seq, -1), hg_lb_logits, hg_norm_w[o], layer=l)
            ys, wos = [y.reshape(bsz * seq, -1)], [od_w_out[o]]
        xf = _ffn(xf, ys, wos, norm_g[l, 2], ffn2_wg[l], ffn2_wu[l], ffn2_wd[l], final_g,
                  final_norm=(l == depth - 1))
    return xf.reshape(bsz, seq, d)
```

```python
import functools

import jax
import jax.numpy as jnp
from jax import lax
from jax.experimental import pallas as pl
from jax.experimental.pallas import tpu as pltpu

F32 = jnp.float32
BF16 = jnp.bfloat16
NN = (((1,), (0,)), ((), ()))
NT = (((1,), (1,)), ((), ()))
TN = (((0,), (0,)), ((), ()))
BNN = (((2,), (1,)), ((0,), (0,)))
BNT = (((2,), (2,)), ((0,), (0,)))
BTN = (((1,), (1,)), ((0,), (0,)))
MIX_PIECES = 2

EPS = 1e-6
LNX_EPS = 64e-5
RWKV_HEADS = 8
RWKV_HD = 64
LORA_W = 64
LORA_A = 64
LORA_G = 128
MOBA_HEADS = 8
MOBA_HD = 64
MOBA_BLOCK = 256
MOBA_TOPK = 3
HG_HEADS = 8
HG_DK = 128
RWKV_CHUNK = 64
RWKV_CHUNKS_PER_STEP = 4
HG_CHUNK = 64
HG_CHUNKS_PER_STEP = 4
VMEM_LIMIT_BYTES = 56 * 1024 * 1024


def _params(*semantics):
    return pltpu.CompilerParams(dimension_semantics=semantics, vmem_limit_bytes=VMEM_LIMIT_BYTES)


def _dot(a, b):
    return jnp.dot(a, b, preferred_element_type=F32)


def _dot_nt(a, b):
    return lax.dot_general(a, b, NT, preferred_element_type=F32)


def _pieces(a, n):
    if isinstance(a, tuple):
        return a
    out = []
    for i in range(n):
        hi = a.astype(BF16)
        out.append(hi)
        if i + 1 < n:
            a = a - hi.astype(F32)
    return tuple(out)


def _mm(a, b, dims=NN, n=MIX_PIECES):
    a = _pieces(a, n)
    b = _pieces(b, n)
    order = max(len(a), len(b)) - 1
    out = None
    for i, ai in enumerate(a):
        for j, bj in enumerate(b):
            if i + j <= order:
                t = lax.dot_general(ai, bj, dims, preferred_element_type=F32)
                out = t if out is None else out + t
    return out


def _rmsnorm(x, g):
    return x * lax.rsqrt(jnp.mean(x * x, axis=-1, keepdims=True) + EPS) * g


def _iota2(shape, dim):
    return lax.broadcasted_iota(jnp.int32, shape, dim)


def _block_ones(n, width):
    return (_iota2((n, n), 0) // width == _iota2((n, n), 1) // width).astype(F32)


def _ffn_kernel(x_ref, *refs, n_y, final_norm, tf):
    y_refs, wo_refs = refs[:n_y], refs[n_y:2 * n_y]
    g_ref, wg_ref, wu_ref, wd_ref, fg_ref, o_ref = refs[2 * n_y:]
    x = x_ref[...]
    for y_ref, wo_ref in zip(y_refs, wo_refs):
        x = x + _dot(y_ref[...].astype(BF16), wo_ref[...])
    h = _rmsnorm(x, g_ref[...]).astype(BF16)
    acc = None
    for c0 in range(0, wg_ref.shape[1], tf):
        gate = _dot(h, wg_ref[:, c0:c0 + tf])
        up = _dot(h, wu_ref[:, c0:c0 + tf])
        act = (gate * jax.nn.sigmoid(gate) * up).astype(BF16)
        part = _dot(act, wd_ref[c0:c0 + tf, :])
        acc = part if acc is None else acc + part
    out = x + 0.5 * acc
    if final_norm:
        out = _rmsnorm(out, fg_ref[...])
    o_ref[...] = out


def _resident(shape):
    return pl.BlockSpec(shape, lambda i: (0,) * len(shape), pipeline_mode=pl.Buffered(1))


def _ffn(x, ys, wos, g, wg, wu, wd, final_g, *, final_norm, tm=512, tf=256):
    m, d = x.shape
    f = wg.shape[1]
    tm = min(tm, m)
    assert f % tf == 0
    return pl.pallas_call(
        functools.partial(_ffn_kernel, n_y=len(ys), final_norm=final_norm, tf=tf),
        grid=(m // tm,),
        in_specs=[pl.BlockSpec((tm, d), lambda i: (i, 0))]
        + [pl.BlockSpec((tm, y.shape[1]), lambda i: (i, 0)) for y in ys]
        + [_resident(w.shape) for w in wos]
        + [_resident((1, d)), _resident((d, f)), _resident((d, f)), _resident((f, d)), _resident((1, d))],
        out_specs=pl.BlockSpec((tm, d), lambda i: (i, 0)),
        out_shape=jax.ShapeDtypeStruct((m, d), F32),
        compiler_params=_params("parallel"),
        name="ffn",
    )(x, *ys, *[w.astype(BF16) for w in wos], g.reshape(1, d), wg.astype(BF16), wu.astype(BF16),
      wd.astype(BF16), final_g.reshape(1, d))


def _inproj_kernel(x_ref, g_ref, *refs, n_w, tn):
    w_refs, o_refs = refs[:n_w], refs[n_w:]
    h = _rmsnorm(x_ref[...], g_ref[...]).astype(BF16)
    for w_ref, o_ref in zip(w_refs, o_refs):
        n = w_ref.shape[1]
        for c0 in range(0, n, tn):
            c1 = min(c0 + tn, n)
            o_ref[:, c0:c1] = _dot(h, w_ref[:, c0:c1])


def _inproj(x, g, ws, *, tm=512, tn=512):
    m, d = x.shape
    tm = min(tm, m)
    return pl.pallas_call(
        functools.partial(_inproj_kernel, n_w=len(ws), tn=tn),
        grid=(m // tm,),
        in_specs=[pl.BlockSpec((tm, d), lambda i: (i, 0)), _resident((1, d))]
        + [_resident(w.shape) for w in ws],
        out_specs=[pl.BlockSpec((tm, w.shape[1]), lambda i: (i, 0)) for w in ws],
        out_shape=[jax.ShapeDtypeStruct((m, w.shape[1]), F32) for w in ws],
        compiler_params=_params("parallel"),
        name="inproj",
    )(x, g.reshape(1, d), *[w.astype(BF16) for w in ws])


def _bdot(a, b, dims):
    return lax.dot_general(a.astype(BF16), b.astype(BF16), dims, preferred_element_type=F32)


def _block_diag(x, half):
    lo = _iota2((1, 1, 2 * half), 2) < half
    return jnp.concatenate([jnp.where(lo, x, 0.0), jnp.where(lo, 0.0, x)], axis=1)


def _unit_lower_inverse(a_strict, c):
    row = _iota2((1, c, 2 * c), 1)
    col = _iota2((1, c, 2 * c), 2) & (c - 1)
    eye = (row == col).astype(F32)
    t = None
    m = 1
    while m < c:
        mask = ((row // (2 * m)) == (col // (2 * m))) & ((row & m) != 0) & ((col & m) == 0)
        lm = jnp.where(mask, a_strict, 0.0)
        if t is None:
            t = eye - lm
        else:
            t = t - _bdot(_bdot(t, _block_diag(lm, c), BNN), _block_diag(t, c), BNN)
        m *= 2
    return t


def _rwkv_kernel(p_ref, mu_ref, w0_ref, w2_ref, a0_ref, a2_ref, g2_ref, kk_ref, ka_ref, rk_ref,
                 lnw_ref, lnb_ref, o_ref, carry_ref, st_ref, *, chunk, n_chunks, heads, hd):
    c = chunk
    rows = n_chunks * c
    dim = heads * hd

    @pl.when(pl.program_id(1) == 0)
    def _():
        carry_ref[...] = jnp.zeros_like(carry_ref)
        st_ref[...] = jnp.zeros_like(st_ref)

    p = p_ref[0]
    row = _iota2((rows, 1), 0)
    prev = jnp.where(row == 0, carry_ref[...], pltpu.roll(p, 1, axis=0))
    carry_ref[...] = p[rows - 1:rows, :]
    xs = p + (prev - p) * mu_ref[...]

    r = xs[:, 0:dim]
    k = xs[:, dim:2 * dim]
    v = xs[:, 2 * dim:3 * dim]
    o1 = 3 * dim
    w_lr = xs[:, o1:o1 + LORA_W]
    a_lr = xs[:, o1 + LORA_W:o1 + LORA_W + LORA_A]
    g_lr = xs[:, o1 + LORA_W + LORA_A:o1 + LORA_W + LORA_A + LORA_G]

    z = w0_ref[...] + _mm(jnp.tanh(w_lr), w2_ref[...])
    softplus = jnp.maximum(-z, 0.0) + jnp.log1p(jnp.exp(-jnp.abs(z)))
    w_raw = -softplus - 0.5
    lw = -jnp.exp(w_raw)
    a = jax.nn.sigmoid(a0_ref[...] + _mm(a_lr, a2_ref[...]))
    g = _mm(jax.nn.sigmoid(g_lr), g2_ref[...])

    tile_ones = (_block_ones(2 * hd, hd).astype(BF16),)

    def head_sum(t):
        return jnp.concatenate([_mm(t[:, i:i + 2 * hd], tile_ones) for i in range(0, dim, 2 * hd)], axis=1)

    kk = k * kk_ref[...]
    kk = kk * lax.rsqrt(jnp.maximum(head_sum(kk * kk), 1e-24))
    k2 = k * (1.0 + (a - 1.0) * ka_ref[...])
    bb = kk * a

    r_i, c_i = _iota2((rows, rows), 0), _iota2((rows, rows), 1)
    tri_incl = (r_i >= c_i) & (r_i // c == c_i // c)
    cum = _mm((tri_incl.astype(BF16),), _pieces(lw, 3))
    ends = [cum[(i + 1) * c - 1:(i + 1) * c, :] for i in range(n_chunks)]
    cum_last = jnp.concatenate([jnp.broadcast_to(e, (c, dim)) for e in ends], axis=0)
    w_incl = jnp.exp(cum)
    w_excl = jnp.exp(cum - lw)
    w_inv = jnp.exp(-cum)
    w_tail = jnp.exp(cum_last - cum)
    w_all = jnp.exp(jnp.concatenate(ends, axis=0))

    kt = kk * w_excl
    rt = r * w_incl
    bt = bb * w_inv
    kd = k2 * w_inv
    bw = bb * w_tail
    kw = k2 * w_tail

    pw = 2 * hd
    pairs = heads // 2
    def by_pair(t, n_rows=c):
        return jnp.stack([t[i * n_rows:(i + 1) * n_rows, j * pw:(j + 1) * pw]
                          for i in range(n_chunks) for j in range(pairs)], axis=0)
    bd_f = lambda t: _block_diag(t, hd)
    kt_p, rt_p, bt_p, kd_p, kw_p, bw_p, v_p = (by_pair(t) for t in (kt, rt, bt, kd, kw, bw, v))
    w_all_p = by_pair(w_all, 1)
    row_t = _iota2((1, c, 2 * c), 1)
    col_t = _iota2((1, c, 2 * c), 2) & (c - 1)
    strict_p, incl_p = row_t > col_t, row_t >= col_t
    same_head = (_iota2((1, pw, pw), 1) // hd) == (_iota2((1, pw, pw), 2) // hd)

    kr = jnp.concatenate([kt_p, rt_p], axis=1)
    g_all = _bdot(kr, jnp.concatenate([bd_f(bt_p), bd_f(kd_p)], axis=1), BNT)
    gb, gk = g_all[:, :, :2 * c], g_all[:, :, 2 * c:]
    a_b = jnp.where(strict_p, gb[:, :c], 0.0)
    a_k = jnp.where(strict_p, gk[:, :c], 0.0)
    p_b = jnp.where(incl_p, gb[:, c:], 0.0)
    p_k = jnp.where(incl_p, gk[:, c:], 0.0)
    t_inv = _unit_lower_inverse(a_b, c)
    akpk = _bdot(jnp.concatenate([a_k, p_k], axis=1), bd_f(v_p), BNN)
    kv1 = _bdot(t_inv, jnp.concatenate([bd_f(kt_p), bd_f(akpk[:, :c])], axis=2), BNN)
    kt1, v1 = kv1[:, :, :pw], kv1[:, :, pw:]
    pbk = _bdot(p_b, jnp.concatenate([bd_f(kt1), bd_f(v1)], axis=2), BNN)
    q_mat = rt_p - pbk[:, :, :pw]
    z_mat = akpk[:, c:] - pbk[:, :, pw:]
    x_mat = jnp.where(same_head, _bdot(bw_p, kt1, BTN), 0.0)
    n_mat = jnp.where(same_head, _bdot(jnp.concatenate([v_p, v1], axis=1),
                                       jnp.concatenate([kw_p, -bw_p], axis=1), BTN), 0.0)
    s = st_ref[...]
    y_rows = []
    for i in range(n_chunks):
        sl = slice(i * pairs, (i + 1) * pairs)
        y_p = _bdot(q_mat[sl], s, BNT) + z_mat[sl]
        s = s * w_all_p[sl] - _bdot(s, x_mat[sl], BNT) + n_mat[sl]
        y_rows.append(jnp.concatenate([y_p[j] for j in range(pairs)], axis=1))
    st_ref[...] = s
    y = jnp.concatenate(y_rows, axis=0)

    inv_hd = 1.0 / hd
    mean = head_sum(y) * inv_hd
    yc = y - mean
    var = head_sum(yc * yc) * inv_hd
    yn = yc * lax.rsqrt(var + LNX_EPS) * lnw_ref[...] + lnb_ref[...]
    bonus = head_sum(r * k2 * rk_ref[...]) * v
    o_ref[0] = (yn + bonus) * g


def _rwkv(p, mu, w0, w2, a0, a2, g2, k_k, k_a, r_k, lnx_w, lnx_b, *, chunk=RWKV_CHUNK, n_chunks=RWKV_CHUNKS_PER_STEP):
    b, s, cols = p.shape
    heads, hd = RWKV_HEADS, RWKV_HD
    dim = heads * hd
    chunk = min(chunk, s)
    n_chunks = min(n_chunks, s // chunk)
    rows = chunk * n_chunks
    row = lambda t: t.reshape(1, -1)
    vecs = [row(mu), row(w0), w2, row(a0), a2, g2, row(k_k), row(k_a), row(r_k), row(lnx_w), row(lnx_b)]
    return pl.pallas_call(
        functools.partial(_rwkv_kernel, chunk=chunk, n_chunks=n_chunks, heads=heads, hd=hd),
        grid=(b, s // rows),
        in_specs=[pl.BlockSpec((1, rows, cols), lambda i, j: (i, j, 0))]
        + [pl.BlockSpec(t.shape, lambda i, j: (0, 0)) for t in vecs],
        out_specs=pl.BlockSpec((1, rows, dim), lambda i, j: (i, j, 0)),
        out_shape=jax.ShapeDtypeStruct((b, s, dim), F32),
        scratch_shapes=[pltpu.VMEM((1, cols), F32), pltpu.VMEM((heads // 2, 2 * hd, 2 * hd), F32)],
        compiler_params=_params("parallel", "arbitrary"),
        name="rwkv",
    )(p, *vecs)


def _moba_kernel(q_ref, k_ref, v_ref, o_ref, km_ref, ka_ref, vat_ref, *, nb, blk, n_sel, heads, hd):
    j = pl.program_id(1)
    s_len = nb * blk
    nbp = km_ref.shape[0]
    scale = hd ** -0.5
    masked = -1e30
    neg_inf = float("-inf")
    slopes = [2.0 ** (-8.0 * (h + 1) / heads) for h in range(heads)]

    @pl.when(j == 0)
    def _():
        km_ref[...] = jnp.zeros_like(km_ref)
        for n in range(nb):
            km_ref[n:n + 1, :] = jnp.mean(k_ref[0, n * blk:(n + 1) * blk, :], axis=0, keepdims=True)
        row = _iota2((s_len, hd), 0)
        lane = _iota2((s_len, hd), 1)
        blk_id = row // blk
        col = (row - blk_id * blk).astype(F32)
        one_hot = jnp.where(lane == blk_id, 1.0, 0.0)
        for h in range(heads):
            sl = slice(h * hd, (h + 1) * hd)
            k_feat = (one_hot + jnp.where(lane == nbp, slopes[h] * col, 0.0)
                      + jnp.where(lane == nbp + 1, (slopes[h] * blk) * blk_id.astype(F32), 0.0))
            ka_ref[h] = jnp.concatenate([k_ref[0, :, sl], k_feat], axis=1).astype(BF16)
        ones_row = jnp.where(_iota2((vat_ref.shape[2] - hd, blk), 0) == 0, 1.0, 0.0)
        for n in range(nb):
            vt = v_ref[0, n * blk:(n + 1) * blk, :].T
            for h in range(heads):
                vat_ref[n, h] = jnp.concatenate([vt[h * hd:(h + 1) * hd], ones_row], axis=0).astype(BF16)

    qt = (q_ref[0] * scale).T
    sub = _iota2((nbp, blk), 0)
    const_rows = jnp.where(_iota2((hd - nbp, blk), 0) < 2, 1.0, 0.0)
    q_aug = []
    for h in range(heads):
        sl = slice(h * hd, (h + 1) * hd)
        qh = qt[sl]
        gate = _mm(_pieces(km_ref[:, sl], 3), _pieces(qh, 3))
        gate = jnp.where(sub < j, gate, neg_inf)
        cnt = jnp.zeros((nbp, blk), jnp.int32)
        for m in range(nb):
            other = gate[m:m + 1, :]
            beats = (other > gate) | ((other == gate) & (m < sub))
            cnt = cnt + beats.astype(jnp.int32)
        keep = ((sub < j) & (cnt < n_sel)) | (sub == j) | (sub >= nb)
        bias = jnp.where(keep, 0.0, masked)
        q_aug.append(jnp.concatenate([qh, bias, const_rows], axis=0).astype(BF16))
    q_aug = jnp.stack(q_aug, axis=0)

    def scores(n):
        start = pl.multiple_of(n * blk, blk)
        kn = ka_ref[:, pl.ds(start, blk), :]
        return lax.dot_general(kn, q_aug, BNN, preferred_element_type=F32), vat_ref[n]

    s, vj = scores(j)
    causal = _iota2((1, blk, blk), 1) <= _iota2((1, blk, blk), 2)
    s = jnp.where(causal, s, masked)
    m0 = jnp.max(s, axis=1, keepdims=True)
    acc0 = lax.dot_general(vj, jnp.exp(s - m0).astype(BF16), BNN, preferred_element_type=F32)

    def past_block(n, carry):
        m_run, acc = carry
        sc, vn = scores(n)
        m_new = jnp.maximum(m_run, jnp.max(sc, axis=1, keepdims=True))
        pr = jnp.exp(sc - m_new).astype(BF16)
        acc_new = jnp.exp(m_run - m_new) * acc + lax.dot_general(vn, pr, BNN, preferred_element_type=F32)
        return m_new, acc_new

    _, acc = lax.fori_loop(0, j, past_block, (m0, acc0))
    out_t = jnp.concatenate([acc[h, :hd] / acc[h, hd:hd + 1] for h in range(heads)], axis=0)
    o_ref[0] = out_t.T


def _moba(q, k, v):
    b, s, dim = q.shape
    heads, hd, blk = MOBA_HEADS, MOBA_HD, MOBA_BLOCK
    assert s % blk == 0
    nb = s // blk
    nbp = -(-nb // 8) * 8
    assert nbp + 2 <= hd
    n_sel = min(MOBA_TOPK, nb - 1)
    return pl.pallas_call(
        functools.partial(_moba_kernel, nb=nb, blk=blk, n_sel=n_sel, heads=heads, hd=hd),
        grid=(b, nb),
        in_specs=[
            pl.BlockSpec((1, blk, dim), lambda i, j: (i, j, 0)),
            pl.BlockSpec((1, s, dim), lambda i, j: (i, 0, 0)),
            pl.BlockSpec((1, s, dim), lambda i, j: (i, 0, 0)),
        ],
        out_specs=pl.BlockSpec((1, blk, dim), lambda i, j: (i, j, 0)),
        out_shape=jax.ShapeDtypeStruct((b, s, dim), F32),
        scratch_shapes=[pltpu.VMEM((nbp, dim), F32),
                        pltpu.VMEM((heads, s, 2 * hd), BF16),
                        pltpu.VMEM((nb, heads, hd + 16, blk), BF16)],
        compiler_params=_params("parallel", "arbitrary"),
        name="moba",
    )(q, k, v)


def _hgrn_kernel(q_ref, f_ref, i_ref, g_ref, lbl_ref, nw_ref, o_ref, st_ref, *, chunk, n_chunks, heads, dk,
                 layer):
    c = chunk

    @pl.when(pl.program_id(1) == 0)
    def _():
        st_ref[...] = jnp.zeros_like(st_ref)

    logits = lbl_ref[...]
    e = jnp.exp(logits - jnp.max(logits, axis=0, keepdims=True))
    sm = e / jnp.sum(e, axis=0, keepdims=True)
    lb = jnp.sum(sm[0:layer + 1, :], axis=0, keepdims=True) - sm[0:1, :]

    rows = n_chunks * c
    dim = heads * dk
    fr = f_ref[0]
    lf = jnp.log(lb + (1.0 - lb) * jax.nn.sigmoid(fr))
    kf = (1.0 - lb) * jax.nn.sigmoid(-fr)
    r_i, c_i = _iota2((rows, rows), 0), _iota2((rows, rows), 1)
    tri_all = (r_i >= c_i) & (r_i // c == c_i // c)
    b = _mm((tri_all.astype(BF16),), _pieces(lf, 3))
    ends = [b[(i + 1) * c - 1:(i + 1) * c, :] for i in range(n_chunks)]
    b_last = jnp.concatenate([jnp.broadcast_to(e, (c, dim)) for e in ends], axis=0)
    qd = q_ref[0] * jnp.exp(b)
    kd = kf * jnp.exp(-b)
    kw = kf * jnp.exp(b_last - b)
    w_all = jnp.exp(jnp.concatenate(ends, axis=0))
    v = i_ref[0]

    def by_head(t, n_rows=c):
        return jnp.stack([t[i * n_rows:(i + 1) * n_rows, h * dk:(h + 1) * dk]
                          for i in range(n_chunks) for h in range(heads)], axis=0)
    hp, pw = heads // 2, 2 * dk
    by_pair = lambda t: jnp.stack([t[i * c:(i + 1) * c, j * pw:(j + 1) * pw]
                                   for i in range(n_chunks) for j in range(hp)], axis=0)
    tri_pair = _iota2((1, c, 2 * c), 1) >= (_iota2((1, c, 2 * c), 2) & (c - 1))
    sc = jnp.where(tri_pair, _mm(by_pair(qd), _block_diag(by_pair(kd), dk), BNT), 0.0)
    o_pair = _bdot(sc, _block_diag(by_pair(v), dk), BNN)
    qd_h, v_h = by_head(qd), by_head(v)
    s_add = _bdot(v_h, by_head(kw), BTN)
    w_all_h = by_head(w_all, 1)
    s = st_ref[...]
    o_rows = []
    for i in range(n_chunks):
        sl = slice(i * heads, (i + 1) * heads)
        o_intra = jnp.stack([o_pair[i * hp + h // 2][:, (h % 2) * dk:(h % 2 + 1) * dk]
                             for h in range(heads)], axis=0)
        o_h = o_intra + _bdot(qd_h[sl], s, BNT)
        s = s * w_all_h[sl] + s_add[sl]
        o_h = o_h * lax.rsqrt(jnp.mean(o_h * o_h, axis=-1, keepdims=True) + EPS)
        o_rows.append(jnp.concatenate([o_h[h] for h in range(heads)], axis=1))
    st_ref[...] = s
    o = jnp.concatenate(o_rows, axis=0)
    o_ref[0] = o * nw_ref[...] * jax.nn.sigmoid(g_ref[0])


def _hgrn(p, lb_logits, norm_w, *, layer, chunk=HG_CHUNK, n_chunks=HG_CHUNKS_PER_STEP):
    b, s, cols = p.shape
    heads, dk = HG_HEADS, HG_DK
    dim = heads * dk
    assert cols == 4 * dim
    chunk = min(chunk, s)
    n_chunks = min(n_chunks, s // chunk)
    rows = chunk * n_chunks
    col_block = lambda n: pl.BlockSpec((1, rows, dim), lambda i, j, n=n: (i, j, n))
    return pl.pallas_call(
        functools.partial(_hgrn_kernel, chunk=chunk, n_chunks=n_chunks, heads=heads, dk=dk, layer=layer),
        grid=(b, s // rows),
        in_specs=[col_block(0), col_block(1), col_block(2), col_block(3),
                  pl.BlockSpec(lb_logits.shape, lambda i, j: (0, 0)),
                  pl.BlockSpec((1, dim), lambda i, j: (0, 0))],
        out_specs=pl.BlockSpec((1, rows, dim), lambda i, j: (i, j, 0)),
        out_shape=jax.ShapeDtypeStruct((b, s, dim), F32),
        scratch_shapes=[pltpu.VMEM((heads, dk, dk), F32)],
        compiler_params=_params("parallel", "arbitrary"),
        name="hgrn",
    )(p, p, p, p, lb_logits, norm_w.reshape(1, dim))


def kernel(x, norm_g, ffn1_wg, ffn1_wu, ffn1_wd, ffn2_wg, ffn2_wu, ffn2_wd, ev_w_in, ev_w_out, rw_mu, rw_w0, rw_w2, rw_a0, rw_a2, rw_g2, rw_k_k, rw_k_a, rw_r_k, rw_lnx_w, rw_lnx_b, od_w_in, od_w_out, hg_norm_w, hg_lb_logits, final_g):
    bsz, seq, d = x.shape
    depth = norm_g.shape[0]
    rwkv_dim = RWKV_HEADS * RWKV_HD
    rwkv_cols = 3 * rwkv_dim + LORA_W + LORA_A + LORA_G
    moba_dim = MOBA_HEADS * MOBA_HD
    xf = x.reshape(bsz * seq, d)
    for l in range(depth):
        xf = _ffn(xf, [], [], norm_g[l, 0], ffn1_wg[l], ffn1_wu[l], ffn1_wd[l], final_g, final_norm=False)
        if l % 2 == 0:
            e = l // 2
            w_in = ev_w_in[e]
            splits = [0, rwkv_cols, rwkv_cols + moba_dim, rwkv_cols + 2 * moba_dim, rwkv_cols + 3 * moba_dim]
            p_r, q, k, v = _inproj(xf, norm_g[l, 1], [w_in[:, a:b] for a, b in zip(splits[:-1], splits[1:])])
            y_a = _rwkv(p_r.reshape(bsz, seq, rwkv_cols), rw_mu[e], rw_w0[e], rw_w2[e], rw_a0[e], rw_a2[e],
                        rw_g2[e], rw_k_k[e], rw_k_a[e], rw_r_k[e], rw_lnx_w[e], rw_lnx_b[e])
            y_b = _moba(q.reshape(bsz, seq, moba_dim), k.reshape(bsz, seq, moba_dim),
                        v.reshape(bsz, seq, moba_dim))
            w_out = ev_w_out[e]
            ys = [y_a.reshape(-1, rwkv_dim), y_b.reshape(-1, moba_dim)]
            wos = [w_out[:rwkv_dim], w_out[rwkv_dim:]]
        else:
            o = l // 2
            (p,) = _inproj(xf, norm_g[l, 1], [od_w_in[o]])
            y = _hgrn(p.reshape(bsz, seq, -1), hg_lb_logits, hg_norm_w[o], layer=l)
            ys, wos = [y.reshape(bsz * seq, -1)], [od_w_out[o]]
        xf = _ffn(xf, ys, wos, norm_g[l, 2], ffn2_wg[l], ffn2_wu[l], ffn2_wd[l], final_g,
                  final_norm=(l == depth - 1))
    return xf.reshape(bsz, seq, d)
```

```python
import functools

import jax
import jax.numpy as jnp
from jax import lax
from jax.experimental import pallas as pl
from jax.experimental.pallas import tpu as pltpu

F32 = jnp.float32
BF16 = jnp.bfloat16
NN = (((1,), (0,)), ((), ()))
NT = (((1,), (1,)), ((), ()))
TN = (((0,), (0,)), ((), ()))
BNN = (((2,), (1,)), ((0,), (0,)))
BNT = (((2,), (2,)), ((0,), (0,)))
BTN = (((1,), (1,)), ((0,), (0,)))
MIX_PIECES = 2

EPS = 1e-6
LNX_EPS = 64e-5
RWKV_HEADS = 8
RWKV_HD = 64
LORA_W = 64
LORA_A = 64
LORA_G = 128
MOBA_HEADS = 8
MOBA_HD = 64
MOBA_BLOCK = 256
MOBA_TOPK = 3
HG_HEADS = 8
HG_DK = 128
RWKV_CHUNK = 64
RWKV_CHUNKS_PER_STEP = 4
HG_CHUNK = 64
HG_CHUNKS_PER_STEP = 4
VMEM_LIMIT_BYTES = 56 * 1024 * 1024


def _params(*semantics):
    return pltpu.CompilerParams(dimension_semantics=semantics, vmem_limit_bytes=VMEM_LIMIT_BYTES)


def _dot(a, b):
    return jnp.dot(a, b, preferred_element_type=F32)


def _dot_nt(a, b):
    return lax.dot_general(a, b, NT, preferred_element_type=F32)


def _pieces(a, n):
    if isinstance(a, tuple):
        return a
    out = []
    for i in range(n):
        hi = a.astype(BF16)
        out.append(hi)
        if i + 1 < n:
            a = a - hi.astype(F32)
    return tuple(out)


def _mm(a, b, dims=NN, n=MIX_PIECES):
    a = _pieces(a, n)
    b = _pieces(b, n)
    order = max(len(a), len(b)) - 1
    out = None
    for i, ai in enumerate(a):
        for j, bj in enumerate(b):
            if i + j <= order:
                t = lax.dot_general(ai, bj, dims, preferred_element_type=F32)
                out = t if out is None else out + t
    return out


def _rmsnorm(x, g):
    return x * lax.rsqrt(jnp.mean(x * x, axis=-1, keepdims=True) + EPS) * g


def _iota2(shape, dim):
    return lax.broadcasted_iota(jnp.int32, shape, dim)


def _block_ones(n, width):
    return (_iota2((n, n), 0) // width == _iota2((n, n), 1) // width).astype(F32)


def _load_weight(w_hbm, layer, dst_ref, stage_ref, sem_ref, slab_rows):
    n_rows = dst_ref.shape[0]
    assert n_rows % slab_rows == 0 and slab_rows <= stage_ref.shape[1]
    n_slabs = n_rows // slab_rows

    def copy(s):
        return pltpu.make_async_copy(w_hbm.at[layer, pl.ds(s * slab_rows, slab_rows), :],
                                     stage_ref.at[s % 2, pl.ds(0, slab_rows), :], sem_ref.at[s % 2])

    copy(0).start()
    for s in range(n_slabs):
        if s + 1 < n_slabs:
            copy(s + 1).start()
        copy(s).wait()
        dst_ref[pl.ds(s * slab_rows, slab_rows), :] = stage_ref[s % 2, pl.ds(0, slab_rows), :].astype(BF16)


def _ffn_kernel(x_ref, *refs, n_y, layer, out_layer, final_norm, tf):
    y_refs = refs[:n_y]
    has_out = n_y > 0
    (wo_hbm,) = refs[n_y:n_y + 1] if has_out else (None,)
    g_ref, wg_hbm, wu_hbm, wd_hbm, fg_ref, o_ref = refs[n_y + has_out:n_y + has_out + 6]
    scratch = refs[n_y + has_out + 6:]
    wg_ref, wu_ref, wd_ref, wide_stage, tall_stage, sem = scratch[:6]
    wo_ref = scratch[6] if has_out else None

    @pl.when(pl.program_id(0) == 0)
    def _():
        _load_weight(wg_hbm, layer, wg_ref, wide_stage, sem, wide_stage.shape[1])
        _load_weight(wu_hbm, layer, wu_ref, wide_stage, sem, wide_stage.shape[1])
        _load_weight(wd_hbm, layer, wd_ref, tall_stage, sem, tall_stage.shape[1])
        if has_out:
            _load_weight(wo_hbm, out_layer, wo_ref, tall_stage, sem, wo_ref.shape[0] // 4)

    x = x_ref[...]
    row0 = 0
    for y_ref in y_refs:
        rows = y_ref.shape[1]
        x = x + _dot(y_ref[...].astype(BF16), wo_ref[row0:row0 + rows, :])
        row0 += rows
    h = _rmsnorm(x, g_ref[...]).astype(BF16)
    acc = None
    for c0 in range(0, wg_ref.shape[1], tf):
        gate = _dot(h, wg_ref[:, c0:c0 + tf])
        up = _dot(h, wu_ref[:, c0:c0 + tf])
        act = (gate * jax.nn.sigmoid(gate) * up).astype(BF16)
        part = _dot(act, wd_ref[c0:c0 + tf, :])
        acc = part if acc is None else acc + part
    out = x + 0.5 * acc
    if final_norm:
        out = _rmsnorm(out, fg_ref[...])
    o_ref[...] = out


def _resident(shape):
    return pl.BlockSpec(shape, lambda i: (0,) * len(shape), pipeline_mode=pl.Buffered(1))


def _ffn(x, ys, wo, out_layer, g, wg, wu, wd, layer, final_g, *, final_norm, tm=512, tf=256, n_slabs=8):
    m, d = x.shape
    f = wg.shape[2]
    tm = min(tm, m)
    assert f % tf == 0 and d % n_slabs == 0 and f % n_slabs == 0
    hbm = pl.BlockSpec(memory_space=pl.ANY)
    has_out = len(ys) > 0
    scratch = [pltpu.VMEM((d, f), BF16), pltpu.VMEM((d, f), BF16), pltpu.VMEM((f, d), BF16),
               pltpu.VMEM((2, d // n_slabs, f), F32), pltpu.VMEM((2, f // n_slabs, d), F32),
               pltpu.SemaphoreType.DMA((2,))]
    if has_out:
        assert sum(y.shape[1] for y in ys) == wo.shape[1] and wo.shape[1] // 4 <= f // n_slabs
        scratch.append(pltpu.VMEM(wo.shape[1:], BF16))
    return pl.pallas_call(
        functools.partial(_ffn_kernel, n_y=len(ys), layer=layer, out_layer=out_layer, final_norm=final_norm,
                          tf=tf),
        grid=(m // tm,),
        in_specs=[pl.BlockSpec((tm, d), lambda i: (i, 0))]
        + [pl.BlockSpec((tm, y.shape[1]), lambda i: (i, 0)) for y in ys]
        + ([hbm] if has_out else [])
        + [_resident((1, d)), hbm, hbm, hbm, _resident((1, d))],
        out_specs=pl.BlockSpec((tm, d), lambda i: (i, 0)),
        out_shape=jax.ShapeDtypeStruct((m, d), F32),
        scratch_shapes=scratch,
        compiler_params=_params("arbitrary"),
        name="ffn",
    )(x, *ys, *([wo] if has_out else []), g.reshape(1, d), wg, wu, wd, final_g.reshape(1, d))


def _inproj_kernel(x_ref, g_ref, w_hbm, *refs, layer, tn):
    o_refs = refs[:-3]
    w_ref, stage, sem = refs[-3:]

    @pl.when(pl.program_id(0) == 0)
    def _():
        _load_weight(w_hbm, layer, w_ref, stage, sem, stage.shape[1])

    h = _rmsnorm(x_ref[...], g_ref[...]).astype(BF16)
    col0 = 0
    for o_ref in o_refs:
        n = o_ref.shape[1]
        for c0 in range(0, n, tn):
            c1 = min(c0 + tn, n)
            o_ref[:, c0:c1] = _dot(h, w_ref[:, col0 + c0:col0 + c1])
        col0 += n


def _inproj(x, g, w, layer, widths, *, tm=512, tn=512, n_slabs=8):
    m, d = x.shape
    n_total = w.shape[2]
    assert sum(widths) == n_total and d % n_slabs == 0
    tm = min(tm, m)
    return pl.pallas_call(
        functools.partial(_inproj_kernel, layer=layer, tn=tn),
        grid=(m // tm,),
        in_specs=[pl.BlockSpec((tm, d), lambda i: (i, 0)), _resident((1, d)), pl.BlockSpec(memory_space=pl.ANY)],
        out_specs=[pl.BlockSpec((tm, n), lambda i: (i, 0)) for n in widths],
        out_shape=[jax.ShapeDtypeStruct((m, n), F32) for n in widths],
        scratch_shapes=[pltpu.VMEM((d, n_total), BF16), pltpu.VMEM((2, d // n_slabs, n_total), F32),
                        pltpu.SemaphoreType.DMA((2,))],
        compiler_params=_params("arbitrary"),
        name="inproj",
    )(x, g.reshape(1, d), w)


def _bdot(a, b, dims):
    return lax.dot_general(a.astype(BF16), b.astype(BF16), dims, preferred_element_type=F32)


def _block_diag(x, half):
    lo = _iota2((1, 1, 2 * half), 2) < half
    return jnp.concatenate([jnp.where(lo, x, 0.0), jnp.where(lo, 0.0, x)], axis=1)


def _unit_lower_inverse(a_strict, c):
    row = _iota2((1, c, 2 * c), 1)
    col = _iota2((1, c, 2 * c), 2) & (c - 1)
    eye = (row == col).astype(F32)
    t = None
    m = 1
    while m < c:
        mask = ((row // (2 * m)) == (col // (2 * m))) & ((row & m) != 0) & ((col & m) == 0)
        lm = jnp.where(mask, a_strict, 0.0)
        if t is None:
            t = eye - lm
        else:
            t = t - _bdot(_bdot(t, _block_diag(lm, c), BNN), _block_diag(t, c), BNN)
        m *= 2
    return t


def _rwkv_kernel(p_ref, mu_ref, w0_ref, w2_ref, a0_ref, a2_ref, g2_ref, kk_ref, ka_ref, rk_ref,
                 lnw_ref, lnb_ref, o_ref, carry_ref, st_ref, *, chunk, n_chunks, heads, hd):
    c = chunk
    rows = n_chunks * c
    dim = heads * hd

    @pl.when(pl.program_id(1) == 0)
    def _():
        carry_ref[...] = jnp.zeros_like(carry_ref)
        st_ref[...] = jnp.zeros_like(st_ref)

    p = p_ref[0]
    row = _iota2((rows, 1), 0)
    prev = jnp.where(row == 0, carry_ref[...], pltpu.roll(p, 1, axis=0))
    carry_ref[...] = p[rows - 1:rows, :]
    xs = p + (prev - p) * mu_ref[...]

    r = xs[:, 0:dim]
    k = xs[:, dim:2 * dim]
    v = xs[:, 2 * dim:3 * dim]
    o1 = 3 * dim
    w_lr = xs[:, o1:o1 + LORA_W]
    a_lr = xs[:, o1 + LORA_W:o1 + LORA_W + LORA_A]
    g_lr = xs[:, o1 + LORA_W + LORA_A:o1 + LORA_W + LORA_A + LORA_G]

    z = w0_ref[...] + _mm(jnp.tanh(w_lr), w2_ref[...])
    softplus = jnp.maximum(-z, 0.0) + jnp.log1p(jnp.exp(-jnp.abs(z)))
    w_raw = -softplus - 0.5
    lw = -jnp.exp(w_raw)
    a = jax.nn.sigmoid(a0_ref[...] + _mm(a_lr, a2_ref[...]))
    g = _mm(jax.nn.sigmoid(g_lr), g2_ref[...])

    tile_ones = (_block_ones(2 * hd, hd).astype(BF16),)

    def head_sum(t):
        return jnp.concatenate([_mm(t[:, i:i + 2 * hd], tile_ones) for i in range(0, dim, 2 * hd)], axis=1)

    kk = k * kk_ref[...]
    kk = kk * lax.rsqrt(jnp.maximum(head_sum(kk * kk), 1e-24))
    k2 = k * (1.0 + (a - 1.0) * ka_ref[...])
    bb = kk * a

    r_i, c_i = _iota2((rows, rows), 0), _iota2((rows, rows), 1)
    tri_incl = (r_i >= c_i) & (r_i // c == c_i // c)
    cum = _mm((tri_incl.astype(BF16),), _pieces(lw, 3))
    ends = [cum[(i + 1) * c - 1:(i + 1) * c, :] for i in range(n_chunks)]
    cum_last = jnp.concatenate([jnp.broadcast_to(e, (c, dim)) for e in ends], axis=0)
    w_incl = jnp.exp(cum)
    w_excl = jnp.exp(cum - lw)
    w_inv = jnp.exp(-cum)
    w_tail = jnp.exp(cum_last - cum)
    w_all = jnp.exp(jnp.concatenate(ends, axis=0))

    kt = kk * w_excl
    rt = r * w_incl
    bt = bb * w_inv
    kd = k2 * w_inv
    bw = bb * w_tail
    kw = k2 * w_tail

    pw = 2 * hd
    pairs = heads // 2
    def by_pair(t, n_rows=c):
        return jnp.stack([t[i * n_rows:(i + 1) * n_rows, j * pw:(j + 1) * pw]
                          for i in range(n_chunks) for j in range(pairs)], axis=0)
    bd_f = lambda t: _block_diag(t, hd)
    kt_p, rt_p, bt_p, kd_p, kw_p, bw_p, v_p = (by_pair(t) for t in (kt, rt, bt, kd, kw, bw, v))
    w_all_p = by_pair(w_all, 1)
    row_t = _iota2((1, c, 2 * c), 1)
    col_t = _iota2((1, c, 2 * c), 2) & (c - 1)
    strict_p, incl_p = row_t > col_t, row_t >= col_t
    same_head = (_iota2((1, pw, pw), 1) // hd) == (_iota2((1, pw, pw), 2) // hd)

    kr = jnp.concatenate([kt_p, rt_p], axis=1)
    g_all = _bdot(kr, jnp.concatenate([bd_f(bt_p), bd_f(kd_p)], axis=1), BNT)
    gb, gk = g_all[:, :, :2 * c], g_all[:, :, 2 * c:]
    a_b = jnp.where(strict_p, gb[:, :c], 0.0)
    a_k = jnp.where(strict_p, gk[:, :c], 0.0)
    p_b = jnp.where(incl_p, gb[:, c:], 0.0)
    p_k = jnp.where(incl_p, gk[:, c:], 0.0)
    t_inv = _unit_lower_inverse(a_b, c)
    akpk = _bdot(jnp.concatenate([a_k, p_k], axis=1), bd_f(v_p), BNN)
    kv1 = _bdot(t_inv, jnp.concatenate([bd_f(kt_p), bd_f(akpk[:, :c])], axis=2), BNN)
    kt1, v1 = kv1[:, :, :pw], kv1[:, :, pw:]
    pbk = _bdot(p_b, jnp.concatenate([bd_f(kt1), bd_f(v1)], axis=2), BNN)
    q_mat = rt_p - pbk[:, :, :pw]
    z_mat = akpk[:, c:] - pbk[:, :, pw:]
    x_mat = jnp.where(same_head, _bdot(bw_p, kt1, BTN), 0.0)
    n_mat = jnp.where(same_head, _bdot(jnp.concatenate([v_p, v1], axis=1),
                                       jnp.concatenate([kw_p, -bw_p], axis=1), BTN), 0.0)
    s = st_ref[...]
    y_rows = []
    for i in range(n_chunks):
        sl = slice(i * pairs, (i + 1) * pairs)
        y_p = _bdot(q_mat[sl], s, BNT) + z_mat[sl]
        s = s * w_all_p[sl] - _bdot(s, x_mat[sl], BNT) + n_mat[sl]
        y_rows.append(jnp.concatenate([y_p[j] for j in range(pairs)], axis=1))
    st_ref[...] = s
    y = jnp.concatenate(y_rows, axis=0)

    inv_hd = 1.0 / hd
    mean = head_sum(y) * inv_hd
    yc = y - mean
    var = head_sum(yc * yc) * inv_hd
    yn = yc * lax.rsqrt(var + LNX_EPS) * lnw_ref[...] + lnb_ref[...]
    bonus = head_sum(r * k2 * rk_ref[...]) * v
    o_ref[0] = (yn + bonus) * g


def _rwkv(p, mu, w0, w2, a0, a2, g2, k_k, k_a, r_k, lnx_w, lnx_b, *, chunk=RWKV_CHUNK, n_chunks=RWKV_CHUNKS_PER_STEP):
    b, s, cols = p.shape
    heads, hd = RWKV_HEADS, RWKV_HD
    dim = heads * hd
    chunk = min(chunk, s)
    n_chunks = min(n_chunks, s // chunk)
    rows = chunk * n_chunks
    row = lambda t: t.reshape(1, -1)
    vecs = [row(mu), row(w0), w2, row(a0), a2, g2, row(k_k), row(k_a), row(r_k), row(lnx_w), row(lnx_b)]
    return pl.pallas_call(
        functools.partial(_rwkv_kernel, chunk=chunk, n_chunks=n_chunks, heads=heads, hd=hd),
        grid=(b, s // rows),
        in_specs=[pl.BlockSpec((1, rows, cols), lambda i, j: (i, j, 0))]
        + [pl.BlockSpec(t.shape, lambda i, j: (0, 0)) for t in vecs],
        out_specs=pl.BlockSpec((1, rows, dim), lambda i, j: (i, j, 0)),
        out_shape=jax.ShapeDtypeStruct((b, s, dim), F32),
        scratch_shapes=[pltpu.VMEM((1, cols), F32), pltpu.VMEM((heads // 2, 2 * hd, 2 * hd), F32)],
        compiler_params=_params("parallel", "arbitrary"),
        name="rwkv",
    )(p, *vecs)


def _moba_kernel(q_ref, k_ref, v_ref, o_ref, km_ref, ka_ref, vat_ref, *, nb, blk, n_sel, heads, hd):
    j = pl.program_id(1)
    s_len = nb * blk
    nbp = km_ref.shape[0]
    scale = hd ** -0.5
    masked = -1e30
    neg_inf = float("-inf")
    slopes = [2.0 ** (-8.0 * (h + 1) / heads) for h in range(heads)]

    @pl.when(j == 0)
    def _():
        km_ref[...] = jnp.zeros_like(km_ref)
        for n in range(nb):
            km_ref[n:n + 1, :] = jnp.mean(k_ref[0, n * blk:(n + 1) * blk, :], axis=0, keepdims=True)
        row = _iota2((s_len, hd), 0)
        lane = _iota2((s_len, hd), 1)
        blk_id = row // blk
        col = (row - blk_id * blk).astype(F32)
        one_hot = jnp.where(lane == blk_id, 1.0, 0.0)
        for h in range(heads):
            sl = slice(h * hd, (h + 1) * hd)
            k_feat = (one_hot + jnp.where(lane == nbp, slopes[h] * col, 0.0)
                      + jnp.where(lane == nbp + 1, (slopes[h] * blk) * blk_id.astype(F32), 0.0))
            ka_ref[h] = jnp.concatenate([k_ref[0, :, sl], k_feat], axis=1).astype(BF16)
        ones_row = jnp.where(_iota2((vat_ref.shape[2] - hd, blk), 0) == 0, 1.0, 0.0)
        for n in range(nb):
            vt = v_ref[0, n * blk:(n + 1) * blk, :].T
            for h in range(heads):
                vat_ref[n, h] = jnp.concatenate([vt[h * hd:(h + 1) * hd], ones_row], axis=0).astype(BF16)

    qt = (q_ref[0] * scale).T
    sub = _iota2((nbp, blk), 0)
    const_rows = jnp.where(_iota2((hd - nbp, blk), 0) < 2, 1.0, 0.0)
    q_aug = []
    for h in range(heads):
        sl = slice(h * hd, (h + 1) * hd)
        qh = qt[sl]
        gate = _mm(_pieces(km_ref[:, sl], 3), _pieces(qh, 3))
        gate = jnp.where(sub < j, gate, neg_inf)
        cnt = jnp.zeros((nbp, blk), jnp.int32)
        for m in range(nb):
            other = gate[m:m + 1, :]
            beats = (other > gate) | ((other == gate) & (m < sub))
            cnt = cnt + beats.astype(jnp.int32)
        keep = ((sub < j) & (cnt < n_sel)) | (sub == j) | (sub >= nb)
        bias = jnp.where(keep, 0.0, masked)
        q_aug.append(jnp.concatenate([qh, bias, const_rows], axis=0).astype(BF16))
    q_aug = jnp.stack(q_aug, axis=0)

    def scores(n):
        start = pl.multiple_of(n * blk, blk)
        kn = ka_ref[:, pl.ds(start, blk), :]
        return lax.dot_general(kn, q_aug, BNN, preferred_element_type=F32), vat_ref[n]

    s, vj = scores(j)
    causal = _iota2((1, blk, blk), 1) <= _iota2((1, blk, blk), 2)
    s = jnp.where(causal, s, masked)
    m0 = jnp.max(s, axis=1, keepdims=True)
    acc0 = lax.dot_general(vj, jnp.exp(s - m0).astype(BF16), BNN, preferred_element_type=F32)

    def past_block(n, carry):
        m_run, acc = carry
        sc, vn = scores(n)
        m_new = jnp.maximum(m_run, jnp.max(sc, axis=1, keepdims=True))
        pr = jnp.exp(sc - m_new).astype(BF16)
        acc_new = jnp.exp(m_run - m_new) * acc + lax.dot_general(vn, pr, BNN, preferred_element_type=F32)
        return m_new, acc_new

    _, acc = lax.fori_loop(0, j, past_block, (m0, acc0))
    out_t = jnp.concatenate([acc[h, :hd] / acc[h, hd:hd + 1] for h in range(heads)], axis=0)
    o_ref[0] = out_t.T


def _moba(q, k, v):
    b, s, dim = q.shape
    heads, hd, blk = MOBA_HEADS, MOBA_HD, MOBA_BLOCK
    assert s % blk == 0
    nb = s // blk
    nbp = -(-nb // 8) * 8
    assert nbp + 2 <= hd
    n_sel = min(MOBA_TOPK, nb - 1)
    return pl.pallas_call(
        functools.partial(_moba_kernel, nb=nb, blk=blk, n_sel=n_sel, heads=heads, hd=hd),
        grid=(b, nb),
        in_specs=[
            pl.BlockSpec((1, blk, dim), lambda i, j: (i, j, 0)),
            pl.BlockSpec((1, s, dim), lambda i, j: (i, 0, 0)),
            pl.BlockSpec((1, s, dim), lambda i, j: (i, 0, 0)),
        ],
        out_specs=pl.BlockSpec((1, blk, dim), lambda i, j: (i, j, 0)),
        out_shape=jax.ShapeDtypeStruct((b, s, dim), F32),
        scratch_shapes=[pltpu.VMEM((nbp, dim), F32),
                        pltpu.VMEM((heads, s, 2 * hd), BF16),
                        pltpu.VMEM((nb, heads, hd + 16, blk), BF16)],
        compiler_params=_params("parallel", "arbitrary"),
        name="moba",
    )(q, k, v)


def _hgrn_kernel(q_ref, f_ref, i_ref, g_ref, lbl_ref, nw_ref, o_ref, st_ref, *, chunk, n_chunks, heads, dk,
                 layer):
    c = chunk

    @pl.when(pl.program_id(1) == 0)
    def _():
        st_ref[...] = jnp.zeros_like(st_ref)

    logits = lbl_ref[...]
    e = jnp.exp(logits - jnp.max(logits, axis=0, keepdims=True))
    sm = e / jnp.sum(e, axis=0, keepdims=True)
    lb = jnp.sum(sm[0:layer + 1, :], axis=0, keepdims=True) - sm[0:1, :]

    rows = n_chunks * c
    dim = heads * dk
    fr = f_ref[0]
    sig = jax.nn.sigmoid(fr)
    lf = jnp.log(lb + (1.0 - lb) * sig)
    kf = (1.0 - lb) * (1.0 - sig)
    r_i, c_i = _iota2((rows, rows), 0), _iota2((rows, rows), 1)
    tri_all = (r_i >= c_i) & (r_i // c == c_i // c)
    b = _mm((tri_all.astype(BF16),), _pieces(lf, 3))
    per_chunk = lambda r: jnp.concatenate(
        [jnp.broadcast_to(b[i * c + r:i * c + r + 1, :], (c, dim)) for i in range(n_chunks)], axis=0)
    ends = [b[(i + 1) * c - 1:(i + 1) * c, :] for i in range(n_chunks)]
    b_last = per_chunk(c - 1)
    b_mid = per_chunk(c // 2 - 1)
    rel = b - b_mid
    qd = q_ref[0] * jnp.exp(rel)
    kd = kf * jnp.exp(-rel)
    q_in = qd * jnp.exp(b_mid)
    kw = kf * jnp.exp(b_last - b)
    w_all = jnp.exp(jnp.concatenate(ends, axis=0))
    v = i_ref[0]

    def by_head(t, n_rows=c):
        return jnp.stack([t[i * n_rows:(i + 1) * n_rows, h * dk:(h + 1) * dk]
                          for i in range(n_chunks) for h in range(heads)], axis=0)
    hp, pw = heads // 2, 2 * dk
    by_pair = lambda t: jnp.stack([t[i * c:(i + 1) * c, j * pw:(j + 1) * pw]
                                   for i in range(n_chunks) for j in range(hp)], axis=0)
    tri_pair = _iota2((1, c, 2 * c), 1) >= (_iota2((1, c, 2 * c), 2) & (c - 1))
    sc = jnp.where(tri_pair, _mm(by_pair(qd), _block_diag(by_pair(kd), dk), BNT), 0.0)
    o_pair = _bdot(sc, _block_diag(by_pair(v), dk), BNN)
    qd_h, v_h = by_head(q_in), by_head(v)
    s_add = _bdot(v_h, by_head(kw), BTN)
    w_all_h = by_head(w_all, 1)
    s = st_ref[...]
    o_rows = []
    for i in range(n_chunks):
        sl = slice(i * heads, (i + 1) * heads)
        o_intra = jnp.stack([o_pair[i * hp + h // 2][:, (h % 2) * dk:(h % 2 + 1) * dk]
                             for h in range(heads)], axis=0)
        o_h = o_intra + _bdot(qd_h[sl], s, BNT)
        s = s * w_all_h[sl] + s_add[sl]
        o_h = o_h * lax.rsqrt(jnp.mean(o_h * o_h, axis=-1, keepdims=True) + EPS)
        o_rows.append(jnp.concatenate([o_h[h] for h in range(heads)], axis=1))
    st_ref[...] = s
    o = jnp.concatenate(o_rows, axis=0)
    o_ref[0] = o * nw_ref[...] * jax.nn.sigmoid(g_ref[0])


def _hgrn(p, lb_logits, norm_w, *, layer, chunk=HG_CHUNK, n_chunks=HG_CHUNKS_PER_STEP):
    b, s, cols = p.shape
    heads, dk = HG_HEADS, HG_DK
    dim = heads * dk
    assert cols == 4 * dim
    chunk = min(chunk, s)
    n_chunks = min(n_chunks, s // chunk)
    rows = chunk * n_chunks
    col_block = lambda n: pl.BlockSpec((1, rows, dim), lambda i, j, n=n: (i, j, n))
    return pl.pallas_call(
        functools.partial(_hgrn_kernel, chunk=chunk, n_chunks=n_chunks, heads=heads, dk=dk, layer=layer),
        grid=(b, s // rows),
        in_specs=[col_block(0), col_block(1), col_block(2), col_block(3),
                  pl.BlockSpec(lb_logits.shape, lambda i, j: (0, 0)),
                  pl.BlockSpec((1, dim), lambda i, j: (0, 0))],
        out_specs=pl.BlockSpec((1, rows, dim), lambda i, j: (i, j, 0)),
        out_shape=jax.ShapeDtypeStruct((b, s, dim), F32),
        scratch_shapes=[pltpu.VMEM((heads, dk, dk), F32)],
        compiler_params=_params("parallel", "arbitrary"),
        name="hgrn",
    )(p, p, p, p, lb_logits, norm_w.reshape(1, dim))


def kernel(x, norm_g, ffn1_wg, ffn1_wu, ffn1_wd, ffn2_wg, ffn2_wu, ffn2_wd, ev_w_in, ev_w_out, rw_mu, rw_w0, rw_w2, rw_a0, rw_a2, rw_g2, rw_k_k, rw_k_a, rw_r_k, rw_lnx_w, rw_lnx_b, od_w_in, od_w_out, hg_norm_w, hg_lb_logits, final_g):
    bsz, seq, d = x.shape
    depth = norm_g.shape[0]
    rwkv_dim = RWKV_HEADS * RWKV_HD
    rwkv_cols = 3 * rwkv_dim + LORA_W + LORA_A + LORA_G
    moba_dim = MOBA_HEADS * MOBA_HD
    xf = x.reshape(bsz * seq, d)
    for l in range(depth):
        xf = _ffn(xf, [], None, 0, norm_g[l, 0], ffn1_wg, ffn1_wu, ffn1_wd, l, final_g, final_norm=False)
        if l % 2 == 0:
            e = l // 2
            p_r, q, k, v = _inproj(xf, norm_g[l, 1], ev_w_in, e, [rwkv_cols, moba_dim, moba_dim, moba_dim])
            y_a = _rwkv(p_r.reshape(bsz, seq, rwkv_cols), rw_mu[e], rw_w0[e], rw_w2[e], rw_a0[e], rw_a2[e],
                        rw_g2[e], rw_k_k[e], rw_k_a[e], rw_r_k[e], rw_lnx_w[e], rw_lnx_b[e])
            y_b = _moba(q.reshape(bsz, seq, moba_dim), k.reshape(bsz, seq, moba_dim),
                        v.reshape(bsz, seq, moba_dim))
            ys, wo, wo_layer = [y_a.reshape(-1, rwkv_dim), y_b.reshape(-1, moba_dim)], ev_w_out, e
        else:
            o = l // 2
            (p,) = _inproj(xf, norm_g[l, 1], od_w_in, o, [od_w_in.shape[2]])
            y = _hgrn(p.reshape(bsz, seq, -1), hg_lb_logits, hg_norm_w[o], layer=l)
            ys, wo, wo_layer = [y.reshape(bsz * seq, -1)], od_w_out, o
        xf = _ffn(xf, ys, wo, wo_layer, norm_g[l, 2], ffn2_wg, ffn2_wu, ffn2_wd, l, final_g,
                  final_norm=(l == depth - 1))
    return xf.reshape(bsz, seq, d)
```

```python
import functools

import jax
import jax.numpy as jnp
from jax import lax
from jax.experimental import pallas as pl
from jax.experimental.pallas import tpu as pltpu

F32 = jnp.float32
BF16 = jnp.bfloat16
NN = (((1,), (0,)), ((), ()))
NT = (((1,), (1,)), ((), ()))
TN = (((0,), (0,)), ((), ()))
BNN = (((2,), (1,)), ((0,), (0,)))
BNT = (((2,), (2,)), ((0,), (0,)))
BTN = (((1,), (1,)), ((0,), (0,)))
MIX_PIECES = 2

EPS = 1e-6
LNX_EPS = 64e-5
RWKV_HEADS = 8
RWKV_HD = 64
LORA_W = 64
LORA_A = 64
LORA_G = 128
MOBA_HEADS = 8
MOBA_HD = 64
MOBA_BLOCK = 256
MOBA_TOPK = 3
HG_HEADS = 8
HG_DK = 128
RWKV_CHUNK = 64
RWKV_CHUNKS_PER_GROUP = 4
RWKV_GROUPS_PER_STEP = 2
HG_CHUNK = 64
HG_CHUNKS_PER_STEP = 4
VMEM_LIMIT_BYTES = 56 * 1024 * 1024


def _params(*semantics):
    return pltpu.CompilerParams(dimension_semantics=semantics, vmem_limit_bytes=VMEM_LIMIT_BYTES)


def _dot(a, b):
    return jnp.dot(a, b, preferred_element_type=F32)


def _dot_nt(a, b):
    return lax.dot_general(a, b, NT, preferred_element_type=F32)


def _pieces(a, n):
    if isinstance(a, tuple):
        return a
    out = []
    for i in range(n):
        hi = a.astype(BF16)
        out.append(hi)
        if i + 1 < n:
            a = a - hi.astype(F32)
    return tuple(out)


def _mm(a, b, dims=NN, n=MIX_PIECES):
    a = _pieces(a, n)
    b = _pieces(b, n)
    order = max(len(a), len(b)) - 1
    out = None
    for i, ai in enumerate(a):
        for j, bj in enumerate(b):
            if i + j <= order:
                t = lax.dot_general(ai, bj, dims, preferred_element_type=F32)
                out = t if out is None else out + t
    return out


def _rmsnorm(x, g):
    return x * lax.rsqrt(jnp.mean(x * x, axis=-1, keepdims=True) + EPS) * g


def _iota2(shape, dim):
    return lax.broadcasted_iota(jnp.int32, shape, dim)


def _block_ones(n, width):
    return (_iota2((n, n), 0) // width == _iota2((n, n), 1) // width).astype(F32)


def _interleave(*gens):
    results = [None] * len(gens)
    live = list(range(len(gens)))
    while live:
        for i in list(live):
            try:
                next(gens[i])
            except StopIteration as stop:
                results[i] = stop.value
                live.remove(i)
    return results


def _load_weight(w_hbm, layer, dst_ref, stage_ref, sem_ref, slab_rows):
    n_rows = dst_ref.shape[0]
    assert n_rows % slab_rows == 0 and slab_rows <= stage_ref.shape[1]
    n_slabs = n_rows // slab_rows

    def copy(s):
        return pltpu.make_async_copy(w_hbm.at[layer, pl.ds(s * slab_rows, slab_rows), :],
                                     stage_ref.at[s % 2, pl.ds(0, slab_rows), :], sem_ref.at[s % 2])

    copy(0).start()
    for s in range(n_slabs):
        if s + 1 < n_slabs:
            copy(s + 1).start()
        copy(s).wait()
        dst_ref[pl.ds(s * slab_rows, slab_rows), :] = stage_ref[s % 2, pl.ds(0, slab_rows), :].astype(BF16)


def _ffn_kernel(x_ref, *refs, n_y, layer, out_layer, final_norm, tf):
    y_refs = refs[:n_y]
    has_out = n_y > 0
    (wo_hbm,) = refs[n_y:n_y + 1] if has_out else (None,)
    g_ref, wg_hbm, wu_hbm, wd_hbm, fg_ref, o_ref = refs[n_y + has_out:n_y + has_out + 6]
    scratch = refs[n_y + has_out + 6:]
    wg_ref, wu_ref, wd_ref, wide_stage, tall_stage, sem, wo_sem = scratch[:7]
    wo_ref = scratch[7] if has_out else None
    n_slabs = wg_ref.shape[1] // tf

    def slab_copies(j):
        slot, cols = j % 2, pl.ds(j * tf, tf)
        return (pltpu.make_async_copy(wg_hbm.at[layer, :, cols], wide_stage.at[slot, 0], sem.at[slot, 0]),
                pltpu.make_async_copy(wu_hbm.at[layer, :, cols], wide_stage.at[slot, 1], sem.at[slot, 1]),
                pltpu.make_async_copy(wd_hbm.at[layer, cols, :], tall_stage.at[slot], sem.at[slot, 2]))

    def body(first_step):
        if first_step:
            for j in range(min(2, n_slabs)):
                for cp in slab_copies(j):
                    cp.start()
            if has_out:
                _load_weight(wo_hbm, out_layer, wo_ref, wo_ref_stage, wo_sem, wo_ref.shape[0] // 4)
        x = x_ref[...]
        row0 = 0
        for y_ref in y_refs:
            rows = y_ref.shape[1]
            x = x + _dot(y_ref[...].astype(BF16), wo_ref[row0:row0 + rows, :])
            row0 += rows
        h = _rmsnorm(x, g_ref[...]).astype(BF16)
        acc = None
        for j in range(n_slabs):
            c0 = j * tf
            if first_step:
                for cp in slab_copies(j):
                    cp.wait()
                wg_ref[:, c0:c0 + tf] = wide_stage[j % 2, 0].astype(BF16)
                wu_ref[:, c0:c0 + tf] = wide_stage[j % 2, 1].astype(BF16)
                wd_ref[c0:c0 + tf, :] = tall_stage[j % 2].astype(BF16)
                if j + 2 < n_slabs:
                    for cp in slab_copies(j + 2):
                        cp.start()
            gate = _dot(h, wg_ref[:, c0:c0 + tf])
            up = _dot(h, wu_ref[:, c0:c0 + tf])
            act = (gate * jax.nn.sigmoid(gate) * up).astype(BF16)
            part = _dot(act, wd_ref[c0:c0 + tf, :])
            acc = part if acc is None else acc + part
        out = x + 0.5 * acc
        if final_norm:
            out = _rmsnorm(out, fg_ref[...])
        o_ref[...] = out

    wo_ref_stage = scratch[8] if has_out else None
    pl.when(pl.program_id(0) == 0)(lambda: body(True))
    pl.when(pl.program_id(0) != 0)(lambda: body(False))


def _resident(shape):
    return pl.BlockSpec(shape, lambda i: (0,) * len(shape), pipeline_mode=pl.Buffered(1))


def _ffn(x, ys, wo, out_layer, g, wg, wu, wd, layer, final_g, *, final_norm, tm=512, tf=256):
    m, d = x.shape
    f = wg.shape[2]
    tm = min(tm, m)
    assert f % tf == 0
    hbm = pl.BlockSpec(memory_space=pl.ANY)
    has_out = len(ys) > 0
    scratch = [pltpu.VMEM((d, f), BF16), pltpu.VMEM((d, f), BF16), pltpu.VMEM((f, d), BF16),
               pltpu.VMEM((2, 2, d, tf), F32), pltpu.VMEM((2, tf, d), F32),
               pltpu.SemaphoreType.DMA((2, 3)), pltpu.SemaphoreType.DMA((2,))]
    if has_out:
        assert sum(y.shape[1] for y in ys) == wo.shape[1] and wo.shape[1] % 4 == 0
        scratch += [pltpu.VMEM(wo.shape[1:], BF16), pltpu.VMEM((2, wo.shape[1] // 4, wo.shape[2]), F32)]
    return pl.pallas_call(
        functools.partial(_ffn_kernel, n_y=len(ys), layer=layer, out_layer=out_layer, final_norm=final_norm,
                          tf=tf),
        grid=(m // tm,),
        in_specs=[pl.BlockSpec((tm, d), lambda i: (i, 0))]
        + [pl.BlockSpec((tm, y.shape[1]), lambda i: (i, 0)) for y in ys]
        + ([hbm] if has_out else [])
        + [_resident((1, d)), hbm, hbm, hbm, _resident((1, d))],
        out_specs=pl.BlockSpec((tm, d), lambda i: (i, 0)),
        out_shape=jax.ShapeDtypeStruct((m, d), F32),
        scratch_shapes=scratch,
        compiler_params=_params("arbitrary"),
        name="ffn",
    )(x, *ys, *([wo] if has_out else []), g.reshape(1, d), wg, wu, wd, final_g.reshape(1, d))


def _inproj_kernel(x_ref, g_ref, w_hbm, *refs, layer, tn):
    o_refs = refs[:-3]
    w_ref, stage, sem = refs[-3:]

    @pl.when(pl.program_id(0) == 0)
    def _():
        _load_weight(w_hbm, layer, w_ref, stage, sem, stage.shape[1])

    h = _rmsnorm(x_ref[...], g_ref[...]).astype(BF16)
    col0 = 0
    for o_ref in o_refs:
        n = o_ref.shape[1]
        for c0 in range(0, n, tn):
            c1 = min(c0 + tn, n)
            o_ref[:, c0:c1] = _dot(h, w_ref[:, col0 + c0:col0 + c1])
        col0 += n


def _inproj(x, g, w, layer, widths, *, tm=512, tn=512, n_slabs=8):
    m, d = x.shape
    n_total = w.shape[2]
    assert sum(widths) == n_total and d % n_slabs == 0
    tm = min(tm, m)
    return pl.pallas_call(
        functools.partial(_inproj_kernel, layer=layer, tn=tn),
        grid=(m // tm,),
        in_specs=[pl.BlockSpec((tm, d), lambda i: (i, 0)), _resident((1, d)), pl.BlockSpec(memory_space=pl.ANY)],
        out_specs=[pl.BlockSpec((tm, n), lambda i: (i, 0)) for n in widths],
        out_shape=[jax.ShapeDtypeStruct((m, n), F32) for n in widths],
        scratch_shapes=[pltpu.VMEM((d, n_total), BF16), pltpu.VMEM((2, d // n_slabs, n_total), F32),
                        pltpu.SemaphoreType.DMA((2,))],
        compiler_params=_params("arbitrary"),
        name="inproj",
    )(x, g.reshape(1, d), w)


def _bdot(a, b, dims):
    return lax.dot_general(a.astype(BF16), b.astype(BF16), dims, preferred_element_type=F32)


def _block_diag(x, half):
    lo = _iota2((1, 1, 2 * half), 2) < half
    return jnp.concatenate([jnp.where(lo, x, 0.0), jnp.where(lo, 0.0, x)], axis=1)


def _unit_lower_inverse(a_strict, c):
    row = _iota2((1, c, 2 * c), 1)
    col = _iota2((1, c, 2 * c), 2) & (c - 1)
    eye = (row == col).astype(F32)
    t = None
    m = 1
    while m < c:
        mask = ((row // (2 * m)) == (col // (2 * m))) & ((row & m) != 0) & ((col & m) == 0)
        lm = jnp.where(mask, a_strict, 0.0)
        if t is None:
            t = eye - lm
        else:
            t = t - _bdot(_bdot(t, _block_diag(lm, c), BNN), _block_diag(t, c), BNN)
            yield
        m *= 2
    return t


def _rwkv_kernel(p_ref, mu_ref, w0_ref, w2_ref, a0_ref, a2_ref, g2_ref, kk_ref, ka_ref, rk_ref,
                 lnw_ref, lnb_ref, o_ref, carry_ref, st_ref, *, chunk, n_chunks, n_groups, heads, hd):
    c = chunk
    step_rows = n_groups * n_chunks * c

    @pl.when(pl.program_id(1) == 0)
    def _():
        carry_ref[...] = jnp.zeros_like(carry_ref)
        st_ref[...] = jnp.zeros_like(st_ref)

    p = p_ref[0]
    row = _iota2((step_rows, 1), 0)
    prev = jnp.where(row == 0, carry_ref[...], pltpu.roll(p, 1, axis=0))
    carry_ref[...] = p[step_rows - 1:step_rows, :]
    xs_all = p + (prev - p) * mu_ref[...]

    rows = n_chunks * c
    front_refs = (w0_ref, w2_ref, a0_ref, a2_ref, g2_ref, kk_ref, ka_ref)
    dims = dict(c=c, n_chunks=n_chunks, heads=heads, hd=hd)
    front = lambda gi: _rwkv_front(xs_all[gi * rows:(gi + 1) * rows], *front_refs, **dims)
    s = st_ref[...]
    (ready,) = _interleave(front(0))
    for gi in range(n_groups):
        back = _rwkv_back(ready, s, rk_ref, lnw_ref, lnb_ref, **dims)
        if gi + 1 < n_groups:
            (out, s), ready = _interleave(back, front(gi + 1))
        else:
            ((out, s),) = _interleave(back)
        o_ref[0, gi * rows:(gi + 1) * rows, :] = out
    st_ref[...] = s


def _head_sum(t, hd):
    tile_ones = _block_ones(2 * hd, hd)
    return jnp.concatenate([_bdot(t[:, i:i + 2 * hd], tile_ones, NN) for i in range(0, t.shape[1], 2 * hd)], axis=1)


def _rwkv_front(xs, w0_ref, w2_ref, a0_ref, a2_ref, g2_ref, kk_ref, ka_ref, *, c, n_chunks, heads, hd):
    rows = n_chunks * c
    dim = heads * hd
    r = xs[:, 0:dim]
    k = xs[:, dim:2 * dim]
    v = xs[:, 2 * dim:3 * dim]
    o1 = 3 * dim
    w_lr = xs[:, o1:o1 + LORA_W]
    a_lr = xs[:, o1 + LORA_W:o1 + LORA_W + LORA_A]
    g_lr = xs[:, o1 + LORA_W + LORA_A:o1 + LORA_W + LORA_A + LORA_G]

    z = w0_ref[...] + _mm(jnp.tanh(w_lr), w2_ref[...])
    softplus = jnp.maximum(-z, 0.0) + jnp.log1p(jnp.exp(-jnp.abs(z)))
    w_raw = -softplus - 0.5
    lw = -jnp.exp(w_raw)
    yield
    a = jax.nn.sigmoid(a0_ref[...] + _mm(a_lr, a2_ref[...]))
    g = _mm(jax.nn.sigmoid(g_lr), g2_ref[...])
    yield
    kk = k * kk_ref[...]
    kk = kk * lax.rsqrt(jnp.maximum(_head_sum(kk * kk, hd), 1e-24))
    k2 = k * (1.0 + (a - 1.0) * ka_ref[...])
    bb = kk * a
    yield

    r_i, c_i = _iota2((rows, rows), 0), _iota2((rows, rows), 1)
    tri_incl = (r_i >= c_i) & (r_i // c == c_i // c)
    cum = _mm((tri_incl.astype(BF16),), _pieces(lw, 2))
    ends = [cum[(i + 1) * c - 1:(i + 1) * c, :] for i in range(n_chunks)]
    cum_last = jnp.concatenate([jnp.broadcast_to(e, (c, dim)) for e in ends], axis=0)
    yield
    w_incl = jnp.exp(cum)
    w_excl = jnp.exp(cum - lw)
    yield
    w_inv = jnp.exp(-cum)
    w_tail = jnp.exp(cum_last - cum)
    w_all = jnp.exp(jnp.concatenate(ends, axis=0))
    yield
    ops = dict(kt=kk * w_excl, rt=r * w_incl, bt=bb * w_inv)
    yield
    ops.update(kd=k2 * w_inv, bw=bb * w_tail, kw=k2 * w_tail)
    yield
    ops.update(v=v, w_all=w_all, r=r, k2=k2, g=g)
    return ops


def _rwkv_back(ops, s, rk_ref, lnw_ref, lnb_ref, *, c, n_chunks, heads, hd):
    pw = 2 * hd
    pairs = heads // 2
    def by_pair(t, n_rows=c):
        return jnp.stack([t[i * n_rows:(i + 1) * n_rows, j * pw:(j + 1) * pw]
                          for i in range(n_chunks) for j in range(pairs)], axis=0)
    bd_f = lambda t: _block_diag(t, hd)
    kt_p, rt_p, bt_p, kd_p, kw_p, bw_p, v_p = (by_pair(ops[n]) for n in ("kt", "rt", "bt", "kd", "kw", "bw", "v"))
    w_all_p = by_pair(ops["w_all"], 1)
    row_t = _iota2((1, c, 2 * c), 1)
    col_t = _iota2((1, c, 2 * c), 2) & (c - 1)
    strict_p, incl_p = row_t > col_t, row_t >= col_t
    same_head = (_iota2((1, pw, pw), 1) // hd) == (_iota2((1, pw, pw), 2) // hd)

    kr = jnp.concatenate([kt_p, rt_p], axis=1)
    g_all = _bdot(kr, jnp.concatenate([bd_f(bt_p), bd_f(kd_p)], axis=1), BNT)
    yield
    gb, gk = g_all[:, :, :2 * c], g_all[:, :, 2 * c:]
    a_b = jnp.where(strict_p, gb[:, :c], 0.0)
    a_k = jnp.where(strict_p, gk[:, :c], 0.0)
    p_b = jnp.where(incl_p, gb[:, c:], 0.0)
    p_k = jnp.where(incl_p, gk[:, c:], 0.0)
    akpk = _bdot(jnp.concatenate([a_k, p_k], axis=1), bd_f(v_p), BNN)
    yield
    t_inv = yield from _unit_lower_inverse(a_b, c)
    kv1 = _bdot(t_inv, jnp.concatenate([bd_f(kt_p), bd_f(akpk[:, :c])], axis=2), BNN)
    yield
    kt1, v1 = kv1[:, :, :pw], kv1[:, :, pw:]
    pbk = _bdot(p_b, jnp.concatenate([bd_f(kt1), bd_f(v1)], axis=2), BNN)
    yield
    q_mat = rt_p - pbk[:, :, :pw]
    z_mat = akpk[:, c:] - pbk[:, :, pw:]
    x_mat = jnp.where(same_head, _bdot(bw_p, kt1, BTN), 0.0)
    yield
    n_mat = jnp.where(same_head, _bdot(jnp.concatenate([v_p, v1], axis=1),
                                       jnp.concatenate([kw_p, -bw_p], axis=1), BTN), 0.0)
    yield
    y_rows = []
    for i in range(n_chunks):
        sl = slice(i * pairs, (i + 1) * pairs)
        y_p = _bdot(q_mat[sl], s, BNT) + z_mat[sl]
        s = s * w_all_p[sl] - _bdot(s, x_mat[sl], BNT) + n_mat[sl]
        y_rows.append(jnp.concatenate([y_p[j] for j in range(pairs)], axis=1))
        yield
    y = jnp.concatenate(y_rows, axis=0)

    inv_hd = 1.0 / hd
    mean = _head_sum(y, hd) * inv_hd
    yc = y - mean
    var = _head_sum(yc * yc, hd) * inv_hd
    yield
    yn = yc * lax.rsqrt(var + LNX_EPS) * lnw_ref[...] + lnb_ref[...]
    bonus = _head_sum(ops["r"] * ops["k2"] * rk_ref[...], hd) * ops["v"]
    return (yn + bonus) * ops["g"], s


def _rwkv(p, mu, w0, w2, a0, a2, g2, k_k, k_a, r_k, lnx_w, lnx_b, *, chunk=RWKV_CHUNK,
          n_chunks=RWKV_CHUNKS_PER_GROUP, n_groups=RWKV_GROUPS_PER_STEP):
    b, s, cols = p.shape
    heads, hd = RWKV_HEADS, RWKV_HD
    dim = heads * hd
    chunk = min(chunk, s)
    n_chunks = min(n_chunks, s // chunk)
    n_groups = min(n_groups, s // (chunk * n_chunks))
    rows = chunk * n_chunks * n_groups
    row = lambda t: t.reshape(1, -1)
    vecs = [row(mu), row(w0), w2, row(a0), a2, g2, row(k_k), row(k_a), row(r_k), row(lnx_w), row(lnx_b)]
    return pl.pallas_call(
        functools.partial(_rwkv_kernel, chunk=chunk, n_chunks=n_chunks, n_groups=n_groups, heads=heads, hd=hd),
        grid=(b, s // rows),
        in_specs=[pl.BlockSpec((1, rows, cols), lambda i, j: (i, j, 0))]
        + [pl.BlockSpec(t.shape, lambda i, j: (0, 0)) for t in vecs],
        out_specs=pl.BlockSpec((1, rows, dim), lambda i, j: (i, j, 0)),
        out_shape=jax.ShapeDtypeStruct((b, s, dim), F32),
        scratch_shapes=[pltpu.VMEM((1, cols), F32), pltpu.VMEM((heads // 2, 2 * hd, 2 * hd), F32)],
        compiler_params=_params("parallel", "arbitrary"),
        name="rwkv",
    )(p, *vecs)


def _moba_kernel(q_ref, k_ref, v_ref, o_ref, km_ref, ka_ref, vat_ref, *, nb, blk, n_sel, heads, hd):
    j = pl.program_id(1)
    s_len = nb * blk
    nbp = km_ref.shape[0]
    scale = hd ** -0.5
    masked = -1e30
    neg_inf = float("-inf")
    slopes = [2.0 ** (-8.0 * (h + 1) / heads) for h in range(heads)]

    @pl.when(j == 0)
    def _():
        km_ref[...] = jnp.zeros_like(km_ref)
        for n in range(nb):
            km_ref[n:n + 1, :] = jnp.mean(k_ref[0, n * blk:(n + 1) * blk, :], axis=0, keepdims=True)
        row = _iota2((s_len, hd), 0)
        lane = _iota2((s_len, hd), 1)
        blk_id = row // blk
        col = (row - blk_id * blk).astype(F32)
        one_hot = jnp.where(lane == blk_id, 1.0, 0.0)
        for h in range(heads):
            sl = slice(h * hd, (h + 1) * hd)
            k_feat = (one_hot + jnp.where(lane == nbp, slopes[h] * col, 0.0)
                      + jnp.where(lane == nbp + 1, (slopes[h] * blk) * blk_id.astype(F32), 0.0))
            ka_ref[h] = jnp.concatenate([k_ref[0, :, sl], k_feat], axis=1).astype(BF16)
        ones_row = jnp.where(_iota2((vat_ref.shape[2] - hd, blk), 0) == 0, 1.0, 0.0)
        for n in range(nb):
            vt = v_ref[0, n * blk:(n + 1) * blk, :].T
            for h in range(heads):
                vat_ref[n, h] = jnp.concatenate([vt[h * hd:(h + 1) * hd], ones_row], axis=0).astype(BF16)

    qt = (q_ref[0] * scale).T
    sub = _iota2((nbp, blk), 0)
    const_rows = jnp.where(_iota2((hd - nbp, blk), 0) < 2, 1.0, 0.0)
    q_aug = []
    for h in range(heads):
        sl = slice(h * hd, (h + 1) * hd)
        qh = qt[sl]
        gate = _mm(_pieces(km_ref[:, sl], 3), _pieces(qh, 3))
        gate = jnp.where(sub < j, gate, neg_inf)
        cnt = jnp.zeros((nbp, blk), jnp.int32)
        for m in range(nb):
            other = gate[m:m + 1, :]
            beats = (other > gate) | ((other == gate) & (m < sub))
            cnt = cnt + beats.astype(jnp.int32)
        keep = ((sub < j) & (cnt < n_sel)) | (sub == j) | (sub >= nb)
        bias = jnp.where(keep, 0.0, masked)
        q_aug.append(jnp.concatenate([qh, bias, const_rows], axis=0).astype(BF16))
    q_aug = jnp.stack(q_aug, axis=0)

    def scores(n):
        start = pl.multiple_of(n * blk, blk)
        kn = ka_ref[:, pl.ds(start, blk), :]
        return lax.dot_general(kn, q_aug, BNN, preferred_element_type=F32), vat_ref[n]

    s, vj = scores(j)
    causal = _iota2((1, blk, blk), 1) <= _iota2((1, blk, blk), 2)
    s = jnp.where(causal, s, masked)
    m0 = jnp.max(s, axis=1, keepdims=True)
    acc0 = lax.dot_general(vj, jnp.exp(s - m0).astype(BF16), BNN, preferred_element_type=F32)

    def past_block(n, carry):
        m_run, acc = carry
        sc, vn = scores(n)
        m_new = jnp.maximum(m_run, jnp.max(sc, axis=1, keepdims=True))
        pr = jnp.exp(sc - m_new).astype(BF16)
        acc_new = jnp.exp(m_run - m_new) * acc + lax.dot_general(vn, pr, BNN, preferred_element_type=F32)
        return m_new, acc_new

    _, acc = lax.fori_loop(0, j, past_block, (m0, acc0))
    out_t = jnp.concatenate([acc[h, :hd] / acc[h, hd:hd + 1] for h in range(heads)], axis=0)
    o_ref[0] = out_t.T


def _moba(q, k, v):
    b, s, dim = q.shape
    heads, hd, blk = MOBA_HEADS, MOBA_HD, MOBA_BLOCK
    assert s % blk == 0
    nb = s // blk
    nbp = -(-nb // 8) * 8
    assert nbp + 2 <= hd
    n_sel = min(MOBA_TOPK, nb - 1)
    return pl.pallas_call(
        functools.partial(_moba_kernel, nb=nb, blk=blk, n_sel=n_sel, heads=heads, hd=hd),
        grid=(b, nb),
        in_specs=[
            pl.BlockSpec((1, blk, dim), lambda i, j: (i, j, 0)),
            pl.BlockSpec((1, s, dim), lambda i, j: (i, 0, 0)),
            pl.BlockSpec((1, s, dim), lambda i, j: (i, 0, 0)),
        ],
        out_specs=pl.BlockSpec((1, blk, dim), lambda i, j: (i, j, 0)),
        out_shape=jax.ShapeDtypeStruct((b, s, dim), F32),
        scratch_shapes=[pltpu.VMEM((nbp, dim), F32),
                        pltpu.VMEM((heads, s, 2 * hd), BF16),
                        pltpu.VMEM((nb, heads, hd + 16, blk), BF16)],
        compiler_params=_params("parallel", "arbitrary"),
        name="moba",
    )(q, k, v)


def _hgrn_kernel(q_ref, f_ref, i_ref, g_ref, lbl_ref, nw_ref, o_ref, st_ref, *, chunk, n_chunks, heads, dk,
                 layer):
    c = chunk

    @pl.when(pl.program_id(1) == 0)
    def _():
        st_ref[...] = jnp.zeros_like(st_ref)

    logits = lbl_ref[...]
    e = jnp.exp(logits - jnp.max(logits, axis=0, keepdims=True))
    sm = e / jnp.sum(e, axis=0, keepdims=True)
    lb = jnp.sum(sm[0:layer + 1, :], axis=0, keepdims=True) - sm[0:1, :]

    rows = n_chunks * c
    dim = heads * dk
    fr = f_ref[0]
    sig = jax.nn.sigmoid(fr)
    lf = jnp.log(lb + (1.0 - lb) * sig)
    kf = (1.0 - lb) * (1.0 - sig)
    r_i, c_i = _iota2((rows, rows), 0), _iota2((rows, rows), 1)
    tri_all = (r_i >= c_i) & (r_i // c == c_i // c)
    b = _mm((tri_all.astype(BF16),), _pieces(lf, 2))
    per_chunk = lambda r: jnp.concatenate(
        [jnp.broadcast_to(b[i * c + r:i * c + r + 1, :], (c, dim)) for i in range(n_chunks)], axis=0)
    ends = [b[(i + 1) * c - 1:(i + 1) * c, :] for i in range(n_chunks)]
    b_last = per_chunk(c - 1)
    b_mid = per_chunk(c // 2 - 1)
    rel = b - b_mid
    qd = q_ref[0] * jnp.exp(rel)
    kd = kf * jnp.exp(-rel)
    q_in = qd * jnp.exp(b_mid)
    kw = kf * jnp.exp(b_last - b)
    w_all = jnp.exp(jnp.concatenate(ends, axis=0))
    v = i_ref[0]

    def by_head(t, n_rows=c):
        return jnp.stack([t[i * n_rows:(i + 1) * n_rows, h * dk:(h + 1) * dk]
                          for i in range(n_chunks) for h in range(heads)], axis=0)
    hp, pw = heads // 2, 2 * dk
    by_pair = lambda t: jnp.stack([t[i * c:(i + 1) * c, j * pw:(j + 1) * pw]
                                   for i in range(n_chunks) for j in range(hp)], axis=0)
    tri_pair = _iota2((1, c, 2 * c), 1) >= (_iota2((1, c, 2 * c), 2) & (c - 1))
    sc = jnp.where(tri_pair, _mm(by_pair(qd), _block_diag(by_pair(kd), dk), BNT), 0.0)
    o_pair = _bdot(sc, _block_diag(by_pair(v), dk), BNN)
    qd_h, v_h = by_head(q_in), by_head(v)
    s_add = _bdot(v_h, by_head(kw), BTN)
    w_all_h = by_head(w_all, 1)
    s = st_ref[...]
    o_rows = []
    for i in range(n_chunks):
        sl = slice(i * heads, (i + 1) * heads)
        o_intra = jnp.stack([o_pair[i * hp + h // 2][:, (h % 2) * dk:(h % 2 + 1) * dk]
                             for h in range(heads)], axis=0)
        o_h = o_intra + _bdot(qd_h[sl], s, BNT)
        s = s * w_all_h[sl] + s_add[sl]
        o_h = o_h * lax.rsqrt(jnp.mean(o_h * o_h, axis=-1, keepdims=True) + EPS)
        o_rows.append(jnp.concatenate([o_h[h] for h in range(heads)], axis=1))
    st_ref[...] = s
    o = jnp.concatenate(o_rows, axis=0)
    o_ref[0] = o * nw_ref[...] * jax.nn.sigmoid(g_ref[0])


def _hgrn(p, lb_logits, norm_w, *, layer, chunk=HG_CHUNK, n_chunks=HG_CHUNKS_PER_STEP):
    b, s, cols = p.shape
    heads, dk = HG_HEADS, HG_DK
    dim = heads * dk
    assert cols == 4 * dim
    chunk = min(chunk, s)
    n_chunks = min(n_chunks, s // chunk)
    rows = chunk * n_chunks
    col_block = lambda n: pl.BlockSpec((1, rows, dim), lambda i, j, n=n: (i, j, n))
    return pl.pallas_call(
        functools.partial(_hgrn_kernel, chunk=chunk, n_chunks=n_chunks, heads=heads, dk=dk, layer=layer),
        grid=(b, s // rows),
        in_specs=[col_block(0), col_block(1), col_block(2), col_block(3),
                  pl.BlockSpec(lb_logits.shape, lambda i, j: (0, 0)),
                  pl.BlockSpec((1, dim), lambda i, j: (0, 0))],
        out_specs=pl.BlockSpec((1, rows, dim), lambda i, j: (i, j, 0)),
        out_shape=jax.ShapeDtypeStruct((b, s, dim), F32),
        scratch_shapes=[pltpu.VMEM((heads, dk, dk), F32)],
        compiler_params=_params("parallel", "arbitrary"),
        name="hgrn",
    )(p, p, p, p, lb_logits, norm_w.reshape(1, dim))


def kernel(x, norm_g, ffn1_wg, ffn1_wu, ffn1_wd, ffn2_wg, ffn2_wu, ffn2_wd, ev_w_in, ev_w_out, rw_mu, rw_w0, rw_w2, rw_a0, rw_a2, rw_g2, rw_k_k, rw_k_a, rw_r_k, rw_lnx_w, rw_lnx_b, od_w_in, od_w_out, hg_norm_w, hg_lb_logits, final_g):
    bsz, seq, d = x.shape
    depth = norm_g.shape[0]
    rwkv_dim = RWKV_HEADS * RWKV_HD
    rwkv_cols = 3 * rwkv_dim + LORA_W + LORA_A + LORA_G
    moba_dim = MOBA_HEADS * MOBA_HD
    xf = x.reshape(bsz * seq, d)
    for l in range(depth):
        xf = _ffn(xf, [], None, 0, norm_g[l, 0], ffn1_wg, ffn1_wu, ffn1_wd, l, final_g, final_norm=False)
        if l % 2 == 0:
            e = l // 2
            p_r, q, k, v = _inproj(xf, norm_g[l, 1], ev_w_in, e, [rwkv_cols, moba_dim, moba_dim, moba_dim])
            y_a = _rwkv(p_r.reshape(bsz, seq, rwkv_cols), rw_mu[e], rw_w0[e], rw_w2[e], rw_a0[e], rw_a2[e],
                        rw_g2[e], rw_k_k[e], rw_k_a[e], rw_r_k[e], rw_lnx_w[e], rw_lnx_b[e])
            y_b = _moba(q.reshape(bsz, seq, moba_dim), k.reshape(bsz, seq, moba_dim),
                        v.reshape(bsz, seq, moba_dim))
            ys, wo, wo_layer = [y_a.reshape(-1, rwkv_dim), y_b.reshape(-1, moba_dim)], ev_w_out, e
        else:
            o = l // 2
            (p,) = _inproj(xf, norm_g[l, 1], od_w_in, o, [od_w_in.shape[2]])
            y = _hgrn(p.reshape(bsz, seq, -1), hg_lb_logits, hg_norm_w[o], layer=l)
            ys, wo, wo_layer = [y.reshape(bsz * seq, -1)], od_w_out, o
        xf = _ffn(xf, ys, wo, wo_layer, norm_g[l, 2], ffn2_wg, ffn2_wu, ffn2_wd, l, final_g,
                  final_norm=(l == depth - 1))
    return xf.reshape(bsz, seq, d)
```

```python
import functools

import jax
import jax.numpy as jnp
from jax import lax
from jax.experimental import pallas as pl
from jax.experimental.pallas import tpu as pltpu

F32 = jnp.float32
BF16 = jnp.bfloat16
NN = (((1,), (0,)), ((), ()))
NT = (((1,), (1,)), ((), ()))
TN = (((0,), (0,)), ((), ()))
BNN = (((2,), (1,)), ((0,), (0,)))
BNT = (((2,), (2,)), ((0,), (0,)))
BTN = (((1,), (1,)), ((0,), (0,)))
MIX_PIECES = 2

EPS = 1e-6
LNX_EPS = 64e-5
RWKV_HEADS = 8
RWKV_HD = 64
LORA_W = 64
LORA_A = 64
LORA_G = 128
MOBA_HEADS = 8
MOBA_HD = 64
MOBA_BLOCK = 256
MOBA_TOPK = 3
HG_HEADS = 8
HG_DK = 128
RWKV_CHUNK = 64
RWKV_CHUNKS_PER_GROUP = 4
RWKV_GROUPS_PER_STEP = 2
HG_CHUNK = 64
HG_CHUNKS_PER_GROUP = 4
HG_GROUPS_PER_STEP = 2
VMEM_LIMIT_BYTES = 56 * 1024 * 1024


def _params(*semantics):
    return pltpu.CompilerParams(dimension_semantics=semantics, vmem_limit_bytes=VMEM_LIMIT_BYTES)


def _dot(a, b):
    return jnp.dot(a, b, preferred_element_type=F32)


def _dot_nt(a, b):
    return lax.dot_general(a, b, NT, preferred_element_type=F32)


def _pieces(a, n):
    if isinstance(a, tuple):
        return a
    out = []
    for i in range(n):
        hi = a.astype(BF16)
        out.append(hi)
        if i + 1 < n:
            a = a - hi.astype(F32)
    return tuple(out)


def _mm(a, b, dims=NN, n=MIX_PIECES):
    a = _pieces(a, n)
    b = _pieces(b, n)
    order = max(len(a), len(b)) - 1
    out = None
    for i, ai in enumerate(a):
        for j, bj in enumerate(b):
            if i + j <= order:
                t = lax.dot_general(ai, bj, dims, preferred_element_type=F32)
                out = t if out is None else out + t
    return out


def _rmsnorm(x, g):
    return x * lax.rsqrt(jnp.mean(x * x, axis=-1, keepdims=True) + EPS) * g


def _iota2(shape, dim):
    return lax.broadcasted_iota(jnp.int32, shape, dim)


def _block_ones(n, width):
    return (_iota2((n, n), 0) // width == _iota2((n, n), 1) // width).astype(F32)


def _interleave(*gens):
    results = [None] * len(gens)
    live = list(range(len(gens)))
    while live:
        for i in list(live):
            try:
                next(gens[i])
            except StopIteration as stop:
                results[i] = stop.value
                live.remove(i)
    return results


def _load_weight(w_hbm, layer, dst_ref, stage_ref, sem_ref, slab_rows):
    n_rows = dst_ref.shape[0]
    assert n_rows % slab_rows == 0 and slab_rows <= stage_ref.shape[1]
    n_slabs = n_rows // slab_rows

    def copy(s):
        return pltpu.make_async_copy(w_hbm.at[layer, pl.ds(s * slab_rows, slab_rows), :],
                                     stage_ref.at[s % 2, pl.ds(0, slab_rows), :], sem_ref.at[s % 2])

    copy(0).start()
    for s in range(n_slabs):
        if s + 1 < n_slabs:
            copy(s + 1).start()
        copy(s).wait()
        dst_ref[pl.ds(s * slab_rows, slab_rows), :] = stage_ref[s % 2, pl.ds(0, slab_rows), :].astype(BF16)


def _ffn_kernel(x_ref, *refs, n_y, layer, out_layer, final_norm, tf):
    y_refs = refs[:n_y]
    has_out = n_y > 0
    (wo_hbm,) = refs[n_y:n_y + 1] if has_out else (None,)
    g_ref, wg_hbm, wu_hbm, wd_hbm, fg_ref, o_ref = refs[n_y + has_out:n_y + has_out + 6]
    scratch = refs[n_y + has_out + 6:]
    wg_ref, wu_ref, wd_ref, wide_stage, tall_stage, sem, wo_sem = scratch[:7]
    wo_ref = scratch[7] if has_out else None
    n_slabs = wg_ref.shape[1] // tf

    def slab_copies(j):
        slot, cols = j % 2, pl.ds(j * tf, tf)
        return (pltpu.make_async_copy(wg_hbm.at[layer, :, cols], wide_stage.at[slot, 0], sem.at[slot, 0]),
                pltpu.make_async_copy(wu_hbm.at[layer, :, cols], wide_stage.at[slot, 1], sem.at[slot, 1]),
                pltpu.make_async_copy(wd_hbm.at[layer, cols, :], tall_stage.at[slot], sem.at[slot, 2]))

    def body(first_step):
        if first_step:
            for j in range(min(2, n_slabs)):
                for cp in slab_copies(j):
                    cp.start()
            if has_out:
                _load_weight(wo_hbm, out_layer, wo_ref, wo_ref_stage, wo_sem, wo_ref.shape[0] // 4)
        x = x_ref[...]
        row0 = 0
        for y_ref in y_refs:
            rows = y_ref.shape[1]
            x = x + _dot(y_ref[...].astype(BF16), wo_ref[row0:row0 + rows, :])
            row0 += rows
        h = _rmsnorm(x, g_ref[...]).astype(BF16)
        acc = None
        for j in range(n_slabs):
            c0 = j * tf
            if first_step:
                for cp in slab_copies(j):
                    cp.wait()
                wg_ref[:, c0:c0 + tf] = wide_stage[j % 2, 0].astype(BF16)
                wu_ref[:, c0:c0 + tf] = wide_stage[j % 2, 1].astype(BF16)
                wd_ref[c0:c0 + tf, :] = tall_stage[j % 2].astype(BF16)
                if j + 2 < n_slabs:
                    for cp in slab_copies(j + 2):
                        cp.start()
            gate = _dot(h, wg_ref[:, c0:c0 + tf])
            up = _dot(h, wu_ref[:, c0:c0 + tf])
            act = (gate * jax.nn.sigmoid(gate) * up).astype(BF16)
            part = _dot(act, wd_ref[c0:c0 + tf, :])
            acc = part if acc is None else acc + part
        out = x + 0.5 * acc
        if final_norm:
            out = _rmsnorm(out, fg_ref[...])
        o_ref[...] = out

    wo_ref_stage = scratch[8] if has_out else None
    pl.when(pl.program_id(0) == 0)(lambda: body(True))
    pl.when(pl.program_id(0) != 0)(lambda: body(False))


def _resident(shape):
    return pl.BlockSpec(shape, lambda i: (0,) * len(shape), pipeline_mode=pl.Buffered(1))


def _ffn(x, ys, wo, out_layer, g, wg, wu, wd, layer, final_g, *, final_norm, tm=512, tf=256):
    m, d = x.shape
    f = wg.shape[2]
    tm = min(tm, m)
    assert f % tf == 0
    hbm = pl.BlockSpec(memory_space=pl.ANY)
    has_out = len(ys) > 0
    scratch = [pltpu.VMEM((d, f), BF16), pltpu.VMEM((d, f), BF16), pltpu.VMEM((f, d), BF16),
               pltpu.VMEM((2, 2, d, tf), F32), pltpu.VMEM((2, tf, d), F32),
               pltpu.SemaphoreType.DMA((2, 3)), pltpu.SemaphoreType.DMA((2,))]
    if has_out:
        assert sum(y.shape[1] for y in ys) == wo.shape[1] and wo.shape[1] % 4 == 0
        scratch += [pltpu.VMEM(wo.shape[1:], BF16), pltpu.VMEM((2, wo.shape[1] // 4, wo.shape[2]), F32)]
    return pl.pallas_call(
        functools.partial(_ffn_kernel, n_y=len(ys), layer=layer, out_layer=out_layer, final_norm=final_norm,
                          tf=tf),
        grid=(m // tm,),
        in_specs=[pl.BlockSpec((tm, d), lambda i: (i, 0))]
        + [pl.BlockSpec((tm, y.shape[1]), lambda i: (i, 0)) for y in ys]
        + ([hbm] if has_out else [])
        + [_resident((1, d)), hbm, hbm, hbm, _resident((1, d))],
        out_specs=pl.BlockSpec((tm, d), lambda i: (i, 0)),
        out_shape=jax.ShapeDtypeStruct((m, d), F32),
        scratch_shapes=scratch,
        compiler_params=_params("arbitrary"),
        name="ffn",
    )(x, *ys, *([wo] if has_out else []), g.reshape(1, d), wg, wu, wd, final_g.reshape(1, d))


def _inproj_kernel(x_ref, g_ref, w_hbm, *refs, layer, tn):
    o_refs = refs[:-3]
    w_ref, stage, sem = refs[-3:]

    @pl.when(pl.program_id(0) == 0)
    def _():
        _load_weight(w_hbm, layer, w_ref, stage, sem, stage.shape[1])

    h = _rmsnorm(x_ref[...], g_ref[...]).astype(BF16)
    col0 = 0
    for o_ref in o_refs:
        n = o_ref.shape[1]
        for c0 in range(0, n, tn):
            c1 = min(c0 + tn, n)
            o_ref[:, c0:c1] = _dot(h, w_ref[:, col0 + c0:col0 + c1])
        col0 += n


def _inproj(x, g, w, layer, widths, *, tm=512, tn=512, n_slabs=8):
    m, d = x.shape
    n_total = w.shape[2]
    assert sum(widths) == n_total and d % n_slabs == 0
    tm = min(tm, m)
    return pl.pallas_call(
        functools.partial(_inproj_kernel, layer=layer, tn=tn),
        grid=(m // tm,),
        in_specs=[pl.BlockSpec((tm, d), lambda i: (i, 0)), _resident((1, d)), pl.BlockSpec(memory_space=pl.ANY)],
        out_specs=[pl.BlockSpec((tm, n), lambda i: (i, 0)) for n in widths],
        out_shape=[jax.ShapeDtypeStruct((m, n), F32) for n in widths],
        scratch_shapes=[pltpu.VMEM((d, n_total), BF16), pltpu.VMEM((2, d // n_slabs, n_total), F32),
                        pltpu.SemaphoreType.DMA((2,))],
        compiler_params=_params("arbitrary"),
        name="inproj",
    )(x, g.reshape(1, d), w)


def _bdot(a, b, dims):
    return lax.dot_general(a.astype(BF16), b.astype(BF16), dims, preferred_element_type=F32)


def _block_diag(x, half):
    lo = _iota2((1, 1, 2 * half), 2) < half
    return jnp.concatenate([jnp.where(lo, x, 0.0), jnp.where(lo, 0.0, x)], axis=1)


def _unit_lower_inverse(a_strict, c):
    row = _iota2((1, c, 2 * c), 1)
    col = _iota2((1, c, 2 * c), 2) & (c - 1)
    eye = (row == col).astype(F32)
    t = None
    m = 1
    while m < c:
        mask = ((row // (2 * m)) == (col // (2 * m))) & ((row & m) != 0) & ((col & m) == 0)
        lm = jnp.where(mask, a_strict, 0.0)
        if t is None:
            t = eye - lm
        else:
            t = t - _bdot(_bdot(t, _block_diag(lm, c), BNN), _block_diag(t, c), BNN)
            yield
        m *= 2
    return t


def _rwkv_kernel(p_ref, mu_ref, w0_ref, w2_ref, a0_ref, a2_ref, g2_ref, kk_ref, ka_ref, rk_ref,
                 lnw_ref, lnb_ref, o_ref, carry_ref, st_ref, *, chunk, n_chunks, n_groups, heads, hd):
    c = chunk
    step_rows = n_groups * n_chunks * c

    @pl.when(pl.program_id(1) == 0)
    def _():
        carry_ref[...] = jnp.zeros_like(carry_ref)
        st_ref[...] = jnp.zeros_like(st_ref)

    p = p_ref[0]
    row = _iota2((step_rows, 1), 0)
    prev = jnp.where(row == 0, carry_ref[...], pltpu.roll(p, 1, axis=0))
    carry_ref[...] = p[step_rows - 1:step_rows, :]
    xs_all = p + (prev - p) * mu_ref[...]

    rows = n_chunks * c
    front_refs = (w0_ref, w2_ref, a0_ref, a2_ref, g2_ref, kk_ref, ka_ref)
    dims = dict(c=c, n_chunks=n_chunks, heads=heads, hd=hd)
    front = lambda gi: _rwkv_front(xs_all[gi * rows:(gi + 1) * rows], *front_refs, **dims)
    s = st_ref[...]
    (ready,) = _interleave(front(0))
    for gi in range(n_groups):
        back = _rwkv_back(ready, s, rk_ref, lnw_ref, lnb_ref, **dims)
        if gi + 1 < n_groups:
            (out, s), ready = _interleave(back, front(gi + 1))
        else:
            ((out, s),) = _interleave(back)
        o_ref[0, gi * rows:(gi + 1) * rows, :] = out
    st_ref[...] = s


def _head_sum(t, hd):
    tile_ones = _block_ones(2 * hd, hd)
    return jnp.concatenate([_bdot(t[:, i:i + 2 * hd], tile_ones, NN) for i in range(0, t.shape[1], 2 * hd)], axis=1)


def _rwkv_front(xs, w0_ref, w2_ref, a0_ref, a2_ref, g2_ref, kk_ref, ka_ref, *, c, n_chunks, heads, hd):
    rows = n_chunks * c
    dim = heads * hd
    r = xs[:, 0:dim]
    k = xs[:, dim:2 * dim]
    v = xs[:, 2 * dim:3 * dim]
    o1 = 3 * dim
    w_lr = xs[:, o1:o1 + LORA_W]
    a_lr = xs[:, o1 + LORA_W:o1 + LORA_W + LORA_A]
    g_lr = xs[:, o1 + LORA_W + LORA_A:o1 + LORA_W + LORA_A + LORA_G]

    z = w0_ref[...] + _mm(jnp.tanh(w_lr), w2_ref[...])
    softplus = jnp.maximum(-z, 0.0) + jnp.log1p(jnp.exp(-jnp.abs(z)))
    w_raw = -softplus - 0.5
    lw = -jnp.exp(w_raw)
    yield
    a = jax.nn.sigmoid(a0_ref[...] + _mm(a_lr, a2_ref[...]))
    g = _mm(jax.nn.sigmoid(g_lr), g2_ref[...])
    yield
    kk = k * kk_ref[...]
    kk = kk * lax.rsqrt(jnp.maximum(_head_sum(kk * kk, hd), 1e-24))
    k2 = k * (1.0 + (a - 1.0) * ka_ref[...])
    bb = kk * a
    yield

    r_i, c_i = _iota2((rows, rows), 0), _iota2((rows, rows), 1)
    tri_incl = (r_i >= c_i) & (r_i // c == c_i // c)
    cum = _mm((tri_incl.astype(BF16),), _pieces(lw, 2))
    ends = [cum[(i + 1) * c - 1:(i + 1) * c, :] for i in range(n_chunks)]
    cum_last = jnp.concatenate([jnp.broadcast_to(e, (c, dim)) for e in ends], axis=0)
    yield
    w_incl = jnp.exp(cum)
    w_excl = jnp.exp(cum - lw)
    yield
    w_inv = jnp.exp(-cum)
    w_tail = jnp.exp(cum_last - cum)
    w_all = jnp.exp(jnp.concatenate(ends, axis=0))
    yield
    ops = dict(kt=kk * w_excl, rt=r * w_incl, bt=bb * w_inv)
    yield
    ops.update(kd=k2 * w_inv, bw=bb * w_tail, kw=k2 * w_tail)
    yield
    ops.update(v=v, w_all=w_all, r=r, k2=k2, g=g)
    return ops


def _rwkv_back(ops, s, rk_ref, lnw_ref, lnb_ref, *, c, n_chunks, heads, hd):
    pw = 2 * hd
    pairs = heads // 2
    def by_pair(t, n_rows=c):
        return jnp.stack([t[i * n_rows:(i + 1) * n_rows, j * pw:(j + 1) * pw]
                          for i in range(n_chunks) for j in range(pairs)], axis=0)
    bd_f = lambda t: _block_diag(t, hd)
    kt_p, rt_p, bt_p, kd_p, kw_p, bw_p, v_p = (by_pair(ops[n]) for n in ("kt", "rt", "bt", "kd", "kw", "bw", "v"))
    w_all_p = by_pair(ops["w_all"], 1)
    row_t = _iota2((1, c, 2 * c), 1)
    col_t = _iota2((1, c, 2 * c), 2) & (c - 1)
    strict_p, incl_p = row_t > col_t, row_t >= col_t
    same_head = (_iota2((1, pw, pw), 1) // hd) == (_iota2((1, pw, pw), 2) // hd)

    kr = jnp.concatenate([kt_p, rt_p], axis=1)
    g_all = _bdot(kr, jnp.concatenate([bd_f(bt_p), bd_f(kd_p)], axis=1), BNT)
    yield
    gb, gk = g_all[:, :, :2 * c], g_all[:, :, 2 * c:]
    a_b = jnp.where(strict_p, gb[:, :c], 0.0)
    a_k = jnp.where(strict_p, gk[:, :c], 0.0)
    p_b = jnp.where(incl_p, gb[:, c:], 0.0)
    p_k = jnp.where(incl_p, gk[:, c:], 0.0)
    akpk = _bdot(jnp.concatenate([a_k, p_k], axis=1), bd_f(v_p), BNN)
    yield
    t_inv = yield from _unit_lower_inverse(a_b, c)
    kv1 = _bdot(t_inv, jnp.concatenate([bd_f(kt_p), bd_f(akpk[:, :c])], axis=2), BNN)
    yield
    kt1, v1 = kv1[:, :, :pw], kv1[:, :, pw:]
    pbk = _bdot(p_b, jnp.concatenate([bd_f(kt1), bd_f(v1)], axis=2), BNN)
    yield
    q_mat = rt_p - pbk[:, :, :pw]
    z_mat = akpk[:, c:] - pbk[:, :, pw:]
    x_mat = jnp.where(same_head, _bdot(bw_p, kt1, BTN), 0.0)
    yield
    n_mat = jnp.where(same_head, _bdot(jnp.concatenate([v_p, v1], axis=1),
                                       jnp.concatenate([kw_p, -bw_p], axis=1), BTN), 0.0)
    yield
    y_rows = []
    for i in range(n_chunks):
        sl = slice(i * pairs, (i + 1) * pairs)
        y_p = _bdot(q_mat[sl], s, BNT) + z_mat[sl]
        s = s * w_all_p[sl] - _bdot(s, x_mat[sl], BNT) + n_mat[sl]
        y_rows.append(jnp.concatenate([y_p[j] for j in range(pairs)], axis=1))
        yield
    y = jnp.concatenate(y_rows, axis=0)

    inv_hd = 1.0 / hd
    mean = _head_sum(y, hd) * inv_hd
    yc = y - mean
    var = _head_sum(yc * yc, hd) * inv_hd
    yield
    yn = yc * lax.rsqrt(var + LNX_EPS) * lnw_ref[...] + lnb_ref[...]
    bonus = _head_sum(ops["r"] * ops["k2"] * rk_ref[...], hd) * ops["v"]
    return (yn + bonus) * ops["g"], s


def _rwkv(p, mu, w0, w2, a0, a2, g2, k_k, k_a, r_k, lnx_w, lnx_b, *, chunk=RWKV_CHUNK,
          n_chunks=RWKV_CHUNKS_PER_GROUP, n_groups=RWKV_GROUPS_PER_STEP):
    b, s, cols = p.shape
    heads, hd = RWKV_HEADS, RWKV_HD
    dim = heads * hd
    chunk = min(chunk, s)
    n_chunks = min(n_chunks, s // chunk)
    n_groups = min(n_groups, s // (chunk * n_chunks))
    rows = chunk * n_chunks * n_groups
    row = lambda t: t.reshape(1, -1)
    vecs = [row(mu), row(w0), w2, row(a0), a2, g2, row(k_k), row(k_a), row(r_k), row(lnx_w), row(lnx_b)]
    return pl.pallas_call(
        functools.partial(_rwkv_kernel, chunk=chunk, n_chunks=n_chunks, n_groups=n_groups, heads=heads, hd=hd),
        grid=(b, s // rows),
        in_specs=[pl.BlockSpec((1, rows, cols), lambda i, j: (i, j, 0))]
        + [pl.BlockSpec(t.shape, lambda i, j: (0, 0)) for t in vecs],
        out_specs=pl.BlockSpec((1, rows, dim), lambda i, j: (i, j, 0)),
        out_shape=jax.ShapeDtypeStruct((b, s, dim), F32),
        scratch_shapes=[pltpu.VMEM((1, cols), F32), pltpu.VMEM((heads // 2, 2 * hd, 2 * hd), F32)],
        compiler_params=_params("parallel", "arbitrary"),
        name="rwkv",
    )(p, *vecs)


def _moba_kernel(q_ref, k_ref, v_ref, o_ref, km_ref, ka_ref, vat_ref, *, nb, blk, n_sel, heads, hd):
    j = pl.program_id(1)
    nbp = km_ref.shape[0]
    scale = hd ** -0.5
    masked = -1e30
    neg_inf = float("-inf")
    slopes = [2.0 ** (-8.0 * (h + 1) / heads) for h in range(heads)]

    @pl.when(j == 0)
    def _():
        km_ref[...] = jnp.zeros_like(km_ref)
        for n in range(nb):
            km_ref[n:n + 1, :] = jnp.mean(k_ref[0, n * blk:(n + 1) * blk, :], axis=0, keepdims=True)
        col = _iota2((blk, 2 * hd), 0).astype(F32)
        lane = _iota2((blk, 2 * hd), 1)
        feat_lane = lane & (hd - 1)
        for n in range(nb):
            rows = slice(n * blk, (n + 1) * blk)
            one_hot = jnp.where(feat_lane == n, 1.0, 0.0)
            for h in range(heads):
                k_feat = (one_hot + jnp.where(feat_lane == nbp, slopes[h] * col, 0.0)
                          + jnp.where(feat_lane == nbp + 1, slopes[h] * blk * n, 0.0))
                k_tile = k_ref[0, rows, (h // 2) * 2 * hd:(h // 2 + 1) * 2 * hd]
                own_lanes = (lane < hd) if h % 2 == 0 else (lane >= hd)
                ka_ref[h, rows, :] = jnp.where(own_lanes, k_tile, k_feat).astype(BF16)
        ones_row = jnp.where(_iota2((vat_ref.shape[2] - hd, blk), 0) == 0, 1.0, 0.0)
        for n in range(nb):
            vt = v_ref[0, n * blk:(n + 1) * blk, :].T
            for h in range(heads):
                vat_ref[n, h] = jnp.concatenate([vt[h * hd:(h + 1) * hd], ones_row], axis=0).astype(BF16)

    qt = (q_ref[0] * scale).T
    sub = _iota2((nbp, blk), 0)
    const_rows = jnp.where(_iota2((hd - nbp, blk), 0) < 2, 1.0, 0.0)
    q_aug = []
    for h in range(heads):
        sl = slice(h * hd, (h + 1) * hd)
        qh = qt[sl]
        gate = _mm(km_ref[:, sl], qh)
        gate = jnp.where(sub < j, gate, neg_inf)
        cnt = jnp.zeros((nbp, blk), jnp.int32)
        for m in range(nb):
            other = gate[m:m + 1, :]
            beats = (other > gate) | ((other == gate) & (m < sub))
            cnt = cnt + beats.astype(jnp.int32)
        keep = ((sub < j) & (cnt < n_sel)) | (sub == j) | (sub >= nb)
        bias = jnp.where(keep, 0.0, masked)
        parts = [qh, bias, const_rows] if h % 2 == 0 else [bias, const_rows, qh]
        q_aug.append(jnp.concatenate(parts, axis=0).astype(BF16))
    q_aug = jnp.stack(q_aug, axis=0)

    def scores(n):
        start = pl.multiple_of(n * blk, blk)
        kn = ka_ref[:, pl.ds(start, blk), :]
        return lax.dot_general(kn, q_aug, BNN, preferred_element_type=F32), vat_ref[n]

    s, vj = scores(j)
    causal = _iota2((1, blk, blk), 1) <= _iota2((1, blk, blk), 2)
    s = jnp.where(causal, s, masked)
    m0 = jnp.max(s, axis=1, keepdims=True)
    acc0 = lax.dot_general(vj, jnp.exp(s - m0).astype(BF16), BNN, preferred_element_type=F32)

    def past_block(n, carry):
        m_run, acc = carry
        sc, vn = scores(n)
        m_new = jnp.maximum(m_run, jnp.max(sc, axis=1, keepdims=True))
        pr = jnp.exp(sc - m_new).astype(BF16)
        acc_new = jnp.exp(m_run - m_new) * acc + lax.dot_general(vn, pr, BNN, preferred_element_type=F32)
        return m_new, acc_new

    _, acc = lax.fori_loop(0, j, past_block, (m0, acc0))
    out_t = jnp.concatenate([acc[h, :hd] / acc[h, hd:hd + 1] for h in range(heads)], axis=0)
    o_ref[0] = out_t.T


def _moba(q, k, v):
    b, s, dim = q.shape
    heads, hd, blk = MOBA_HEADS, MOBA_HD, MOBA_BLOCK
    assert s % blk == 0
    nb = s // blk
    nbp = -(-nb // 8) * 8
    assert nbp + 2 <= hd
    n_sel = min(MOBA_TOPK, nb - 1)
    return pl.pallas_call(
        functools.partial(_moba_kernel, nb=nb, blk=blk, n_sel=n_sel, heads=heads, hd=hd),
        grid=(b, nb),
        in_specs=[
            pl.BlockSpec((1, blk, dim), lambda i, j: (i, j, 0)),
            pl.BlockSpec((1, s, dim), lambda i, j: (i, 0, 0)),
            pl.BlockSpec((1, s, dim), lambda i, j: (i, 0, 0)),
        ],
        out_specs=pl.BlockSpec((1, blk, dim), lambda i, j: (i, j, 0)),
        out_shape=jax.ShapeDtypeStruct((b, s, dim), F32),
        scratch_shapes=[pltpu.VMEM((nbp, dim), F32),
                        pltpu.VMEM((heads, s, 2 * hd), BF16),
                        pltpu.VMEM((nb, heads, hd + 16, blk), BF16)],
        compiler_params=_params("parallel", "arbitrary"),
        name="moba",
    )(q, k, v)


def _hgrn_kernel(q_ref, f_ref, i_ref, g_ref, lbl_ref, nw_ref, o_ref, st_ref, *, chunk, n_chunks, n_groups, heads,
                 dk, layer):
    c = chunk

    @pl.when(pl.program_id(1) == 0)
    def _():
        st_ref[...] = jnp.zeros_like(st_ref)

    logits = lbl_ref[...]
    e = jnp.exp(logits - jnp.max(logits, axis=0, keepdims=True))
    sm = e / jnp.sum(e, axis=0, keepdims=True)
    lb = jnp.sum(sm[0:layer + 1, :], axis=0, keepdims=True) - sm[0:1, :]

    rows = n_chunks * c
    dims = dict(c=c, n_chunks=n_chunks, heads=heads, dk=dk)
    grp = lambda ref, gi: ref[0, gi * rows:(gi + 1) * rows, :]
    front = lambda gi: _hgrn_front(grp(q_ref, gi), grp(f_ref, gi), grp(i_ref, gi), lb, **dims)
    s = st_ref[...]
    (ready,) = _interleave(front(0))
    for gi in range(n_groups):
        back = _hgrn_back(ready, s, grp(g_ref, gi), nw_ref[...], **dims)
        if gi + 1 < n_groups:
            (out, s), ready = _interleave(back, front(gi + 1))
        else:
            ((out, s),) = _interleave(back)
        o_ref[0, gi * rows:(gi + 1) * rows, :] = out
    st_ref[...] = s


def _hgrn_front(q, fr, v, lb, *, c, n_chunks, heads, dk):
    rows = n_chunks * c
    dim = heads * dk
    sig = jax.nn.sigmoid(fr)
    lf = jnp.log(lb + (1.0 - lb) * sig)
    kf = (1.0 - lb) * (1.0 - sig)
    yield
    r_i, c_i = _iota2((rows, rows), 0), _iota2((rows, rows), 1)
    tri_all = (r_i >= c_i) & (r_i // c == c_i // c)
    b = _mm((tri_all.astype(BF16),), _pieces(lf, 2))
    per_chunk = lambda r: jnp.concatenate(
        [jnp.broadcast_to(b[i * c + r:i * c + r + 1, :], (c, dim)) for i in range(n_chunks)], axis=0)
    ends = [b[(i + 1) * c - 1:(i + 1) * c, :] for i in range(n_chunks)]
    b_last = per_chunk(c - 1)
    yield
    b_mid = per_chunk(c // 2 - 1)
    rel = b - b_mid
    qd = q * jnp.exp(rel)
    yield
    kd = kf * jnp.exp(-rel)
    yield
    q_in = qd * jnp.exp(b_mid)
    yield
    kw = kf * jnp.exp(b_last - b)
    w_all = jnp.exp(jnp.concatenate(ends, axis=0))
    return dict(qd=qd, kd=kd, q_in=q_in, kw=kw, w_all=w_all, v=v)


def _hgrn_back(ops, s, g, norm_w, *, c, n_chunks, heads, dk):
    def by_head(t, n_rows=c):
        return jnp.stack([t[i * n_rows:(i + 1) * n_rows, h * dk:(h + 1) * dk]
                          for i in range(n_chunks) for h in range(heads)], axis=0)
    hp, pw = heads // 2, 2 * dk
    by_pair = lambda t: jnp.stack([t[i * c:(i + 1) * c, j * pw:(j + 1) * pw]
                                   for i in range(n_chunks) for j in range(hp)], axis=0)
    tri_pair = _iota2((1, c, 2 * c), 1) >= (_iota2((1, c, 2 * c), 2) & (c - 1))
    sc = jnp.where(tri_pair, _mm(by_pair(ops["qd"]), _block_diag(by_pair(ops["kd"]), dk), BNT), 0.0)
    yield
    o_pair = _bdot(sc, _block_diag(by_pair(ops["v"]), dk), BNN)
    yield
    qd_h, v_h = by_head(ops["q_in"]), by_head(ops["v"])
    s_add = _bdot(v_h, by_head(ops["kw"]), BTN)
    yield
    w_all_h = by_head(ops["w_all"], 1)
    o_rows = []
    for i in range(n_chunks):
        sl = slice(i * heads, (i + 1) * heads)
        o_intra = jnp.stack([o_pair[i * hp + h // 2][:, (h % 2) * dk:(h % 2 + 1) * dk]
                             for h in range(heads)], axis=0)
        o_h = o_intra + _bdot(qd_h[sl], s, BNT)
        s = s * w_all_h[sl] + s_add[sl]
        o_h = o_h * lax.rsqrt(jnp.mean(o_h * o_h, axis=-1, keepdims=True) + EPS)
        o_rows.append(jnp.concatenate([o_h[h] for h in range(heads)], axis=1))
        yield
    o = jnp.concatenate(o_rows, axis=0)
    return o * norm_w * jax.nn.sigmoid(g), s


def _hgrn(p, lb_logits, norm_w, *, layer, chunk=HG_CHUNK, n_chunks=HG_CHUNKS_PER_GROUP,
          n_groups=HG_GROUPS_PER_STEP):
    b, s, cols = p.shape
    heads, dk = HG_HEADS, HG_DK
    dim = heads * dk
    assert cols == 4 * dim
    chunk = min(chunk, s)
    n_chunks = min(n_chunks, s // chunk)
    n_groups = min(n_groups, s // (chunk * n_chunks))
    rows = chunk * n_chunks * n_groups
    col_block = lambda n: pl.BlockSpec((1, rows, dim), lambda i, j, n=n: (i, j, n))
    return pl.pallas_call(
        functools.partial(_hgrn_kernel, chunk=chunk, n_chunks=n_chunks, n_groups=n_groups, heads=heads, dk=dk,
                          layer=layer),
        grid=(b, s // rows),
        in_specs=[col_block(0), col_block(1), col_block(2), col_block(3),
                  pl.BlockSpec(lb_logits.shape, lambda i, j: (0, 0)),
                  pl.BlockSpec((1, dim), lambda i, j: (0, 0))],
        out_specs=pl.BlockSpec((1, rows, dim), lambda i, j: (i, j, 0)),
        out_shape=jax.ShapeDtypeStruct((b, s, dim), F32),
        scratch_shapes=[pltpu.VMEM((heads, dk, dk), F32)],
        compiler_params=_params("parallel", "arbitrary"),
        name="hgrn",
    )(p, p, p, p, lb_logits, norm_w.reshape(1, dim))


def kernel(x, norm_g, ffn1_wg, ffn1_wu, ffn1_wd, ffn2_wg, ffn2_wu, ffn2_wd, ev_w_in, ev_w_out, rw_mu, rw_w0, rw_w2, rw_a0, rw_a2, rw_g2, rw_k_k, rw_k_a, rw_r_k, rw_lnx_w, rw_lnx_b, od_w_in, od_w_out, hg_norm_w, hg_lb_logits, final_g):
    bsz, seq, d = x.shape
    depth = norm_g.shape[0]
    rwkv_dim = RWKV_HEADS * RWKV_HD
    rwkv_cols = 3 * rwkv_dim + LORA_W + LORA_A + LORA_G
    moba_dim = MOBA_HEADS * MOBA_HD
    xf = x.reshape(bsz * seq, d)
    for l in range(depth):
        xf = _ffn(xf, [], None, 0, norm_g[l, 0], ffn1_wg, ffn1_wu, ffn1_wd, l, final_g, final_norm=False)
        if l % 2 == 0:
            e = l // 2
            p_r, q, k, v = _inproj(xf, norm_g[l, 1], ev_w_in, e, [rwkv_cols, moba_dim, moba_dim, moba_dim])
            y_a = _rwkv(p_r.reshape(bsz, seq, rwkv_cols), rw_mu[e], rw_w0[e], rw_w2[e], rw_a0[e], rw_a2[e],
                        rw_g2[e], rw_k_k[e], rw_k_a[e], rw_r_k[e], rw_lnx_w[e], rw_lnx_b[e])
            y_b = _moba(q.reshape(bsz, seq, moba_dim), k.reshape(bsz, seq, moba_dim),
                        v.reshape(bsz, seq, moba_dim))
            ys, wo, wo_layer = [y_a.reshape(-1, rwkv_dim), y_b.reshape(-1, moba_dim)], ev_w_out, e
        else:
            o = l // 2
            (p,) = _inproj(xf, norm_g[l, 1], od_w_in, o, [od_w_in.shape[2]])
            y = _hgrn(p.reshape(bsz, seq, -1), hg_lb_logits, hg_norm_w[o], layer=l)
            ys, wo, wo_layer = [y.reshape(bsz * seq, -1)], od_w_out, o
        xf = _ffn(xf, ys, wo, wo_layer, norm_g[l, 2], ffn2_wg, ffn2_wu, ffn2_wd, l, final_g,
                  final_norm=(l == depth - 1))
    return xf.reshape(bsz, seq, d)
```

```python
import functools

import jax
import jax.numpy as jnp
from jax import lax
from jax.experimental import pallas as pl
from jax.experimental.pallas import tpu as pltpu

F32 = jnp.float32
BF16 = jnp.bfloat16
NN = (((1,), (0,)), ((), ()))
BNN = (((2,), (1,)), ((0,), (0,)))
BNT = (((2,), (2,)), ((0,), (0,)))
BTN = (((1,), (1,)), ((0,), (0,)))
MIX_PIECES = 2

EPS = 1e-6
LNX_EPS = 64e-5
RWKV_HEADS = 8
RWKV_HD = 64
LORA_W = 64
LORA_A = 64
LORA_G = 128
MOBA_HEADS = 8
MOBA_HD = 64
MOBA_BLOCK = 256
MOBA_TOPK = 3
HG_HEADS = 8
HG_DK = 128
RWKV_CHUNK = 64
RWKV_CHUNKS_PER_GROUP = 4
RWKV_GROUPS_PER_STEP = 2
HG_CHUNK = 64
HG_CHUNKS_PER_GROUP = 4
HG_GROUPS_PER_STEP = 2
VMEM_LIMIT_BYTES = 56 * 1024 * 1024


def _params(*semantics):
    return pltpu.CompilerParams(dimension_semantics=semantics, vmem_limit_bytes=VMEM_LIMIT_BYTES)


def _dot(a, b):
    return jnp.dot(a, b, preferred_element_type=F32)


def _pieces(a, n):
    if isinstance(a, tuple):
        return a
    out = []
    for i in range(n):
        hi = a.astype(BF16)
        out.append(hi)
        if i + 1 < n:
            a = a - hi.astype(F32)
    return tuple(out)


def _mm(a, b, dims=NN, n=MIX_PIECES):
    a = _pieces(a, n)
    b = _pieces(b, n)
    order = max(len(a), len(b)) - 1
    out = None
    for i, ai in enumerate(a):
        for j, bj in enumerate(b):
            if i + j <= order:
                t = lax.dot_general(ai, bj, dims, preferred_element_type=F32)
                out = t if out is None else out + t
    return out


def _mm_fused(a, b):
    a_hi, a_lo = _pieces(a, 2)
    b_hi, b_lo = _pieces(b, 2)
    return _dot(jnp.concatenate([a_hi, a_hi, a_lo], axis=1), jnp.concatenate([b_hi, b_lo, b_hi], axis=0))


def _cumsum_matmul(mask, x):
    m = mask.astype(BF16)
    hi, lo = _pieces(x, 2)
    return _dot(jnp.concatenate([m, m], axis=1), jnp.concatenate([hi, lo], axis=0))


def _rmsnorm(x, g):
    return x * lax.rsqrt(jnp.mean(x * x, axis=-1, keepdims=True) + EPS) * g


def _iota2(shape, dim):
    return lax.broadcasted_iota(jnp.int32, shape, dim)


def _block_ones(n, width):
    return (_iota2((n, n), 0) // width == _iota2((n, n), 1) // width).astype(F32)


def _interleave(*gens):
    results = [None] * len(gens)
    live = list(range(len(gens)))
    while live:
        for i in list(live):
            try:
                next(gens[i])
            except StopIteration as stop:
                results[i] = stop.value
                live.remove(i)
    return results


def _load_weight(w_hbm, layer, dst_ref, stage_ref, sem_ref, slab_rows):
    n_rows = dst_ref.shape[0]
    assert n_rows % slab_rows == 0 and slab_rows <= stage_ref.shape[1]
    n_slabs = n_rows // slab_rows

    def copy(s):
        return pltpu.make_async_copy(w_hbm.at[layer, pl.ds(s * slab_rows, slab_rows), :],
                                     stage_ref.at[s % 2, pl.ds(0, slab_rows), :], sem_ref.at[s % 2])

    copy(0).start()
    for s in range(n_slabs):
        if s + 1 < n_slabs:
            copy(s + 1).start()
        copy(s).wait()
        dst_ref[pl.ds(s * slab_rows, slab_rows), :] = stage_ref[s % 2, pl.ds(0, slab_rows), :].astype(BF16)


def _ffn_kernel(x_ref, *refs, n_y, layer, out_layer, final_norm, tf):
    y_refs = refs[:n_y]
    has_out = n_y > 0
    (wo_hbm,) = refs[n_y:n_y + 1] if has_out else (None,)
    g_ref, wg_hbm, wu_hbm, wd_hbm, fg_ref, o_ref = refs[n_y + has_out:n_y + has_out + 6]
    scratch = refs[n_y + has_out + 6:]
    wg_ref, wu_ref, wd_ref, wide_stage, tall_stage, sem, wo_sem = scratch[:7]
    wo_ref = scratch[7] if has_out else None
    n_slabs = wg_ref.shape[1] // tf

    def slab_copies(j):
        slot, cols = j % 2, pl.ds(j * tf, tf)
        return (pltpu.make_async_copy(wg_hbm.at[layer, :, cols], wide_stage.at[slot, 0], sem.at[slot, 0]),
                pltpu.make_async_copy(wu_hbm.at[layer, :, cols], wide_stage.at[slot, 1], sem.at[slot, 1]),
                pltpu.make_async_copy(wd_hbm.at[layer, cols, :], tall_stage.at[slot], sem.at[slot, 2]))

    def body(first_step):
        if first_step:
            for j in range(min(2, n_slabs)):
                for cp in slab_copies(j):
                    cp.start()
            if has_out:
                _load_weight(wo_hbm, out_layer, wo_ref, wo_ref_stage, wo_sem, wo_ref.shape[0] // 4)
        x = x_ref[...]
        row0 = 0
        for y_ref in y_refs:
            rows = y_ref.shape[1]
            x = x + _dot(y_ref[...].astype(BF16), wo_ref[row0:row0 + rows, :])
            row0 += rows
        h = _rmsnorm(x, g_ref[...]).astype(BF16)
        acc = None
        for j in range(n_slabs):
            c0 = j * tf
            if first_step:
                for cp in slab_copies(j):
                    cp.wait()
                wg_ref[:, c0:c0 + tf] = wide_stage[j % 2, 0].astype(BF16)
                wu_ref[:, c0:c0 + tf] = wide_stage[j % 2, 1].astype(BF16)
                wd_ref[c0:c0 + tf, :] = tall_stage[j % 2].astype(BF16)
                if j + 2 < n_slabs:
                    for cp in slab_copies(j + 2):
                        cp.start()
            gate = _dot(h, wg_ref[:, c0:c0 + tf])
            up = _dot(h, wu_ref[:, c0:c0 + tf])
            act = (gate * jax.nn.sigmoid(gate) * up).astype(BF16)
            part = _dot(act, wd_ref[c0:c0 + tf, :])
            acc = part if acc is None else acc + part
        out = x + 0.5 * acc
        if final_norm:
            out = _rmsnorm(out, fg_ref[...])
        o_ref[...] = out

    wo_ref_stage = scratch[8] if has_out else None
    pl.when(pl.program_id(0) == 0)(lambda: body(True))
    pl.when(pl.program_id(0) != 0)(lambda: body(False))


def _resident(shape):
    return pl.BlockSpec(shape, lambda i: (0,) * len(shape), pipeline_mode=pl.Buffered(1))


def _ffn(x, ys, wo, out_layer, g, wg, wu, wd, layer, final_g, *, final_norm, tm=512, tf=256):
    m, d = x.shape
    f = wg.shape[2]
    tm = min(tm, m)
    assert f % tf == 0
    hbm = pl.BlockSpec(memory_space=pl.ANY)
    has_out = len(ys) > 0
    scratch = [pltpu.VMEM((d, f), BF16), pltpu.VMEM((d, f), BF16), pltpu.VMEM((f, d), BF16),
               pltpu.VMEM((2, 2, d, tf), F32), pltpu.VMEM((2, tf, d), F32),
               pltpu.SemaphoreType.DMA((2, 3)), pltpu.SemaphoreType.DMA((2,))]
    if has_out:
        assert sum(y.shape[1] for y in ys) == wo.shape[1] and wo.shape[1] % 4 == 0
        scratch += [pltpu.VMEM(wo.shape[1:], BF16), pltpu.VMEM((2, wo.shape[1] // 4, wo.shape[2]), F32)]
    return pl.pallas_call(
        functools.partial(_ffn_kernel, n_y=len(ys), layer=layer, out_layer=out_layer, final_norm=final_norm,
                          tf=tf),
        grid=(m // tm,),
        in_specs=[pl.BlockSpec((tm, d), lambda i: (i, 0))]
        + [pl.BlockSpec((tm, y.shape[1]), lambda i: (i, 0)) for y in ys]
        + ([hbm] if has_out else [])
        + [_resident((1, d)), hbm, hbm, hbm, _resident((1, d))],
        out_specs=pl.BlockSpec((tm, d), lambda i: (i, 0)),
        out_shape=jax.ShapeDtypeStruct((m, d), F32),
        scratch_shapes=scratch,
        compiler_params=_params("arbitrary"),
        name="ffn",
    )(x, *ys, *([wo] if has_out else []), g.reshape(1, d), wg, wu, wd, final_g.reshape(1, d))


def _inproj_kernel(x_ref, g_ref, w_hbm, *refs, layer, tn):
    o_refs = refs[:-3]
    w_ref, stage, sem = refs[-3:]

    @pl.when(pl.program_id(0) == 0)
    def _():
        _load_weight(w_hbm, layer, w_ref, stage, sem, stage.shape[1])

    h = _rmsnorm(x_ref[...], g_ref[...]).astype(BF16)
    col0 = 0
    for o_ref in o_refs:
        n = o_ref.shape[1]
        for c0 in range(0, n, tn):
            c1 = min(c0 + tn, n)
            o_ref[:, c0:c1] = _dot(h, w_ref[:, col0 + c0:col0 + c1])
        col0 += n


def _inproj(x, g, w, layer, widths, *, tm=512, tn=512, n_slabs=8):
    m, d = x.shape
    n_total = w.shape[2]
    assert sum(widths) == n_total and d % n_slabs == 0
    tm = min(tm, m)
    return pl.pallas_call(
        functools.partial(_inproj_kernel, layer=layer, tn=tn),
        grid=(m // tm,),
        in_specs=[pl.BlockSpec((tm, d), lambda i: (i, 0)), _resident((1, d)), pl.BlockSpec(memory_space=pl.ANY)],
        out_specs=[pl.BlockSpec((tm, n), lambda i: (i, 0)) for n in widths],
        out_shape=[jax.ShapeDtypeStruct((m, n), F32) for n in widths],
        scratch_shapes=[pltpu.VMEM((d, n_total), BF16), pltpu.VMEM((2, d // n_slabs, n_total), F32),
                        pltpu.SemaphoreType.DMA((2,))],
        compiler_params=_params("arbitrary"),
        name="inproj",
    )(x, g.reshape(1, d), w)


def _bdot(a, b, dims):
    return lax.dot_general(a.astype(BF16), b.astype(BF16), dims, preferred_element_type=F32)


def _block_diag(x, half):
    lo = _iota2((1, 1, 2 * half), 2) < half
    return jnp.concatenate([jnp.where(lo, x, 0.0), jnp.where(lo, 0.0, x)], axis=1)


def _unit_lower_inverse(a_strict, c):
    row = _iota2((1, c, 2 * c), 1)
    col = _iota2((1, c, 2 * c), 2) & (c - 1)
    eye = (row == col).astype(F32)
    t = None
    m = 1
    while m < c:
        mask = ((row // (2 * m)) == (col // (2 * m))) & ((row & m) != 0) & ((col & m) == 0)
        lm = jnp.where(mask, a_strict, 0.0)
        if t is None:
            t = eye - lm
        else:
            t = t - _bdot(_bdot(t, _block_diag(lm, c), BNN), _block_diag(t, c), BNN)
            yield
        m *= 2
    return t


def _rwkv_kernel(p_ref, mu_ref, w0_ref, w2_ref, a0_ref, a2_ref, g2_ref, kk_ref, ka_ref, rk_ref,
                 lnw_ref, lnb_ref, o_ref, carry_ref, st_ref, *, chunk, n_chunks, n_groups, heads, hd):
    c = chunk
    step_rows = n_groups * n_chunks * c

    @pl.when(pl.program_id(1) == 0)
    def _():
        carry_ref[...] = jnp.zeros_like(carry_ref)
        st_ref[...] = jnp.zeros_like(st_ref)

    p = p_ref[0]
    row = _iota2((step_rows, 1), 0)
    prev = jnp.where(row == 0, carry_ref[...], pltpu.roll(p, 1, axis=0))
    carry_ref[...] = p[step_rows - 1:step_rows, :]
    xs_all = p + (prev - p) * mu_ref[...]

    rows = n_chunks * c
    front_refs = (w0_ref, w2_ref, a0_ref, a2_ref, g2_ref, kk_ref, ka_ref)
    dims = dict(c=c, n_chunks=n_chunks, heads=heads, hd=hd)
    front = lambda gi: _rwkv_front(xs_all[gi * rows:(gi + 1) * rows], *front_refs, **dims)
    s = st_ref[...]
    (ready,) = _interleave(front(0))
    for gi in range(n_groups):
        back = _rwkv_back(ready, s, rk_ref, lnw_ref, lnb_ref, **dims)
        if gi + 1 < n_groups:
            (out, s), ready = _interleave(back, front(gi + 1))
        else:
            ((out, s),) = _interleave(back)
        o_ref[0, gi * rows:(gi + 1) * rows, :] = out
    st_ref[...] = s


def _head_sum(t, hd):
    tile_ones = _block_ones(2 * hd, hd)
    return jnp.concatenate([_bdot(t[:, i:i + 2 * hd], tile_ones, NN) for i in range(0, t.shape[1], 2 * hd)], axis=1)


def _rwkv_front(xs, w0_ref, w2_ref, a0_ref, a2_ref, g2_ref, kk_ref, ka_ref, *, c, n_chunks, heads, hd):
    rows = n_chunks * c
    dim = heads * hd
    r = xs[:, 0:dim]
    k = xs[:, dim:2 * dim]
    v = xs[:, 2 * dim:3 * dim]
    o1 = 3 * dim
    g_lr = xs[:, o1 + LORA_W + LORA_A:o1 + LORA_W + LORA_A + LORA_G]

    wa = xs[:, o1:o1 + LORA_W + LORA_A]
    wa = jnp.where(_iota2(wa.shape, 1) < LORA_W, jnp.tanh(wa), wa)
    w2a2 = jnp.concatenate(
        [jnp.concatenate([w2_ref[...], jnp.zeros((LORA_W, dim), F32)], axis=1),
         jnp.concatenate([jnp.zeros((LORA_A, dim), F32), a2_ref[...]], axis=1)], axis=0)
    za = _mm_fused(wa, w2a2)
    z = w0_ref[...] + za[:, :dim]
    softplus = jnp.maximum(-z, 0.0) + jnp.log1p(jnp.exp(-jnp.abs(z)))
    w_raw = -softplus - 0.5
    lw = -jnp.exp(w_raw)
    yield
    a = jax.nn.sigmoid(a0_ref[...] + za[:, dim:])
    g = _mm_fused(jax.nn.sigmoid(g_lr), g2_ref[...])
    yield
    kk = k * kk_ref[...]
    kk = kk * lax.rsqrt(jnp.maximum(_head_sum(kk * kk, hd), 1e-24))
    k2 = k * (1.0 + (a - 1.0) * ka_ref[...])
    bb = kk * a
    yield

    r_i, c_i = _iota2((rows, rows), 0), _iota2((rows, rows), 1)
    tri_incl = (r_i >= c_i) & (r_i // c == c_i // c)
    cum = _cumsum_matmul(tri_incl, lw)
    ends = [cum[(i + 1) * c - 1:(i + 1) * c, :] for i in range(n_chunks)]
    cum_last = jnp.concatenate([jnp.broadcast_to(e, (c, dim)) for e in ends], axis=0)
    yield
    w_incl = jnp.exp(cum)
    w_excl = jnp.exp(cum - lw)
    yield
    w_inv = jnp.exp(-cum)
    w_tail = jnp.exp(cum_last - cum)
    w_all = jnp.exp(jnp.concatenate(ends, axis=0))
    yield
    ops = dict(kt=kk * w_excl, rt=r * w_incl, bt=bb * w_inv)
    yield
    ops.update(kd=k2 * w_inv, bw=bb * w_tail, kw=k2 * w_tail)
    yield
    ops.update(v=v, w_all=w_all, r=r, k2=k2, g=g)
    return ops


def _rwkv_back(ops, s, rk_ref, lnw_ref, lnb_ref, *, c, n_chunks, heads, hd):
    pw = 2 * hd
    pairs = heads // 2
    def by_pair(t, n_rows=c):
        return jnp.stack([t[i * n_rows:(i + 1) * n_rows, j * pw:(j + 1) * pw]
                          for i in range(n_chunks) for j in range(pairs)], axis=0)
    bd_f = lambda t: _block_diag(t, hd)
    kt_p, rt_p, bt_p, kd_p, kw_p, bw_p, v_p = (by_pair(ops[n]) for n in ("kt", "rt", "bt", "kd", "kw", "bw", "v"))
    w_all_p = by_pair(ops["w_all"], 1)
    row_t = _iota2((1, c, 2 * c), 1)
    col_t = _iota2((1, c, 2 * c), 2) & (c - 1)
    strict_p, incl_p = row_t > col_t, row_t >= col_t
    same_head = (_iota2((1, pw, pw), 1) // hd) == (_iota2((1, pw, pw), 2) // hd)

    kr = jnp.concatenate([kt_p, rt_p], axis=1)
    g_all = _bdot(kr, jnp.concatenate([bd_f(bt_p), bd_f(kd_p)], axis=1), BNT)
    yield
    gb, gk = g_all[:, :, :2 * c], g_all[:, :, 2 * c:]
    a_b = jnp.where(strict_p, gb[:, :c], 0.0)
    a_k = jnp.where(strict_p, gk[:, :c], 0.0)
    p_b = jnp.where(incl_p, gb[:, c:], 0.0)
    p_k = jnp.where(incl_p, gk[:, c:], 0.0)
    akpk = _bdot(jnp.concatenate([a_k, p_k], axis=1), bd_f(v_p), BNN)
    yield
    t_inv = yield from _unit_lower_inverse(a_b, c)
    kv1 = _bdot(t_inv, jnp.concatenate([bd_f(kt_p), bd_f(akpk[:, :c])], axis=2), BNN)
    yield
    kt1, v1 = kv1[:, :, :pw], kv1[:, :, pw:]
    pbk = _bdot(p_b, jnp.concatenate([bd_f(kt1), bd_f(v1)], axis=2), BNN)
    yield
    q_mat = rt_p - pbk[:, :, :pw]
    z_mat = akpk[:, c:] - pbk[:, :, pw:]
    x_mat = jnp.where(same_head, _bdot(bw_p, kt1, BTN), 0.0)
    yield
    n_mat = jnp.where(same_head, _bdot(jnp.concatenate([v_p, v1], axis=1),
                                       jnp.concatenate([kw_p, -bw_p], axis=1), BTN), 0.0)
    yield
    y_rows = []
    for i in range(n_chunks):
        sl = slice(i * pairs, (i + 1) * pairs)
        y_p = _bdot(q_mat[sl], s, BNT) + z_mat[sl]
        s = s * w_all_p[sl] - _bdot(s, x_mat[sl], BNT) + n_mat[sl]
        y_rows.append(jnp.concatenate([y_p[j] for j in range(pairs)], axis=1))
        yield
    y = jnp.concatenate(y_rows, axis=0)

    inv_hd = 1.0 / hd
    mean = _head_sum(y, hd) * inv_hd
    yc = y - mean
    var = _head_sum(yc * yc, hd) * inv_hd
    yield
    yn = yc * lax.rsqrt(var + LNX_EPS) * lnw_ref[...] + lnb_ref[...]
    bonus = _head_sum(ops["r"] * ops["k2"] * rk_ref[...], hd) * ops["v"]
    return (yn + bonus) * ops["g"], s


def _rwkv(p, mu, w0, w2, a0, a2, g2, k_k, k_a, r_k, lnx_w, lnx_b, *, chunk=RWKV_CHUNK,
          n_chunks=RWKV_CHUNKS_PER_GROUP, n_groups=RWKV_GROUPS_PER_STEP):
    b, s, cols = p.shape
    heads, hd = RWKV_HEADS, RWKV_HD
    dim = heads * hd
    chunk = min(chunk, s)
    n_chunks = min(n_chunks, s // chunk)
    n_groups = min(n_groups, s // (chunk * n_chunks))
    rows = chunk * n_chunks * n_groups
    row = lambda t: t.reshape(1, -1)
    vecs = [row(mu), row(w0), w2, row(a0), a2, g2, row(k_k), row(k_a), row(r_k), row(lnx_w), row(lnx_b)]
    return pl.pallas_call(
        functools.partial(_rwkv_kernel, chunk=chunk, n_chunks=n_chunks, n_groups=n_groups, heads=heads, hd=hd),
        grid=(b, s // rows),
        in_specs=[pl.BlockSpec((1, rows, cols), lambda i, j: (i, j, 0))]
        + [pl.BlockSpec(t.shape, lambda i, j: (0, 0)) for t in vecs],
        out_specs=pl.BlockSpec((1, rows, dim), lambda i, j: (i, j, 0)),
        out_shape=jax.ShapeDtypeStruct((b, s, dim), F32),
        scratch_shapes=[pltpu.VMEM((1, cols), F32), pltpu.VMEM((heads // 2, 2 * hd, 2 * hd), F32)],
        compiler_params=_params("parallel", "arbitrary"),
        name="rwkv",
    )(p, *vecs)


def _moba_kernel(q_ref, k_ref, v_ref, o_ref, km_ref, ka_ref, vat_ref, *, nb, blk, n_sel, heads, hd):
    j = pl.program_id(1)
    nbp = km_ref.shape[0]
    scale = hd ** -0.5
    masked = -1e30
    neg_inf = float("-inf")
    slopes = [2.0 ** (-8.0 * (h + 1) / heads) for h in range(heads)]

    @pl.when(j == 0)
    def _():
        km_ref[...] = jnp.zeros_like(km_ref)
        for n in range(nb):
            km_ref[n:n + 1, :] = jnp.mean(k_ref[0, n * blk:(n + 1) * blk, :], axis=0, keepdims=True)
        col = _iota2((blk, 2 * hd), 0).astype(F32)
        lane = _iota2((blk, 2 * hd), 1)
        feat_lane = lane & (hd - 1)
        for n in range(nb):
            rows = slice(n * blk, (n + 1) * blk)
            one_hot = jnp.where(feat_lane == n, 1.0, 0.0)
            for h in range(heads):
                k_feat = (one_hot + jnp.where(feat_lane == nbp, slopes[h] * col, 0.0)
                          + jnp.where(feat_lane == nbp + 1, slopes[h] * blk * n, 0.0))
                k_tile = k_ref[0, rows, (h // 2) * 2 * hd:(h // 2 + 1) * 2 * hd]
                own_lanes = (lane < hd) if h % 2 == 0 else (lane >= hd)
                ka_ref[h, rows, :] = jnp.where(own_lanes, k_tile, k_feat).astype(BF16)
        ones_row = jnp.where(_iota2((vat_ref.shape[2] - hd, blk), 0) == 0, 1.0, 0.0)
        for n in range(nb):
            vt = v_ref[0, n * blk:(n + 1) * blk, :].T
            for h in range(heads):
                vat_ref[n, h] = jnp.concatenate([vt[h * hd:(h + 1) * hd], ones_row], axis=0).astype(BF16)

    qt = (q_ref[0] * scale).T
    sub = _iota2((nbp, blk), 0)
    const_rows = jnp.where(_iota2((hd - nbp, blk), 0) < 2, 1.0, 0.0)
    q_aug = []
    for h in range(heads):
        sl = slice(h * hd, (h + 1) * hd)
        qh = qt[sl]
        gate = _mm(km_ref[:, sl], qh)
        gate = jnp.where(sub < j, gate, neg_inf)
        cnt = jnp.zeros((nbp, blk), jnp.int32)
        for m in range(nb):
            other = gate[m:m + 1, :]
            beats = (other > gate) | ((other == gate) & (m < sub))
            cnt = cnt + beats.astype(jnp.int32)
        keep = ((sub < j) & (cnt < n_sel)) | (sub == j) | (sub >= nb)
        bias = jnp.where(keep, 0.0, masked)
        parts = [qh, bias, const_rows] if h % 2 == 0 else [bias, const_rows, qh]
        q_aug.append(jnp.concatenate(parts, axis=0).astype(BF16))
    q_aug = jnp.stack(q_aug, axis=0)

    def scores(n):
        start = pl.multiple_of(n * blk, blk)
        kn = ka_ref[:, pl.ds(start, blk), :]
        return lax.dot_general(kn, q_aug, BNN, preferred_element_type=F32), vat_ref[n]

    s, vj = scores(j)
    causal = _iota2((1, blk, blk), 1) <= _iota2((1, blk, blk), 2)
    s = jnp.where(causal, s, masked)
    m0 = jnp.max(s, axis=1, keepdims=True)
    acc0 = lax.dot_general(vj, jnp.exp(s - m0).astype(BF16), BNN, preferred_element_type=F32)

    def past_block(n, carry):
        m_run, acc = carry
        sc, vn = scores(n)
        m_new = jnp.maximum(m_run, jnp.max(sc, axis=1, keepdims=True))
        pr = jnp.exp(sc - m_new).astype(BF16)
        acc_new = jnp.exp(m_run - m_new) * acc + lax.dot_general(vn, pr, BNN, preferred_element_type=F32)
        return m_new, acc_new

    _, acc = lax.fori_loop(0, j, past_block, (m0, acc0))
    out_t = jnp.concatenate([acc[h, :hd] / acc[h, hd:hd + 1] for h in range(heads)], axis=0)
    o_ref[0] = out_t.T


def _moba(q, k, v):
    b, s, dim = q.shape
    heads, hd, blk = MOBA_HEADS, MOBA_HD, MOBA_BLOCK
    assert s % blk == 0
    nb = s // blk
    nbp = -(-nb // 8) * 8
    assert nbp + 2 <= hd
    n_sel = min(MOBA_TOPK, nb - 1)
    return pl.pallas_call(
        functools.partial(_moba_kernel, nb=nb, blk=blk, n_sel=n_sel, heads=heads, hd=hd),
        grid=(b, nb),
        in_specs=[
            pl.BlockSpec((1, blk, dim), lambda i, j: (i, j, 0)),
            pl.BlockSpec((1, s, dim), lambda i, j: (i, 0, 0)),
            pl.BlockSpec((1, s, dim), lambda i, j: (i, 0, 0)),
        ],
        out_specs=pl.BlockSpec((1, blk, dim), lambda i, j: (i, j, 0)),
        out_shape=jax.ShapeDtypeStruct((b, s, dim), F32),
        scratch_shapes=[pltpu.VMEM((nbp, dim), F32),
                        pltpu.VMEM((heads, s, 2 * hd), BF16),
                        pltpu.VMEM((nb, heads, hd + 16, blk), BF16)],
        compiler_params=_params("parallel", "arbitrary"),
        name="moba",
    )(q, k, v)


def _hgrn_kernel(q_ref, f_ref, i_ref, g_ref, lbl_ref, nw_ref, o_ref, st_ref, *, chunk, n_chunks, n_groups, heads,
                 dk, layer):
    c = chunk

    @pl.when(pl.program_id(1) == 0)
    def _():
        st_ref[...] = jnp.zeros_like(st_ref)

    logits = lbl_ref[...]
    e = jnp.exp(logits - jnp.max(logits, axis=0, keepdims=True))
    sm = e / jnp.sum(e, axis=0, keepdims=True)
    lb = jnp.sum(sm[0:layer + 1, :], axis=0, keepdims=True) - sm[0:1, :]

    rows = n_chunks * c
    dims = dict(c=c, n_chunks=n_chunks, heads=heads, dk=dk)
    grp = lambda ref, gi: ref[0, gi * rows:(gi + 1) * rows, :]
    front = lambda gi: _hgrn_front(grp(q_ref, gi), grp(f_ref, gi), grp(i_ref, gi), lb, **dims)
    s = st_ref[...]
    (ready,) = _interleave(front(0))
    for gi in range(n_groups):
        back = _hgrn_back(ready, s, grp(g_ref, gi), nw_ref[...], **dims)
        if gi + 1 < n_groups:
            (out, s), ready = _interleave(back, front(gi + 1))
        else:
            ((out, s),) = _interleave(back)
        o_ref[0, gi * rows:(gi + 1) * rows, :] = out
    st_ref[...] = s


def _hgrn_front(q, fr, v, lb, *, c, n_chunks, heads, dk):
    rows = n_chunks * c
    dim = heads * dk
    sig = jax.nn.sigmoid(fr)
    lf = jnp.log(lb + (1.0 - lb) * sig)
    kf = (1.0 - lb) * (1.0 - sig)
    yield
    r_i, c_i = _iota2((rows, rows), 0), _iota2((rows, rows), 1)
    tri_all = (r_i >= c_i) & (r_i // c == c_i // c)
    b = _cumsum_matmul(tri_all, lf)
    per_chunk = lambda r: jnp.concatenate(
        [jnp.broadcast_to(b[i * c + r:i * c + r + 1, :], (c, dim)) for i in range(n_chunks)], axis=0)
    ends = [b[(i + 1) * c - 1:(i + 1) * c, :] for i in range(n_chunks)]
    b_last = per_chunk(c - 1)
    yield
    b_mid = per_chunk(c // 2 - 1)
    rel = b - b_mid
    qd = q * jnp.exp(rel)
    yield
    kd = kf * jnp.exp(-rel)
    yield
    q_in = qd * jnp.exp(b_mid)
    yield
    kw = kf * jnp.exp(b_last - b)
    w_all = jnp.exp(jnp.concatenate(ends, axis=0))
    return dict(qd=qd, kd=kd, q_in=q_in, kw=kw, w_all=w_all, v=v)


def _hgrn_back(ops, s, g, norm_w, *, c, n_chunks, heads, dk):
    def by_head(t, n_rows=c):
        return jnp.stack([t[i * n_rows:(i + 1) * n_rows, h * dk:(h + 1) * dk]
                          for i in range(n_chunks) for h in range(heads)], axis=0)
    hp, pw = heads // 2, 2 * dk
    by_pair = lambda t: jnp.stack([t[i * c:(i + 1) * c, j * pw:(j + 1) * pw]
                                   for i in range(n_chunks) for j in range(hp)], axis=0)
    tri_pair = _iota2((1, c, 2 * c), 1) >= (_iota2((1, c, 2 * c), 2) & (c - 1))
    q_hi, q_lo = _pieces(by_pair(ops["qd"]), 2)
    k_hi, k_lo = _pieces(_block_diag(by_pair(ops["kd"]), dk), 2)
    sc = lax.dot_general(jnp.concatenate([q_hi, q_hi, q_lo], axis=2), jnp.concatenate([k_hi, k_lo, k_hi], axis=2),
                         BNT, preferred_element_type=F32)
    sc = jnp.where(tri_pair, sc, 0.0)
    yield
    o_pair = _bdot(sc, _block_diag(by_pair(ops["v"]), dk), BNN)
    yield
    qd_h, v_h = by_head(ops["q_in"]), by_head(ops["v"])
    s_add = _bdot(v_h, by_head(ops["kw"]), BTN)
    yield
    w_all_h = by_head(ops["w_all"], 1)
    o_rows = []
    for i in range(n_chunks):
        sl = slice(i * heads, (i + 1) * heads)
        o_intra = jnp.stack([o_pair[i * hp + h // 2][:, (h % 2) * dk:(h % 2 + 1) * dk]
                             for h in range(heads)], axis=0)
        o_h = o_intra + _bdot(qd_h[sl], s, BNT)
        s = s * w_all_h[sl] + s_add[sl]
        o_h = o_h * lax.rsqrt(jnp.mean(o_h * o_h, axis=-1, keepdims=True) + EPS)
        o_rows.append(jnp.concatenate([o_h[h] for h in range(heads)], axis=1))
        yield
    o = jnp.concatenate(o_rows, axis=0)
    return o * norm_w * jax.nn.sigmoid(g), s


def _hgrn(p, lb_logits, norm_w, *, layer, chunk=HG_CHUNK, n_chunks=HG_CHUNKS_PER_GROUP,
          n_groups=HG_GROUPS_PER_STEP):
    b, s, cols = p.shape
    heads, dk = HG_HEADS, HG_DK
    dim = heads * dk
    assert cols == 4 * dim
    chunk = min(chunk, s)
    n_chunks = min(n_chunks, s // chunk)
    n_groups = min(n_groups, s // (chunk * n_chunks))
    rows = chunk * n_chunks * n_groups
    col_block = lambda n: pl.BlockSpec((1, rows, dim), lambda i, j, n=n: (i, j, n))
    return pl.pallas_call(
        functools.partial(_hgrn_kernel, chunk=chunk, n_chunks=n_chunks, n_groups=n_groups, heads=heads, dk=dk,
                          layer=layer),
        grid=(b, s // rows),
        in_specs=[col_block(0), col_block(1), col_block(2), col_block(3),
                  pl.BlockSpec(lb_logits.shape, lambda i, j: (0, 0)),
                  pl.BlockSpec((1, dim), lambda i, j: (0, 0))],
        out_specs=pl.BlockSpec((1, rows, dim), lambda i, j: (i, j, 0)),
        out_shape=jax.ShapeDtypeStruct((b, s, dim), F32),
        scratch_shapes=[pltpu.VMEM((heads, dk, dk), F32)],
        compiler_params=_params("parallel", "arbitrary"),
        name="hgrn",
    )(p, p, p, p, lb_logits, norm_w.reshape(1, dim))


def kernel(x, norm_g, ffn1_wg, ffn1_wu, ffn1_wd, ffn2_wg, ffn2_wu, ffn2_wd, ev_w_in, ev_w_out, rw_mu, rw_w0, rw_w2, rw_a0, rw_a2, rw_g2, rw_k_k, rw_k_a, rw_r_k, rw_lnx_w, rw_lnx_b, od_w_in, od_w_out, hg_norm_w, hg_lb_logits, final_g):
    bsz, seq, d = x.shape
    depth = norm_g.shape[0]
    rwkv_dim = RWKV_HEADS * RWKV_HD
    rwkv_cols = 3 * rwkv_dim + LORA_W + LORA_A + LORA_G
    moba_dim = MOBA_HEADS * MOBA_HD
    xf = x.reshape(bsz * seq, d)
    for l in range(depth):
        xf = _ffn(xf, [], None, 0, norm_g[l, 0], ffn1_wg, ffn1_wu, ffn1_wd, l, final_g, final_norm=False)
        if l % 2 == 0:
            e = l // 2
            p_r, q, k, v = _inproj(xf, norm_g[l, 1], ev_w_in, e, [rwkv_cols, moba_dim, moba_dim, moba_dim])
            y_a = _rwkv(p_r.reshape(bsz, seq, rwkv_cols), rw_mu[e], rw_w0[e], rw_w2[e], rw_a0[e], rw_a2[e],
                        rw_g2[e], rw_k_k[e], rw_k_a[e], rw_r_k[e], rw_lnx_w[e], rw_lnx_b[e])
            y_b = _moba(q.reshape(bsz, seq, moba_dim), k.reshape(bsz, seq, moba_dim),
                        v.reshape(bsz, seq, moba_dim))
            ys, wo, wo_layer = [y_a.reshape(-1, rwkv_dim), y_b.reshape(-1, moba_dim)], ev_w_out, e
        else:
            o = l // 2
            (p,) = _inproj(xf, norm_g[l, 1], od_w_in, o, [od_w_in.shape[2]])
            y = _hgrn(p.reshape(bsz, seq, -1), hg_lb_logits, hg_norm_w[o], layer=l)
            ys, wo, wo_layer = [y.reshape(bsz * seq, -1)], od_w_out, o
        xf = _ffn(xf, ys, wo, wo_layer, norm_g[l, 2], ffn2_wg, ffn2_wu, ffn2_wd, l, final_g,
                  final_norm=(l == depth - 1))
    return xf.reshape(bsz, seq, d)
```

```python
import functools

import jax
import jax.numpy as jnp
from jax import lax
from jax.experimental import pallas as pl
from jax.experimental.pallas import tpu as pltpu

F32 = jnp.float32
BF16 = jnp.bfloat16
NN = (((1,), (0,)), ((), ()))
BNN = (((2,), (1,)), ((0,), (0,)))
BNT = (((2,), (2,)), ((0,), (0,)))
BTN = (((1,), (1,)), ((0,), (0,)))
MIX_PIECES = 2

EPS = 1e-6
LNX_EPS = 64e-5
RWKV_HEADS = 8
RWKV_HD = 64
LORA_W = 64
LORA_A = 64
LORA_G = 128
MOBA_HEADS = 8
MOBA_HD = 64
MOBA_BLOCK = 256
MOBA_TOPK = 3
HG_HEADS = 8
HG_DK = 128
RWKV_CHUNK = 64
RWKV_CHUNKS_PER_GROUP = 4
RWKV_GROUPS_PER_STEP = 2
HG_CHUNK = 64
HG_CHUNKS_PER_GROUP = 4
HG_GROUPS_PER_STEP = 2
SUBLANES = 8
BF16_SUBLANES = 16
V7X_VMEM_BYTES = 64 * 1024 * 1024
VMEM_LIMIT_BYTES = V7X_VMEM_BYTES - 8 * 1024 * 1024


def _params(*semantics):
    return pltpu.CompilerParams(dimension_semantics=semantics, vmem_limit_bytes=VMEM_LIMIT_BYTES)


def _dot(a, b):
    return jnp.dot(a, b, preferred_element_type=F32)


def _pieces(a, n):
    if isinstance(a, tuple):
        return a
    out = []
    for i in range(n):
        hi = a.astype(BF16)
        out.append(hi)
        if i + 1 < n:
            a = a - hi.astype(F32)
    return tuple(out)


def _mm(a, b, dims=NN, n=MIX_PIECES):
    a = _pieces(a, n)
    b = _pieces(b, n)
    order = max(len(a), len(b)) - 1
    out = None
    for i, ai in enumerate(a):
        for j, bj in enumerate(b):
            if i + j <= order:
                t = lax.dot_general(ai, bj, dims, preferred_element_type=F32)
                out = t if out is None else out + t
    return out


def _mm_fused(a, b):
    a_hi, a_lo = _pieces(a, 2)
    b_hi, b_lo = _pieces(b, 2)
    return _dot(jnp.concatenate([a_hi, a_hi, a_lo], axis=1), jnp.concatenate([b_hi, b_lo, b_hi], axis=0))


def _cumsum_matmul(mask, x):
    m = mask.astype(BF16)
    hi, lo = _pieces(x, 2)
    return _dot(jnp.concatenate([m, m], axis=1), jnp.concatenate([hi, lo], axis=0))


def _rmsnorm(x, g):
    return x * lax.rsqrt(jnp.mean(x * x, axis=-1, keepdims=True) + EPS) * g


def _iota2(shape, dim):
    return lax.broadcasted_iota(jnp.int32, shape, dim)


def _block_ones(n, width):
    return (_iota2((n, n), 0) // width == _iota2((n, n), 1) // width).astype(F32)


def _interleave(*gens):
    results = [None] * len(gens)
    live = list(range(len(gens)))
    while live:
        for i in list(live):
            try:
                next(gens[i])
            except StopIteration as stop:
                results[i] = stop.value
                live.remove(i)
    return results


def _load_weight(w_hbm, layer, dst_ref, stage_ref, sem_ref, slab_rows):
    n_rows = dst_ref.shape[0]
    assert n_rows % slab_rows == 0 and slab_rows <= stage_ref.shape[1]
    n_slabs = n_rows // slab_rows

    def copy(s):
        return pltpu.make_async_copy(w_hbm.at[layer, pl.ds(s * slab_rows, slab_rows), :],
                                     stage_ref.at[s % 2, pl.ds(0, slab_rows), :], sem_ref.at[s % 2])

    copy(0).start()
    for s in range(n_slabs):
        if s + 1 < n_slabs:
            copy(s + 1).start()
        copy(s).wait()
        dst_ref[pl.ds(s * slab_rows, slab_rows), :] = stage_ref[s % 2, pl.ds(0, slab_rows), :].astype(BF16)


def _ffn_kernel(x_ref, *refs, n_y, layer, out_layer, final_norm, tf):
    y_refs = refs[:n_y]
    has_out = n_y > 0
    (wo_hbm,) = refs[n_y:n_y + 1] if has_out else (None,)
    g_ref, wg_hbm, wu_hbm, wd_hbm, fg_ref, o_ref = refs[n_y + has_out:n_y + has_out + 6]
    scratch = refs[n_y + has_out + 6:]
    wg_ref, wu_ref, wd_ref, wide_stage, tall_stage, sem, wo_sem = scratch[:7]
    wo_ref = scratch[7] if has_out else None
    n_slabs = wg_ref.shape[1] // tf

    def slab_copies(j):
        slot, cols = j % 2, pl.ds(j * tf, tf)
        return (pltpu.make_async_copy(wg_hbm.at[layer, :, cols], wide_stage.at[slot, 0], sem.at[slot, 0]),
                pltpu.make_async_copy(wu_hbm.at[layer, :, cols], wide_stage.at[slot, 1], sem.at[slot, 1]),
                pltpu.make_async_copy(wd_hbm.at[layer, cols, :], tall_stage.at[slot], sem.at[slot, 2]))

    def body(first_step):
        if first_step:
            for j in range(min(2, n_slabs)):
                for cp in slab_copies(j):
                    cp.start()
            if has_out:
                _load_weight(wo_hbm, out_layer, wo_ref, wo_ref_stage, wo_sem, wo_ref.shape[0] // 4)
        x = x_ref[...]
        row0 = 0
        for y_ref in y_refs:
            rows = y_ref.shape[1]
            x = x + _dot(y_ref[...].astype(BF16), wo_ref[row0:row0 + rows, :])
            row0 += rows
        h = _rmsnorm(x, g_ref[...]).astype(BF16)
        acc = None
        for j in range(n_slabs):
            c0 = j * tf
            if first_step:
                for cp in slab_copies(j):
                    cp.wait()
                wg_ref[:, c0:c0 + tf] = wide_stage[j % 2, 0].astype(BF16)
                wu_ref[:, c0:c0 + tf] = wide_stage[j % 2, 1].astype(BF16)
                wd_ref[c0:c0 + tf, :] = tall_stage[j % 2].astype(BF16)
                if j + 2 < n_slabs:
                    for cp in slab_copies(j + 2):
                        cp.start()
            gate = _dot(h, wg_ref[:, c0:c0 + tf])
            up = _dot(h, wu_ref[:, c0:c0 + tf])
            act = (gate * jax.nn.sigmoid(gate) * up).astype(BF16)
            part = _dot(act, wd_ref[c0:c0 + tf, :])
            acc = part if acc is None else acc + part
        out = x + 0.5 * acc
        if final_norm:
            out = _rmsnorm(out, fg_ref[...])
        o_ref[...] = out

    wo_ref_stage = scratch[8] if has_out else None
    pl.when(pl.program_id(0) == 0)(lambda: body(True))
    pl.when(pl.program_id(0) != 0)(lambda: body(False))


def _resident(shape):
    return pl.BlockSpec(shape, lambda i: (0,) * len(shape), pipeline_mode=pl.Buffered(1))


def _ffn(x, ys, wo, out_layer, g, wg, wu, wd, layer, final_g, *, final_norm, tm=512, tf=256):
    m, d = x.shape
    f = wg.shape[2]
    tm = min(tm, m)
    assert f % tf == 0
    hbm = pl.BlockSpec(memory_space=pl.ANY)
    has_out = len(ys) > 0
    scratch = [pltpu.VMEM((d, f), BF16), pltpu.VMEM((d, f), BF16), pltpu.VMEM((f, d), BF16),
               pltpu.VMEM((2, 2, d, tf), F32), pltpu.VMEM((2, tf, d), F32),
               pltpu.SemaphoreType.DMA((2, 3)), pltpu.SemaphoreType.DMA((2,))]
    if has_out:
        assert sum(y.shape[1] for y in ys) == wo.shape[1] and wo.shape[1] % 4 == 0
        scratch += [pltpu.VMEM(wo.shape[1:], BF16), pltpu.VMEM((2, wo.shape[1] // 4, wo.shape[2]), F32)]
    return pl.pallas_call(
        functools.partial(_ffn_kernel, n_y=len(ys), layer=layer, out_layer=out_layer, final_norm=final_norm,
                          tf=tf),
        grid=(m // tm,),
        in_specs=[pl.BlockSpec((tm, d), lambda i: (i, 0))]
        + [pl.BlockSpec((tm, y.shape[1]), lambda i: (i, 0)) for y in ys]
        + ([hbm] if has_out else [])
        + [_resident((1, d)), hbm, hbm, hbm, _resident((1, d))],
        out_specs=pl.BlockSpec((tm, d), lambda i: (i, 0)),
        out_shape=jax.ShapeDtypeStruct((m, d), F32),
        scratch_shapes=scratch,
        compiler_params=_params("arbitrary"),
        name="ffn",
    )(x, *ys, *([wo] if has_out else []), g.reshape(1, d), wg, wu, wd, final_g.reshape(1, d))


def _inproj_kernel(x_ref, g_ref, w_hbm, *refs, layer, tn):
    o_refs = refs[:-3]
    w_ref, stage, sem = refs[-3:]

    @pl.when(pl.program_id(0) == 0)
    def _():
        _load_weight(w_hbm, layer, w_ref, stage, sem, stage.shape[1])

    h = _rmsnorm(x_ref[...], g_ref[...]).astype(BF16)
    col0 = 0
    for o_ref in o_refs:
        n = o_ref.shape[1]
        for c0 in range(0, n, tn):
            c1 = min(c0 + tn, n)
            o_ref[:, c0:c1] = _dot(h, w_ref[:, col0 + c0:col0 + c1])
        col0 += n


def _inproj(x, g, w, layer, widths, *, tm=512, tn=512, n_slabs=8):
    m, d = x.shape
    n_total = w.shape[2]
    assert sum(widths) == n_total and d % n_slabs == 0
    tm = min(tm, m)
    return pl.pallas_call(
        functools.partial(_inproj_kernel, layer=layer, tn=tn),
        grid=(m // tm,),
        in_specs=[pl.BlockSpec((tm, d), lambda i: (i, 0)), _resident((1, d)), pl.BlockSpec(memory_space=pl.ANY)],
        out_specs=[pl.BlockSpec((tm, n), lambda i: (i, 0)) for n in widths],
        out_shape=[jax.ShapeDtypeStruct((m, n), F32) for n in widths],
        scratch_shapes=[pltpu.VMEM((d, n_total), BF16), pltpu.VMEM((2, d // n_slabs, n_total), F32),
                        pltpu.SemaphoreType.DMA((2,))],
        compiler_params=_params("arbitrary"),
        name="inproj",
    )(x, g.reshape(1, d), w)


def _bdot(a, b, dims):
    return lax.dot_general(a.astype(BF16), b.astype(BF16), dims, preferred_element_type=F32)


def _block_diag(x, half):
    lo = _iota2((1, 1, 2 * half), 2) < half
    return jnp.concatenate([jnp.where(lo, x, 0.0), jnp.where(lo, 0.0, x)], axis=1)


def _unit_lower_inverse(a_strict, c):
    row = _iota2((1, c, 2 * c), 1)
    col = _iota2((1, c, 2 * c), 2) & (c - 1)
    eye = (row == col).astype(F32)
    t = None
    m = 1
    while m < c:
        mask = ((row // (2 * m)) == (col // (2 * m))) & ((row & m) != 0) & ((col & m) == 0)
        lm = jnp.where(mask, a_strict, 0.0)
        if t is None:
            t = eye - lm
        else:
            t = t - _bdot(_bdot(t, _block_diag(lm, c), BNN), _block_diag(t, c), BNN)
            yield
        m *= 2
    return t


def _rwkv_kernel(p_ref, mu_ref, w0_ref, w2_ref, a0_ref, a2_ref, g2_ref, kk_ref, ka_ref, rk_ref,
                 lnw_ref, lnb_ref, o_ref, carry_ref, st_ref, *, chunk, n_chunks, n_groups, heads, hd):
    c = chunk
    step_rows = n_groups * n_chunks * c

    @pl.when(pl.program_id(1) == 0)
    def _():
        carry_ref[...] = jnp.zeros_like(carry_ref)
        st_ref[...] = jnp.zeros_like(st_ref)

    p = p_ref[0]
    row = _iota2((step_rows, 1), 0)
    prev = jnp.where(row == 0, carry_ref[...], pltpu.roll(p, 1, axis=0))
    carry_ref[...] = p[step_rows - 1:step_rows, :]
    xs_all = p + (prev - p) * mu_ref[...]

    rows = n_chunks * c
    front_refs = (w0_ref, w2_ref, a0_ref, a2_ref, g2_ref, kk_ref, ka_ref)
    dims = dict(c=c, n_chunks=n_chunks, heads=heads, hd=hd)
    front = lambda gi: _rwkv_front(xs_all[gi * rows:(gi + 1) * rows], *front_refs, **dims)
    s = st_ref[...]
    (ready,) = _interleave(front(0))
    for gi in range(n_groups):
        back = _rwkv_back(ready, s, rk_ref, lnw_ref, lnb_ref, **dims)
        if gi + 1 < n_groups:
            (out, s), ready = _interleave(back, front(gi + 1))
        else:
            ((out, s),) = _interleave(back)
        o_ref[0, gi * rows:(gi + 1) * rows, :] = out
    st_ref[...] = s


def _head_sum(t, hd):
    tile_ones = _block_ones(2 * hd, hd)
    return jnp.concatenate([_bdot(t[:, i:i + 2 * hd], tile_ones, NN) for i in range(0, t.shape[1], 2 * hd)], axis=1)


def _rwkv_front(xs, w0_ref, w2_ref, a0_ref, a2_ref, g2_ref, kk_ref, ka_ref, *, c, n_chunks, heads, hd):
    rows = n_chunks * c
    dim = heads * hd
    r = xs[:, 0:dim]
    k = xs[:, dim:2 * dim]
    v = xs[:, 2 * dim:3 * dim]
    o1 = 3 * dim
    g_lr = xs[:, o1 + LORA_W + LORA_A:o1 + LORA_W + LORA_A + LORA_G]

    wa = xs[:, o1:o1 + LORA_W + LORA_A]
    wa = jnp.where(_iota2(wa.shape, 1) < LORA_W, jnp.tanh(wa), wa)
    w2a2 = jnp.concatenate(
        [jnp.concatenate([w2_ref[...], jnp.zeros((LORA_W, dim), F32)], axis=1),
         jnp.concatenate([jnp.zeros((LORA_A, dim), F32), a2_ref[...]], axis=1)], axis=0)
    za = _mm_fused(wa, w2a2)
    z = w0_ref[...] + za[:, :dim]
    softplus = jnp.maximum(-z, 0.0) + jnp.log1p(jnp.exp(-jnp.abs(z)))
    w_raw = -softplus - 0.5
    lw = -jnp.exp(w_raw)
    yield
    a = jax.nn.sigmoid(a0_ref[...] + za[:, dim:])
    g = _mm_fused(jax.nn.sigmoid(g_lr), g2_ref[...])
    yield
    kk = k * kk_ref[...]
    kk = kk * lax.rsqrt(jnp.maximum(_head_sum(kk * kk, hd), 1e-24))
    k2 = k * (1.0 + (a - 1.0) * ka_ref[...])
    bb = kk * a
    yield

    r_i, c_i = _iota2((rows, rows), 0), _iota2((rows, rows), 1)
    tri_incl = (r_i >= c_i) & (r_i // c == c_i // c)
    cum = _cumsum_matmul(tri_incl, lw)
    ends = [cum[(i + 1) * c - 1:(i + 1) * c, :] for i in range(n_chunks)]
    cum_last = jnp.concatenate([jnp.broadcast_to(e, (c, dim)) for e in ends], axis=0)
    yield
    w_incl = jnp.exp(cum)
    w_excl = jnp.exp(cum - lw)
    yield
    w_inv = jnp.exp(-cum)
    w_tail = jnp.exp(cum_last - cum)
    w_all = jnp.exp(jnp.concatenate(ends, axis=0))
    yield
    ops = dict(kt=kk * w_excl, rt=r * w_incl, bt=bb * w_inv)
    yield
    ops.update(kd=k2 * w_inv, bw=bb * w_tail, kw=k2 * w_tail)
    yield
    ops.update(v=v, w_all=w_all, r=r, k2=k2, g=g)
    return ops


def _rwkv_back(ops, s, rk_ref, lnw_ref, lnb_ref, *, c, n_chunks, heads, hd):
    pw = 2 * hd
    pairs = heads // 2
    def by_pair(t, n_rows=c):
        return jnp.stack([t[i * n_rows:(i + 1) * n_rows, j * pw:(j + 1) * pw]
                          for i in range(n_chunks) for j in range(pairs)], axis=0)
    bd_f = lambda t: _block_diag(t, hd)
    kt_p, rt_p, bt_p, kd_p, kw_p, bw_p, v_p = (by_pair(ops[n]) for n in ("kt", "rt", "bt", "kd", "kw", "bw", "v"))
    w_all_p = by_pair(ops["w_all"], 1)
    row_t = _iota2((1, c, 2 * c), 1)
    col_t = _iota2((1, c, 2 * c), 2) & (c - 1)
    strict_p, incl_p = row_t > col_t, row_t >= col_t
    same_head = (_iota2((1, pw, pw), 1) // hd) == (_iota2((1, pw, pw), 2) // hd)

    kr = jnp.concatenate([kt_p, rt_p], axis=1)
    g_all = _bdot(kr, jnp.concatenate([bd_f(bt_p), bd_f(kd_p)], axis=1), BNT)
    yield
    gb, gk = g_all[:, :, :2 * c], g_all[:, :, 2 * c:]
    a_b = jnp.where(strict_p, gb[:, :c], 0.0)
    a_k = jnp.where(strict_p, gk[:, :c], 0.0)
    p_b = jnp.where(incl_p, gb[:, c:], 0.0)
    p_k = jnp.where(incl_p, gk[:, c:], 0.0)
    akpk = _bdot(jnp.concatenate([a_k, p_k], axis=1), bd_f(v_p), BNN)
    yield
    t_inv = yield from _unit_lower_inverse(a_b, c)
    kv1 = _bdot(t_inv, jnp.concatenate([bd_f(kt_p), bd_f(akpk[:, :c])], axis=2), BNN)
    yield
    kt1, v1 = kv1[:, :, :pw], kv1[:, :, pw:]
    pbk = _bdot(p_b, jnp.concatenate([bd_f(kt1), bd_f(v1)], axis=2), BNN)
    yield
    q_mat = rt_p - pbk[:, :, :pw]
    z_mat = akpk[:, c:] - pbk[:, :, pw:]
    x_mat = jnp.where(same_head, _bdot(bw_p, kt1, BTN), 0.0)
    yield
    n_mat = jnp.where(same_head, _bdot(jnp.concatenate([v_p, v1], axis=1),
                                       jnp.concatenate([kw_p, -bw_p], axis=1), BTN), 0.0)
    yield
    y_rows = []
    for i in range(n_chunks):
        sl = slice(i * pairs, (i + 1) * pairs)
        y_p = _bdot(q_mat[sl], s, BNT) + z_mat[sl]
        s = s * w_all_p[sl] - _bdot(s, x_mat[sl], BNT) + n_mat[sl]
        y_rows.append(jnp.concatenate([y_p[j] for j in range(pairs)], axis=1))
        yield
    y = jnp.concatenate(y_rows, axis=0)

    inv_hd = 1.0 / hd
    mean = _head_sum(y, hd) * inv_hd
    yc = y - mean
    var = _head_sum(yc * yc, hd) * inv_hd
    yield
    yn = yc * lax.rsqrt(var + LNX_EPS) * lnw_ref[...] + lnb_ref[...]
    bonus = _head_sum(ops["r"] * ops["k2"] * rk_ref[...], hd) * ops["v"]
    return (yn + bonus) * ops["g"], s


def _rwkv(p, mu, w0, w2, a0, a2, g2, k_k, k_a, r_k, lnx_w, lnx_b, *, chunk=RWKV_CHUNK,
          n_chunks=RWKV_CHUNKS_PER_GROUP, n_groups=RWKV_GROUPS_PER_STEP):
    b, s, cols = p.shape
    heads, hd = RWKV_HEADS, RWKV_HD
    dim = heads * hd
    chunk = min(chunk, s)
    assert chunk & (chunk - 1) == 0
    n_chunks = min(n_chunks, s // chunk)
    n_groups = min(n_groups, s // (chunk * n_chunks))
    rows = chunk * n_chunks * n_groups
    row = lambda t: t.reshape(1, -1)
    vecs = [row(mu), row(w0), w2, row(a0), a2, g2, row(k_k), row(k_a), row(r_k), row(lnx_w), row(lnx_b)]
    return pl.pallas_call(
        functools.partial(_rwkv_kernel, chunk=chunk, n_chunks=n_chunks, n_groups=n_groups, heads=heads, hd=hd),
        grid=(b, s // rows),
        in_specs=[pl.BlockSpec((1, rows, cols), lambda i, j: (i, j, 0))]
        + [pl.BlockSpec(t.shape, lambda i, j: (0, 0)) for t in vecs],
        out_specs=pl.BlockSpec((1, rows, dim), lambda i, j: (i, j, 0)),
        out_shape=jax.ShapeDtypeStruct((b, s, dim), F32),
        scratch_shapes=[pltpu.VMEM((1, cols), F32), pltpu.VMEM((heads // 2, 2 * hd, 2 * hd), F32)],
        compiler_params=_params("parallel", "arbitrary"),
        name="rwkv",
    )(p, *vecs)


def _moba_kernel(q_ref, k_ref, v_ref, o_ref, km_ref, ka_ref, vat_ref, *, nb, blk, n_sel, heads, hd):
    j = pl.program_id(1)
    nbp = km_ref.shape[0]
    scale = hd ** -0.5
    masked = -1e30
    neg_inf = float("-inf")
    slopes = [2.0 ** (-8.0 * (h + 1) / heads) for h in range(heads)]

    @pl.when(j == 0)
    def _():
        km_ref[...] = jnp.zeros_like(km_ref)
        for n in range(nb):
            km_ref[n:n + 1, :] = jnp.mean(k_ref[0, n * blk:(n + 1) * blk, :], axis=0, keepdims=True)
        col = _iota2((blk, 2 * hd), 0).astype(F32)
        lane = _iota2((blk, 2 * hd), 1)
        feat_lane = lane & (hd - 1)
        for n in range(nb):
            rows = slice(n * blk, (n + 1) * blk)
            one_hot = jnp.where(feat_lane == n, 1.0, 0.0)
            for h in range(heads):
                k_feat = (one_hot + jnp.where(feat_lane == nbp, slopes[h] * col, 0.0)
                          + jnp.where(feat_lane == nbp + 1, slopes[h] * blk * n, 0.0))
                k_tile = k_ref[0, rows, (h // 2) * 2 * hd:(h // 2 + 1) * 2 * hd]
                own_lanes = (lane < hd) if h % 2 == 0 else (lane >= hd)
                ka_ref[h, rows, :] = jnp.where(own_lanes, k_tile, k_feat).astype(BF16)
        ones_row = jnp.where(_iota2((vat_ref.shape[2] - hd, blk), 0) == 0, 1.0, 0.0)
        for n in range(nb):
            vt = v_ref[0, n * blk:(n + 1) * blk, :].T
            for h in range(heads):
                vat_ref[n, h] = jnp.concatenate([vt[h * hd:(h + 1) * hd], ones_row], axis=0).astype(BF16)

    qt = (q_ref[0] * scale).T
    sub = _iota2((nbp, blk), 0)
    const_rows = jnp.where(_iota2((hd - nbp, blk), 0) < 2, 1.0, 0.0)
    q_aug = []
    for h in range(heads):
        sl = slice(h * hd, (h + 1) * hd)
        qh = qt[sl]
        gate = _mm(km_ref[:, sl], qh)
        gate = jnp.where(sub < j, gate, neg_inf)
        cnt = jnp.zeros((nbp, blk), jnp.int32)
        for m in range(nb):
            other = gate[m:m + 1, :]
            beats = (other > gate) | ((other == gate) & (m < sub))
            cnt = cnt + beats.astype(jnp.int32)
        keep = ((sub < j) & (cnt < n_sel)) | (sub == j) | (sub >= nb)
        bias = jnp.where(keep, 0.0, masked)
        parts = [qh, bias, const_rows] if h % 2 == 0 else [bias, const_rows, qh]
        q_aug.append(jnp.concatenate(parts, axis=0).astype(BF16))
    q_aug = jnp.stack(q_aug, axis=0)

    def scores(n):
        start = pl.multiple_of(n * blk, blk)
        kn = ka_ref[:, pl.ds(start, blk), :]
        return lax.dot_general(kn, q_aug, BNN, preferred_element_type=F32), vat_ref[n]

    s, vj = scores(j)
    causal = _iota2((1, blk, blk), 1) <= _iota2((1, blk, blk), 2)
    s = jnp.where(causal, s, masked)
    m0 = jnp.max(s, axis=1, keepdims=True)
    acc0 = lax.dot_general(vj, jnp.exp(s - m0).astype(BF16), BNN, preferred_element_type=F32)

    def past_block(n, carry):
        m_run, acc = carry
        sc, vn = scores(n)
        m_new = jnp.maximum(m_run, jnp.max(sc, axis=1, keepdims=True))
        pr = jnp.exp(sc - m_new).astype(BF16)
        acc_new = jnp.exp(m_run - m_new) * acc + lax.dot_general(vn, pr, BNN, preferred_element_type=F32)
        return m_new, acc_new

    _, acc = lax.fori_loop(0, j, past_block, (m0, acc0))
    out_t = jnp.concatenate([acc[h, :hd] / acc[h, hd:hd + 1] for h in range(heads)], axis=0)
    o_ref[0] = out_t.T


def _moba(q, k, v):
    b, s, dim = q.shape
    heads, hd, blk = MOBA_HEADS, MOBA_HD, MOBA_BLOCK
    assert s % blk == 0
    nb = s // blk
    nbp = -(-nb // SUBLANES) * SUBLANES
    assert nbp + 2 <= hd
    n_sel = min(MOBA_TOPK, nb - 1)
    return pl.pallas_call(
        functools.partial(_moba_kernel, nb=nb, blk=blk, n_sel=n_sel, heads=heads, hd=hd),
        grid=(b, nb),
        in_specs=[
            pl.BlockSpec((1, blk, dim), lambda i, j: (i, j, 0)),
            pl.BlockSpec((1, s, dim), lambda i, j: (i, 0, 0)),
            pl.BlockSpec((1, s, dim), lambda i, j: (i, 0, 0)),
        ],
        out_specs=pl.BlockSpec((1, blk, dim), lambda i, j: (i, j, 0)),
        out_shape=jax.ShapeDtypeStruct((b, s, dim), F32),
        scratch_shapes=[pltpu.VMEM((nbp, dim), F32),
                        pltpu.VMEM((heads, s, 2 * hd), BF16),
                        pltpu.VMEM((nb, heads, hd + BF16_SUBLANES, blk), BF16)],
        compiler_params=_params("parallel", "arbitrary"),
        name="moba",
    )(q, k, v)


def _hgrn_kernel(q_ref, f_ref, i_ref, g_ref, lbl_ref, nw_ref, o_ref, st_ref, *, chunk, n_chunks, n_groups, heads,
                 dk, layer):
    c = chunk

    @pl.when(pl.program_id(1) == 0)
    def _():
        st_ref[...] = jnp.zeros_like(st_ref)

    logits = lbl_ref[...]
    e = jnp.exp(logits - jnp.max(logits, axis=0, keepdims=True))
    sm = e / jnp.sum(e, axis=0, keepdims=True)
    lb = jnp.sum(sm[0:layer + 1, :], axis=0, keepdims=True) - sm[0:1, :]

    rows = n_chunks * c
    dims = dict(c=c, n_chunks=n_chunks, heads=heads, dk=dk)
    grp = lambda ref, gi: ref[0, gi * rows:(gi + 1) * rows, :]
    front = lambda gi: _hgrn_front(grp(q_ref, gi), grp(f_ref, gi), grp(i_ref, gi), lb, **dims)
    s = st_ref[...]
    (ready,) = _interleave(front(0))
    for gi in range(n_groups):
        back = _hgrn_back(ready, s, grp(g_ref, gi), nw_ref[...], **dims)
        if gi + 1 < n_groups:
            (out, s), ready = _interleave(back, front(gi + 1))
        else:
            ((out, s),) = _interleave(back)
        o_ref[0, gi * rows:(gi + 1) * rows, :] = out
    st_ref[...] = s


def _hgrn_front(q, fr, v, lb, *, c, n_chunks, heads, dk):
    rows = n_chunks * c
    dim = heads * dk
    sig = jax.nn.sigmoid(fr)
    lf = jnp.log(lb + (1.0 - lb) * sig)
    kf = (1.0 - lb) * (1.0 - sig)
    yield
    r_i, c_i = _iota2((rows, rows), 0), _iota2((rows, rows), 1)
    tri_all = (r_i >= c_i) & (r_i // c == c_i // c)
    b = _cumsum_matmul(tri_all, lf)
    per_chunk = lambda r: jnp.concatenate(
        [jnp.broadcast_to(b[i * c + r:i * c + r + 1, :], (c, dim)) for i in range(n_chunks)], axis=0)
    ends = [b[(i + 1) * c - 1:(i + 1) * c, :] for i in range(n_chunks)]
    b_last = per_chunk(c - 1)
    yield
    b_mid = per_chunk(c // 2 - 1)
    rel = b - b_mid
    qd = q * jnp.exp(rel)
    yield
    kd = kf * jnp.exp(-rel)
    yield
    q_in = qd * jnp.exp(b_mid)
    yield
    kw = kf * jnp.exp(b_last - b)
    w_all = jnp.exp(jnp.concatenate(ends, axis=0))
    return dict(qd=qd, kd=kd, q_in=q_in, kw=kw, w_all=w_all, v=v)


def _hgrn_back(ops, s, g, norm_w, *, c, n_chunks, heads, dk):
    def by_head(t, n_rows=c):
        return jnp.stack([t[i * n_rows:(i + 1) * n_rows, h * dk:(h + 1) * dk]
                          for i in range(n_chunks) for h in range(heads)], axis=0)
    hp, pw = heads // 2, 2 * dk
    by_pair = lambda t: jnp.stack([t[i * c:(i + 1) * c, j * pw:(j + 1) * pw]
                                   for i in range(n_chunks) for j in range(hp)], axis=0)
    tri_pair = _iota2((1, c, 2 * c), 1) >= (_iota2((1, c, 2 * c), 2) & (c - 1))
    q_hi, q_lo = _pieces(by_pair(ops["qd"]), 2)
    k_hi, k_lo = _pieces(_block_diag(by_pair(ops["kd"]), dk), 2)
    sc = lax.dot_general(jnp.concatenate([q_hi, q_hi, q_lo], axis=2), jnp.concatenate([k_hi, k_lo, k_hi], axis=2),
                         BNT, preferred_element_type=F32)
    sc = jnp.where(tri_pair, sc, 0.0)
    yield
    o_pair = _bdot(sc, _block_diag(by_pair(ops["v"]), dk), BNN)
    yield
    qd_h, v_h = by_head(ops["q_in"]), by_head(ops["v"])
    s_add = _bdot(v_h, by_head(ops["kw"]), BTN)
    yield
    w_all_h = by_head(ops["w_all"], 1)
    o_rows = []
    for i in range(n_chunks):
        sl = slice(i * heads, (i + 1) * heads)
        o_intra = jnp.stack([o_pair[i * hp + h // 2][:, (h % 2) * dk:(h % 2 + 1) * dk]
                             for h in range(heads)], axis=0)
        o_h = o_intra + _bdot(qd_h[sl], s, BNT)
        s = s * w_all_h[sl] + s_add[sl]
        o_h = o_h * lax.rsqrt(jnp.mean(o_h * o_h, axis=-1, keepdims=True) + EPS)
        o_rows.append(jnp.concatenate([o_h[h] for h in range(heads)], axis=1))
        yield
    o = jnp.concatenate(o_rows, axis=0)
    return o * norm_w * jax.nn.sigmoid(g), s


def _hgrn(p, lb_logits, norm_w, *, layer, chunk=HG_CHUNK, n_chunks=HG_CHUNKS_PER_GROUP,
          n_groups=HG_GROUPS_PER_STEP):
    b, s, cols = p.shape
    heads, dk = HG_HEADS, HG_DK
    dim = heads * dk
    assert cols == 4 * dim
    chunk = min(chunk, s)
    assert chunk & (chunk - 1) == 0
    n_chunks = min(n_chunks, s // chunk)
    n_groups = min(n_groups, s // (chunk * n_chunks))
    rows = chunk * n_chunks * n_groups
    col_block = lambda n: pl.BlockSpec((1, rows, dim), lambda i, j, n=n: (i, j, n))
    return pl.pallas_call(
        functools.partial(_hgrn_kernel, chunk=chunk, n_chunks=n_chunks, n_groups=n_groups, heads=heads, dk=dk,
                          layer=layer),
        grid=(b, s // rows),
        in_specs=[col_block(0), col_block(1), col_block(2), col_block(3),
                  pl.BlockSpec(lb_logits.shape, lambda i, j: (0, 0)),
                  pl.BlockSpec((1, dim), lambda i, j: (0, 0))],
        out_specs=pl.BlockSpec((1, rows, dim), lambda i, j: (i, j, 0)),
        out_shape=jax.ShapeDtypeStruct((b, s, dim), F32),
        scratch_shapes=[pltpu.VMEM((heads, dk, dk), F32)],
        compiler_params=_params("parallel", "arbitrary"),
        name="hgrn",
    )(p, p, p, p, lb_logits, norm_w.reshape(1, dim))


def kernel(x, norm_g, ffn1_wg, ffn1_wu, ffn1_wd, ffn2_wg, ffn2_wu, ffn2_wd, ev_w_in, ev_w_out, rw_mu, rw_w0, rw_w2, rw_a0, rw_a2, rw_g2, rw_k_k, rw_k_a, rw_r_k, rw_lnx_w, rw_lnx_b, od_w_in, od_w_out, hg_norm_w, hg_lb_logits, final_g):
    bsz, seq, d = x.shape
    depth = norm_g.shape[0]
    rwkv_dim = RWKV_HEADS * RWKV_HD
    rwkv_cols = 3 * rwkv_dim + LORA_W + LORA_A + LORA_G
    moba_dim = MOBA_HEADS * MOBA_HD
    xf = x.reshape(bsz * seq, d)
    for l in range(depth):
        xf = _ffn(xf, [], None, 0, norm_g[l, 0], ffn1_wg, ffn1_wu, ffn1_wd, l, final_g, final_norm=False)
        if l % 2 == 0:
            e = l // 2
            p_r, q, k, v = _inproj(xf, norm_g[l, 1], ev_w_in, e, [rwkv_cols, moba_dim, moba_dim, moba_dim])
            y_a = _rwkv(p_r.reshape(bsz, seq, rwkv_cols), rw_mu[e], rw_w0[e], rw_w2[e], rw_a0[e], rw_a2[e],
                        rw_g2[e], rw_k_k[e], rw_k_a[e], rw_r_k[e], rw_lnx_w[e], rw_lnx_b[e])
            y_b = _moba(q.reshape(bsz, seq, moba_dim), k.reshape(bsz, seq, moba_dim),
                        v.reshape(bsz, seq, moba_dim))
            ys, wo, wo_layer = [y_a.reshape(-1, rwkv_dim), y_b.reshape(-1, moba_dim)], ev_w_out, e
        else:
            o = l // 2
            (p,) = _inproj(xf, norm_g[l, 1], od_w_in, o, [od_w_in.shape[2]])
            y = _hgrn(p.reshape(bsz, seq, -1), hg_lb_logits, hg_norm_w[o], layer=l)
            ys, wo, wo_layer = [y.reshape(bsz * seq, -1)], od_w_out, o
        xf = _ffn(xf, ys, wo, wo_layer, norm_g[l, 2], ffn2_wg, ffn2_wu, ffn2_wd, l, final_g,
                  final_norm=(l == depth - 1))
    return xf.reshape(bsz, seq, d)
```

```python
import functools

import jax
import jax.numpy as jnp
from jax import lax
from jax.experimental import pallas as pl
from jax.experimental.pallas import tpu as pltpu

F32 = jnp.float32
BF16 = jnp.bfloat16
NN = (((1,), (0,)), ((), ()))
BNN = (((2,), (1,)), ((0,), (0,)))
BNT = (((2,), (2,)), ((0,), (0,)))
BTN = (((1,), (1,)), ((0,), (0,)))
MIX_PIECES = 2

EPS = 1e-6
LNX_EPS = 64e-5
RWKV_HEADS = 8
RWKV_HD = 64
LORA_W = 64
LORA_A = 64
LORA_G = 128
MOBA_HEADS = 8
MOBA_HD = 64
MOBA_BLOCK = 256
MOBA_TOPK = 3
HG_HEADS = 8
HG_DK = 128
RWKV_CHUNK = 64
RWKV_CHUNKS_PER_GROUP = 4
RWKV_GROUPS_PER_STEP = 2
HG_CHUNK = 64
HG_CHUNKS_PER_GROUP = 4
HG_GROUPS_PER_STEP = 2
SUBLANES = 8
BF16_SUBLANES = 16
V7X_VMEM_BYTES = 64 * 1024 * 1024
VMEM_LIMIT_BYTES = V7X_VMEM_BYTES - 8 * 1024 * 1024


def _params(*semantics):
    return pltpu.CompilerParams(dimension_semantics=semantics, vmem_limit_bytes=VMEM_LIMIT_BYTES)


def _dot(a, b):
    return jnp.dot(a, b, preferred_element_type=F32)


def _pieces(a, n):
    if isinstance(a, tuple):
        return a
    out = []
    for i in range(n):
        hi = a.astype(BF16)
        out.append(hi)
        if i + 1 < n:
            a = a - hi.astype(F32)
    return tuple(out)


def _mm(a, b, dims=NN, n=MIX_PIECES):
    a = _pieces(a, n)
    b = _pieces(b, n)
    order = max(len(a), len(b)) - 1
    out = None
    for i, ai in enumerate(a):
        for j, bj in enumerate(b):
            if i + j <= order:
                t = lax.dot_general(ai, bj, dims, preferred_element_type=F32)
                out = t if out is None else out + t
    return out


def _mm_fused(a, b):
    a_hi, a_lo = _pieces(a, 2)
    b_hi, b_lo = _pieces(b, 2)
    return _dot(jnp.concatenate([a_hi, a_hi, a_lo], axis=1), jnp.concatenate([b_hi, b_lo, b_hi], axis=0))


def _cumsum_matmul(mask, x):
    m = mask.astype(BF16)
    hi, lo = _pieces(x, 2)
    return _dot(jnp.concatenate([m, m], axis=1), jnp.concatenate([hi, lo], axis=0))


def _rmsnorm(x, g):
    return x * lax.rsqrt(jnp.mean(x * x, axis=-1, keepdims=True) + EPS) * g


def _iota2(shape, dim):
    return lax.broadcasted_iota(jnp.int32, shape, dim)


def _block_ones(n, width):
    return (_iota2((n, n), 0) // width == _iota2((n, n), 1) // width).astype(F32)


def _interleave(*gens):
    results = [None] * len(gens)
    live = list(range(len(gens)))
    while live:
        for i in list(live):
            try:
                next(gens[i])
            except StopIteration as stop:
                results[i] = stop.value
                live.remove(i)
    return results


def _load_weight(w_hbm, layer, dst_ref, stage_ref, sem_ref, slab_rows):
    n_rows = dst_ref.shape[0]
    assert n_rows % slab_rows == 0 and slab_rows <= stage_ref.shape[1]
    n_slabs = n_rows // slab_rows

    def copy(s):
        return pltpu.make_async_copy(w_hbm.at[layer, pl.ds(s * slab_rows, slab_rows), :],
                                     stage_ref.at[s % 2, pl.ds(0, slab_rows), :], sem_ref.at[s % 2])

    copy(0).start()
    for s in range(n_slabs):
        if s + 1 < n_slabs:
            copy(s + 1).start()
        copy(s).wait()
        dst_ref[pl.ds(s * slab_rows, slab_rows), :] = stage_ref[s % 2, pl.ds(0, slab_rows), :].astype(BF16)


def _ffn_kernel(x_ref, *refs, n_y, layer, out_layer, final_norm, tf):
    y_refs = refs[:n_y]
    has_out = n_y > 0
    (wo_hbm,) = refs[n_y:n_y + 1] if has_out else (None,)
    g_ref, wg_hbm, wu_hbm, wd_hbm, fg_ref, o_ref = refs[n_y + has_out:n_y + has_out + 6]
    scratch = refs[n_y + has_out + 6:]
    wg_ref, wu_ref, wd_ref, wide_stage, tall_stage, sem, wo_sem = scratch[:7]
    wo_ref = scratch[7] if has_out else None
    n_slabs = wg_ref.shape[1] // tf

    def slab_copies(j):
        slot, cols = j % 2, pl.ds(j * tf, tf)
        return (pltpu.make_async_copy(wg_hbm.at[layer, :, cols], wide_stage.at[slot, 0], sem.at[slot, 0]),
                pltpu.make_async_copy(wu_hbm.at[layer, :, cols], wide_stage.at[slot, 1], sem.at[slot, 1]),
                pltpu.make_async_copy(wd_hbm.at[layer, cols, :], tall_stage.at[slot], sem.at[slot, 2]))

    def body(first_step):
        if first_step:
            for j in range(min(2, n_slabs)):
                for cp in slab_copies(j):
                    cp.start()
            if has_out:
                _load_weight(wo_hbm, out_layer, wo_ref, wo_ref_stage, wo_sem, wo_ref.shape[0] // 4)
        x = x_ref[...]
        row0 = 0
        for y_ref in y_refs:
            rows = y_ref.shape[1]
            x = x + _dot(y_ref[...].astype(BF16), wo_ref[row0:row0 + rows, :])
            row0 += rows
        h = _rmsnorm(x, g_ref[...]).astype(BF16)
        acc = None
        for j in range(n_slabs):
            c0 = j * tf
            if first_step:
                for cp in slab_copies(j):
                    cp.wait()
                wg_ref[:, c0:c0 + tf] = wide_stage[j % 2, 0].astype(BF16)
                wu_ref[:, c0:c0 + tf] = wide_stage[j % 2, 1].astype(BF16)
                wd_ref[c0:c0 + tf, :] = tall_stage[j % 2].astype(BF16)
                if j + 2 < n_slabs:
                    for cp in slab_copies(j + 2):
                        cp.start()
            gate = _dot(h, wg_ref[:, c0:c0 + tf])
            up = _dot(h, wu_ref[:, c0:c0 + tf])
            act = (gate * jax.nn.sigmoid(gate) * up).astype(BF16)
            part = _dot(act, wd_ref[c0:c0 + tf, :])
            acc = part if acc is None else acc + part
        out = x + 0.5 * acc
        if final_norm:
            out = _rmsnorm(out, fg_ref[...])
        o_ref[...] = out

    wo_ref_stage = scratch[8] if has_out else None
    pl.when(pl.program_id(0) == 0)(lambda: body(True))
    pl.when(pl.program_id(0) != 0)(lambda: body(False))


def _resident(shape):
    return pl.BlockSpec(shape, lambda i: (0,) * len(shape), pipeline_mode=pl.Buffered(1))


def _ffn(x, ys, wo, out_layer, g, wg, wu, wd, layer, final_g, *, final_norm, tm=512, tf=256):
    m, d = x.shape
    f = wg.shape[2]
    tm = min(tm, m)
    assert f % tf == 0
    hbm = pl.BlockSpec(memory_space=pl.ANY)
    has_out = len(ys) > 0
    scratch = [pltpu.VMEM((d, f), BF16), pltpu.VMEM((d, f), BF16), pltpu.VMEM((f, d), BF16),
               pltpu.VMEM((2, 2, d, tf), F32), pltpu.VMEM((2, tf, d), F32),
               pltpu.SemaphoreType.DMA((2, 3)), pltpu.SemaphoreType.DMA((2,))]
    if has_out:
        assert sum(y.shape[1] for y in ys) == wo.shape[1] and wo.shape[1] % 4 == 0
        scratch += [pltpu.VMEM(wo.shape[1:], BF16), pltpu.VMEM((2, wo.shape[1] // 4, wo.shape[2]), F32)]
    return pl.pallas_call(
        functools.partial(_ffn_kernel, n_y=len(ys), layer=layer, out_layer=out_layer, final_norm=final_norm,
                          tf=tf),
        grid=(m // tm,),
        in_specs=[pl.BlockSpec((tm, d), lambda i: (i, 0))]
        + [pl.BlockSpec((tm, y.shape[1]), lambda i: (i, 0)) for y in ys]
        + ([hbm] if has_out else [])
        + [_resident((1, d)), hbm, hbm, hbm, _resident((1, d))],
        out_specs=pl.BlockSpec((tm, d), lambda i: (i, 0)),
        out_shape=jax.ShapeDtypeStruct((m, d), F32),
        scratch_shapes=scratch,
        compiler_params=_params("arbitrary"),
        name="ffn",
    )(x, *ys, *([wo] if has_out else []), g.reshape(1, d), wg, wu, wd, final_g.reshape(1, d))


def _inproj_kernel(x_ref, g_ref, w_hbm, *refs, layer, tn, shift_seq, gate_layer):
    has_shift = shift_seq is not None
    has_gates = gate_layer is not None
    aux_ref = refs[0] if (has_shift or has_gates) else None
    refs = refs[(has_shift or has_gates):]
    n_scratch = 3 + has_shift
    o_refs = refs[:-n_scratch]
    w_ref, stage, sem = refs[-n_scratch:][:3]
    carry_ref = refs[-1] if has_shift else None

    @pl.when(pl.program_id(0) == 0)
    def _():
        _load_weight(w_hbm, layer, w_ref, stage, sem, stage.shape[1])
        if has_shift:
            carry_ref[...] = jnp.zeros_like(carry_ref)

    h = _rmsnorm(x_ref[...], g_ref[...]).astype(BF16)
    col0 = 0
    for gi, o_ref in enumerate(o_refs):
        n = o_ref.shape[1]
        chunk = lambda c0, col0=col0, n=n: _dot(h, w_ref[:, col0 + c0:col0 + min(c0 + tn, n)])
        if gi == 0 and has_shift:
            p = jnp.concatenate([chunk(c0) for c0 in range(0, n, tn)], axis=1)
            tm = p.shape[0]
            first = (pl.program_id(0) * tm) % shift_seq == 0
            prev_last = jnp.where(first, 0.0, carry_ref[...])
            prev = jnp.where(_iota2((tm, 1), 0) == 0, prev_last, pltpu.roll(p, 1, axis=0))
            carry_ref[...] = p[tm - 1:tm, :]
            o_ref[...] = p + (prev - p) * aux_ref[...]
        elif has_gates:
            gate = _hgrn_gate_fn(aux_ref[...], gate_layer, n // 4, tn)
            for i in sorted(range(n // tn), key=lambda i: gate(i, None) is None):
                part = chunk(i * tn)
                activated = gate(i, part)
                o_ref[:, i * tn:(i + 1) * tn] = part if activated is None else activated
        else:
            for c0 in range(0, n, tn):
                o_ref[:, c0:min(c0 + tn, n)] = chunk(c0)
        col0 += n


def _hgrn_gate_fn(logits, layer, dim, tn):
    assert dim % tn == 0
    e = jnp.exp(logits - jnp.max(logits, axis=0, keepdims=True))
    sm = e / jnp.sum(e, axis=0, keepdims=True)
    lb = jnp.sum(sm[0:layer + 1, :], axis=0, keepdims=True) - sm[0:1, :]

    def gate(i, part):
        c0 = i * tn
        if dim <= c0 < 2 * dim:
            if part is None:
                return True
            lb_c = lb[:, c0 - dim:c0 - dim + tn]
            return lb_c + (1.0 - lb_c) * jax.nn.sigmoid(part)
        if c0 >= 3 * dim:
            return True if part is None else jax.nn.sigmoid(part)
        return None

    return gate


def _inproj(x, g, w, layer, widths, *, shift_mu=None, shift_seq=None, gate_logits=None, gate_layer=None, tm=512,
            tn=512, n_slabs=8):
    m, d = x.shape
    n_total = w.shape[2]
    assert sum(widths) == n_total and d % n_slabs == 0
    tm = min(tm, m)
    has_shift, has_gates = shift_mu is not None, gate_logits is not None
    assert not (has_shift and has_gates)
    assert not has_shift or shift_seq % tm == 0
    assert not has_gates or len(widths) == 1
    aux = [shift_mu.reshape(1, -1)] if has_shift else ([gate_logits] if has_gates else [])
    return pl.pallas_call(
        functools.partial(_inproj_kernel, layer=layer, tn=tn, shift_seq=shift_seq if has_shift else None,
                          gate_layer=gate_layer if has_gates else None),
        grid=(m // tm,),
        in_specs=[pl.BlockSpec((tm, d), lambda i: (i, 0)), _resident((1, d)), pl.BlockSpec(memory_space=pl.ANY)]
        + [_resident(a.shape) for a in aux],
        out_specs=[pl.BlockSpec((tm, n), lambda i: (i, 0)) for n in widths],
        out_shape=[jax.ShapeDtypeStruct((m, n), F32) for n in widths],
        scratch_shapes=[pltpu.VMEM((d, n_total), BF16), pltpu.VMEM((2, d // n_slabs, n_total), F32),
                        pltpu.SemaphoreType.DMA((2,))]
        + ([pltpu.VMEM((1, widths[0]), F32)] if has_shift else []),
        compiler_params=_params("arbitrary"),
        name="inproj",
    )(x, g.reshape(1, d), w, *aux)


def _bdot(a, b, dims):
    return lax.dot_general(a.astype(BF16), b.astype(BF16), dims, preferred_element_type=F32)


def _block_diag(x, half):
    lo = _iota2((1, 1, 2 * half), 2) < half
    zero = jnp.zeros((), x.dtype)
    return jnp.concatenate([jnp.where(lo, x, zero), jnp.where(lo, zero, x)], axis=1)


def _unit_lower_inverse(a_strict, c):
    row = _iota2((1, c, 2 * c), 1)
    col = _iota2((1, c, 2 * c), 2) & (c - 1)
    eye = (row == col).astype(F32)
    t = None
    m = 1
    while m < c:
        mask = ((row // (2 * m)) == (col // (2 * m))) & ((row & m) != 0) & ((col & m) == 0)
        lm = jnp.where(mask, a_strict, 0.0)
        if t is None:
            t = eye - lm
        else:
            t = t - _bdot(_bdot(t, _block_diag(lm.astype(BF16), c), BNN), _block_diag(t.astype(BF16), c), BNN)
            yield
        m *= 2
    return t


def _rwkv_kernel(xs_ref, w0_ref, w2_ref, a0_ref, a2_ref, g2_ref, kk_ref, ka_ref, rk_ref,
                 lnw_ref, lnb_ref, o_ref, st_ref, *, chunk, n_chunks, n_groups, heads, hd):
    c = chunk

    @pl.when(pl.program_id(1) == 0)
    def _():
        st_ref[...] = jnp.zeros_like(st_ref)

    rows = n_chunks * c
    front_refs = (w0_ref, w2_ref, a0_ref, a2_ref, g2_ref, kk_ref, ka_ref)
    dims = dict(c=c, n_chunks=n_chunks, heads=heads, hd=hd)
    front = lambda gi: _rwkv_front(xs_ref[0, gi * rows:(gi + 1) * rows, :], *front_refs, **dims)
    s = st_ref[...]
    (ready,) = _interleave(front(0))
    for gi in range(n_groups):
        back = _rwkv_back(ready, s, rk_ref, lnw_ref, lnb_ref, **dims)
        if gi + 1 < n_groups:
            (out, s), ready = _interleave(back, front(gi + 1))
        else:
            ((out, s),) = _interleave(back)
        o_ref[0, gi * rows:(gi + 1) * rows, :] = out
    st_ref[...] = s


def _head_sum(t, hd):
    tile_ones = _block_ones(2 * hd, hd)
    return jnp.concatenate([_bdot(t[:, i:i + 2 * hd], tile_ones, NN) for i in range(0, t.shape[1], 2 * hd)], axis=1)


def _rwkv_front(xs, w0_ref, w2_ref, a0_ref, a2_ref, g2_ref, kk_ref, ka_ref, *, c, n_chunks, heads, hd):
    rows = n_chunks * c
    dim = heads * hd
    r = xs[:, 0:dim]
    k = xs[:, dim:2 * dim]
    v = xs[:, 2 * dim:3 * dim]
    o1 = 3 * dim
    g_lr = xs[:, o1 + LORA_W + LORA_A:o1 + LORA_W + LORA_A + LORA_G]

    wa = xs[:, o1:o1 + LORA_W + LORA_A]
    wa = jnp.where(_iota2(wa.shape, 1) < LORA_W, jnp.tanh(wa), wa)
    w2a2 = jnp.concatenate(
        [jnp.concatenate([w2_ref[...], jnp.zeros((LORA_W, dim), F32)], axis=1),
         jnp.concatenate([jnp.zeros((LORA_A, dim), F32), a2_ref[...]], axis=1)], axis=0)
    za = _mm_fused(wa, w2a2)
    z = w0_ref[...] + za[:, :dim]
    softplus = jnp.maximum(-z, 0.0) + jnp.log(1.0 + jnp.exp(-jnp.abs(z)))
    w_raw = -softplus - 0.5
    lw = -jnp.exp(w_raw)
    yield
    a = jax.nn.sigmoid(a0_ref[...] + za[:, dim:])
    g = _mm_fused(jax.nn.sigmoid(g_lr), g2_ref[...])
    yield
    kk = k * kk_ref[...]
    kk = kk * lax.rsqrt(jnp.maximum(_head_sum(kk * kk, hd), 1e-24))
    k2 = k * (1.0 + (a - 1.0) * ka_ref[...])
    bb = kk * a
    yield

    r_i, c_i = _iota2((rows, rows), 0), _iota2((rows, rows), 1)
    tri_incl = (r_i >= c_i) & (r_i // c == c_i // c)
    cum = _cumsum_matmul(tri_incl, lw)
    ends = [cum[(i + 1) * c - 1:(i + 1) * c, :] for i in range(n_chunks)]
    cum_last = jnp.concatenate([jnp.broadcast_to(e, (c, dim)) for e in ends], axis=0)
    yield
    w_incl = jnp.exp(cum)
    w_excl = jnp.exp(cum - lw)
    yield
    w_inv = jnp.exp(-cum)
    w_tail = jnp.exp(cum_last - cum)
    w_all = jnp.exp(jnp.concatenate(ends, axis=0))
    yield
    ops = dict(kt=kk * w_excl, rt=r * w_incl, bt=bb * w_inv)
    yield
    ops.update(kd=k2 * w_inv, bw=bb * w_tail, kw=k2 * w_tail)
    yield
    ops.update(v=v, w_all=w_all, r=r, k2=k2, g=g)
    return ops


def _rwkv_back(ops, s, rk_ref, lnw_ref, lnb_ref, *, c, n_chunks, heads, hd):
    pw = 2 * hd
    pairs = heads // 2
    def by_pair(t, n_rows=c):
        return jnp.stack([t[i * n_rows:(i + 1) * n_rows, j * pw:(j + 1) * pw]
                          for i in range(n_chunks) for j in range(pairs)], axis=0)
    bd_f = lambda t: _block_diag(t.astype(BF16), hd)
    kt_p, rt_p, bt_p, kd_p, kw_p, bw_p, v_p = (by_pair(ops[n]) for n in ("kt", "rt", "bt", "kd", "kw", "bw", "v"))
    w_all_p = by_pair(ops["w_all"], 1)
    row_t = _iota2((1, c, 2 * c), 1)
    col_t = _iota2((1, c, 2 * c), 2) & (c - 1)
    strict_p, incl_p = row_t > col_t, row_t >= col_t
    same_head = (_iota2((1, pw, pw), 1) // hd) == (_iota2((1, pw, pw), 2) // hd)

    kr = jnp.concatenate([kt_p, rt_p], axis=1)
    g_all = _bdot(kr, jnp.concatenate([bd_f(bt_p), bd_f(kd_p)], axis=1), BNT)
    yield
    gb, gk = g_all[:, :, :2 * c], g_all[:, :, 2 * c:]
    a_b = jnp.where(strict_p, gb[:, :c], 0.0)
    a_k = jnp.where(strict_p, gk[:, :c], 0.0)
    p_b = jnp.where(incl_p, gb[:, c:], 0.0)
    p_k = jnp.where(incl_p, gk[:, c:], 0.0)
    akpk = _bdot(jnp.concatenate([a_k, p_k], axis=1), bd_f(v_p), BNN)
    yield
    t_inv = yield from _unit_lower_inverse(a_b, c)
    kv1 = _bdot(t_inv, jnp.concatenate([bd_f(kt_p), bd_f(akpk[:, :c])], axis=2), BNN)
    yield
    kt1, v1 = kv1[:, :, :pw], kv1[:, :, pw:]
    pbk = _bdot(p_b, jnp.concatenate([bd_f(kt1), bd_f(v1)], axis=2), BNN)
    yield
    q_mat = rt_p - pbk[:, :, :pw]
    z_mat = akpk[:, c:] - pbk[:, :, pw:]
    x_mat = jnp.where(same_head, _bdot(bw_p, kt1, BTN), 0.0)
    yield
    n_mat = jnp.where(same_head, _bdot(jnp.concatenate([v_p, v1], axis=1),
                                       jnp.concatenate([kw_p, -bw_p], axis=1), BTN), 0.0)
    yield
    y_rows = []
    for i in range(n_chunks):
        sl = slice(i * pairs, (i + 1) * pairs)
        y_p = _bdot(q_mat[sl], s, BNT) + z_mat[sl]
        s = s * w_all_p[sl] - _bdot(s, x_mat[sl], BNT) + n_mat[sl]
        y_rows.append(jnp.concatenate([y_p[j] for j in range(pairs)], axis=1))
        yield
    y = jnp.concatenate(y_rows, axis=0)

    inv_hd = 1.0 / hd
    mean = _head_sum(y, hd) * inv_hd
    yc = y - mean
    var = _head_sum(yc * yc, hd) * inv_hd
    yield
    yn = yc * lax.rsqrt(var + LNX_EPS) * lnw_ref[...] + lnb_ref[...]
    bonus = _head_sum(ops["r"] * ops["k2"] * rk_ref[...], hd) * ops["v"]
    return (yn + bonus) * ops["g"], s


def _rwkv(xs, w0, w2, a0, a2, g2, k_k, k_a, r_k, lnx_w, lnx_b, *, chunk=RWKV_CHUNK,
          n_chunks=RWKV_CHUNKS_PER_GROUP, n_groups=RWKV_GROUPS_PER_STEP):
    b, s, cols = xs.shape
    heads, hd = RWKV_HEADS, RWKV_HD
    dim = heads * hd
    chunk = min(chunk, s)
    assert chunk & (chunk - 1) == 0
    n_chunks = min(n_chunks, s // chunk)
    n_groups = min(n_groups, s // (chunk * n_chunks))
    rows = chunk * n_chunks * n_groups
    row = lambda t: t.reshape(1, -1)
    vecs = [row(w0), w2, row(a0), a2, g2, row(k_k), row(k_a), row(r_k), row(lnx_w), row(lnx_b)]
    return pl.pallas_call(
        functools.partial(_rwkv_kernel, chunk=chunk, n_chunks=n_chunks, n_groups=n_groups, heads=heads, hd=hd),
        grid=(b, s // rows),
        in_specs=[pl.BlockSpec((1, rows, cols), lambda i, j: (i, j, 0))]
        + [pl.BlockSpec(t.shape, lambda i, j: (0, 0)) for t in vecs],
        out_specs=pl.BlockSpec((1, rows, dim), lambda i, j: (i, j, 0)),
        out_shape=jax.ShapeDtypeStruct((b, s, dim), F32),
        scratch_shapes=[pltpu.VMEM((heads // 2, 2 * hd, 2 * hd), F32)],
        compiler_params=_params("parallel", "arbitrary"),
        name="rwkv",
    )(xs, *vecs)


def _moba_kernel(q_ref, k_ref, v_ref, o_ref, km_ref, ka_ref, vat_ref, *, nb, blk, n_sel, heads, hd):
    j = pl.program_id(1)
    nbp = km_ref.shape[0]
    scale = hd ** -0.5
    masked = -1e30
    neg_inf = float("-inf")
    slopes = [2.0 ** (-8.0 * (h + 1) / heads) for h in range(heads)]

    @pl.when(j == 0)
    def _():
        km_ref[...] = jnp.zeros_like(km_ref)
        for n in range(nb):
            km_ref[n:n + 1, :] = jnp.mean(k_ref[0, n * blk:(n + 1) * blk, :], axis=0, keepdims=True)
        col = _iota2((blk, 2 * hd), 0).astype(F32)
        lane = _iota2((blk, 2 * hd), 1)
        feat_lane = lane & (hd - 1)
        for n in range(nb):
            rows = slice(n * blk, (n + 1) * blk)
            one_hot = jnp.where(feat_lane == n, 1.0, 0.0)
            for h in range(heads):
                k_feat = (one_hot + jnp.where(feat_lane == nbp, slopes[h] * col, 0.0)
                          + jnp.where(feat_lane == nbp + 1, slopes[h] * blk * n, 0.0))
                k_tile = k_ref[0, rows, (h // 2) * 2 * hd:(h // 2 + 1) * 2 * hd]
                own_lanes = (lane < hd) if h % 2 == 0 else (lane >= hd)
                ka_ref[h, rows, :] = jnp.where(own_lanes, k_tile, k_feat).astype(BF16)
        ones_row = jnp.where(_iota2((vat_ref.shape[2] - hd, blk), 0) == 0, 1.0, 0.0)
        for n in range(nb):
            vt = v_ref[0, n * blk:(n + 1) * blk, :].T
            for h in range(heads):
                vat_ref[n, h] = jnp.concatenate([vt[h * hd:(h + 1) * hd], ones_row], axis=0).astype(BF16)

    qt = (q_ref[0] * scale).T
    sub = _iota2((nbp, blk), 0)
    const_rows = jnp.where(_iota2((hd - nbp, blk), 0) < 2, 1.0, 0.0)
    q_aug = []
    for h in range(heads):
        sl = slice(h * hd, (h + 1) * hd)
        qh = qt[sl]
        gate = _mm(km_ref[:, sl], qh)
        gate = jnp.where(sub < j, gate, neg_inf)
        cnt = jnp.zeros((nbp, blk), jnp.int32)
        for m in range(nb):
            other = gate[m:m + 1, :]
            beats = (other > gate) | ((other == gate) & (m < sub))
            cnt = cnt + beats.astype(jnp.int32)
        keep = ((sub < j) & (cnt < n_sel)) | (sub == j) | (sub >= nb)
        bias = jnp.where(keep, 0.0, masked)
        parts = [qh, bias, const_rows] if h % 2 == 0 else [bias, const_rows, qh]
        q_aug.append(jnp.concatenate(parts, axis=0).astype(BF16))
    q_aug = jnp.stack(q_aug, axis=0)

    def scores(n):
        start = pl.multiple_of(n * blk, blk)
        kn = ka_ref[:, pl.ds(start, blk), :]
        return lax.dot_general(kn, q_aug, BNN, preferred_element_type=F32), vat_ref[n]

    s, vj = scores(j)
    causal = _iota2((1, blk, blk), 1) <= _iota2((1, blk, blk), 2)
    s = jnp.where(causal, s, masked)
    m0 = jnp.max(s, axis=1, keepdims=True)
    acc0 = lax.dot_general(vj, jnp.exp(s - m0).astype(BF16), BNN, preferred_element_type=F32)

    def past_block(n, carry):
        m_run, acc = carry
        sc, vn = scores(n)
        m_new = jnp.maximum(m_run, jnp.max(sc, axis=1, keepdims=True))
        pr = jnp.exp(sc - m_new).astype(BF16)
        acc_new = jnp.exp(m_run - m_new) * acc + lax.dot_general(vn, pr, BNN, preferred_element_type=F32)
        return m_new, acc_new

    _, acc = lax.fori_loop(0, j, past_block, (m0, acc0))
    out_t = jnp.concatenate([acc[h, :hd] / acc[h, hd:hd + 1] for h in range(heads)], axis=0)
    o_ref[0] = out_t.T


def _moba(q, k, v):
    b, s, dim = q.shape
    heads, hd, blk = MOBA_HEADS, MOBA_HD, MOBA_BLOCK
    assert s % blk == 0
    nb = s // blk
    nbp = -(-nb // SUBLANES) * SUBLANES
    assert nbp + 2 <= hd
    n_sel = min(MOBA_TOPK, nb - 1)
    return pl.pallas_call(
        functools.partial(_moba_kernel, nb=nb, blk=blk, n_sel=n_sel, heads=heads, hd=hd),
        grid=(b, nb),
        in_specs=[
            pl.BlockSpec((1, blk, dim), lambda i, j: (i, j, 0)),
            pl.BlockSpec((1, s, dim), lambda i, j: (i, 0, 0)),
            pl.BlockSpec((1, s, dim), lambda i, j: (i, 0, 0)),
        ],
        out_specs=pl.BlockSpec((1, blk, dim), lambda i, j: (i, j, 0)),
        out_shape=jax.ShapeDtypeStruct((b, s, dim), F32),
        scratch_shapes=[pltpu.VMEM((nbp, dim), F32),
                        pltpu.VMEM((heads, s, 2 * hd), BF16),
                        pltpu.VMEM((nb, heads, hd + BF16_SUBLANES, blk), BF16)],
        compiler_params=_params("parallel", "arbitrary"),
        name="moba",
    )(q, k, v)


def _hgrn_kernel(q_ref, f_ref, i_ref, g_ref, nw_ref, o_ref, st_ref, *, chunk, n_chunks, n_groups, heads, dk):
    c = chunk

    @pl.when(pl.program_id(1) == 0)
    def _():
        st_ref[...] = jnp.zeros_like(st_ref)

    rows = n_chunks * c
    dims = dict(c=c, n_chunks=n_chunks, heads=heads, dk=dk)
    grp = lambda ref, gi: ref[0, gi * rows:(gi + 1) * rows, :]
    front = lambda gi: _hgrn_front(grp(q_ref, gi), grp(f_ref, gi), grp(i_ref, gi), **dims)
    s = st_ref[...]
    (ready,) = _interleave(front(0))
    for gi in range(n_groups):
        back = _hgrn_back(ready, s, grp(g_ref, gi), nw_ref[...], **dims)
        if gi + 1 < n_groups:
            (out, s), ready = _interleave(back, front(gi + 1))
        else:
            ((out, s),) = _interleave(back)
        o_ref[0, gi * rows:(gi + 1) * rows, :] = out
    st_ref[...] = s


def _hgrn_front(q, f, v, *, c, n_chunks, heads, dk):
    rows = n_chunks * c
    dim = heads * dk
    lf = jnp.log(f)
    kf = 1.0 - f
    yield
    r_i, c_i = _iota2((rows, rows), 0), _iota2((rows, rows), 1)
    tri_all = (r_i >= c_i) & (r_i // c == c_i // c)
    b = _cumsum_matmul(tri_all, lf)
    per_chunk = lambda r: jnp.concatenate(
        [jnp.broadcast_to(b[i * c + r:i * c + r + 1, :], (c, dim)) for i in range(n_chunks)], axis=0)
    ends = [b[(i + 1) * c - 1:(i + 1) * c, :] for i in range(n_chunks)]
    b_last = per_chunk(c - 1)
    yield
    b_mid = per_chunk(c // 2 - 1)
    rel = b - b_mid
    qd = q * jnp.exp(rel)
    yield
    kd = kf * jnp.exp(-rel)
    yield
    q_in = qd * jnp.exp(b_mid)
    yield
    kw = kf * jnp.exp(b_last - b)
    w_all = jnp.exp(jnp.concatenate(ends, axis=0))
    return dict(qd=qd, kd=kd, q_in=q_in, kw=kw, w_all=w_all, v=v)


def _hgrn_back(ops, s, g, norm_w, *, c, n_chunks, heads, dk):
    def by_head(t, n_rows=c):
        return jnp.stack([t[i * n_rows:(i + 1) * n_rows, h * dk:(h + 1) * dk]
                          for i in range(n_chunks) for h in range(heads)], axis=0)
    hp, pw = heads // 2, 2 * dk
    by_pair = lambda t: jnp.stack([t[i * c:(i + 1) * c, j * pw:(j + 1) * pw]
                                   for i in range(n_chunks) for j in range(hp)], axis=0)
    tri_pair = _iota2((1, c, 2 * c), 1) >= (_iota2((1, c, 2 * c), 2) & (c - 1))
    q_hi, q_lo = _pieces(by_pair(ops["qd"]), 2)
    k_hi, k_lo = _pieces(_block_diag(by_pair(ops["kd"]), dk), 2)
    sc = lax.dot_general(jnp.concatenate([q_hi, q_hi, q_lo], axis=2), jnp.concatenate([k_hi, k_lo, k_hi], axis=2),
                         BNT, preferred_element_type=F32)
    sc = jnp.where(tri_pair, sc, 0.0)
    yield
    o_pair = _bdot(sc, _block_diag(by_pair(ops["v"]).astype(BF16), dk), BNN)
    yield
    qd_h, v_h = by_head(ops["q_in"]), by_head(ops["v"])
    s_add = _bdot(v_h, by_head(ops["kw"]), BTN)
    yield
    w_all_h = by_head(ops["w_all"], 1)
    o_rows = []
    for i in range(n_chunks):
        sl = slice(i * heads, (i + 1) * heads)
        o_intra = jnp.stack([o_pair[i * hp + h // 2][:, (h % 2) * dk:(h % 2 + 1) * dk]
                             for h in range(heads)], axis=0)
        o_h = o_intra + _bdot(qd_h[sl], s, BNT)
        s = s * w_all_h[sl] + s_add[sl]
        o_h = o_h * lax.rsqrt(jnp.mean(o_h * o_h, axis=-1, keepdims=True) + EPS)
        o_rows.append(jnp.concatenate([o_h[h] for h in range(heads)], axis=1))
        yield
    o = jnp.concatenate(o_rows, axis=0)
    return o * norm_w * g, s


def _hgrn(p, norm_w, *, chunk=HG_CHUNK, n_chunks=HG_CHUNKS_PER_GROUP, n_groups=HG_GROUPS_PER_STEP):
    b, s, cols = p.shape
    heads, dk = HG_HEADS, HG_DK
    dim = heads * dk
    assert cols == 4 * dim
    chunk = min(chunk, s)
    assert chunk & (chunk - 1) == 0
    n_chunks = min(n_chunks, s // chunk)
    n_groups = min(n_groups, s // (chunk * n_chunks))
    rows = chunk * n_chunks * n_groups
    col_block = lambda n: pl.BlockSpec((1, rows, dim), lambda i, j, n=n: (i, j, n))
    return pl.pallas_call(
        functools.partial(_hgrn_kernel, chunk=chunk, n_chunks=n_chunks, n_groups=n_groups, heads=heads, dk=dk),
        grid=(b, s // rows),
        in_specs=[col_block(0), col_block(1), col_block(2), col_block(3),
                  pl.BlockSpec((1, dim), lambda i, j: (0, 0))],
        out_specs=pl.BlockSpec((1, rows, dim), lambda i, j: (i, j, 0)),
        out_shape=jax.ShapeDtypeStruct((b, s, dim), F32),
        scratch_shapes=[pltpu.VMEM((heads, dk, dk), F32)],
        compiler_params=_params("parallel", "arbitrary"),
        name="hgrn",
    )(p, p, p, p, norm_w.reshape(1, dim))


def kernel(x, norm_g, ffn1_wg, ffn1_wu, ffn1_wd, ffn2_wg, ffn2_wu, ffn2_wd, ev_w_in, ev_w_out, rw_mu, rw_w0, rw_w2, rw_a0, rw_a2, rw_g2, rw_k_k, rw_k_a, rw_r_k, rw_lnx_w, rw_lnx_b, od_w_in, od_w_out, hg_norm_w, hg_lb_logits, final_g):
    bsz, seq, d = x.shape
    depth = norm_g.shape[0]
    rwkv_dim = RWKV_HEADS * RWKV_HD
    rwkv_cols = 3 * rwkv_dim + LORA_W + LORA_A + LORA_G
    moba_dim = MOBA_HEADS * MOBA_HD
    xf = x.reshape(bsz * seq, d)
    for l in range(depth):
        xf = _ffn(xf, [], None, 0, norm_g[l, 0], ffn1_wg, ffn1_wu, ffn1_wd, l, final_g, final_norm=False)
        if l % 2 == 0:
            e = l // 2
            xs_r, q, k, v = _inproj(xf, norm_g[l, 1], ev_w_in, e, [rwkv_cols, moba_dim, moba_dim, moba_dim],
                                    shift_mu=rw_mu[e], shift_seq=seq)
            y_a = _rwkv(xs_r.reshape(bsz, seq, rwkv_cols), rw_w0[e], rw_w2[e], rw_a0[e], rw_a2[e],
                        rw_g2[e], rw_k_k[e], rw_k_a[e], rw_r_k[e], rw_lnx_w[e], rw_lnx_b[e])
            y_b = _moba(q.reshape(bsz, seq, moba_dim), k.reshape(bsz, seq, moba_dim),
                        v.reshape(bsz, seq, moba_dim))
            ys, wo, wo_layer = [y_a.reshape(-1, rwkv_dim), y_b.reshape(-1, moba_dim)], ev_w_out, e
        else:
            o = l // 2
            (p,) = _inproj(xf, norm_g[l, 1], od_w_in, o, [od_w_in.shape[2]], gate_logits=hg_lb_logits, gate_layer=l)
            y = _hgrn(p.reshape(bsz, seq, -1), hg_norm_w[o])
            ys, wo, wo_layer = [y.reshape(bsz * seq, -1)], od_w_out, o
        xf = _ffn(xf, ys, wo, wo_layer, norm_g[l, 2], ffn2_wg, ffn2_wu, ffn2_wd, l, final_g,
                  final_norm=(l == depth - 1))
    return xf.reshape(bsz, seq, d)
```

```python
import functools

import jax
import jax.numpy as jnp
from jax import lax
from jax.experimental import pallas as pl
from jax.experimental.pallas import tpu as pltpu

F32 = jnp.float32
BF16 = jnp.bfloat16
NN = (((1,), (0,)), ((), ()))
BNN = (((2,), (1,)), ((0,), (0,)))
BNT = (((2,), (2,)), ((0,), (0,)))
BTN = (((1,), (1,)), ((0,), (0,)))
MIX_PIECES = 2

EPS = 1e-6
LNX_EPS = 64e-5
RWKV_HEADS = 8
RWKV_HD = 64
LORA_W = 64
LORA_A = 64
LORA_G = 128
MOBA_HEADS = 8
MOBA_HD = 64
MOBA_BLOCK = 256
MOBA_TOPK = 3
HG_HEADS = 8
HG_DK = 128
RWKV_CHUNK = 64
RWKV_CHUNKS_PER_GROUP = 4
RWKV_GROUPS_PER_STEP = 2
HG_CHUNK = 64
HG_CHUNKS_PER_GROUP = 4
HG_GROUPS_PER_STEP = 2
SUBLANES = 8
BF16_SUBLANES = 16
V7X_VMEM_BYTES = 64 * 1024 * 1024
VMEM_LIMIT_BYTES = V7X_VMEM_BYTES - 8 * 1024 * 1024


def _params(*semantics):
    return pltpu.CompilerParams(dimension_semantics=semantics, vmem_limit_bytes=VMEM_LIMIT_BYTES)


def _dot(a, b):
    return jnp.dot(a, b, preferred_element_type=F32)


def _pieces(a, n):
    if isinstance(a, tuple):
        return a
    out = []
    for i in range(n):
        hi = a.astype(BF16)
        out.append(hi)
        if i + 1 < n:
            a = a - hi.astype(F32)
    return tuple(out)


def _mm(a, b, dims=NN, n=MIX_PIECES):
    a = _pieces(a, n)
    b = _pieces(b, n)
    order = max(len(a), len(b)) - 1
    out = None
    for i, ai in enumerate(a):
        for j, bj in enumerate(b):
            if i + j <= order:
                t = lax.dot_general(ai, bj, dims, preferred_element_type=F32)
                out = t if out is None else out + t
    return out


def _mm_fused(a, b):
    a_hi, a_lo = _pieces(a, 2)
    b_hi, b_lo = _pieces(b, 2)
    return _dot(jnp.concatenate([a_hi, a_hi, a_lo], axis=1), jnp.concatenate([b_hi, b_lo, b_hi], axis=0))


def _cumsum_matmul(mask, x):
    m = mask.astype(BF16)
    hi, lo = _pieces(x, 2)
    return _dot(jnp.concatenate([m, m], axis=1), jnp.concatenate([hi, lo], axis=0))


def _rmsnorm(x, g):
    return x * lax.rsqrt(jnp.mean(x * x, axis=-1, keepdims=True) + EPS) * g


def _iota2(shape, dim):
    return lax.broadcasted_iota(jnp.int32, shape, dim)


def _block_ones(n, width):
    return (_iota2((n, n), 0) // width == _iota2((n, n), 1) // width).astype(F32)


def _interleave(*gens):
    results = [None] * len(gens)
    live = list(range(len(gens)))
    while live:
        for i in list(live):
            try:
                next(gens[i])
            except StopIteration as stop:
                results[i] = stop.value
                live.remove(i)
    return results


def _load_weight(w_hbm, layer, dst_ref, stage_ref, sem_ref, slab_rows):
    n_rows = dst_ref.shape[0]
    assert n_rows % slab_rows == 0 and slab_rows <= stage_ref.shape[1]
    n_slabs = n_rows // slab_rows

    def copy(s):
        return pltpu.make_async_copy(w_hbm.at[layer, pl.ds(s * slab_rows, slab_rows), :],
                                     stage_ref.at[s % 2, pl.ds(0, slab_rows), :], sem_ref.at[s % 2])

    copy(0).start()
    for s in range(n_slabs):
        if s + 1 < n_slabs:
            copy(s + 1).start()
        copy(s).wait()
        dst_ref[pl.ds(s * slab_rows, slab_rows), :] = stage_ref[s % 2, pl.ds(0, slab_rows), :].astype(BF16)


def _ffn_kernel(x_ref, *refs, n_y, layer, out_layer, final_norm, tf):
    y_refs = refs[:n_y]
    has_out = n_y > 0
    (wo_hbm,) = refs[n_y:n_y + 1] if has_out else (None,)
    g_ref, wg_hbm, wu_hbm, wd_hbm, fg_ref, o_ref = refs[n_y + has_out:n_y + has_out + 6]
    scratch = refs[n_y + has_out + 6:]
    wg_ref, wu_ref, wd_ref, wide_stage, tall_stage, sem, wo_sem = scratch[:7]
    wo_ref = scratch[7] if has_out else None
    n_slabs = wg_ref.shape[1] // tf

    def slab_copies(j):
        slot, cols = j % 2, pl.ds(j * tf, tf)
        return (pltpu.make_async_copy(wg_hbm.at[layer, :, cols], wide_stage.at[slot, 0], sem.at[slot, 0]),
                pltpu.make_async_copy(wu_hbm.at[layer, :, cols], wide_stage.at[slot, 1], sem.at[slot, 1]),
                pltpu.make_async_copy(wd_hbm.at[layer, cols, :], tall_stage.at[slot], sem.at[slot, 2]))

    def body(first_step):
        if first_step:
            for j in range(min(2, n_slabs)):
                for cp in slab_copies(j):
                    cp.start()
            if has_out:
                _load_weight(wo_hbm, out_layer, wo_ref, wo_ref_stage, wo_sem, wo_ref.shape[0] // 4)
        x = x_ref[...]
        row0 = 0
        for y_ref in y_refs:
            rows = y_ref.shape[1]
            x = x + _dot(y_ref[...].astype(BF16), wo_ref[row0:row0 + rows, :])
            row0 += rows
        h = _rmsnorm(x, g_ref[...]).astype(BF16)
        acc = None
        for j in range(n_slabs):
            c0 = j * tf
            if first_step:
                for cp in slab_copies(j):
                    cp.wait()
                wg_ref[:, c0:c0 + tf] = wide_stage[j % 2, 0].astype(BF16)
                wu_ref[:, c0:c0 + tf] = wide_stage[j % 2, 1].astype(BF16)
                wd_ref[c0:c0 + tf, :] = tall_stage[j % 2].astype(BF16)
                if j + 2 < n_slabs:
                    for cp in slab_copies(j + 2):
                        cp.start()
            gate = _dot(h, wg_ref[:, c0:c0 + tf])
            up = _dot(h, wu_ref[:, c0:c0 + tf])
            act = (gate * jax.nn.sigmoid(gate) * up).astype(BF16)
            part = _dot(act, wd_ref[c0:c0 + tf, :])
            acc = part if acc is None else acc + part
        out = x + 0.5 * acc
        if final_norm:
            out = _rmsnorm(out, fg_ref[...])
        o_ref[...] = out

    wo_ref_stage = scratch[8] if has_out else None
    pl.when(pl.program_id(0) == 0)(lambda: body(True))
    pl.when(pl.program_id(0) != 0)(lambda: body(False))


def _resident(shape):
    return pl.BlockSpec(shape, lambda i: (0,) * len(shape), pipeline_mode=pl.Buffered(1))


def _ffn(x, ys, wo, out_layer, g, wg, wu, wd, layer, final_g, *, final_norm, tm=512, tf=256):
    m, d = x.shape
    f = wg.shape[2]
    tm = min(tm, m)
    assert f % tf == 0
    hbm = pl.BlockSpec(memory_space=pl.ANY)
    has_out = len(ys) > 0
    scratch = [pltpu.VMEM((d, f), BF16), pltpu.VMEM((d, f), BF16), pltpu.VMEM((f, d), BF16),
               pltpu.VMEM((2, 2, d, tf), F32), pltpu.VMEM((2, tf, d), F32),
               pltpu.SemaphoreType.DMA((2, 3)), pltpu.SemaphoreType.DMA((2,))]
    if has_out:
        assert sum(y.shape[1] for y in ys) == wo.shape[1] and wo.shape[1] % 4 == 0
        scratch += [pltpu.VMEM(wo.shape[1:], BF16), pltpu.VMEM((2, wo.shape[1] // 4, wo.shape[2]), F32)]
    return pl.pallas_call(
        functools.partial(_ffn_kernel, n_y=len(ys), layer=layer, out_layer=out_layer, final_norm=final_norm,
                          tf=tf),
        grid=(m // tm,),
        in_specs=[pl.BlockSpec((tm, d), lambda i: (i, 0))]
        + [pl.BlockSpec((tm, y.shape[1]), lambda i: (i, 0)) for y in ys]
        + ([hbm] if has_out else [])
        + [_resident((1, d)), hbm, hbm, hbm, _resident((1, d))],
        out_specs=pl.BlockSpec((tm, d), lambda i: (i, 0)),
        out_shape=jax.ShapeDtypeStruct((m, d), F32),
        scratch_shapes=scratch,
        compiler_params=_params("arbitrary"),
        name="ffn",
    )(x, *ys, *([wo] if has_out else []), g.reshape(1, d), wg, wu, wd, final_g.reshape(1, d))


def _inproj_kernel(x_ref, g_ref, w_hbm, *refs, layer, tn, shift_seq):
    has_shift = shift_seq is not None
    mu_ref = refs[0] if has_shift else None
    refs = refs[has_shift:]
    n_scratch = 3 + has_shift
    o_refs = refs[:-n_scratch]
    w_ref, stage, sem = refs[-n_scratch:][:3]
    carry_ref = refs[-1] if has_shift else None

    @pl.when(pl.program_id(0) == 0)
    def _():
        _load_weight(w_hbm, layer, w_ref, stage, sem, stage.shape[1])
        if has_shift:
            carry_ref[...] = jnp.zeros_like(carry_ref)

    h = _rmsnorm(x_ref[...], g_ref[...]).astype(BF16)
    col0 = 0
    for gi, o_ref in enumerate(o_refs):
        n = o_ref.shape[1]
        parts = [_dot(h, w_ref[:, col0 + c0:col0 + min(c0 + tn, n)]) for c0 in range(0, n, tn)]
        if gi == 0 and has_shift:
            p = jnp.concatenate(parts, axis=1)
            tm = p.shape[0]
            first = (pl.program_id(0) * tm) % shift_seq == 0
            prev_last = jnp.where(first, 0.0, carry_ref[...])
            prev = jnp.where(_iota2((tm, 1), 0) == 0, prev_last, pltpu.roll(p, 1, axis=0))
            carry_ref[...] = p[tm - 1:tm, :]
            o_ref[...] = p + (prev - p) * mu_ref[...]
        else:
            for c0, part in zip(range(0, n, tn), parts):
                o_ref[:, c0:c0 + part.shape[1]] = part
        col0 += n


def _inproj(x, g, w, layer, widths, *, shift_mu=None, shift_seq=None, tm=512, tn=512, n_slabs=8):
    m, d = x.shape
    n_total = w.shape[2]
    assert sum(widths) == n_total and d % n_slabs == 0
    tm = min(tm, m)
    has_shift = shift_mu is not None
    assert not has_shift or shift_seq % tm == 0
    return pl.pallas_call(
        functools.partial(_inproj_kernel, layer=layer, tn=tn, shift_seq=shift_seq if has_shift else None),
        grid=(m // tm,),
        in_specs=[pl.BlockSpec((tm, d), lambda i: (i, 0)), _resident((1, d)), pl.BlockSpec(memory_space=pl.ANY)]
        + ([_resident((1, widths[0]))] if has_shift else []),
        out_specs=[pl.BlockSpec((tm, n), lambda i: (i, 0)) for n in widths],
        out_shape=[jax.ShapeDtypeStruct((m, n), F32) for n in widths],
        scratch_shapes=[pltpu.VMEM((d, n_total), BF16), pltpu.VMEM((2, d // n_slabs, n_total), F32),
                        pltpu.SemaphoreType.DMA((2,))]
        + ([pltpu.VMEM((1, widths[0]), F32)] if has_shift else []),
        compiler_params=_params("arbitrary"),
        name="inproj",
    )(x, g.reshape(1, d), w, *([shift_mu.reshape(1, -1)] if has_shift else []))


def _bdot(a, b, dims):
    return lax.dot_general(a.astype(BF16), b.astype(BF16), dims, preferred_element_type=F32)


def _block_diag(x, half):
    lo = _iota2((1, 1, 2 * half), 2) < half
    zero = jnp.zeros((), x.dtype)
    return jnp.concatenate([jnp.where(lo, x, zero), jnp.where(lo, zero, x)], axis=1)


def _unit_lower_inverse(a_strict, c):
    row = _iota2((1, c, 2 * c), 1)
    col = _iota2((1, c, 2 * c), 2) & (c - 1)
    eye = (row == col).astype(F32)
    t = None
    m = 1
    while m < c:
        mask = ((row // (2 * m)) == (col // (2 * m))) & ((row & m) != 0) & ((col & m) == 0)
        lm = jnp.where(mask, a_strict, 0.0)
        if t is None:
            t = eye - lm
        else:
            t = t - _bdot(_bdot(t, _block_diag(lm.astype(BF16), c), BNN), _block_diag(t.astype(BF16), c), BNN)
            yield
        m *= 2
    return t


def _rwkv_kernel(xs_ref, w0_ref, w2_ref, a0_ref, a2_ref, g2_ref, kk_ref, ka_ref, rk_ref,
                 lnw_ref, lnb_ref, o_ref, st_ref, *, chunk, n_chunks, n_groups, heads, hd):
    c = chunk

    @pl.when(pl.program_id(1) == 0)
    def _():
        st_ref[...] = jnp.zeros_like(st_ref)

    rows = n_chunks * c
    front_refs = (w0_ref, w2_ref, a0_ref, a2_ref, g2_ref, kk_ref, ka_ref)
    dims = dict(c=c, n_chunks=n_chunks, heads=heads, hd=hd)
    front = lambda gi: _rwkv_front(xs_ref[0, gi * rows:(gi + 1) * rows, :], *front_refs, **dims)
    s = st_ref[...]
    (ready,) = _interleave(front(0))
    for gi in range(n_groups):
        back = _rwkv_back(ready, s, rk_ref, lnw_ref, lnb_ref, **dims)
        if gi + 1 < n_groups:
            (out, s), ready = _interleave(back, front(gi + 1))
        else:
            ((out, s),) = _interleave(back)
        o_ref[0, gi * rows:(gi + 1) * rows, :] = out
    st_ref[...] = s


def _head_sum(t, hd):
    tile_ones = _block_ones(2 * hd, hd)
    return jnp.concatenate([_bdot(t[:, i:i + 2 * hd], tile_ones, NN) for i in range(0, t.shape[1], 2 * hd)], axis=1)


def _rwkv_front(xs, w0_ref, w2_ref, a0_ref, a2_ref, g2_ref, kk_ref, ka_ref, *, c, n_chunks, heads, hd):
    rows = n_chunks * c
    dim = heads * hd
    r = xs[:, 0:dim]
    k = xs[:, dim:2 * dim]
    v = xs[:, 2 * dim:3 * dim]
    o1 = 3 * dim
    g_lr = xs[:, o1 + LORA_W + LORA_A:o1 + LORA_W + LORA_A + LORA_G]

    wa = xs[:, o1:o1 + LORA_W + LORA_A]
    wa = jnp.where(_iota2(wa.shape, 1) < LORA_W, jnp.tanh(wa), wa)
    w2a2 = jnp.concatenate(
        [jnp.concatenate([w2_ref[...], jnp.zeros((LORA_W, dim), F32)], axis=1),
         jnp.concatenate([jnp.zeros((LORA_A, dim), F32), a2_ref[...]], axis=1)], axis=0)
    za = _mm_fused(wa, w2a2)
    z = w0_ref[...] + za[:, :dim]
    softplus = jnp.maximum(-z, 0.0) + jnp.log(1.0 + jnp.exp(-jnp.abs(z)))
    w_raw = -softplus - 0.5
    lw = -jnp.exp(w_raw)
    yield
    a = jax.nn.sigmoid(a0_ref[...] + za[:, dim:])
    g = _mm_fused(jax.nn.sigmoid(g_lr), g2_ref[...])
    yield
    kk = k * kk_ref[...]
    kk = kk * lax.rsqrt(jnp.maximum(_head_sum(kk * kk, hd), 1e-24))
    k2 = k * (1.0 + (a - 1.0) * ka_ref[...])
    bb = kk * a
    yield

    r_i, c_i = _iota2((rows, rows), 0), _iota2((rows, rows), 1)
    tri_incl = (r_i >= c_i) & (r_i // c == c_i // c)
    cum = _cumsum_matmul(tri_incl, lw)
    ends = [cum[(i + 1) * c - 1:(i + 1) * c, :] for i in range(n_chunks)]
    cum_last = jnp.concatenate([jnp.broadcast_to(e, (c, dim)) for e in ends], axis=0)
    yield
    w_incl = jnp.exp(cum)
    w_excl = jnp.exp(cum - lw)
    yield
    w_inv = jnp.exp(-cum)
    w_tail = jnp.exp(cum_last - cum)
    w_all = jnp.exp(jnp.concatenate(ends, axis=0))
    yield
    ops = dict(kt=kk * w_excl, rt=r * w_incl, bt=bb * w_inv)
    yield
    ops.update(kd=k2 * w_inv, bw=bb * w_tail, kw=k2 * w_tail)
    yield
    ops.update(v=v, w_all=w_all, r=r, k2=k2, g=g)
    return ops


def _rwkv_back(ops, s, rk_ref, lnw_ref, lnb_ref, *, c, n_chunks, heads, hd):
    pw = 2 * hd
    pairs = heads // 2
    def by_pair(t, n_rows=c):
        return jnp.stack([t[i * n_rows:(i + 1) * n_rows, j * pw:(j + 1) * pw]
                          for i in range(n_chunks) for j in range(pairs)], axis=0)
    bd_f = lambda t: _block_diag(t.astype(BF16), hd)
    kt_p, rt_p, bt_p, kd_p, kw_p, bw_p, v_p = (by_pair(ops[n]) for n in ("kt", "rt", "bt", "kd", "kw", "bw", "v"))
    w_all_p = by_pair(ops["w_all"], 1)
    row_t = _iota2((1, c, 2 * c), 1)
    col_t = _iota2((1, c, 2 * c), 2) & (c - 1)
    strict_p, incl_p = row_t > col_t, row_t >= col_t
    same_head = (_iota2((1, pw, pw), 1) // hd) == (_iota2((1, pw, pw), 2) // hd)

    kr = jnp.concatenate([kt_p, rt_p], axis=1)
    g_all = _bdot(kr, jnp.concatenate([bd_f(bt_p), bd_f(kd_p)], axis=1), BNT)
    yield
    gb, gk = g_all[:, :, :2 * c], g_all[:, :, 2 * c:]
    a_b = jnp.where(strict_p, gb[:, :c], 0.0)
    a_k = jnp.where(strict_p, gk[:, :c], 0.0)
    p_b = jnp.where(incl_p, gb[:, c:], 0.0)
    p_k = jnp.where(incl_p, gk[:, c:], 0.0)
    akpk = _bdot(jnp.concatenate([a_k, p_k], axis=1), bd_f(v_p), BNN)
    yield
    t_inv = yield from _unit_lower_inverse(a_b, c)
    kv1 = _bdot(t_inv, jnp.concatenate([bd_f(kt_p), bd_f(akpk[:, :c])], axis=2), BNN)
    yield
    kt1, v1 = kv1[:, :, :pw], kv1[:, :, pw:]
    pbk = _bdot(p_b, jnp.concatenate([bd_f(kt1), bd_f(v1)], axis=2), BNN)
    yield
    q_mat = rt_p - pbk[:, :, :pw]
    z_mat = akpk[:, c:] - pbk[:, :, pw:]
    x_mat = jnp.where(same_head, _bdot(bw_p, kt1, BTN), 0.0)
    yield
    n_mat = jnp.where(same_head, _bdot(jnp.concatenate([v_p, v1], axis=1),
                                       jnp.concatenate([kw_p, -bw_p], axis=1), BTN), 0.0)
    yield
    y_rows = []
    for i in range(n_chunks):
        sl = slice(i * pairs, (i + 1) * pairs)
        y_p = _bdot(q_mat[sl], s, BNT) + z_mat[sl]
        s = s * w_all_p[sl] - _bdot(s, x_mat[sl], BNT) + n_mat[sl]
        y_rows.append(jnp.concatenate([y_p[j] for j in range(pairs)], axis=1))
        yield
    y = jnp.concatenate(y_rows, axis=0)

    inv_hd = 1.0 / hd
    mean = _head_sum(y, hd) * inv_hd
    yc = y - mean
    var = _head_sum(yc * yc, hd) * inv_hd
    yield
    yn = yc * lax.rsqrt(var + LNX_EPS) * lnw_ref[...] + lnb_ref[...]
    bonus = _head_sum(ops["r"] * ops["k2"] * rk_ref[...], hd) * ops["v"]
    return (yn + bonus) * ops["g"], s


def _rwkv(xs, w0, w2, a0, a2, g2, k_k, k_a, r_k, lnx_w, lnx_b, *, chunk=RWKV_CHUNK,
          n_chunks=RWKV_CHUNKS_PER_GROUP, n_groups=RWKV_GROUPS_PER_STEP):
    b, s, cols = xs.shape
    heads, hd = RWKV_HEADS, RWKV_HD
    dim = heads * hd
    chunk = min(chunk, s)
    assert chunk & (chunk - 1) == 0
    n_chunks = min(n_chunks, s // chunk)
    n_groups = min(n_groups, s // (chunk * n_chunks))
    rows = chunk * n_chunks * n_groups
    row = lambda t: t.reshape(1, -1)
    vecs = [row(w0), w2, row(a0), a2, g2, row(k_k), row(k_a), row(r_k), row(lnx_w), row(lnx_b)]
    return pl.pallas_call(
        functools.partial(_rwkv_kernel, chunk=chunk, n_chunks=n_chunks, n_groups=n_groups, heads=heads, hd=hd),
        grid=(b, s // rows),
        in_specs=[pl.BlockSpec((1, rows, cols), lambda i, j: (i, j, 0))]
        + [pl.BlockSpec(t.shape, lambda i, j: (0, 0)) for t in vecs],
        out_specs=pl.BlockSpec((1, rows, dim), lambda i, j: (i, j, 0)),
        out_shape=jax.ShapeDtypeStruct((b, s, dim), F32),
        scratch_shapes=[pltpu.VMEM((heads // 2, 2 * hd, 2 * hd), F32)],
        compiler_params=_params("parallel", "arbitrary"),
        name="rwkv",
    )(xs, *vecs)


def _moba_kernel(q_ref, k_ref, v_ref, o_ref, km_ref, ka_ref, vat_ref, *, nb, blk, n_sel, heads, hd):
    j = pl.program_id(1)
    nbp = km_ref.shape[0]
    scale = hd ** -0.5
    masked = -1e30
    neg_inf = float("-inf")
    slopes = [2.0 ** (-8.0 * (h + 1) / heads) for h in range(heads)]

    @pl.when(j == 0)
    def _():
        km_ref[...] = jnp.zeros_like(km_ref)
        for n in range(nb):
            km_ref[n:n + 1, :] = jnp.mean(k_ref[0, n * blk:(n + 1) * blk, :], axis=0, keepdims=True)
        col = _iota2((blk, 2 * hd), 0).astype(F32)
        lane = _iota2((blk, 2 * hd), 1)
        feat_lane = lane & (hd - 1)
        for n in range(nb):
            rows = slice(n * blk, (n + 1) * blk)
            one_hot = jnp.where(feat_lane == n, 1.0, 0.0)
            for h in range(heads):
                k_feat = (one_hot + jnp.where(feat_lane == nbp, slopes[h] * col, 0.0)
                          + jnp.where(feat_lane == nbp + 1, slopes[h] * blk * n, 0.0))
                k_tile = k_ref[0, rows, (h // 2) * 2 * hd:(h // 2 + 1) * 2 * hd]
                own_lanes = (lane < hd) if h % 2 == 0 else (lane >= hd)
                ka_ref[h, rows, :] = jnp.where(own_lanes, k_tile, k_feat).astype(BF16)
        ones_row = jnp.where(_iota2((vat_ref.shape[2] - hd, blk), 0) == 0, 1.0, 0.0)
        for n in range(nb):
            vt = v_ref[0, n * blk:(n + 1) * blk, :].T
            for h in range(heads):
                vat_ref[n, h] = jnp.concatenate([vt[h * hd:(h + 1) * hd], ones_row], axis=0).astype(BF16)

    qt = (q_ref[0] * scale).T
    sub = _iota2((nbp, blk), 0)
    const_rows = jnp.where(_iota2((hd - nbp, blk), 0) < 2, 1.0, 0.0)
    q_aug = []
    for h in range(heads):
        sl = slice(h * hd, (h + 1) * hd)
        qh = qt[sl]
        gate = _mm(km_ref[:, sl], qh)
        gate = jnp.where(sub < j, gate, neg_inf)
        cnt = jnp.zeros((nbp, blk), jnp.int32)
        for m in range(nb):
            other = gate[m:m + 1, :]
            beats = (other > gate) | ((other == gate) & (m < sub))
            cnt = cnt + beats.astype(jnp.int32)
        keep = ((sub < j) & (cnt < n_sel)) | (sub == j) | (sub >= nb)
        bias = jnp.where(keep, 0.0, masked)
        parts = [qh, bias, const_rows] if h % 2 == 0 else [bias, const_rows, qh]
        q_aug.append(jnp.concatenate(parts, axis=0).astype(BF16))
    q_aug = jnp.stack(q_aug, axis=0)

    def scores(n):
        start = pl.multiple_of(n * blk, blk)
        kn = ka_ref[:, pl.ds(start, blk), :]
        return lax.dot_general(kn, q_aug, BNN, preferred_element_type=F32), vat_ref[n]

    s, vj = scores(j)
    causal = _iota2((1, blk, blk), 1) <= _iota2((1, blk, blk), 2)
    s = jnp.where(causal, s, masked)
    m0 = jnp.max(s, axis=1, keepdims=True)
    acc0 = lax.dot_general(vj, jnp.exp(s - m0).astype(BF16), BNN, preferred_element_type=F32)

    def past_block(n, carry):
        m_run, acc = carry
        sc, vn = scores(n)
        m_new = jnp.maximum(m_run, jnp.max(sc, axis=1, keepdims=True))
        pr = jnp.exp(sc - m_new).astype(BF16)
        acc_new = jnp.exp(m_run - m_new) * acc + lax.dot_general(vn, pr, BNN, preferred_element_type=F32)
        return m_new, acc_new

    _, acc = lax.fori_loop(0, j, past_block, (m0, acc0))
    out_t = jnp.concatenate([acc[h, :hd] / acc[h, hd:hd + 1] for h in range(heads)], axis=0)
    o_ref[0] = out_t.T


def _moba(q, k, v):
    b, s, dim = q.shape
    heads, hd, blk = MOBA_HEADS, MOBA_HD, MOBA_BLOCK
    assert s % blk == 0
    nb = s // blk
    nbp = -(-nb // SUBLANES) * SUBLANES
    assert nbp + 2 <= hd
    n_sel = min(MOBA_TOPK, nb - 1)
    return pl.pallas_call(
        functools.partial(_moba_kernel, nb=nb, blk=blk, n_sel=n_sel, heads=heads, hd=hd),
        grid=(b, nb),
        in_specs=[
            pl.BlockSpec((1, blk, dim), lambda i, j: (i, j, 0)),
            pl.BlockSpec((1, s, dim), lambda i, j: (i, 0, 0)),
            pl.BlockSpec((1, s, dim), lambda i, j: (i, 0, 0)),
        ],
        out_specs=pl.BlockSpec((1, blk, dim), lambda i, j: (i, j, 0)),
        out_shape=jax.ShapeDtypeStruct((b, s, dim), F32),
        scratch_shapes=[pltpu.VMEM((nbp, dim), F32),
                        pltpu.VMEM((heads, s, 2 * hd), BF16),
                        pltpu.VMEM((nb, heads, hd + BF16_SUBLANES, blk), BF16)],
        compiler_params=_params("parallel", "arbitrary"),
        name="moba",
    )(q, k, v)


def _hgrn_kernel(q_ref, f_ref, i_ref, g_ref, lbl_ref, nw_ref, o_ref, st_ref, *, chunk, n_chunks, n_groups, heads,
                 dk, layer):
    c = chunk

    @pl.when(pl.program_id(1) == 0)
    def _():
        st_ref[...] = jnp.zeros_like(st_ref)

    logits = lbl_ref[...]
    e = jnp.exp(logits - jnp.max(logits, axis=0, keepdims=True))
    sm = e / jnp.sum(e, axis=0, keepdims=True)
    lb = jnp.sum(sm[0:layer + 1, :], axis=0, keepdims=True) - sm[0:1, :]

    rows = n_chunks * c
    dims = dict(c=c, n_chunks=n_chunks, heads=heads, dk=dk)
    grp = lambda ref, gi: ref[0, gi * rows:(gi + 1) * rows, :]
    front = lambda gi: _hgrn_front(grp(q_ref, gi), grp(f_ref, gi), grp(i_ref, gi), lb, **dims)
    s = st_ref[...]
    (ready,) = _interleave(front(0))
    for gi in range(n_groups):
        back = _hgrn_back(ready, s, grp(g_ref, gi), nw_ref[...], **dims)
        if gi + 1 < n_groups:
            (out, s), ready = _interleave(back, front(gi + 1))
        else:
            ((out, s),) = _interleave(back)
        o_ref[0, gi * rows:(gi + 1) * rows, :] = out
    st_ref[...] = s


def _hgrn_front(q, fr, v, lb, *, c, n_chunks, heads, dk):
    rows = n_chunks * c
    dim = heads * dk
    sig = jax.nn.sigmoid(fr)
    lf = jnp.log(lb + (1.0 - lb) * sig)
    kf = (1.0 - lb) * (1.0 - sig)
    yield
    r_i, c_i = _iota2((rows, rows), 0), _iota2((rows, rows), 1)
    tri_all = (r_i >= c_i) & (r_i // c == c_i // c)
    b = _cumsum_matmul(tri_all, lf)
    per_chunk = lambda r: jnp.concatenate(
        [jnp.broadcast_to(b[i * c + r:i * c + r + 1, :], (c, dim)) for i in range(n_chunks)], axis=0)
    ends = [b[(i + 1) * c - 1:(i + 1) * c, :] for i in range(n_chunks)]
    b_last = per_chunk(c - 1)
    yield
    b_mid = per_chunk(c // 2 - 1)
    rel = b - b_mid
    qd = q * jnp.exp(rel)
    yield
    kd = kf * jnp.exp(-rel)
    yield
    q_in = qd * jnp.exp(b_mid)
    yield
    kw = kf * jnp.exp(b_last - b)
    w_all = jnp.exp(jnp.concatenate(ends, axis=0))
    return dict(qd=qd, kd=kd, q_in=q_in, kw=kw, w_all=w_all, v=v)


def _hgrn_back(ops, s, g, norm_w, *, c, n_chunks, heads, dk):
    def by_head(t, n_rows=c):
        return jnp.stack([t[i * n_rows:(i + 1) * n_rows, h * dk:(h + 1) * dk]
                          for i in range(n_chunks) for h in range(heads)], axis=0)
    hp, pw = heads // 2, 2 * dk
    by_pair = lambda t: jnp.stack([t[i * c:(i + 1) * c, j * pw:(j + 1) * pw]
                                   for i in range(n_chunks) for j in range(hp)], axis=0)
    tri_pair = _iota2((1, c, 2 * c), 1) >= (_iota2((1, c, 2 * c), 2) & (c - 1))
    q_hi, q_lo = _pieces(by_pair(ops["qd"]), 2)
    k_hi, k_lo = _pieces(_block_diag(by_pair(ops["kd"]), dk), 2)
    sc = lax.dot_general(jnp.concatenate([q_hi, q_hi, q_lo], axis=2), jnp.concatenate([k_hi, k_lo, k_hi], axis=2),
                         BNT, preferred_element_type=F32)
    sc = jnp.where(tri_pair, sc, 0.0)
    yield
    o_pair = _bdot(sc, _block_diag(by_pair(ops["v"]).astype(BF16), dk), BNN)
    yield
    qd_h, v_h = by_head(ops["q_in"]), by_head(ops["v"])
    s_add = _bdot(v_h, by_head(ops["kw"]), BTN)
    yield
    w_all_h = by_head(ops["w_all"], 1)
    o_rows = []
    for i in range(n_chunks):
        sl = slice(i * heads, (i + 1) * heads)
        o_intra = jnp.stack([o_pair[i * hp + h // 2][:, (h % 2) * dk:(h % 2 + 1) * dk]
                             for h in range(heads)], axis=0)
        o_h = o_intra + _bdot(qd_h[sl], s, BNT)
        s = s * w_all_h[sl] + s_add[sl]
        o_h = o_h * lax.rsqrt(jnp.mean(o_h * o_h, axis=-1, keepdims=True) + EPS)
        o_rows.append(jnp.concatenate([o_h[h] for h in range(heads)], axis=1))
        yield
    o = jnp.concatenate(o_rows, axis=0)
    return o * norm_w * jax.nn.sigmoid(g), s


def _hgrn(p, lb_logits, norm_w, *, layer, chunk=HG_CHUNK, n_chunks=HG_CHUNKS_PER_GROUP,
          n_groups=HG_GROUPS_PER_STEP):
    b, s, cols = p.shape
    heads, dk = HG_HEADS, HG_DK
    dim = heads * dk
    assert cols == 4 * dim
    chunk = min(chunk, s)
    assert chunk & (chunk - 1) == 0
    n_chunks = min(n_chunks, s // chunk)
    n_groups = min(n_groups, s // (chunk * n_chunks))
    rows = chunk * n_chunks * n_groups
    col_block = lambda n: pl.BlockSpec((1, rows, dim), lambda i, j, n=n: (i, j, n))
    return pl.pallas_call(
        functools.partial(_hgrn_kernel, chunk=chunk, n_chunks=n_chunks, n_groups=n_groups, heads=heads, dk=dk,
                          layer=layer),
        grid=(b, s // rows),
        in_specs=[col_block(0), col_block(1), col_block(2), col_block(3),
                  pl.BlockSpec(lb_logits.shape, lambda i, j: (0, 0)),
                  pl.BlockSpec((1, dim), lambda i, j: (0, 0))],
        out_specs=pl.BlockSpec((1, rows, dim), lambda i, j: (i, j, 0)),
        out_shape=jax.ShapeDtypeStruct((b, s, dim), F32),
        scratch_shapes=[pltpu.VMEM((heads, dk, dk), F32)],
        compiler_params=_params("parallel", "arbitrary"),
        name="hgrn",
    )(p, p, p, p, lb_logits, norm_w.reshape(1, dim))


def _proj_hgrn_kernel(x_ref, g_ref, w_hbm, lbl_ref, nw_ref, o_ref, w_ref, stage, sem, st_ref, *, w_layer, layer,
                      tiles_per_seq, tn, chunk, n_chunks, heads, dk):
    c = chunk
    dim = heads * dk
    rows = n_chunks * c
    n_groups = x_ref.shape[0] // rows
    step = pl.program_id(0)

    @pl.when(step == 0)
    def _():
        _load_weight(w_hbm, w_layer, w_ref, stage, sem, stage.shape[1])
        st_ref[...] = jnp.zeros_like(st_ref)

    logits = lbl_ref[...]
    e = jnp.exp(logits - jnp.max(logits, axis=0, keepdims=True))
    sm = e / jnp.sum(e, axis=0, keepdims=True)
    lb = jnp.sum(sm[0:layer + 1, :], axis=0, keepdims=True) - sm[0:1, :]

    h = _rmsnorm(x_ref[...], g_ref[...]).astype(BF16)
    quarter = lambda k: jnp.concatenate([_dot(h, w_ref[:, c0:c0 + tn]) for c0 in range(k * dim, (k + 1) * dim, tn)],
                                        axis=1)
    f_raw, q, v = quarter(1), quarter(0), quarter(2)

    def gate_quarter():
        parts = []
        for c0 in range(3 * dim, 4 * dim, tn):
            parts.append(_dot(h, w_ref[:, c0:c0 + tn]))
            yield
        return jnp.concatenate(parts, axis=1)

    dims = dict(c=c, n_chunks=n_chunks, heads=heads, dk=dk)
    grp = lambda t, gi: t[gi * rows:(gi + 1) * rows, :]
    front = lambda gi: _hgrn_front(grp(q, gi), grp(f_raw, gi), grp(v, gi), lb, **dims)
    ready, g_raw = _interleave(front(0), gate_quarter())
    s = jnp.where(step % tiles_per_seq == 0, 0.0, st_ref[...])
    for gi in range(n_groups):
        back = _hgrn_back(ready, s, grp(g_raw, gi), nw_ref[...], **dims)
        if gi + 1 < n_groups:
            (out, s), ready = _interleave(back, front(gi + 1))
        else:
            ((out, s),) = _interleave(back)
        o_ref[gi * rows:(gi + 1) * rows, :] = out
    st_ref[...] = s


def _proj_hgrn(x, g, w, w_layer, lb_logits, layer, norm_w, seq, *, tm=512, tn=512, n_slabs=8, chunk=HG_CHUNK,
               n_chunks=HG_CHUNKS_PER_GROUP):
    m, d = x.shape
    heads, dk = HG_HEADS, HG_DK
    dim = heads * dk
    n_total = w.shape[2]
    assert n_total == 4 * dim and dim % tn == 0 and d % n_slabs == 0
    assert chunk & (chunk - 1) == 0
    assert seq % tm == 0 and tm % (chunk * n_chunks) == 0
    return pl.pallas_call(
        functools.partial(_proj_hgrn_kernel, w_layer=w_layer, layer=layer, tiles_per_seq=seq // tm, tn=tn,
                          chunk=chunk, n_chunks=n_chunks, heads=heads, dk=dk),
        grid=(m // tm,),
        in_specs=[pl.BlockSpec((tm, d), lambda i: (i, 0)), _resident((1, d)), pl.BlockSpec(memory_space=pl.ANY),
                  _resident(lb_logits.shape), _resident((1, dim))],
        out_specs=pl.BlockSpec((tm, dim), lambda i: (i, 0)),
        out_shape=jax.ShapeDtypeStruct((m, dim), F32),
        scratch_shapes=[pltpu.VMEM((d, n_total), BF16), pltpu.VMEM((2, d // n_slabs, n_total), F32),
                        pltpu.SemaphoreType.DMA((2,)), pltpu.VMEM((heads, dk, dk), F32)],
        compiler_params=_params("arbitrary"),
        name="proj_hgrn",
    )(x, g.reshape(1, d), w, lb_logits, norm_w.reshape(1, dim))


def kernel(x, norm_g, ffn1_wg, ffn1_wu, ffn1_wd, ffn2_wg, ffn2_wu, ffn2_wd, ev_w_in, ev_w_out, rw_mu, rw_w0, rw_w2, rw_a0, rw_a2, rw_g2, rw_k_k, rw_k_a, rw_r_k, rw_lnx_w, rw_lnx_b, od_w_in, od_w_out, hg_norm_w, hg_lb_logits, final_g):
    bsz, seq, d = x.shape
    depth = norm_g.shape[0]
    rwkv_dim = RWKV_HEADS * RWKV_HD
    rwkv_cols = 3 * rwkv_dim + LORA_W + LORA_A + LORA_G
    moba_dim = MOBA_HEADS * MOBA_HD
    xf = x.reshape(bsz * seq, d)
    for l in range(depth):
        xf = _ffn(xf, [], None, 0, norm_g[l, 0], ffn1_wg, ffn1_wu, ffn1_wd, l, final_g, final_norm=False)
        if l % 2 == 0:
            e = l // 2
            xs_r, q, k, v = _inproj(xf, norm_g[l, 1], ev_w_in, e, [rwkv_cols, moba_dim, moba_dim, moba_dim],
                                    shift_mu=rw_mu[e], shift_seq=seq)
            y_a = _rwkv(xs_r.reshape(bsz, seq, rwkv_cols), rw_w0[e], rw_w2[e], rw_a0[e], rw_a2[e],
                        rw_g2[e], rw_k_k[e], rw_k_a[e], rw_r_k[e], rw_lnx_w[e], rw_lnx_b[e])
            y_b = _moba(q.reshape(bsz, seq, moba_dim), k.reshape(bsz, seq, moba_dim),
                        v.reshape(bsz, seq, moba_dim))
            ys, wo, wo_layer = [y_a.reshape(-1, rwkv_dim), y_b.reshape(-1, moba_dim)], ev_w_out, e
        else:
            o = l // 2
            y = _proj_hgrn(xf, norm_g[l, 1], od_w_in, o, hg_lb_logits, l, hg_norm_w[o], seq)
            ys, wo, wo_layer = [y], od_w_out, o
        xf = _ffn(xf, ys, wo, wo_layer, norm_g[l, 2], ffn2_wg, ffn2_wu, ffn2_wd, l, final_g,
                  final_norm=(l == depth - 1))
    return xf.reshape(bsz, seq, d)
```

```python
import functools

import jax
import jax.numpy as jnp
from jax import lax
from jax.experimental import pallas as pl
from jax.experimental.pallas import tpu as pltpu

F32 = jnp.float32
BF16 = jnp.bfloat16
NN = (((1,), (0,)), ((), ()))
BNN = (((2,), (1,)), ((0,), (0,)))
BNT = (((2,), (2,)), ((0,), (0,)))
BTN = (((1,), (1,)), ((0,), (0,)))
MIX_PIECES = 2

EPS = 1e-6
LNX_EPS = 64e-5
RWKV_HEADS = 8
RWKV_HD = 64
LORA_W = 64
LORA_A = 64
LORA_G = 128
MOBA_HEADS = 8
MOBA_HD = 64
MOBA_BLOCK = 256
MOBA_TOPK = 3
HG_HEADS = 8
HG_DK = 128
RWKV_CHUNK = 64
RWKV_CHUNKS_PER_GROUP = 4
HG_CHUNK = 64
HG_CHUNKS_PER_GROUP = 4
SUBLANES = 8
BF16_SUBLANES = 16
V7X_VMEM_BYTES = 64 * 1024 * 1024
VMEM_LIMIT_BYTES = V7X_VMEM_BYTES - 8 * 1024 * 1024


def _params(*semantics):
    return pltpu.CompilerParams(dimension_semantics=semantics, vmem_limit_bytes=VMEM_LIMIT_BYTES)


def _dot(a, b):
    return jnp.dot(a, b, preferred_element_type=F32)


def _pieces(a, n):
    if isinstance(a, tuple):
        return a
    out = []
    for i in range(n):
        hi = a.astype(BF16)
        out.append(hi)
        if i + 1 < n:
            a = a - hi.astype(F32)
    return tuple(out)


def _mm(a, b, dims=NN, n=MIX_PIECES):
    a = _pieces(a, n)
    b = _pieces(b, n)
    order = max(len(a), len(b)) - 1
    out = None
    for i, ai in enumerate(a):
        for j, bj in enumerate(b):
            if i + j <= order:
                t = lax.dot_general(ai, bj, dims, preferred_element_type=F32)
                out = t if out is None else out + t
    return out


def _mm_fused(a, b):
    a_hi, a_lo = _pieces(a, 2)
    b_hi, b_lo = _pieces(b, 2)
    return _dot(jnp.concatenate([a_hi, a_hi, a_lo], axis=1), jnp.concatenate([b_hi, b_lo, b_hi], axis=0))


def _cumsum_matmul(mask, x):
    m = mask.astype(BF16)
    hi, lo = _pieces(x, 2)
    return _dot(jnp.concatenate([m, m], axis=1), jnp.concatenate([hi, lo], axis=0))


def _rmsnorm(x, g):
    return x * lax.rsqrt(jnp.mean(x * x, axis=-1, keepdims=True) + EPS) * g


def _iota2(shape, dim):
    return lax.broadcasted_iota(jnp.int32, shape, dim)


def _block_ones(n, width):
    return (_iota2((n, n), 0) // width == _iota2((n, n), 1) // width).astype(F32)


def _interleave(*gens):
    results = [None] * len(gens)
    live = list(range(len(gens)))
    while live:
        for i in list(live):
            try:
                next(gens[i])
            except StopIteration as stop:
                results[i] = stop.value
                live.remove(i)
    return results


def _load_weight(w_hbm, layer, dst_ref, stage_ref, sem_ref, slab_rows):
    n_rows = dst_ref.shape[0]
    assert n_rows % slab_rows == 0 and slab_rows <= stage_ref.shape[1]
    n_slabs = n_rows // slab_rows

    def copy(s):
        return pltpu.make_async_copy(w_hbm.at[layer, pl.ds(s * slab_rows, slab_rows), :],
                                     stage_ref.at[s % 2, pl.ds(0, slab_rows), :], sem_ref.at[s % 2])

    copy(0).start()
    for s in range(n_slabs):
        if s + 1 < n_slabs:
            copy(s + 1).start()
        copy(s).wait()
        dst_ref[pl.ds(s * slab_rows, slab_rows), :] = stage_ref[s % 2, pl.ds(0, slab_rows), :].astype(BF16)


def _ffn_kernel(x_ref, *refs, n_y, layer, out_layer, final_norm, tf):
    y_refs = refs[:n_y]
    has_out = n_y > 0
    (wo_hbm,) = refs[n_y:n_y + 1] if has_out else (None,)
    g_ref, wg_hbm, wu_hbm, wd_hbm, fg_ref, o_ref = refs[n_y + has_out:n_y + has_out + 6]
    scratch = refs[n_y + has_out + 6:]
    wg_ref, wu_ref, wd_ref, wide_stage, tall_stage, sem, wo_sem = scratch[:7]
    wo_ref = scratch[7] if has_out else None
    n_slabs = wg_ref.shape[1] // tf

    def slab_copies(j):
        slot, cols = j % 2, pl.ds(j * tf, tf)
        return (pltpu.make_async_copy(wg_hbm.at[layer, :, cols], wide_stage.at[slot, 0], sem.at[slot, 0]),
                pltpu.make_async_copy(wu_hbm.at[layer, :, cols], wide_stage.at[slot, 1], sem.at[slot, 1]),
                pltpu.make_async_copy(wd_hbm.at[layer, cols, :], tall_stage.at[slot], sem.at[slot, 2]))

    def body(first_step):
        if first_step:
            for j in range(min(2, n_slabs)):
                for cp in slab_copies(j):
                    cp.start()
            if has_out:
                _load_weight(wo_hbm, out_layer, wo_ref, wo_ref_stage, wo_sem, wo_ref.shape[0] // 4)
        x = x_ref[...]
        row0 = 0
        for y_ref in y_refs:
            rows = y_ref.shape[1]
            x = x + _dot(y_ref[...].astype(BF16), wo_ref[row0:row0 + rows, :])
            row0 += rows
        h = _rmsnorm(x, g_ref[...]).astype(BF16)
        acc = None
        for j in range(n_slabs):
            c0 = j * tf
            if first_step:
                for cp in slab_copies(j):
                    cp.wait()
                wg_ref[:, c0:c0 + tf] = wide_stage[j % 2, 0].astype(BF16)
                wu_ref[:, c0:c0 + tf] = wide_stage[j % 2, 1].astype(BF16)
                wd_ref[c0:c0 + tf, :] = tall_stage[j % 2].astype(BF16)
                if j + 2 < n_slabs:
                    for cp in slab_copies(j + 2):
                        cp.start()
            gate = _dot(h, wg_ref[:, c0:c0 + tf])
            up = _dot(h, wu_ref[:, c0:c0 + tf])
            act = (gate * jax.nn.sigmoid(gate) * up).astype(BF16)
            part = _dot(act, wd_ref[c0:c0 + tf, :])
            acc = part if acc is None else acc + part
        out = x + 0.5 * acc
        if final_norm:
            out = _rmsnorm(out, fg_ref[...])
        o_ref[...] = out

    wo_ref_stage = scratch[8] if has_out else None
    pl.when(pl.program_id(0) == 0)(lambda: body(True))
    pl.when(pl.program_id(0) != 0)(lambda: body(False))


def _resident(shape):
    return pl.BlockSpec(shape, lambda i: (0,) * len(shape), pipeline_mode=pl.Buffered(1))


def _ffn(x, ys, wo, out_layer, g, wg, wu, wd, layer, final_g, *, final_norm, tm=512, tf=256):
    m, d = x.shape
    f = wg.shape[2]
    tm = min(tm, m)
    assert f % tf == 0
    hbm = pl.BlockSpec(memory_space=pl.ANY)
    has_out = len(ys) > 0
    scratch = [pltpu.VMEM((d, f), BF16), pltpu.VMEM((d, f), BF16), pltpu.VMEM((f, d), BF16),
               pltpu.VMEM((2, 2, d, tf), F32), pltpu.VMEM((2, tf, d), F32),
               pltpu.SemaphoreType.DMA((2, 3)), pltpu.SemaphoreType.DMA((2,))]
    if has_out:
        assert sum(y.shape[1] for y in ys) == wo.shape[1] and wo.shape[1] % 4 == 0
        scratch += [pltpu.VMEM(wo.shape[1:], BF16), pltpu.VMEM((2, wo.shape[1] // 4, wo.shape[2]), F32)]
    return pl.pallas_call(
        functools.partial(_ffn_kernel, n_y=len(ys), layer=layer, out_layer=out_layer, final_norm=final_norm,
                          tf=tf),
        grid=(m // tm,),
        in_specs=[pl.BlockSpec((tm, d), lambda i: (i, 0))]
        + [pl.BlockSpec((tm, y.shape[1]), lambda i: (i, 0)) for y in ys]
        + ([hbm] if has_out else [])
        + [_resident((1, d)), hbm, hbm, hbm, _resident((1, d))],
        out_specs=pl.BlockSpec((tm, d), lambda i: (i, 0)),
        out_shape=jax.ShapeDtypeStruct((m, d), F32),
        scratch_shapes=scratch,
        compiler_params=_params("arbitrary"),
        name="ffn",
    )(x, *ys, *([wo] if has_out else []), g.reshape(1, d), wg, wu, wd, final_g.reshape(1, d))


def _bdot(a, b, dims):
    return lax.dot_general(a.astype(BF16), b.astype(BF16), dims, preferred_element_type=F32)


def _block_diag(x, half):
    lo = _iota2((1, 1, 2 * half), 2) < half
    zero = jnp.zeros((), x.dtype)
    return jnp.concatenate([jnp.where(lo, x, zero), jnp.where(lo, zero, x)], axis=1)


def _unit_lower_inverse(a_strict, c):
    row = _iota2((1, c, 2 * c), 1)
    col = _iota2((1, c, 2 * c), 2) & (c - 1)
    eye = (row == col).astype(F32)
    t = None
    m = 1
    while m < c:
        mask = ((row // (2 * m)) == (col // (2 * m))) & ((row & m) != 0) & ((col & m) == 0)
        lm = jnp.where(mask, a_strict, 0.0)
        if t is None:
            t = eye - lm
        else:
            t = t - _bdot(_bdot(t, _block_diag(lm.astype(BF16), c), BNN), _block_diag(t.astype(BF16), c), BNN)
            yield
        m *= 2
    return t


def _head_sum(t, hd):
    tile_ones = _block_ones(2 * hd, hd)
    return jnp.concatenate([_bdot(t[:, i:i + 2 * hd], tile_ones, NN) for i in range(0, t.shape[1], 2 * hd)], axis=1)


def _rwkv_front(xs, w0_ref, w2_ref, a0_ref, a2_ref, g2_ref, kk_ref, ka_ref, *, c, n_chunks, heads, hd):
    rows = n_chunks * c
    dim = heads * hd
    r = xs[:, 0:dim]
    k = xs[:, dim:2 * dim]
    v = xs[:, 2 * dim:3 * dim]
    o1 = 3 * dim
    g_lr = xs[:, o1 + LORA_W + LORA_A:o1 + LORA_W + LORA_A + LORA_G]

    wa = xs[:, o1:o1 + LORA_W + LORA_A]
    wa = jnp.where(_iota2(wa.shape, 1) < LORA_W, jnp.tanh(wa), wa)
    w2a2 = jnp.concatenate(
        [jnp.concatenate([w2_ref[...], jnp.zeros((LORA_W, dim), F32)], axis=1),
         jnp.concatenate([jnp.zeros((LORA_A, dim), F32), a2_ref[...]], axis=1)], axis=0)
    za = _mm_fused(wa, w2a2)
    z = w0_ref[...] + za[:, :dim]
    softplus = jnp.maximum(-z, 0.0) + jnp.log(1.0 + jnp.exp(-jnp.abs(z)))
    w_raw = -softplus - 0.5
    lw = -jnp.exp(w_raw)
    yield
    a = jax.nn.sigmoid(a0_ref[...] + za[:, dim:])
    g = _mm_fused(jax.nn.sigmoid(g_lr), g2_ref[...])
    yield
    kk = k * kk_ref[...]
    kk = kk * lax.rsqrt(jnp.maximum(_head_sum(kk * kk, hd), 1e-24))
    k2 = k * (1.0 + (a - 1.0) * ka_ref[...])
    bb = kk * a
    yield

    r_i, c_i = _iota2((rows, rows), 0), _iota2((rows, rows), 1)
    tri_incl = (r_i >= c_i) & (r_i // c == c_i // c)
    cum = _cumsum_matmul(tri_incl, lw)
    ends = [cum[(i + 1) * c - 1:(i + 1) * c, :] for i in range(n_chunks)]
    cum_last = jnp.concatenate([jnp.broadcast_to(e, (c, dim)) for e in ends], axis=0)
    yield
    w_incl = jnp.exp(cum)
    w_excl = jnp.exp(cum - lw)
    yield
    w_inv = jnp.exp(-cum)
    w_tail = jnp.exp(cum_last - cum)
    w_all = jnp.exp(jnp.concatenate(ends, axis=0))
    yield
    ops = dict(kt=kk * w_excl, rt=r * w_incl, bt=bb * w_inv)
    yield
    ops.update(kd=k2 * w_inv, bw=bb * w_tail, kw=k2 * w_tail)
    yield
    ops.update(v=v, w_all=w_all, r=r, k2=k2, g=g)
    return ops


def _rwkv_back(ops, s, rk_ref, lnw_ref, lnb_ref, *, c, n_chunks, heads, hd):
    pw = 2 * hd
    pairs = heads // 2
    def by_pair(t, n_rows=c):
        return jnp.stack([t[i * n_rows:(i + 1) * n_rows, j * pw:(j + 1) * pw]
                          for i in range(n_chunks) for j in range(pairs)], axis=0)
    bd_f = lambda t: _block_diag(t.astype(BF16), hd)
    kt_p, rt_p, bt_p, kd_p, kw_p, bw_p, v_p = (by_pair(ops[n]) for n in ("kt", "rt", "bt", "kd", "kw", "bw", "v"))
    w_all_p = by_pair(ops["w_all"], 1)
    row_t = _iota2((1, c, 2 * c), 1)
    col_t = _iota2((1, c, 2 * c), 2) & (c - 1)
    strict_p, incl_p = row_t > col_t, row_t >= col_t
    same_head = (_iota2((1, pw, pw), 1) // hd) == (_iota2((1, pw, pw), 2) // hd)

    kr = jnp.concatenate([kt_p, rt_p], axis=1)
    g_all = _bdot(kr, jnp.concatenate([bd_f(bt_p), bd_f(kd_p)], axis=1), BNT)
    yield
    gb, gk = g_all[:, :, :2 * c], g_all[:, :, 2 * c:]
    a_b = jnp.where(strict_p, gb[:, :c], 0.0)
    a_k = jnp.where(strict_p, gk[:, :c], 0.0)
    p_b = jnp.where(incl_p, gb[:, c:], 0.0)
    p_k = jnp.where(incl_p, gk[:, c:], 0.0)
    akpk = _bdot(jnp.concatenate([a_k, p_k], axis=1), bd_f(v_p), BNN)
    yield
    t_inv = yield from _unit_lower_inverse(a_b, c)
    kv1 = _bdot(t_inv, jnp.concatenate([bd_f(kt_p), bd_f(akpk[:, :c])], axis=2), BNN)
    yield
    kt1, v1 = kv1[:, :, :pw], kv1[:, :, pw:]
    pbk = _bdot(p_b, jnp.concatenate([bd_f(kt1), bd_f(v1)], axis=2), BNN)
    yield
    q_mat = rt_p - pbk[:, :, :pw]
    z_mat = akpk[:, c:] - pbk[:, :, pw:]
    x_mat = jnp.where(same_head, _bdot(bw_p, kt1, BTN), 0.0)
    yield
    n_mat = jnp.where(same_head, _bdot(jnp.concatenate([v_p, v1], axis=1),
                                       jnp.concatenate([kw_p, -bw_p], axis=1), BTN), 0.0)
    yield
    y_rows = []
    for i in range(n_chunks):
        sl = slice(i * pairs, (i + 1) * pairs)
        y_p = _bdot(q_mat[sl], s, BNT) + z_mat[sl]
        s = s * w_all_p[sl] - _bdot(s, x_mat[sl], BNT) + n_mat[sl]
        y_rows.append(jnp.concatenate([y_p[j] for j in range(pairs)], axis=1))
        yield
    y = jnp.concatenate(y_rows, axis=0)

    inv_hd = 1.0 / hd
    mean = _head_sum(y, hd) * inv_hd
    yc = y - mean
    var = _head_sum(yc * yc, hd) * inv_hd
    yield
    yn = yc * lax.rsqrt(var + LNX_EPS) * lnw_ref[...] + lnb_ref[...]
    bonus = _head_sum(ops["r"] * ops["k2"] * rk_ref[...], hd) * ops["v"]
    return (yn + bonus) * ops["g"], s


def _proj_rwkv_kernel(x_ref, g_ref, w_hbm, mu_ref, w0_ref, w2_ref, a0_ref, a2_ref, g2_ref, kk_ref, ka_ref, rk_ref,
                      lnw_ref, lnb_ref, y_ref, q_ref, k_ref, v_ref, w_ref, stage, sem, carry_ref, st_ref, *,
                      w_layer, tiles_per_seq, tn, chunk, n_chunks, heads, hd):
    c = chunk
    rows = n_chunks * c
    tm = x_ref.shape[0]
    n_groups = tm // rows
    cols = mu_ref.shape[1]
    step = pl.program_id(0)

    @pl.when(step == 0)
    def _():
        _load_weight(w_hbm, w_layer, w_ref, stage, sem, stage.shape[1])
        carry_ref[...] = jnp.zeros_like(carry_ref)
        st_ref[...] = jnp.zeros_like(st_ref)

    h = _rmsnorm(x_ref[...], g_ref[...]).astype(BF16)
    p = jnp.concatenate([_dot(h, w_ref[:, c0:min(c0 + tn, cols)]) for c0 in range(0, cols, tn)], axis=1)
    first = step % tiles_per_seq == 0
    prev_last = jnp.where(first, 0.0, carry_ref[...])
    prev = jnp.where(_iota2((tm, 1), 0) == 0, prev_last, pltpu.roll(p, 1, axis=0))
    carry_ref[...] = p[tm - 1:tm, :]
    xs = p + (prev - p) * mu_ref[...]

    def attention_groups():
        col0 = cols
        for o_ref in (q_ref, k_ref, v_ref):
            n = o_ref.shape[1]
            for c0 in range(0, n, tn):
                o_ref[:, c0:min(c0 + tn, n)] = _dot(h, w_ref[:, col0 + c0:col0 + min(c0 + tn, n)])
                yield
            col0 += n

    front_refs = (w0_ref, w2_ref, a0_ref, a2_ref, g2_ref, kk_ref, ka_ref)
    dims = dict(c=c, n_chunks=n_chunks, heads=heads, hd=hd)
    front = lambda gi: _rwkv_front(xs[gi * rows:(gi + 1) * rows], *front_refs, **dims)
    ready, _ = _interleave(front(0), attention_groups())
    s = jnp.where(first, 0.0, st_ref[...])
    for gi in range(n_groups):
        back = _rwkv_back(ready, s, rk_ref, lnw_ref, lnb_ref, **dims)
        if gi + 1 < n_groups:
            (out, s), ready = _interleave(back, front(gi + 1))
        else:
            ((out, s),) = _interleave(back)
        y_ref[gi * rows:(gi + 1) * rows, :] = out
    st_ref[...] = s


def _proj_rwkv(x, g, w, w_layer, seq, mu, w0, w2, a0, a2, g2, k_k, k_a, r_k, lnx_w, lnx_b, attn_widths, *, tm=512,
               tn=512, n_slabs=8, chunk=RWKV_CHUNK, n_chunks=RWKV_CHUNKS_PER_GROUP):
    m, d = x.shape
    heads, hd = RWKV_HEADS, RWKV_HD
    dim = heads * hd
    cols = mu.shape[-1]
    n_total = w.shape[2]
    assert n_total == cols + sum(attn_widths) and len(attn_widths) == 3 and d % n_slabs == 0
    assert chunk & (chunk - 1) == 0
    assert seq % tm == 0 and tm % (chunk * n_chunks) == 0
    row = lambda t: t.reshape(1, -1)
    vecs = [row(mu), row(w0), w2, row(a0), a2, g2, row(k_k), row(k_a), row(r_k), row(lnx_w), row(lnx_b)]
    widths = [dim] + list(attn_widths)
    return pl.pallas_call(
        functools.partial(_proj_rwkv_kernel, w_layer=w_layer, tiles_per_seq=seq // tm, tn=tn, chunk=chunk,
                          n_chunks=n_chunks, heads=heads, hd=hd),
        grid=(m // tm,),
        in_specs=[pl.BlockSpec((tm, d), lambda i: (i, 0)), _resident((1, d)), pl.BlockSpec(memory_space=pl.ANY)]
        + [_resident(t.shape) for t in vecs],
        out_specs=[pl.BlockSpec((tm, n), lambda i: (i, 0)) for n in widths],
        out_shape=[jax.ShapeDtypeStruct((m, n), F32) for n in widths],
        scratch_shapes=[pltpu.VMEM((d, n_total), BF16), pltpu.VMEM((2, d // n_slabs, n_total), F32),
                        pltpu.SemaphoreType.DMA((2,)), pltpu.VMEM((1, cols), F32),
                        pltpu.VMEM((heads // 2, 2 * hd, 2 * hd), F32)],
        compiler_params=_params("arbitrary"),
        name="proj_rwkv",
    )(x, g.reshape(1, d), w, *vecs)


def _moba_kernel(q_ref, k_ref, v_ref, o_ref, km_ref, ka_ref, vat_ref, *, nb, blk, n_sel, heads, hd):
    j = pl.program_id(1)
    nbp = km_ref.shape[0]
    scale = hd ** -0.5
    masked = -1e30
    neg_inf = float("-inf")
    slopes = [2.0 ** (-8.0 * (h + 1) / heads) for h in range(heads)]

    @pl.when(j == 0)
    def _():
        km_ref[...] = jnp.zeros_like(km_ref)
        for n in range(nb):
            km_ref[n:n + 1, :] = jnp.mean(k_ref[0, n * blk:(n + 1) * blk, :], axis=0, keepdims=True)
        col = _iota2((blk, 2 * hd), 0).astype(F32)
        lane = _iota2((blk, 2 * hd), 1)
        feat_lane = lane & (hd - 1)
        for n in range(nb):
            rows = slice(n * blk, (n + 1) * blk)
            one_hot = jnp.where(feat_lane == n, 1.0, 0.0)
            for h in range(heads):
                k_feat = (one_hot + jnp.where(feat_lane == nbp, slopes[h] * col, 0.0)
                          + jnp.where(feat_lane == nbp + 1, slopes[h] * blk * n, 0.0))
                k_tile = k_ref[0, rows, (h // 2) * 2 * hd:(h // 2 + 1) * 2 * hd]
                own_lanes = (lane < hd) if h % 2 == 0 else (lane >= hd)
                ka_ref[h, rows, :] = jnp.where(own_lanes, k_tile, k_feat).astype(BF16)
        ones_row = jnp.where(_iota2((vat_ref.shape[2] - hd, blk), 0) == 0, 1.0, 0.0)
        for n in range(nb):
            vt = v_ref[0, n * blk:(n + 1) * blk, :].T
            for h in range(heads):
                vat_ref[n, h] = jnp.concatenate([vt[h * hd:(h + 1) * hd], ones_row], axis=0).astype(BF16)

    qt = (q_ref[0] * scale).T
    sub = _iota2((nbp, blk), 0)
    const_rows = jnp.where(_iota2((hd - nbp, blk), 0) < 2, 1.0, 0.0)
    q_aug = []
    for h in range(heads):
        sl = slice(h * hd, (h + 1) * hd)
        qh = qt[sl]
        gate = _mm(km_ref[:, sl], qh)
        gate = jnp.where(sub < j, gate, neg_inf)
        cnt = jnp.zeros((nbp, blk), jnp.int32)
        for m in range(nb):
            other = gate[m:m + 1, :]
            beats = (other > gate) | ((other == gate) & (m < sub))
            cnt = cnt + beats.astype(jnp.int32)
        keep = ((sub < j) & (cnt < n_sel)) | (sub == j) | (sub >= nb)
        bias = jnp.where(keep, 0.0, masked)
        parts = [qh, bias, const_rows] if h % 2 == 0 else [bias, const_rows, qh]
        q_aug.append(jnp.concatenate(parts, axis=0).astype(BF16))
    q_aug = jnp.stack(q_aug, axis=0)

    def scores(n):
        start = pl.multiple_of(n * blk, blk)
        kn = ka_ref[:, pl.ds(start, blk), :]
        return lax.dot_general(kn, q_aug, BNN, preferred_element_type=F32), vat_ref[n]

    s, vj = scores(j)
    causal = _iota2((1, blk, blk), 1) <= _iota2((1, blk, blk), 2)
    s = jnp.where(causal, s, masked)
    m0 = jnp.max(s, axis=1, keepdims=True)
    acc0 = lax.dot_general(vj, jnp.exp(s - m0).astype(BF16), BNN, preferred_element_type=F32)

    def past_block(n, carry):
        m_run, acc = carry
        sc, vn = scores(n)
        m_new = jnp.maximum(m_run, jnp.max(sc, axis=1, keepdims=True))
        pr = jnp.exp(sc - m_new).astype(BF16)
        acc_new = jnp.exp(m_run - m_new) * acc + lax.dot_general(vn, pr, BNN, preferred_element_type=F32)
        return m_new, acc_new

    _, acc = lax.fori_loop(0, j, past_block, (m0, acc0))
    out_t = jnp.concatenate([acc[h, :hd] / acc[h, hd:hd + 1] for h in range(heads)], axis=0)
    o_ref[0] = out_t.T


def _moba(q, k, v):
    b, s, dim = q.shape
    heads, hd, blk = MOBA_HEADS, MOBA_HD, MOBA_BLOCK
    assert s % blk == 0
    nb = s // blk
    nbp = -(-nb // SUBLANES) * SUBLANES
    assert nbp + 2 <= hd
    n_sel = min(MOBA_TOPK, nb - 1)
    return pl.pallas_call(
        functools.partial(_moba_kernel, nb=nb, blk=blk, n_sel=n_sel, heads=heads, hd=hd),
        grid=(b, nb),
        in_specs=[
            pl.BlockSpec((1, blk, dim), lambda i, j: (i, j, 0)),
            pl.BlockSpec((1, s, dim), lambda i, j: (i, 0, 0)),
            pl.BlockSpec((1, s, dim), lambda i, j: (i, 0, 0)),
        ],
        out_specs=pl.BlockSpec((1, blk, dim), lambda i, j: (i, j, 0)),
        out_shape=jax.ShapeDtypeStruct((b, s, dim), F32),
        scratch_shapes=[pltpu.VMEM((nbp, dim), F32),
                        pltpu.VMEM((heads, s, 2 * hd), BF16),
                        pltpu.VMEM((nb, heads, hd + BF16_SUBLANES, blk), BF16)],
        compiler_params=_params("parallel", "arbitrary"),
        name="moba",
    )(q, k, v)


def _hgrn_front(q, fr, v, lb, *, c, n_chunks, heads, dk):
    rows = n_chunks * c
    dim = heads * dk
    sig = jax.nn.sigmoid(fr)
    lf = jnp.log(lb + (1.0 - lb) * sig)
    kf = (1.0 - lb) * (1.0 - sig)
    yield
    r_i, c_i = _iota2((rows, rows), 0), _iota2((rows, rows), 1)
    tri_all = (r_i >= c_i) & (r_i // c == c_i // c)
    b = _cumsum_matmul(tri_all, lf)
    per_chunk = lambda r: jnp.concatenate(
        [jnp.broadcast_to(b[i * c + r:i * c + r + 1, :], (c, dim)) for i in range(n_chunks)], axis=0)
    ends = [b[(i + 1) * c - 1:(i + 1) * c, :] for i in range(n_chunks)]
    b_last = per_chunk(c - 1)
    yield
    b_mid = per_chunk(c // 2 - 1)
    rel = b - b_mid
    qd = q * jnp.exp(rel)
    yield
    kd = kf * jnp.exp(-rel)
    yield
    q_in = qd * jnp.exp(b_mid)
    yield
    kw = kf * jnp.exp(b_last - b)
    w_all = jnp.exp(jnp.concatenate(ends, axis=0))
    return dict(qd=qd, kd=kd, q_in=q_in, kw=kw, w_all=w_all, v=v)


def _hgrn_back(ops, s, g, norm_w, *, c, n_chunks, heads, dk):
    def by_head(t, n_rows=c):
        return jnp.stack([t[i * n_rows:(i + 1) * n_rows, h * dk:(h + 1) * dk]
                          for i in range(n_chunks) for h in range(heads)], axis=0)
    hp, pw = heads // 2, 2 * dk
    by_pair = lambda t: jnp.stack([t[i * c:(i + 1) * c, j * pw:(j + 1) * pw]
                                   for i in range(n_chunks) for j in range(hp)], axis=0)
    tri_pair = _iota2((1, c, 2 * c), 1) >= (_iota2((1, c, 2 * c), 2) & (c - 1))
    q_hi, q_lo = _pieces(by_pair(ops["qd"]), 2)
    k_hi, k_lo = _pieces(_block_diag(by_pair(ops["kd"]), dk), 2)
    sc = lax.dot_general(jnp.concatenate([q_hi, q_hi, q_lo], axis=2), jnp.concatenate([k_hi, k_lo, k_hi], axis=2),
                         BNT, preferred_element_type=F32)
    sc = jnp.where(tri_pair, sc, 0.0)
    yield
    o_pair = _bdot(sc, _block_diag(by_pair(ops["v"]).astype(BF16), dk), BNN)
    yield
    qd_h, v_h = by_head(ops["q_in"]), by_head(ops["v"])
    s_add = _bdot(v_h, by_head(ops["kw"]), BTN)
    yield
    w_all_h = by_head(ops["w_all"], 1)
    o_rows = []
    for i in range(n_chunks):
        sl = slice(i * heads, (i + 1) * heads)
        o_intra = jnp.stack([o_pair[i * hp + h // 2][:, (h % 2) * dk:(h % 2 + 1) * dk]
                             for h in range(heads)], axis=0)
        o_h = o_intra + _bdot(qd_h[sl], s, BNT)
        s = s * w_all_h[sl] + s_add[sl]
        o_h = o_h * lax.rsqrt(jnp.mean(o_h * o_h, axis=-1, keepdims=True) + EPS)
        o_rows.append(jnp.concatenate([o_h[h] for h in range(heads)], axis=1))
        yield
    o = jnp.concatenate(o_rows, axis=0)
    return o * norm_w * jax.nn.sigmoid(g), s


def _proj_hgrn_kernel(x_ref, g_ref, w_hbm, lbl_ref, nw_ref, o_ref, w_ref, stage, sem, st_ref, *, w_layer, layer,
                      tiles_per_seq, tn, chunk, n_chunks, heads, dk):
    c = chunk
    dim = heads * dk
    rows = n_chunks * c
    n_groups = x_ref.shape[0] // rows
    step = pl.program_id(0)

    @pl.when(step == 0)
    def _():
        _load_weight(w_hbm, w_layer, w_ref, stage, sem, stage.shape[1])
        st_ref[...] = jnp.zeros_like(st_ref)

    logits = lbl_ref[...]
    e = jnp.exp(logits - jnp.max(logits, axis=0, keepdims=True))
    sm = e / jnp.sum(e, axis=0, keepdims=True)
    lb = jnp.sum(sm[0:layer + 1, :], axis=0, keepdims=True) - sm[0:1, :]

    h = _rmsnorm(x_ref[...], g_ref[...]).astype(BF16)
    quarter = lambda k: jnp.concatenate([_dot(h, w_ref[:, c0:c0 + tn]) for c0 in range(k * dim, (k + 1) * dim, tn)],
                                        axis=1)
    f_raw, q, v = quarter(1), quarter(0), quarter(2)

    def gate_quarter():
        parts = []
        for c0 in range(3 * dim, 4 * dim, tn):
            parts.append(_dot(h, w_ref[:, c0:c0 + tn]))
            yield
        return jnp.concatenate(parts, axis=1)

    dims = dict(c=c, n_chunks=n_chunks, heads=heads, dk=dk)
    grp = lambda t, gi: t[gi * rows:(gi + 1) * rows, :]
    front = lambda gi: _hgrn_front(grp(q, gi), grp(f_raw, gi), grp(v, gi), lb, **dims)
    ready, g_raw = _interleave(front(0), gate_quarter())
    s = jnp.where(step % tiles_per_seq == 0, 0.0, st_ref[...])
    for gi in range(n_groups):
        back = _hgrn_back(ready, s, grp(g_raw, gi), nw_ref[...], **dims)
        if gi + 1 < n_groups:
            (out, s), ready = _interleave(back, front(gi + 1))
        else:
            ((out, s),) = _interleave(back)
        o_ref[gi * rows:(gi + 1) * rows, :] = out
    st_ref[...] = s


def _proj_hgrn(x, g, w, w_layer, lb_logits, layer, norm_w, seq, *, tm=512, tn=512, n_slabs=8, chunk=HG_CHUNK,
               n_chunks=HG_CHUNKS_PER_GROUP):
    m, d = x.shape
    heads, dk = HG_HEADS, HG_DK
    dim = heads * dk
    n_total = w.shape[2]
    assert n_total == 4 * dim and dim % tn == 0 and d % n_slabs == 0
    assert chunk & (chunk - 1) == 0
    assert seq % tm == 0 and tm % (chunk * n_chunks) == 0
    return pl.pallas_call(
        functools.partial(_proj_hgrn_kernel, w_layer=w_layer, layer=layer, tiles_per_seq=seq // tm, tn=tn,
                          chunk=chunk, n_chunks=n_chunks, heads=heads, dk=dk),
        grid=(m // tm,),
        in_specs=[pl.BlockSpec((tm, d), lambda i: (i, 0)), _resident((1, d)), pl.BlockSpec(memory_space=pl.ANY),
                  _resident(lb_logits.shape), _resident((1, dim))],
        out_specs=pl.BlockSpec((tm, dim), lambda i: (i, 0)),
        out_shape=jax.ShapeDtypeStruct((m, dim), F32),
        scratch_shapes=[pltpu.VMEM((d, n_total), BF16), pltpu.VMEM((2, d // n_slabs, n_total), F32),
                        pltpu.SemaphoreType.DMA((2,)), pltpu.VMEM((heads, dk, dk), F32)],
        compiler_params=_params("arbitrary"),
        name="proj_hgrn",
    )(x, g.reshape(1, d), w, lb_logits, norm_w.reshape(1, dim))


def kernel(x, norm_g, ffn1_wg, ffn1_wu, ffn1_wd, ffn2_wg, ffn2_wu, ffn2_wd, ev_w_in, ev_w_out, rw_mu, rw_w0, rw_w2, rw_a0, rw_a2, rw_g2, rw_k_k, rw_k_a, rw_r_k, rw_lnx_w, rw_lnx_b, od_w_in, od_w_out, hg_norm_w, hg_lb_logits, final_g):
    bsz, seq, d = x.shape
    depth = norm_g.shape[0]
    rwkv_dim = RWKV_HEADS * RWKV_HD
    rwkv_cols = 3 * rwkv_dim + LORA_W + LORA_A + LORA_G
    moba_dim = MOBA_HEADS * MOBA_HD
    xf = x.reshape(bsz * seq, d)
    for l in range(depth):
        xf = _ffn(xf, [], None, 0, norm_g[l, 0], ffn1_wg, ffn1_wu, ffn1_wd, l, final_g, final_norm=False)
        if l % 2 == 0:
            e = l // 2
            assert ev_w_in.shape[2] == rwkv_cols + 3 * moba_dim
            y_a, q, k, v = _proj_rwkv(xf, norm_g[l, 1], ev_w_in, e, seq, rw_mu[e], rw_w0[e], rw_w2[e], rw_a0[e],
                                      rw_a2[e], rw_g2[e], rw_k_k[e], rw_k_a[e], rw_r_k[e], rw_lnx_w[e],
                                      rw_lnx_b[e], [moba_dim, moba_dim, moba_dim])
            y_b = _moba(q.reshape(bsz, seq, moba_dim), k.reshape(bsz, seq, moba_dim),
                        v.reshape(bsz, seq, moba_dim))
            ys, wo, wo_layer = [y_a, y_b.reshape(-1, moba_dim)], ev_w_out, e
        else:
            o = l // 2
            y = _proj_hgrn(xf, norm_g[l, 1], od_w_in, o, hg_lb_logits, l, hg_norm_w[o], seq)
            ys, wo, wo_layer = [y], od_w_out, o
        xf = _ffn(xf, ys, wo, wo_layer, norm_g[l, 2], ffn2_wg, ffn2_wu, ffn2_wd, l, final_g,
                  final_norm=(l == depth - 1))
    return xf.reshape(bsz, seq, d)
```

```python
import functools

import jax
import jax.numpy as jnp
from jax import lax
from jax.experimental import pallas as pl
from jax.experimental.pallas import tpu as pltpu

F32 = jnp.float32
BF16 = jnp.bfloat16
NN = (((1,), (0,)), ((), ()))
BNN = (((2,), (1,)), ((0,), (0,)))
BNT = (((2,), (2,)), ((0,), (0,)))
BTN = (((1,), (1,)), ((0,), (0,)))
MIX_PIECES = 2

EPS = 1e-6
LNX_EPS = 64e-5
RWKV_HEADS = 8
RWKV_HD = 64
LORA_W = 64
LORA_A = 64
LORA_G = 128
MOBA_HEADS = 8
MOBA_HD = 64
MOBA_BLOCK = 256
MOBA_TOPK = 3
MOBA_SOFTMAX_LAG = 5
MOBA_VALUE_LAG = 8
HG_HEADS = 8
HG_DK = 128
RWKV_CHUNK = 64
RWKV_CHUNKS_PER_GROUP = 4
HG_CHUNK = 64
HG_CHUNKS_PER_GROUP = 4
SUBLANES = 8
BF16_SUBLANES = 16
V7X_VMEM_BYTES = 64 * 1024 * 1024
VMEM_LIMIT_BYTES = V7X_VMEM_BYTES - 8 * 1024 * 1024


def _params(*semantics):
    return pltpu.CompilerParams(dimension_semantics=semantics, vmem_limit_bytes=VMEM_LIMIT_BYTES)


def _dot(a, b):
    return jnp.dot(a, b, preferred_element_type=F32)


def _pieces(a, n):
    if isinstance(a, tuple):
        return a
    out = []
    for i in range(n):
        hi = a.astype(BF16)
        out.append(hi)
        if i + 1 < n:
            a = a - hi.astype(F32)
    return tuple(out)


def _mm(a, b, dims=NN, n=MIX_PIECES):
    a = _pieces(a, n)
    b = _pieces(b, n)
    order = max(len(a), len(b)) - 1
    out = None
    for i, ai in enumerate(a):
        for j, bj in enumerate(b):
            if i + j <= order:
                t = lax.dot_general(ai, bj, dims, preferred_element_type=F32)
                out = t if out is None else out + t
    return out


def _mm_fused(a, b):
    a_hi, a_lo = _pieces(a, 2)
    b_hi, b_lo = _pieces(b, 2)
    return _dot(jnp.concatenate([a_hi, a_hi, a_lo], axis=1), jnp.concatenate([b_hi, b_lo, b_hi], axis=0))


def _cumsum_matmul(mask, x):
    m = mask.astype(BF16)
    hi, lo = _pieces(x, 2)
    return _dot(jnp.concatenate([m, m], axis=1), jnp.concatenate([hi, lo], axis=0))


def _rmsnorm(x, g):
    return x * lax.rsqrt(jnp.mean(x * x, axis=-1, keepdims=True) + EPS) * g


def _iota2(shape, dim):
    return lax.broadcasted_iota(jnp.int32, shape, dim)


def _block_ones(n, width):
    return (_iota2((n, n), 0) // width == _iota2((n, n), 1) // width).astype(F32)


def _interleave(*gens):
    results = [None] * len(gens)
    live = list(range(len(gens)))
    while live:
        for i in list(live):
            try:
                next(gens[i])
            except StopIteration as stop:
                results[i] = stop.value
                live.remove(i)
    return results


def _load_weight(w_hbm, layer, dst_ref, stage_ref, sem_ref, slab_rows):
    n_rows = dst_ref.shape[0]
    assert n_rows % slab_rows == 0 and slab_rows <= stage_ref.shape[1]
    n_slabs = n_rows // slab_rows

    def copy(s):
        return pltpu.make_async_copy(w_hbm.at[layer, pl.ds(s * slab_rows, slab_rows), :],
                                     stage_ref.at[s % 2, pl.ds(0, slab_rows), :], sem_ref.at[s % 2])

    copy(0).start()
    for s in range(n_slabs):
        if s + 1 < n_slabs:
            copy(s + 1).start()
        copy(s).wait()
        dst_ref[pl.ds(s * slab_rows, slab_rows), :] = stage_ref[s % 2, pl.ds(0, slab_rows), :].astype(BF16)


def _ffn_kernel(x_ref, *refs, n_y, layer, out_layer, final_norm, tf):
    y_refs = refs[:n_y]
    has_out = n_y > 0
    (wo_hbm,) = refs[n_y:n_y + 1] if has_out else (None,)
    g_ref, wg_hbm, wu_hbm, wd_hbm, fg_ref, o_ref = refs[n_y + has_out:n_y + has_out + 6]
    scratch = refs[n_y + has_out + 6:]
    wg_ref, wu_ref, wd_ref, wide_stage, tall_stage, sem, wo_sem = scratch[:7]
    wo_ref = scratch[7] if has_out else None
    n_slabs = wg_ref.shape[1] // tf

    def slab_copies(j):
        slot, cols = j % 2, pl.ds(j * tf, tf)
        return (pltpu.make_async_copy(wg_hbm.at[layer, :, cols], wide_stage.at[slot, 0], sem.at[slot, 0]),
                pltpu.make_async_copy(wu_hbm.at[layer, :, cols], wide_stage.at[slot, 1], sem.at[slot, 1]),
                pltpu.make_async_copy(wd_hbm.at[layer, cols, :], tall_stage.at[slot], sem.at[slot, 2]))

    def body(first_step):
        if first_step:
            for j in range(min(2, n_slabs)):
                for cp in slab_copies(j):
                    cp.start()
            if has_out:
                _load_weight(wo_hbm, out_layer, wo_ref, wo_ref_stage, wo_sem, wo_ref.shape[0] // 4)
        x = x_ref[...]
        row0 = 0
        for y_ref in y_refs:
            rows = y_ref.shape[1]
            x = x + _dot(y_ref[...].astype(BF16), wo_ref[row0:row0 + rows, :])
            row0 += rows
        h = _rmsnorm(x, g_ref[...]).astype(BF16)
        acc = None
        for j in range(n_slabs):
            c0 = j * tf
            if first_step:
                for cp in slab_copies(j):
                    cp.wait()
                wg_ref[:, c0:c0 + tf] = wide_stage[j % 2, 0].astype(BF16)
                wu_ref[:, c0:c0 + tf] = wide_stage[j % 2, 1].astype(BF16)
                wd_ref[c0:c0 + tf, :] = tall_stage[j % 2].astype(BF16)
                if j + 2 < n_slabs:
                    for cp in slab_copies(j + 2):
                        cp.start()
            gate = _dot(h, wg_ref[:, c0:c0 + tf])
            up = _dot(h, wu_ref[:, c0:c0 + tf])
            act = (gate * jax.nn.sigmoid(gate) * up).astype(BF16)
            part = _dot(act, wd_ref[c0:c0 + tf, :])
            acc = part if acc is None else acc + part
        out = x + 0.5 * acc
        if final_norm:
            out = _rmsnorm(out, fg_ref[...])
        o_ref[...] = out

    wo_ref_stage = scratch[8] if has_out else None
    pl.when(pl.program_id(0) == 0)(lambda: body(True))
    pl.when(pl.program_id(0) != 0)(lambda: body(False))


def _resident(shape):
    return pl.BlockSpec(shape, lambda i: (0,) * len(shape), pipeline_mode=pl.Buffered(1))


def _ffn(x, ys, wo, out_layer, g, wg, wu, wd, layer, final_g, *, final_norm, tm=512, tf=256):
    m, d = x.shape
    f = wg.shape[2]
    tm = min(tm, m)
    assert f % tf == 0
    hbm = pl.BlockSpec(memory_space=pl.ANY)
    has_out = len(ys) > 0
    scratch = [pltpu.VMEM((d, f), BF16), pltpu.VMEM((d, f), BF16), pltpu.VMEM((f, d), BF16),
               pltpu.VMEM((2, 2, d, tf), F32), pltpu.VMEM((2, tf, d), F32),
               pltpu.SemaphoreType.DMA((2, 3)), pltpu.SemaphoreType.DMA((2,))]
    if has_out:
        assert sum(y.shape[1] for y in ys) == wo.shape[1] and wo.shape[1] % 4 == 0
        scratch += [pltpu.VMEM(wo.shape[1:], BF16), pltpu.VMEM((2, wo.shape[1] // 4, wo.shape[2]), F32)]
    return pl.pallas_call(
        functools.partial(_ffn_kernel, n_y=len(ys), layer=layer, out_layer=out_layer, final_norm=final_norm,
                          tf=tf),
        grid=(m // tm,),
        in_specs=[pl.BlockSpec((tm, d), lambda i: (i, 0))]
        + [pl.BlockSpec((tm, y.shape[1]), lambda i: (i, 0)) for y in ys]
        + ([hbm] if has_out else [])
        + [_resident((1, d)), hbm, hbm, hbm, _resident((1, d))],
        out_specs=pl.BlockSpec((tm, d), lambda i: (i, 0)),
        out_shape=jax.ShapeDtypeStruct((m, d), F32),
        scratch_shapes=scratch,
        compiler_params=_params("arbitrary"),
        name="ffn",
    )(x, *ys, *([wo] if has_out else []), g.reshape(1, d), wg, wu, wd, final_g.reshape(1, d))


def _bdot(a, b, dims):
    return lax.dot_general(a.astype(BF16), b.astype(BF16), dims, preferred_element_type=F32)


def _block_diag(x, half):
    lo = _iota2((1, 1, 2 * half), 2) < half
    zero = jnp.zeros((), x.dtype)
    return jnp.concatenate([jnp.where(lo, x, zero), jnp.where(lo, zero, x)], axis=1)


def _unit_lower_inverse(a_strict, c):
    row = _iota2((1, c, 2 * c), 1)
    col = _iota2((1, c, 2 * c), 2) & (c - 1)
    eye = (row == col).astype(F32)
    t = None
    m = 1
    while m < c:
        mask = ((row // (2 * m)) == (col // (2 * m))) & ((row & m) != 0) & ((col & m) == 0)
        lm = jnp.where(mask, a_strict, 0.0)
        if t is None:
            t = eye - lm
        else:
            t = t - _bdot(_bdot(t, _block_diag(lm.astype(BF16), c), BNN), _block_diag(t.astype(BF16), c), BNN)
            yield
        m *= 2
    return t


def _head_sum(t, hd):
    tile_ones = _block_ones(2 * hd, hd)
    return jnp.concatenate([_bdot(t[:, i:i + 2 * hd], tile_ones, NN) for i in range(0, t.shape[1], 2 * hd)], axis=1)


def _rwkv_front(xs, w0_ref, w2_ref, a0_ref, a2_ref, g2_ref, kk_ref, ka_ref, *, c, n_chunks, heads, hd):
    rows = n_chunks * c
    dim = heads * hd
    r = xs[:, 0:dim]
    k = xs[:, dim:2 * dim]
    v = xs[:, 2 * dim:3 * dim]
    o1 = 3 * dim
    g_lr = xs[:, o1 + LORA_W + LORA_A:o1 + LORA_W + LORA_A + LORA_G]

    wa = xs[:, o1:o1 + LORA_W + LORA_A]
    wa = jnp.where(_iota2(wa.shape, 1) < LORA_W, jnp.tanh(wa), wa)
    w2a2 = jnp.concatenate(
        [jnp.concatenate([w2_ref[...], jnp.zeros((LORA_W, dim), F32)], axis=1),
         jnp.concatenate([jnp.zeros((LORA_A, dim), F32), a2_ref[...]], axis=1)], axis=0)
    za = _mm_fused(wa, w2a2)
    z = w0_ref[...] + za[:, :dim]
    softplus = jnp.maximum(-z, 0.0) + jnp.log(1.0 + jnp.exp(-jnp.abs(z)))
    w_raw = -softplus - 0.5
    lw = -jnp.exp(w_raw)
    yield
    a = jax.nn.sigmoid(a0_ref[...] + za[:, dim:])
    g = _mm_fused(jax.nn.sigmoid(g_lr), g2_ref[...])
    yield
    kk = k * kk_ref[...]
    kk = kk * lax.rsqrt(jnp.maximum(_head_sum(kk * kk, hd), 1e-24))
    k2 = k * (1.0 + (a - 1.0) * ka_ref[...])
    bb = kk * a
    yield

    r_i, c_i = _iota2((rows, rows), 0), _iota2((rows, rows), 1)
    tri_incl = (r_i >= c_i) & (r_i // c == c_i // c)
    cum = _cumsum_matmul(tri_incl, lw)
    ends = [cum[(i + 1) * c - 1:(i + 1) * c, :] for i in range(n_chunks)]
    cum_last = jnp.concatenate([jnp.broadcast_to(e, (c, dim)) for e in ends], axis=0)
    yield
    w_incl = jnp.exp(cum)
    w_excl = jnp.exp(cum - lw)
    yield
    w_inv = jnp.exp(-cum)
    w_tail = jnp.exp(cum_last - cum)
    w_all = jnp.exp(jnp.concatenate(ends, axis=0))
    yield
    ops = dict(kt=kk * w_excl, rt=r * w_incl, bt=bb * w_inv)
    yield
    ops.update(kd=k2 * w_inv, bw=bb * w_tail, kw=k2 * w_tail)
    yield
    ops.update(v=v, w_all=w_all, r=r, k2=k2, g=g)
    return ops


def _rwkv_back(ops, s, rk_ref, lnw_ref, lnb_ref, *, c, n_chunks, heads, hd):
    pw = 2 * hd
    pairs = heads // 2
    def by_pair(t, n_rows=c):
        return jnp.stack([t[i * n_rows:(i + 1) * n_rows, j * pw:(j + 1) * pw]
                          for i in range(n_chunks) for j in range(pairs)], axis=0)
    bd_f = lambda t: _block_diag(t.astype(BF16), hd)
    kt_p, rt_p, bt_p, kd_p, kw_p, bw_p, v_p = (by_pair(ops[n]) for n in ("kt", "rt", "bt", "kd", "kw", "bw", "v"))
    w_all_p = by_pair(ops["w_all"], 1)
    row_t = _iota2((1, c, 2 * c), 1)
    col_t = _iota2((1, c, 2 * c), 2) & (c - 1)
    strict_p, incl_p = row_t > col_t, row_t >= col_t
    same_head = (_iota2((1, pw, pw), 1) // hd) == (_iota2((1, pw, pw), 2) // hd)

    kr = jnp.concatenate([kt_p, rt_p], axis=1)
    g_all = _bdot(kr, jnp.concatenate([bd_f(bt_p), bd_f(kd_p)], axis=1), BNT)
    yield
    gb, gk = g_all[:, :, :2 * c], g_all[:, :, 2 * c:]
    a_b = jnp.where(strict_p, gb[:, :c], 0.0)
    a_k = jnp.where(strict_p, gk[:, :c], 0.0)
    p_b = jnp.where(incl_p, gb[:, c:], 0.0)
    p_k = jnp.where(incl_p, gk[:, c:], 0.0)
    akpk = _bdot(jnp.concatenate([a_k, p_k], axis=1), bd_f(v_p), BNN)
    yield
    t_inv = yield from _unit_lower_inverse(a_b, c)
    kv1 = _bdot(t_inv, jnp.concatenate([bd_f(kt_p), bd_f(akpk[:, :c])], axis=2), BNN)
    yield
    kt1, v1 = kv1[:, :, :pw], kv1[:, :, pw:]
    pbk = _bdot(p_b, jnp.concatenate([bd_f(kt1), bd_f(v1)], axis=2), BNN)
    yield
    q_mat = rt_p - pbk[:, :, :pw]
    z_mat = akpk[:, c:] - pbk[:, :, pw:]
    x_mat = jnp.where(same_head, _bdot(bw_p, kt1, BTN), 0.0)
    yield
    n_mat = jnp.where(same_head, _bdot(jnp.concatenate([v_p, v1], axis=1),
                                       jnp.concatenate([kw_p, -bw_p], axis=1), BTN), 0.0)
    yield
    y_rows = []
    for i in range(n_chunks):
        sl = slice(i * pairs, (i + 1) * pairs)
        y_p = _bdot(q_mat[sl], s, BNT) + z_mat[sl]
        s = s * w_all_p[sl] - _bdot(s, x_mat[sl], BNT) + n_mat[sl]
        y_rows.append(jnp.concatenate([y_p[j] for j in range(pairs)], axis=1))
        yield
    y = jnp.concatenate(y_rows, axis=0)

    inv_hd = 1.0 / hd
    mean = _head_sum(y, hd) * inv_hd
    yc = y - mean
    var = _head_sum(yc * yc, hd) * inv_hd
    yield
    yn = yc * lax.rsqrt(var + LNX_EPS) * lnw_ref[...] + lnb_ref[...]
    bonus = _head_sum(ops["r"] * ops["k2"] * rk_ref[...], hd) * ops["v"]
    return (yn + bonus) * ops["g"], s


def _proj_rwkv_kernel(x_ref, g_ref, w_hbm, mu_ref, w0_ref, w2_ref, a0_ref, a2_ref, g2_ref, kk_ref, ka_ref, rk_ref,
                      lnw_ref, lnb_ref, y_ref, q_ref, k_ref, v_ref, w_ref, stage, sem, carry_ref, st_ref, *,
                      w_layer, tiles_per_seq, tn, chunk, n_chunks, heads, hd):
    c = chunk
    rows = n_chunks * c
    tm = x_ref.shape[0]
    n_groups = tm // rows
    cols = mu_ref.shape[1]
    step = pl.program_id(0)

    @pl.when(step == 0)
    def _():
        _load_weight(w_hbm, w_layer, w_ref, stage, sem, stage.shape[1])
        carry_ref[...] = jnp.zeros_like(carry_ref)
        st_ref[...] = jnp.zeros_like(st_ref)

    h = _rmsnorm(x_ref[...], g_ref[...]).astype(BF16)
    p = jnp.concatenate([_dot(h, w_ref[:, c0:min(c0 + tn, cols)]) for c0 in range(0, cols, tn)], axis=1)
    first = step % tiles_per_seq == 0
    prev_last = jnp.where(first, 0.0, carry_ref[...])
    prev = jnp.where(_iota2((tm, 1), 0) == 0, prev_last, pltpu.roll(p, 1, axis=0))
    carry_ref[...] = p[tm - 1:tm, :]
    xs = p + (prev - p) * mu_ref[...]

    def attention_groups():
        col0 = cols
        for o_ref in (q_ref, k_ref, v_ref):
            n = o_ref.shape[1]
            for c0 in range(0, n, tn):
                o_ref[:, c0:min(c0 + tn, n)] = _dot(h, w_ref[:, col0 + c0:col0 + min(c0 + tn, n)])
                yield
            col0 += n

    front_refs = (w0_ref, w2_ref, a0_ref, a2_ref, g2_ref, kk_ref, ka_ref)
    dims = dict(c=c, n_chunks=n_chunks, heads=heads, hd=hd)
    front = lambda gi: _rwkv_front(xs[gi * rows:(gi + 1) * rows], *front_refs, **dims)
    ready, _ = _interleave(front(0), attention_groups())
    s = jnp.where(first, 0.0, st_ref[...])
    for gi in range(n_groups):
        back = _rwkv_back(ready, s, rk_ref, lnw_ref, lnb_ref, **dims)
        if gi + 1 < n_groups:
            (out, s), ready = _interleave(back, front(gi + 1))
        else:
            ((out, s),) = _interleave(back)
        y_ref[gi * rows:(gi + 1) * rows, :] = out
    st_ref[...] = s


def _proj_rwkv(x, g, w, w_layer, seq, mu, w0, w2, a0, a2, g2, k_k, k_a, r_k, lnx_w, lnx_b, attn_widths, *, tm=512,
               tn=512, n_slabs=8, chunk=RWKV_CHUNK, n_chunks=RWKV_CHUNKS_PER_GROUP):
    m, d = x.shape
    heads, hd = RWKV_HEADS, RWKV_HD
    dim = heads * hd
    cols = mu.shape[-1]
    n_total = w.shape[2]
    assert n_total == cols + sum(attn_widths) and len(attn_widths) == 3 and d % n_slabs == 0
    assert chunk & (chunk - 1) == 0
    assert seq % tm == 0 and tm % (chunk * n_chunks) == 0
    row = lambda t: t.reshape(1, -1)
    vecs = [row(mu), row(w0), w2, row(a0), a2, g2, row(k_k), row(k_a), row(r_k), row(lnx_w), row(lnx_b)]
    widths = [dim] + list(attn_widths)
    return pl.pallas_call(
        functools.partial(_proj_rwkv_kernel, w_layer=w_layer, tiles_per_seq=seq // tm, tn=tn, chunk=chunk,
                          n_chunks=n_chunks, heads=heads, hd=hd),
        grid=(m // tm,),
        in_specs=[pl.BlockSpec((tm, d), lambda i: (i, 0)), _resident((1, d)), pl.BlockSpec(memory_space=pl.ANY)]
        + [_resident(t.shape) for t in vecs],
        out_specs=[pl.BlockSpec((tm, n), lambda i: (i, 0)) for n in widths],
        out_shape=[jax.ShapeDtypeStruct((m, n), F32) for n in widths],
        scratch_shapes=[pltpu.VMEM((d, n_total), BF16), pltpu.VMEM((2, d // n_slabs, n_total), F32),
                        pltpu.SemaphoreType.DMA((2,)), pltpu.VMEM((1, cols), F32),
                        pltpu.VMEM((heads // 2, 2 * hd, 2 * hd), F32)],
        compiler_params=_params("arbitrary"),
        name="proj_rwkv",
    )(x, g.reshape(1, d), w, *vecs)


def _moba_kernel(q_ref, k_ref, v_ref, o_ref, km_ref, ka_ref, vat_ref, *, nb, blk, n_sel, heads, hd):
    j = pl.program_id(1)
    nbp = km_ref.shape[0]
    scale = hd ** -0.5
    masked = -1e30
    neg_inf = float("-inf")
    slopes = [2.0 ** (-8.0 * (h + 1) / heads) for h in range(heads)]

    @pl.when(j == 0)
    def _():
        km_ref[...] = jnp.zeros_like(km_ref)
        for n in range(nb):
            km_ref[n:n + 1, :] = jnp.mean(k_ref[0, n * blk:(n + 1) * blk, :], axis=0, keepdims=True)
        col = _iota2((blk, 2 * hd), 0).astype(F32)
        lane = _iota2((blk, 2 * hd), 1)
        feat_lane = lane & (hd - 1)
        for n in range(nb):
            rows = slice(n * blk, (n + 1) * blk)
            one_hot = jnp.where(feat_lane == n, 1.0, 0.0)
            for h in range(heads):
                k_feat = (one_hot + jnp.where(feat_lane == nbp, slopes[h] * col, 0.0)
                          + jnp.where(feat_lane == nbp + 1, slopes[h] * blk * n, 0.0))
                k_tile = k_ref[0, rows, (h // 2) * 2 * hd:(h // 2 + 1) * 2 * hd]
                own_lanes = (lane < hd) if h % 2 == 0 else (lane >= hd)
                ka_ref[h, rows, :] = jnp.where(own_lanes, k_tile, k_feat).astype(BF16)
        ones_row = jnp.where(_iota2((vat_ref.shape[2] - hd, blk), 0) == 0, 1.0, 0.0)
        for n in range(nb):
            vt = v_ref[0, n * blk:(n + 1) * blk, :].T
            for h in range(heads):
                vat_ref[n, h] = jnp.concatenate([vt[h * hd:(h + 1) * hd], ones_row], axis=0).astype(BF16)

    qt = (q_ref[0] * scale).T
    sub = _iota2((nbp, blk), 0)
    const_rows = jnp.where(_iota2((hd - nbp, blk), 0) < 2, 1.0, 0.0)
    q_aug = []
    for h in range(heads):
        sl = slice(h * hd, (h + 1) * hd)
        qh = qt[sl]
        gate = _mm(km_ref[:, sl], qh)
        gate = jnp.where(sub < j, gate, neg_inf)
        cnt = jnp.zeros((nbp, blk), jnp.int32)
        for m in range(nb):
            other = gate[m:m + 1, :]
            beats = (other > gate) | ((other == gate) & (m < sub))
            cnt = cnt + beats.astype(jnp.int32)
        keep = ((sub < j) & (cnt < n_sel)) | (sub == j) | (sub >= nb)
        bias = jnp.where(keep, 0.0, masked)
        parts = [qh, bias, const_rows] if h % 2 == 0 else [bias, const_rows, qh]
        q_aug.append(jnp.concatenate(parts, axis=0).astype(BF16))
    q_aug = jnp.stack(q_aug, axis=0)

    def scores(n):
        start = pl.multiple_of(n * blk, blk)
        kn = ka_ref[:, pl.ds(start, blk), :]
        return lax.dot_general(kn, q_aug, BNN, preferred_element_type=F32), vat_ref[n]

    s, vj = scores(j)
    causal = _iota2((1, blk, blk), 1) <= _iota2((1, blk, blk), 2)
    s = jnp.where(causal, s, masked)
    m0 = jnp.max(s, axis=1, keepdims=True)
    acc0 = lax.dot_general(vj, jnp.exp(s - m0).astype(BF16), BNN, preferred_element_type=F32)

    half = blk // 2
    units = [(h, slice(qh * half, (qh + 1) * half)) for h in range(heads) for qh in range(2)]
    q_units = [q_aug[h][:, cols] for h, cols in units]

    def past_block(n, carry):
        start = pl.multiple_of(n * blk, blk)
        scores_of, soft_of, out = {}, {}, []
        for i in range(len(units) + MOBA_VALUE_LAG):
            if i < len(units):
                scores_of[i] = _dot(ka_ref[units[i][0], pl.ds(start, blk), :], q_units[i])
            if 0 <= i - MOBA_SOFTMAX_LAG < len(units):
                u = i - MOBA_SOFTMAX_LAG
                sc, m_old = scores_of.pop(u), carry[u][0]
                m_new = jnp.maximum(m_old, jnp.max(sc, axis=0, keepdims=True))
                soft_of[u] = (m_new, jnp.exp(m_old - m_new), jnp.exp(sc - m_new).astype(BF16))
            if 0 <= i - MOBA_VALUE_LAG < len(units):
                u = i - MOBA_VALUE_LAG
                m_new, alpha, pr = soft_of.pop(u)
                out.append((m_new, alpha * carry[u][1] + _dot(vat_ref[n, units[u][0]], pr)))
        return tuple(out)

    init = tuple((m0[h][:, cols], acc0[h][:, cols]) for h, cols in units)
    final = lax.fori_loop(0, j, past_block, init)
    out_t = jnp.concatenate(
        [jnp.concatenate([final[2 * h + qh][1][:hd] / final[2 * h + qh][1][hd:hd + 1] for qh in range(2)], axis=1)
         for h in range(heads)], axis=0)
    o_ref[0] = out_t.T


def _moba(q, k, v):
    b, s, dim = q.shape
    heads, hd, blk = MOBA_HEADS, MOBA_HD, MOBA_BLOCK
    assert s % blk == 0
    nb = s // blk
    nbp = -(-nb // SUBLANES) * SUBLANES
    assert nbp + 2 <= hd
    n_sel = min(MOBA_TOPK, nb - 1)
    return pl.pallas_call(
        functools.partial(_moba_kernel, nb=nb, blk=blk, n_sel=n_sel, heads=heads, hd=hd),
        grid=(b, nb),
        in_specs=[
            pl.BlockSpec((1, blk, dim), lambda i, j: (i, j, 0)),
            pl.BlockSpec((1, s, dim), lambda i, j: (i, 0, 0)),
            pl.BlockSpec((1, s, dim), lambda i, j: (i, 0, 0)),
        ],
        out_specs=pl.BlockSpec((1, blk, dim), lambda i, j: (i, j, 0)),
        out_shape=jax.ShapeDtypeStruct((b, s, dim), F32),
        scratch_shapes=[pltpu.VMEM((nbp, dim), F32),
                        pltpu.VMEM((heads, s, 2 * hd), BF16),
                        pltpu.VMEM((nb, heads, hd + BF16_SUBLANES, blk), BF16)],
        compiler_params=_params("parallel", "arbitrary"),
        name="moba",
    )(q, k, v)


def _hgrn_front(q, fr, v, lb, *, c, n_chunks, heads, dk):
    rows = n_chunks * c
    dim = heads * dk
    sig = jax.nn.sigmoid(fr)
    lf = jnp.log(lb + (1.0 - lb) * sig)
    kf = (1.0 - lb) * (1.0 - sig)
    yield
    r_i, c_i = _iota2((rows, rows), 0), _iota2((rows, rows), 1)
    tri_all = (r_i >= c_i) & (r_i // c == c_i // c)
    b = _cumsum_matmul(tri_all, lf)
    per_chunk = lambda r: jnp.concatenate(
        [jnp.broadcast_to(b[i * c + r:i * c + r + 1, :], (c, dim)) for i in range(n_chunks)], axis=0)
    ends = [b[(i + 1) * c - 1:(i + 1) * c, :] for i in range(n_chunks)]
    b_last = per_chunk(c - 1)
    yield
    b_mid = per_chunk(c // 2 - 1)
    rel = b - b_mid
    qd = q * jnp.exp(rel)
    yield
    kd = kf * jnp.exp(-rel)
    yield
    q_in = qd * jnp.exp(b_mid)
    yield
    kw = kf * jnp.exp(b_last - b)
    w_all = jnp.exp(jnp.concatenate(ends, axis=0))
    return dict(qd=qd, kd=kd, q_in=q_in, kw=kw, w_all=w_all, v=v)


def _hgrn_back(ops, s, g, norm_w, *, c, n_chunks, heads, dk):
    def by_head(t, n_rows=c):
        return jnp.stack([t[i * n_rows:(i + 1) * n_rows, h * dk:(h + 1) * dk]
                          for i in range(n_chunks) for h in range(heads)], axis=0)
    hp, pw = heads // 2, 2 * dk
    by_pair = lambda t: jnp.stack([t[i * c:(i + 1) * c, j * pw:(j + 1) * pw]
                                   for i in range(n_chunks) for j in range(hp)], axis=0)
    tri_pair = _iota2((1, c, 2 * c), 1) >= (_iota2((1, c, 2 * c), 2) & (c - 1))
    q_hi, q_lo = _pieces(by_pair(ops["qd"]), 2)
    k_hi, k_lo = _pieces(_block_diag(by_pair(ops["kd"]), dk), 2)
    sc = lax.dot_general(jnp.concatenate([q_hi, q_hi, q_lo], axis=2), jnp.concatenate([k_hi, k_lo, k_hi], axis=2),
                         BNT, preferred_element_type=F32)
    sc = jnp.where(tri_pair, sc, 0.0)
    yield
    o_pair = _bdot(sc, _block_diag(by_pair(ops["v"]).astype(BF16), dk), BNN)
    yield
    qd_h, v_h = by_head(ops["q_in"]), by_head(ops["v"])
    s_add = _bdot(v_h, by_head(ops["kw"]), BTN)
    yield
    w_all_h = by_head(ops["w_all"], 1)
    o_rows = []
    for i in range(n_chunks):
        sl = slice(i * heads, (i + 1) * heads)
        o_intra = jnp.stack([o_pair[i * hp + h // 2][:, (h % 2) * dk:(h % 2 + 1) * dk]
                             for h in range(heads)], axis=0)
        o_h = o_intra + _bdot(qd_h[sl], s, BNT)
        s = s * w_all_h[sl] + s_add[sl]
        o_h = o_h * lax.rsqrt(jnp.mean(o_h * o_h, axis=-1, keepdims=True) + EPS)
        o_rows.append(jnp.concatenate([o_h[h] for h in range(heads)], axis=1))
        yield
    o = jnp.concatenate(o_rows, axis=0)
    return o * norm_w * jax.nn.sigmoid(g), s


def _proj_hgrn_kernel(x_ref, g_ref, w_hbm, lbl_ref, nw_ref, o_ref, w_ref, stage, sem, st_ref, *, w_layer, layer,
                      tiles_per_seq, tn, chunk, n_chunks, heads, dk):
    c = chunk
    dim = heads * dk
    rows = n_chunks * c
    n_groups = x_ref.shape[0] // rows
    step = pl.program_id(0)

    @pl.when(step == 0)
    def _():
        _load_weight(w_hbm, w_layer, w_ref, stage, sem, stage.shape[1])
        st_ref[...] = jnp.zeros_like(st_ref)

    logits = lbl_ref[...]
    e = jnp.exp(logits - jnp.max(logits, axis=0, keepdims=True))
    sm = e / jnp.sum(e, axis=0, keepdims=True)
    lb = jnp.sum(sm[0:layer + 1, :], axis=0, keepdims=True) - sm[0:1, :]

    h = _rmsnorm(x_ref[...], g_ref[...]).astype(BF16)
    quarter = lambda k: jnp.concatenate([_dot(h, w_ref[:, c0:c0 + tn]) for c0 in range(k * dim, (k + 1) * dim, tn)],
                                        axis=1)
    f_raw, q, v = quarter(1), quarter(0), quarter(2)

    def gate_quarter():
        parts = []
        for c0 in range(3 * dim, 4 * dim, tn):
            parts.append(_dot(h, w_ref[:, c0:c0 + tn]))
            yield
        return jnp.concatenate(parts, axis=1)

    dims = dict(c=c, n_chunks=n_chunks, heads=heads, dk=dk)
    grp = lambda t, gi: t[gi * rows:(gi + 1) * rows, :]
    front = lambda gi: _hgrn_front(grp(q, gi), grp(f_raw, gi), grp(v, gi), lb, **dims)
    ready, g_raw = _interleave(front(0), gate_quarter())
    s = jnp.where(step % tiles_per_seq == 0, 0.0, st_ref[...])
    for gi in range(n_groups):
        back = _hgrn_back(ready, s, grp(g_raw, gi), nw_ref[...], **dims)
        if gi + 1 < n_groups:
            (out, s), ready = _interleave(back, front(gi + 1))
        else:
            ((out, s),) = _interleave(back)
        o_ref[gi * rows:(gi + 1) * rows, :] = out
    st_ref[...] = s


def _proj_hgrn(x, g, w, w_layer, lb_logits, layer, norm_w, seq, *, tm=512, tn=512, n_slabs=8, chunk=HG_CHUNK,
               n_chunks=HG_CHUNKS_PER_GROUP):
    m, d = x.shape
    heads, dk = HG_HEADS, HG_DK
    dim = heads * dk
    n_total = w.shape[2]
    assert n_total == 4 * dim and dim % tn == 0 and d % n_slabs == 0
    assert chunk & (chunk - 1) == 0
    assert seq % tm == 0 and tm % (chunk * n_chunks) == 0
    return pl.pallas_call(
        functools.partial(_proj_hgrn_kernel, w_layer=w_layer, layer=layer, tiles_per_seq=seq // tm, tn=tn,
                          chunk=chunk, n_chunks=n_chunks, heads=heads, dk=dk),
        grid=(m // tm,),
        in_specs=[pl.BlockSpec((tm, d), lambda i: (i, 0)), _resident((1, d)), pl.BlockSpec(memory_space=pl.ANY),
                  _resident(lb_logits.shape), _resident((1, dim))],
        out_specs=pl.BlockSpec((tm, dim), lambda i: (i, 0)),
        out_shape=jax.ShapeDtypeStruct((m, dim), F32),
        scratch_shapes=[pltpu.VMEM((d, n_total), BF16), pltpu.VMEM((2, d // n_slabs, n_total), F32),
                        pltpu.SemaphoreType.DMA((2,)), pltpu.VMEM((heads, dk, dk), F32)],
        compiler_params=_params("arbitrary"),
        name="proj_hgrn",
    )(x, g.reshape(1, d), w, lb_logits, norm_w.reshape(1, dim))


def kernel(x, norm_g, ffn1_wg, ffn1_wu, ffn1_wd, ffn2_wg, ffn2_wu, ffn2_wd, ev_w_in, ev_w_out, rw_mu, rw_w0, rw_w2, rw_a0, rw_a2, rw_g2, rw_k_k, rw_k_a, rw_r_k, rw_lnx_w, rw_lnx_b, od_w_in, od_w_out, hg_norm_w, hg_lb_logits, final_g):
    bsz, seq, d = x.shape
    depth = norm_g.shape[0]
    rwkv_dim = RWKV_HEADS * RWKV_HD
    rwkv_cols = 3 * rwkv_dim + LORA_W + LORA_A + LORA_G
    moba_dim = MOBA_HEADS * MOBA_HD
    xf = x.reshape(bsz * seq, d)
    for l in range(depth):
        xf = _ffn(xf, [], None, 0, norm_g[l, 0], ffn1_wg, ffn1_wu, ffn1_wd, l, final_g, final_norm=False)
        if l % 2 == 0:
            e = l // 2
            assert ev_w_in.shape[2] == rwkv_cols + 3 * moba_dim
            y_a, q, k, v = _proj_rwkv(xf, norm_g[l, 1], ev_w_in, e, seq, rw_mu[e], rw_w0[e], rw_w2[e], rw_a0[e],
                                      rw_a2[e], rw_g2[e], rw_k_k[e], rw_k_a[e], rw_r_k[e], rw_lnx_w[e],
                                      rw_lnx_b[e], [moba_dim, moba_dim, moba_dim])
            y_b = _moba(q.reshape(bsz, seq, moba_dim), k.reshape(bsz, seq, moba_dim),
                        v.reshape(bsz, seq, moba_dim))
            ys, wo, wo_layer = [y_a, y_b.reshape(-1, moba_dim)], ev_w_out, e
        else:
            o = l // 2
            y = _proj_hgrn(xf, norm_g[l, 1], od_w_in, o, hg_lb_logits, l, hg_norm_w[o], seq)
            ys, wo, wo_layer = [y], od_w_out, o
        xf = _ffn(xf, ys, wo, wo_layer, norm_g[l, 2], ffn2_wg, ffn2_wu, ffn2_wd, l, final_g,
                  final_norm=(l == depth - 1))
    return xf.reshape(bsz, seq, d)
```

```python
import functools

import jax
import jax.numpy as jnp
from jax import lax
from jax.experimental import pallas as pl
from jax.experimental.pallas import tpu as pltpu

F32 = jnp.float32
BF16 = jnp.bfloat16
NN = (((1,), (0,)), ((), ()))
BNN = (((2,), (1,)), ((0,), (0,)))
BNT = (((2,), (2,)), ((0,), (0,)))
BTN = (((1,), (1,)), ((0,), (0,)))
MIX_PIECES = 2

EPS = 1e-6
LNX_EPS = 64e-5
RWKV_HEADS = 8
RWKV_HD = 64
LORA_W = 64
LORA_A = 64
LORA_G = 128
MOBA_HEADS = 8
MOBA_HD = 64
MOBA_BLOCK = 256
MOBA_TOPK = 3
MOBA_SOFTMAX_LAG = 5
MOBA_VALUE_LAG = 8
HG_HEADS = 8
HG_DK = 128
RWKV_CHUNK = 64
RWKV_CHUNKS_PER_GROUP = 4
HG_CHUNK = 64
HG_CHUNKS_PER_GROUP = 4
SUBLANES = 8
BF16_SUBLANES = 16
V7X_VMEM_BYTES = 64 * 1024 * 1024
VMEM_LIMIT_BYTES = V7X_VMEM_BYTES - 8 * 1024 * 1024


def _params(*semantics):
    return pltpu.CompilerParams(dimension_semantics=semantics, vmem_limit_bytes=VMEM_LIMIT_BYTES)


def _dot(a, b):
    return jnp.dot(a, b, preferred_element_type=F32)


def _pieces(a, n):
    if isinstance(a, tuple):
        return a
    out = []
    for i in range(n):
        hi = a.astype(BF16)
        out.append(hi)
        if i + 1 < n:
            a = a - hi.astype(F32)
    return tuple(out)


def _mm(a, b, dims=NN, n=MIX_PIECES):
    a = _pieces(a, n)
    b = _pieces(b, n)
    order = max(len(a), len(b)) - 1
    out = None
    for i, ai in enumerate(a):
        for j, bj in enumerate(b):
            if i + j <= order:
                t = lax.dot_general(ai, bj, dims, preferred_element_type=F32)
                out = t if out is None else out + t
    return out


def _mm_fused(a, b):
    a_hi, a_lo = _pieces(a, 2)
    b_hi, b_lo = _pieces(b, 2)
    return _dot(jnp.concatenate([a_hi, a_hi, a_lo], axis=1), jnp.concatenate([b_hi, b_lo, b_hi], axis=0))


def _cumsum_matmul(mask, x):
    m = mask.astype(BF16)
    hi, lo = _pieces(x, 2)
    return _dot(jnp.concatenate([m, m], axis=1), jnp.concatenate([hi, lo], axis=0))


def _rmsnorm(x, g):
    return x * lax.rsqrt(jnp.mean(x * x, axis=-1, keepdims=True) + EPS) * g


def _iota2(shape, dim):
    return lax.broadcasted_iota(jnp.int32, shape, dim)


def _block_ones(n, width):
    return (_iota2((n, n), 0) // width == _iota2((n, n), 1) // width).astype(F32)


def _interleave(*gens):
    results = [None] * len(gens)
    live = list(range(len(gens)))
    while live:
        for i in list(live):
            try:
                next(gens[i])
            except StopIteration as stop:
                results[i] = stop.value
                live.remove(i)
    return results


def _load_weight(w_hbm, layer, dst_ref, stage_ref, sem_ref, slab_rows):
    n_rows = dst_ref.shape[0]
    assert n_rows % slab_rows == 0 and slab_rows <= stage_ref.shape[1]
    n_slabs = n_rows // slab_rows

    def copy(s):
        return pltpu.make_async_copy(w_hbm.at[layer, pl.ds(s * slab_rows, slab_rows), :],
                                     stage_ref.at[s % 2, pl.ds(0, slab_rows), :], sem_ref.at[s % 2])

    copy(0).start()
    for s in range(n_slabs):
        if s + 1 < n_slabs:
            copy(s + 1).start()
        copy(s).wait()
        dst_ref[pl.ds(s * slab_rows, slab_rows), :] = stage_ref[s % 2, pl.ds(0, slab_rows), :].astype(BF16)


def _ffn_kernel(x_ref, *refs, n_y, layer, out_layer, final_norm, tf):
    y_refs = refs[:n_y]
    has_out = n_y > 0
    (wo_hbm,) = refs[n_y:n_y + 1] if has_out else (None,)
    g_ref, wg_hbm, wu_hbm, wd_hbm, fg_ref, o_ref = refs[n_y + has_out:n_y + has_out + 6]
    scratch = refs[n_y + has_out + 6:]
    wg_ref, wu_ref, wd_ref, wide_stage, tall_stage, sem, wo_sem = scratch[:7]
    wo_ref = scratch[7] if has_out else None
    n_slabs = wg_ref.shape[1] // tf

    def slab_copies(j):
        slot, cols = j % 2, pl.ds(j * tf, tf)
        return (pltpu.make_async_copy(wg_hbm.at[layer, :, cols], wide_stage.at[slot, 0], sem.at[slot, 0]),
                pltpu.make_async_copy(wu_hbm.at[layer, :, cols], wide_stage.at[slot, 1], sem.at[slot, 1]),
                pltpu.make_async_copy(wd_hbm.at[layer, cols, :], tall_stage.at[slot], sem.at[slot, 2]))

    def body(first_step):
        if first_step:
            for j in range(min(2, n_slabs)):
                for cp in slab_copies(j):
                    cp.start()
            if has_out:
                _load_weight(wo_hbm, out_layer, wo_ref, wo_ref_stage, wo_sem, wo_ref.shape[0] // 4)
        x = x_ref[...]
        row0 = 0
        for y_ref in y_refs:
            rows = y_ref.shape[1]
            x = x + _dot(y_ref[...].astype(BF16), wo_ref[row0:row0 + rows, :])
            row0 += rows
        h = _rmsnorm(x, g_ref[...]).astype(BF16)
        acc = None
        for j in range(n_slabs):
            c0 = j * tf
            if first_step:
                for cp in slab_copies(j):
                    cp.wait()
                wg_ref[:, c0:c0 + tf] = wide_stage[j % 2, 0].astype(BF16)
                wu_ref[:, c0:c0 + tf] = wide_stage[j % 2, 1].astype(BF16)
                wd_ref[c0:c0 + tf, :] = tall_stage[j % 2].astype(BF16)
                if j + 2 < n_slabs:
                    for cp in slab_copies(j + 2):
                        cp.start()
            gate = _dot(h, wg_ref[:, c0:c0 + tf])
            up = _dot(h, wu_ref[:, c0:c0 + tf])
            act = (gate * jax.nn.sigmoid(gate) * up).astype(BF16)
            part = _dot(act, wd_ref[c0:c0 + tf, :])
            acc = part if acc is None else acc + part
        out = x + 0.5 * acc
        if final_norm:
            out = _rmsnorm(out, fg_ref[...])
        o_ref[...] = out

    wo_ref_stage = scratch[8] if has_out else None
    pl.when(pl.program_id(0) == 0)(lambda: body(True))
    pl.when(pl.program_id(0) != 0)(lambda: body(False))


def _resident(shape):
    return pl.BlockSpec(shape, lambda i: (0,) * len(shape), pipeline_mode=pl.Buffered(1))


def _ffn(x, ys, wo, out_layer, g, wg, wu, wd, layer, final_g, *, final_norm, tm=512, tf=256):
    m, d = x.shape
    f = wg.shape[2]
    tm = min(tm, m)
    assert f % tf == 0
    hbm = pl.BlockSpec(memory_space=pl.ANY)
    has_out = len(ys) > 0
    scratch = [pltpu.VMEM((d, f), BF16), pltpu.VMEM((d, f), BF16), pltpu.VMEM((f, d), BF16),
               pltpu.VMEM((2, 2, d, tf), F32), pltpu.VMEM((2, tf, d), F32),
               pltpu.SemaphoreType.DMA((2, 3)), pltpu.SemaphoreType.DMA((2,))]
    if has_out:
        assert sum(y.shape[1] for y in ys) == wo.shape[1] and wo.shape[1] % 4 == 0
        scratch += [pltpu.VMEM(wo.shape[1:], BF16), pltpu.VMEM((2, wo.shape[1] // 4, wo.shape[2]), F32)]
    return pl.pallas_call(
        functools.partial(_ffn_kernel, n_y=len(ys), layer=layer, out_layer=out_layer, final_norm=final_norm,
                          tf=tf),
        grid=(m // tm,),
        in_specs=[pl.BlockSpec((tm, d), lambda i: (i, 0))]
        + [pl.BlockSpec((tm, y.shape[1]), lambda i: (i, 0)) for y in ys]
        + ([hbm] if has_out else [])
        + [_resident((1, d)), hbm, hbm, hbm, _resident((1, d))],
        out_specs=pl.BlockSpec((tm, d), lambda i: (i, 0)),
        out_shape=jax.ShapeDtypeStruct((m, d), F32),
        scratch_shapes=scratch,
        compiler_params=_params("arbitrary"),
        name="ffn",
    )(x, *ys, *([wo] if has_out else []), g.reshape(1, d), wg, wu, wd, final_g.reshape(1, d))


def _bdot(a, b, dims):
    return lax.dot_general(a.astype(BF16), b.astype(BF16), dims, preferred_element_type=F32)


def _block_diag(x, half):
    lo = _iota2((1, 1, 2 * half), 2) < half
    zero = jnp.zeros((), x.dtype)
    return jnp.concatenate([jnp.where(lo, x, zero), jnp.where(lo, zero, x)], axis=1)


def _unit_lower_inverse(a_strict, c):
    row = _iota2((1, c, 2 * c), 1)
    col = _iota2((1, c, 2 * c), 2) & (c - 1)
    eye = (row == col).astype(F32)
    t = None
    m = 1
    while m < c:
        mask = ((row // (2 * m)) == (col // (2 * m))) & ((row & m) != 0) & ((col & m) == 0)
        lm = jnp.where(mask, a_strict, 0.0)
        if t is None:
            t = eye - lm
        else:
            t = t - _bdot(_bdot(t, _block_diag(lm.astype(BF16), c), BNN), _block_diag(t.astype(BF16), c), BNN)
            yield
        m *= 2
    return t


def _head_sum(t, hd):
    tile_ones = _block_ones(2 * hd, hd)
    return jnp.concatenate([_bdot(t[:, i:i + 2 * hd], tile_ones, NN) for i in range(0, t.shape[1], 2 * hd)], axis=1)


def _rwkv_front(xs, w0_ref, w2_ref, a0_ref, a2_ref, g2_ref, kk_ref, ka_ref, *, c, n_chunks, heads, hd):
    rows = n_chunks * c
    dim = heads * hd
    r = xs[:, 0:dim]
    k = xs[:, dim:2 * dim]
    v = xs[:, 2 * dim:3 * dim]
    o1 = 3 * dim
    g_lr = xs[:, o1 + LORA_W + LORA_A:o1 + LORA_W + LORA_A + LORA_G]

    wa = xs[:, o1:o1 + LORA_W + LORA_A]
    wa = jnp.where(_iota2(wa.shape, 1) < LORA_W, jnp.tanh(wa), wa)
    w2a2 = jnp.concatenate(
        [jnp.concatenate([w2_ref[...], jnp.zeros((LORA_W, dim), F32)], axis=1),
         jnp.concatenate([jnp.zeros((LORA_A, dim), F32), a2_ref[...]], axis=1)], axis=0)
    za = _mm_fused(wa, w2a2)
    z = w0_ref[...] + za[:, :dim]
    softplus = jnp.maximum(-z, 0.0) + jnp.log(1.0 + jnp.exp(-jnp.abs(z)))
    w_raw = -softplus - 0.5
    lw = -jnp.exp(w_raw)
    yield
    a = jax.nn.sigmoid(a0_ref[...] + za[:, dim:])
    g = _mm_fused(jax.nn.sigmoid(g_lr), g2_ref[...])
    yield
    kk = k * kk_ref[...]
    kk = kk * lax.rsqrt(jnp.maximum(_head_sum(kk * kk, hd), 1e-24))
    k2 = k * (1.0 + (a - 1.0) * ka_ref[...])
    bb = kk * a
    yield

    r_i, c_i = _iota2((rows, rows), 0), _iota2((rows, rows), 1)
    tri_incl = (r_i >= c_i) & (r_i // c == c_i // c)
    cum = _cumsum_matmul(tri_incl, lw)
    ends = [cum[(i + 1) * c - 1:(i + 1) * c, :] for i in range(n_chunks)]
    cum_last = jnp.concatenate([jnp.broadcast_to(e, (c, dim)) for e in ends], axis=0)
    yield
    w_incl = jnp.exp(cum)
    w_excl = jnp.exp(cum - lw)
    yield
    w_inv = jnp.exp(-cum)
    w_tail = jnp.exp(cum_last - cum)
    w_all = jnp.exp(jnp.concatenate(ends, axis=0))
    yield
    ops = dict(kt=kk * w_excl, rt=r * w_incl, bt=bb * w_inv)
    yield
    ops.update(kd=k2 * w_inv, bw=bb * w_tail, kw=k2 * w_tail)
    yield
    ops.update(v=v, w_all=w_all, r=r, k2=k2, g=g)
    return ops


def _rwkv_back(ops, s, rk_ref, lnw_ref, lnb_ref, *, c, n_chunks, heads, hd):
    pw = 2 * hd
    pairs = heads // 2
    def by_pair(t, n_rows=c):
        return jnp.stack([t[i * n_rows:(i + 1) * n_rows, j * pw:(j + 1) * pw]
                          for i in range(n_chunks) for j in range(pairs)], axis=0)
    bd_f = lambda t: _block_diag(t.astype(BF16), hd)
    kt_p, rt_p, bt_p, kd_p, kw_p, bw_p, v_p = (by_pair(ops[n]) for n in ("kt", "rt", "bt", "kd", "kw", "bw", "v"))
    w_all_p = by_pair(ops["w_all"], 1)
    row_t = _iota2((1, c, 2 * c), 1)
    col_t = _iota2((1, c, 2 * c), 2) & (c - 1)
    strict_p, incl_p = row_t > col_t, row_t >= col_t
    same_head = (_iota2((1, pw, pw), 1) // hd) == (_iota2((1, pw, pw), 2) // hd)

    kr = jnp.concatenate([kt_p, rt_p], axis=1)
    g_all = _bdot(kr, jnp.concatenate([bd_f(bt_p), bd_f(kd_p)], axis=1), BNT)
    yield
    gb, gk = g_all[:, :, :2 * c], g_all[:, :, 2 * c:]
    a_b = jnp.where(strict_p, gb[:, :c], 0.0)
    a_k = jnp.where(strict_p, gk[:, :c], 0.0)
    p_b = jnp.where(incl_p, gb[:, c:], 0.0)
    p_k = jnp.where(incl_p, gk[:, c:], 0.0)
    akpk = _bdot(jnp.concatenate([a_k, p_k], axis=1), bd_f(v_p), BNN)
    yield
    t_inv = yield from _unit_lower_inverse(a_b, c)
    kv1 = _bdot(t_inv, jnp.concatenate([bd_f(kt_p), bd_f(akpk[:, :c])], axis=2), BNN)
    yield
    kt1, v1 = kv1[:, :, :pw], kv1[:, :, pw:]
    pbk = _bdot(p_b, jnp.concatenate([bd_f(kt1), bd_f(v1)], axis=2), BNN)
    yield
    q_mat = rt_p - pbk[:, :, :pw]
    z_mat = akpk[:, c:] - pbk[:, :, pw:]
    x_mat = jnp.where(same_head, _bdot(bw_p, kt1, BTN), 0.0)
    yield
    n_mat = jnp.where(same_head, _bdot(jnp.concatenate([v_p, v1], axis=1),
                                       jnp.concatenate([kw_p, -bw_p], axis=1), BTN), 0.0)
    yield
    y_rows = []
    for i in range(n_chunks):
        sl = slice(i * pairs, (i + 1) * pairs)
        y_p = _bdot(q_mat[sl], s, BNT) + z_mat[sl]
        s = s * w_all_p[sl] - _bdot(s, x_mat[sl], BNT) + n_mat[sl]
        y_rows.append(jnp.concatenate([y_p[j] for j in range(pairs)], axis=1))
        yield
    y = jnp.concatenate(y_rows, axis=0)

    inv_hd = 1.0 / hd
    mean = _head_sum(y, hd) * inv_hd
    yc = y - mean
    var = _head_sum(yc * yc, hd) * inv_hd
    yield
    yn = yc * lax.rsqrt(var + LNX_EPS) * lnw_ref[...] + lnb_ref[...]
    bonus = _head_sum(ops["r"] * ops["k2"] * rk_ref[...], hd) * ops["v"]
    return (yn + bonus) * ops["g"], s


def _proj_rwkv_kernel(x_ref, g_ref, w_hbm, mu_ref, w0_ref, w2_ref, a0_ref, a2_ref, g2_ref, kk_ref, ka_ref, rk_ref,
                      lnw_ref, lnb_ref, y_ref, q_ref, k_ref, v_ref, w_ref, stage, sem, carry_ref, st_ref, *,
                      w_layer, tiles_per_seq, tn, chunk, n_chunks, heads, hd):
    c = chunk
    rows = n_chunks * c
    tm = x_ref.shape[0]
    n_groups = tm // rows
    cols = mu_ref.shape[1]
    step = pl.program_id(0)

    @pl.when(step == 0)
    def _():
        _load_weight(w_hbm, w_layer, w_ref, stage, sem, stage.shape[1])
        carry_ref[...] = jnp.zeros_like(carry_ref)
        st_ref[...] = jnp.zeros_like(st_ref)

    h = _rmsnorm(x_ref[...], g_ref[...]).astype(BF16)
    p = jnp.concatenate([_dot(h, w_ref[:, c0:min(c0 + tn, cols)]) for c0 in range(0, cols, tn)], axis=1)
    first = step % tiles_per_seq == 0
    prev_last = jnp.where(first, 0.0, carry_ref[...])
    prev = jnp.where(_iota2((tm, 1), 0) == 0, prev_last, pltpu.roll(p, 1, axis=0))
    carry_ref[...] = p[tm - 1:tm, :]
    xs = p + (prev - p) * mu_ref[...]

    def attention_groups():
        col0 = cols
        for o_ref in (q_ref, k_ref, v_ref):
            n = o_ref.shape[1]
            for c0 in range(0, n, tn):
                o_ref[:, c0:min(c0 + tn, n)] = _dot(h, w_ref[:, col0 + c0:col0 + min(c0 + tn, n)])
                yield
            col0 += n

    front_refs = (w0_ref, w2_ref, a0_ref, a2_ref, g2_ref, kk_ref, ka_ref)
    dims = dict(c=c, n_chunks=n_chunks, heads=heads, hd=hd)
    front = lambda gi: _rwkv_front(xs[gi * rows:(gi + 1) * rows], *front_refs, **dims)
    ready, _ = _interleave(front(0), attention_groups())
    s = jnp.where(first, 0.0, st_ref[...])
    for gi in range(n_groups):
        back = _rwkv_back(ready, s, rk_ref, lnw_ref, lnb_ref, **dims)
        if gi + 1 < n_groups:
            (out, s), ready = _interleave(back, front(gi + 1))
        else:
            ((out, s),) = _interleave(back)
        y_ref[gi * rows:(gi + 1) * rows, :] = out
    st_ref[...] = s


def _proj_rwkv(x, g, w, w_layer, seq, mu, w0, w2, a0, a2, g2, k_k, k_a, r_k, lnx_w, lnx_b, attn_widths, *, tm=512,
               tn=512, n_slabs=8, chunk=RWKV_CHUNK, n_chunks=RWKV_CHUNKS_PER_GROUP):
    m, d = x.shape
    heads, hd = RWKV_HEADS, RWKV_HD
    dim = heads * hd
    cols = mu.shape[-1]
    n_total = w.shape[2]
    assert n_total == cols + sum(attn_widths) and len(attn_widths) == 3 and d % n_slabs == 0
    assert chunk & (chunk - 1) == 0
    assert seq % tm == 0 and tm % (chunk * n_chunks) == 0
    row = lambda t: t.reshape(1, -1)
    vecs = [row(mu), row(w0), w2, row(a0), a2, g2, row(k_k), row(k_a), row(r_k), row(lnx_w), row(lnx_b)]
    widths = [dim] + list(attn_widths)
    return pl.pallas_call(
        functools.partial(_proj_rwkv_kernel, w_layer=w_layer, tiles_per_seq=seq // tm, tn=tn, chunk=chunk,
                          n_chunks=n_chunks, heads=heads, hd=hd),
        grid=(m // tm,),
        in_specs=[pl.BlockSpec((tm, d), lambda i: (i, 0)), _resident((1, d)), pl.BlockSpec(memory_space=pl.ANY)]
        + [_resident(t.shape) for t in vecs],
        out_specs=[pl.BlockSpec((tm, n), lambda i: (i, 0)) for n in widths],
        out_shape=[jax.ShapeDtypeStruct((m, n), F32) for n in widths],
        scratch_shapes=[pltpu.VMEM((d, n_total), BF16), pltpu.VMEM((2, d // n_slabs, n_total), F32),
                        pltpu.SemaphoreType.DMA((2,)), pltpu.VMEM((1, cols), F32),
                        pltpu.VMEM((heads // 2, 2 * hd, 2 * hd), F32)],
        compiler_params=_params("arbitrary"),
        name="proj_rwkv",
    )(x, g.reshape(1, d), w, *vecs)


def _moba_kernel(q_ref, k_ref, v_ref, o_ref, km_ref, ka_ref, vat_ref, *, nb, blk, n_sel, heads, hd):
    j = pl.program_id(1)
    nbp = km_ref.shape[0]
    scale = hd ** -0.5
    masked = -1e30
    neg_inf = float("-inf")
    slopes = [2.0 ** (-8.0 * (h + 1) / heads) for h in range(heads)]

    @pl.when(j == 0)
    def _():
        km_ref[...] = jnp.zeros_like(km_ref)
        for n in range(nb):
            km_ref[n:n + 1, :] = jnp.mean(k_ref[0, n * blk:(n + 1) * blk, :], axis=0, keepdims=True)
        col = _iota2((blk, 2 * hd), 0).astype(F32)
        lane = _iota2((blk, 2 * hd), 1)
        feat_lane = lane & (hd - 1)
        for n in range(nb):
            rows = slice(n * blk, (n + 1) * blk)
            one_hot = jnp.where(feat_lane == n, 1.0, 0.0)
            for h in range(heads):
                k_feat = (one_hot + jnp.where(feat_lane == nbp, slopes[h] * col, 0.0)
                          + jnp.where(feat_lane == nbp + 1, slopes[h] * blk * n, 0.0))
                k_tile = k_ref[0, rows, (h // 2) * 2 * hd:(h // 2 + 1) * 2 * hd]
                own_lanes = (lane < hd) if h % 2 == 0 else (lane >= hd)
                ka_ref[h, rows, :] = jnp.where(own_lanes, k_tile, k_feat).astype(BF16)
        ones_row = jnp.where(_iota2((vat_ref.shape[2] - hd, blk), 0) == 0, 1.0, 0.0)
        for n in range(nb):
            vt = v_ref[0, n * blk:(n + 1) * blk, :].T
            for h in range(heads):
                vat_ref[n, h] = jnp.concatenate([vt[h * hd:(h + 1) * hd], ones_row], axis=0).astype(BF16)

    qt = (q_ref[0] * scale).T
    sub = _iota2((nbp, blk), 0)
    const_rows = jnp.where(_iota2((hd - nbp, blk), 0) < 2, 1.0, 0.0)
    q_aug = []
    for h in range(heads):
        sl = slice(h * hd, (h + 1) * hd)
        qh = qt[sl]
        gate = _mm(km_ref[:, sl], qh)
        gate = jnp.where(sub < j, gate, neg_inf)
        cnt = jnp.zeros((nbp, blk), jnp.int32)
        for m in range(nb):
            other = gate[m:m + 1, :]
            beats = (other > gate) | ((other == gate) & (m < sub))
            cnt = cnt + beats.astype(jnp.int32)
        keep = ((sub < j) & (cnt < n_sel)) | (sub == j) | (sub >= nb)
        bias = jnp.where(keep, 0.0, masked)
        parts = [qh, bias, const_rows] if h % 2 == 0 else [bias, const_rows, qh]
        q_aug.append(jnp.concatenate(parts, axis=0).astype(BF16))
    q_aug = jnp.stack(q_aug, axis=0)

    half = blk // 2
    units = [(h, qh * half) for h in range(heads) for qh in range(2)]
    q_units = [q_aug[h][:, q0:q0 + half] for h, q0 in units]

    def key_block(n, carry, own):
        start = pl.multiple_of(n * blk, blk)
        scores_of, soft_of, out = {}, {}, []
        for i in range(len(units) + MOBA_VALUE_LAG):
            if i < len(units):
                scores_of[i] = _dot(ka_ref[units[i][0], pl.ds(start, blk), :], q_units[i])
            if 0 <= i - MOBA_SOFTMAX_LAG < len(units):
                u = i - MOBA_SOFTMAX_LAG
                sc, m_old = scores_of.pop(u), carry[u][0]
                if own:
                    sc = jnp.where(_iota2((blk, half), 0) <= _iota2((blk, half), 1) + units[u][1], sc, masked)
                m_new = jnp.maximum(m_old, jnp.max(sc, axis=0, keepdims=True))
                soft_of[u] = (m_new, jnp.exp(m_old - m_new), jnp.exp(sc - m_new).astype(BF16))
            if 0 <= i - MOBA_VALUE_LAG < len(units):
                u = i - MOBA_VALUE_LAG
                m_new, alpha, pr = soft_of.pop(u)
                out.append((m_new, alpha * carry[u][1] + _dot(vat_ref[n, units[u][0]], pr)))
        return tuple(out)

    empty = tuple((jnp.full((1, half), masked, F32), jnp.zeros((vat_ref.shape[2], half), F32)) for _ in units)
    init = key_block(j, empty, own=True)
    final = lax.fori_loop(0, j, lambda n, carry: key_block(n, carry, own=False), init)
    out_t = jnp.concatenate(
        [jnp.concatenate([final[2 * h + qh][1][:hd] / final[2 * h + qh][1][hd:hd + 1] for qh in range(2)], axis=1)
         for h in range(heads)], axis=0)
    o_ref[0] = out_t.T


def _moba(q, k, v):
    b, s, dim = q.shape
    heads, hd, blk = MOBA_HEADS, MOBA_HD, MOBA_BLOCK
    assert s % blk == 0
    nb = s // blk
    nbp = -(-nb // SUBLANES) * SUBLANES
    assert nbp + 2 <= hd
    n_sel = min(MOBA_TOPK, nb - 1)
    return pl.pallas_call(
        functools.partial(_moba_kernel, nb=nb, blk=blk, n_sel=n_sel, heads=heads, hd=hd),
        grid=(b, nb),
        in_specs=[
            pl.BlockSpec((1, blk, dim), lambda i, j: (i, j, 0)),
            pl.BlockSpec((1, s, dim), lambda i, j: (i, 0, 0)),
            pl.BlockSpec((1, s, dim), lambda i, j: (i, 0, 0)),
        ],
        out_specs=pl.BlockSpec((1, blk, dim), lambda i, j: (i, j, 0)),
        out_shape=jax.ShapeDtypeStruct((b, s, dim), F32),
        scratch_shapes=[pltpu.VMEM((nbp, dim), F32),
                        pltpu.VMEM((heads, s, 2 * hd), BF16),
                        pltpu.VMEM((nb, heads, hd + BF16_SUBLANES, blk), BF16)],
        compiler_params=_params("parallel", "arbitrary"),
        name="moba",
    )(q, k, v)


def _hgrn_front(q, fr, v, lb, *, c, n_chunks, heads, dk):
    rows = n_chunks * c
    dim = heads * dk
    sig = jax.nn.sigmoid(fr)
    lf = jnp.log(lb + (1.0 - lb) * sig)
    kf = (1.0 - lb) * (1.0 - sig)
    yield
    r_i, c_i = _iota2((rows, rows), 0), _iota2((rows, rows), 1)
    tri_all = (r_i >= c_i) & (r_i // c == c_i // c)
    b = _cumsum_matmul(tri_all, lf)
    per_chunk = lambda r: jnp.concatenate(
        [jnp.broadcast_to(b[i * c + r:i * c + r + 1, :], (c, dim)) for i in range(n_chunks)], axis=0)
    ends = [b[(i + 1) * c - 1:(i + 1) * c, :] for i in range(n_chunks)]
    b_last = per_chunk(c - 1)
    yield
    b_mid = per_chunk(c // 2 - 1)
    rel = b - b_mid
    qd = q * jnp.exp(rel)
    yield
    kd = kf * jnp.exp(-rel)
    yield
    q_in = qd * jnp.exp(b_mid)
    yield
    kw = kf * jnp.exp(b_last - b)
    w_all = jnp.exp(jnp.concatenate(ends, axis=0))
    return dict(qd=qd, kd=kd, q_in=q_in, kw=kw, w_all=w_all, v=v)


def _hgrn_back(ops, s, g, norm_w, *, c, n_chunks, heads, dk):
    def by_head(t, n_rows=c):
        return jnp.stack([t[i * n_rows:(i + 1) * n_rows, h * dk:(h + 1) * dk]
                          for i in range(n_chunks) for h in range(heads)], axis=0)
    hp, pw = heads // 2, 2 * dk
    by_pair = lambda t: jnp.stack([t[i * c:(i + 1) * c, j * pw:(j + 1) * pw]
                                   for i in range(n_chunks) for j in range(hp)], axis=0)
    tri_pair = _iota2((1, c, 2 * c), 1) >= (_iota2((1, c, 2 * c), 2) & (c - 1))
    q_hi, q_lo = _pieces(by_pair(ops["qd"]), 2)
    k_hi, k_lo = _pieces(_block_diag(by_pair(ops["kd"]), dk), 2)
    sc = lax.dot_general(jnp.concatenate([q_hi, q_hi, q_lo], axis=2), jnp.concatenate([k_hi, k_lo, k_hi], axis=2),
                         BNT, preferred_element_type=F32)
    sc = jnp.where(tri_pair, sc, 0.0)
    yield
    o_pair = _bdot(sc, _block_diag(by_pair(ops["v"]).astype(BF16), dk), BNN)
    yield
    qd_h, v_h = by_head(ops["q_in"]), by_head(ops["v"])
    s_add = _bdot(v_h, by_head(ops["kw"]), BTN)
    yield
    w_all_h = by_head(ops["w_all"], 1)
    o_rows = []
    for i in range(n_chunks):
        sl = slice(i * heads, (i + 1) * heads)
        o_intra = jnp.stack([o_pair[i * hp + h // 2][:, (h % 2) * dk:(h % 2 + 1) * dk]
                             for h in range(heads)], axis=0)
        o_h = o_intra + _bdot(qd_h[sl], s, BNT)
        s = s * w_all_h[sl] + s_add[sl]
        o_h = o_h * lax.rsqrt(jnp.mean(o_h * o_h, axis=-1, keepdims=True) + EPS)
        o_rows.append(jnp.concatenate([o_h[h] for h in range(heads)], axis=1))
        yield
    o = jnp.concatenate(o_rows, axis=0)
    return o * norm_w * jax.nn.sigmoid(g), s


def _proj_hgrn_kernel(x_ref, g_ref, w_hbm, lbl_ref, nw_ref, o_ref, w_ref, stage, sem, st_ref, *, w_layer, layer,
                      tiles_per_seq, tn, chunk, n_chunks, heads, dk):
    c = chunk
    dim = heads * dk
    rows = n_chunks * c
    n_groups = x_ref.shape[0] // rows
    step = pl.program_id(0)

    @pl.when(step == 0)
    def _():
        _load_weight(w_hbm, w_layer, w_ref, stage, sem, stage.shape[1])
        st_ref[...] = jnp.zeros_like(st_ref)

    logits = lbl_ref[...]
    e = jnp.exp(logits - jnp.max(logits, axis=0, keepdims=True))
    sm = e / jnp.sum(e, axis=0, keepdims=True)
    lb = jnp.sum(sm[0:layer + 1, :], axis=0, keepdims=True) - sm[0:1, :]

    h = _rmsnorm(x_ref[...], g_ref[...]).astype(BF16)
    quarter = lambda k: jnp.concatenate([_dot(h, w_ref[:, c0:c0 + tn]) for c0 in range(k * dim, (k + 1) * dim, tn)],
                                        axis=1)
    f_raw, q, v = quarter(1), quarter(0), quarter(2)

    def gate_quarter():
        parts = []
        for c0 in range(3 * dim, 4 * dim, tn):
            parts.append(_dot(h, w_ref[:, c0:c0 + tn]))
            yield
        return jnp.concatenate(parts, axis=1)

    dims = dict(c=c, n_chunks=n_chunks, heads=heads, dk=dk)
    grp = lambda t, gi: t[gi * rows:(gi + 1) * rows, :]
    front = lambda gi: _hgrn_front(grp(q, gi), grp(f_raw, gi), grp(v, gi), lb, **dims)
    ready, g_raw = _interleave(front(0), gate_quarter())
    s = jnp.where(step % tiles_per_seq == 0, 0.0, st_ref[...])
    for gi in range(n_groups):
        back = _hgrn_back(ready, s, grp(g_raw, gi), nw_ref[...], **dims)
        if gi + 1 < n_groups:
            (out, s), ready = _interleave(back, front(gi + 1))
        else:
            ((out, s),) = _interleave(back)
        o_ref[gi * rows:(gi + 1) * rows, :] = out
    st_ref[...] = s


def _proj_hgrn(x, g, w, w_layer, lb_logits, layer, norm_w, seq, *, tm=512, tn=512, n_slabs=8, chunk=HG_CHUNK,
               n_chunks=HG_CHUNKS_PER_GROUP):
    m, d = x.shape
    heads, dk = HG_HEADS, HG_DK
    dim = heads * dk
    n_total = w.shape[2]
    assert n_total == 4 * dim and dim % tn == 0 and d % n_slabs == 0
    assert chunk & (chunk - 1) == 0
    assert seq % tm == 0 and tm % (chunk * n_chunks) == 0
    return pl.pallas_call(
        functools.partial(_proj_hgrn_kernel, w_layer=w_layer, layer=layer, tiles_per_seq=seq // tm, tn=tn,
                          chunk=chunk, n_chunks=n_chunks, heads=heads, dk=dk),
        grid=(m // tm,),
        in_specs=[pl.BlockSpec((tm, d), lambda i: (i, 0)), _resident((1, d)), pl.BlockSpec(memory_space=pl.ANY),
                  _resident(lb_logits.shape), _resident((1, dim))],
        out_specs=pl.BlockSpec((tm, dim), lambda i: (i, 0)),
        out_shape=jax.ShapeDtypeStruct((m, dim), F32),
        scratch_shapes=[pltpu.VMEM((d, n_total), BF16), pltpu.VMEM((2, d // n_slabs, n_total), F32),
                        pltpu.SemaphoreType.DMA((2,)), pltpu.VMEM((heads, dk, dk), F32)],
        compiler_params=_params("arbitrary"),
        name="proj_hgrn",
    )(x, g.reshape(1, d), w, lb_logits, norm_w.reshape(1, dim))


def kernel(x, norm_g, ffn1_wg, ffn1_wu, ffn1_wd, ffn2_wg, ffn2_wu, ffn2_wd, ev_w_in, ev_w_out, rw_mu, rw_w0, rw_w2, rw_a0, rw_a2, rw_g2, rw_k_k, rw_k_a, rw_r_k, rw_lnx_w, rw_lnx_b, od_w_in, od_w_out, hg_norm_w, hg_lb_logits, final_g):
    bsz, seq, d = x.shape
    depth = norm_g.shape[0]
    rwkv_dim = RWKV_HEADS * RWKV_HD
    rwkv_cols = 3 * rwkv_dim + LORA_W + LORA_A + LORA_G
    moba_dim = MOBA_HEADS * MOBA_HD
    xf = x.reshape(bsz * seq, d)
    for l in range(depth):
        xf = _ffn(xf, [], None, 0, norm_g[l, 0], ffn1_wg, ffn1_wu, ffn1_wd, l, final_g, final_norm=False)
        if l % 2 == 0:
            e = l // 2
            assert ev_w_in.shape[2] == rwkv_cols + 3 * moba_dim
            y_a, q, k, v = _proj_rwkv(xf, norm_g[l, 1], ev_w_in, e, seq, rw_mu[e], rw_w0[e], rw_w2[e], rw_a0[e],
                                      rw_a2[e], rw_g2[e], rw_k_k[e], rw_k_a[e], rw_r_k[e], rw_lnx_w[e],
                                      rw_lnx_b[e], [moba_dim, moba_dim, moba_dim])
            y_b = _moba(q.reshape(bsz, seq, moba_dim), k.reshape(bsz, seq, moba_dim),
                        v.reshape(bsz, seq, moba_dim))
            ys, wo, wo_layer = [y_a, y_b.reshape(-1, moba_dim)], ev_w_out, e
        else:
            o = l // 2
            y = _proj_hgrn(xf, norm_g[l, 1], od_w_in, o, hg_lb_logits, l, hg_norm_w[o], seq)
            ys, wo, wo_layer = [y], od_w_out, o
        xf = _ffn(xf, ys, wo, wo_layer, norm_g[l, 2], ffn2_wg, ffn2_wu, ffn2_wd, l, final_g,
                  final_norm=(l == depth - 1))
    return xf.reshape(bsz, seq, d)
```

```python
import functools

import jax
import jax.numpy as jnp
from jax import lax
from jax.experimental import pallas as pl
from jax.experimental.pallas import tpu as pltpu

F32 = jnp.float32
BF16 = jnp.bfloat16
NN = (((1,), (0,)), ((), ()))
BNN = (((2,), (1,)), ((0,), (0,)))
BNT = (((2,), (2,)), ((0,), (0,)))
BTN = (((1,), (1,)), ((0,), (0,)))
MIX_PIECES = 2

EPS = 1e-6
LNX_EPS = 64e-5
RWKV_HEADS = 8
RWKV_HD = 64
LORA_W = 64
LORA_A = 64
LORA_G = 128
MOBA_HEADS = 8
MOBA_HD = 64
MOBA_BLOCK = 256
MOBA_TOPK = 3
MOBA_SOFTMAX_LAG = 5
MOBA_VALUE_LAG = 8
HG_HEADS = 8
HG_DK = 128
RWKV_CHUNK = 64
RWKV_CHUNKS_PER_GROUP = 4
HG_CHUNK = 64
HG_CHUNKS_PER_GROUP = 4
SUBLANES = 8
BF16_SUBLANES = 16
V7X_VMEM_BYTES = 64 * 1024 * 1024
VMEM_LIMIT_BYTES = V7X_VMEM_BYTES - 8 * 1024 * 1024


def _params(*semantics):
    return pltpu.CompilerParams(dimension_semantics=semantics, vmem_limit_bytes=VMEM_LIMIT_BYTES)


def _dot(a, b):
    return jnp.dot(a, b, preferred_element_type=F32)


def _pieces(a, n):
    if isinstance(a, tuple):
        return a
    out = []
    for i in range(n):
        hi = a.astype(BF16)
        out.append(hi)
        if i + 1 < n:
            a = a - hi.astype(F32)
    return tuple(out)


def _mm(a, b, dims=NN, n=MIX_PIECES):
    a = _pieces(a, n)
    b = _pieces(b, n)
    order = max(len(a), len(b)) - 1
    out = None
    for i, ai in enumerate(a):
        for j, bj in enumerate(b):
            if i + j <= order:
                t = lax.dot_general(ai, bj, dims, preferred_element_type=F32)
                out = t if out is None else out + t
    return out


def _mm_fused(a, b):
    a_hi, a_lo = _pieces(a, 2)
    b_hi, b_lo = _pieces(b, 2)
    return _dot(jnp.concatenate([a_hi, a_hi, a_lo], axis=1), jnp.concatenate([b_hi, b_lo, b_hi], axis=0))


def _cumsum_matmul(mask, x):
    m = mask.astype(BF16)
    hi, lo = _pieces(x, 2)
    return _dot(jnp.concatenate([m, m], axis=1), jnp.concatenate([hi, lo], axis=0))


def _rmsnorm(x, g):
    return x * lax.rsqrt(jnp.mean(x * x, axis=-1, keepdims=True) + EPS) * g


def _iota2(shape, dim):
    return lax.broadcasted_iota(jnp.int32, shape, dim)


def _block_ones(n, width):
    return (_iota2((n, n), 0) // width == _iota2((n, n), 1) // width).astype(F32)


def _interleave(*gens):
    results = [None] * len(gens)
    live = list(range(len(gens)))
    while live:
        for i in list(live):
            try:
                next(gens[i])
            except StopIteration as stop:
                results[i] = stop.value
                live.remove(i)
    return results


def _load_weight(w_hbm, layer, dst_ref, stage_ref, sem_ref, slab_rows):
    n_rows = dst_ref.shape[0]
    assert n_rows % slab_rows == 0 and slab_rows <= stage_ref.shape[1]
    n_slabs = n_rows // slab_rows

    def copy(s):
        return pltpu.make_async_copy(w_hbm.at[layer, pl.ds(s * slab_rows, slab_rows), :],
                                     stage_ref.at[s % 2, pl.ds(0, slab_rows), :], sem_ref.at[s % 2])

    copy(0).start()
    for s in range(n_slabs):
        if s + 1 < n_slabs:
            copy(s + 1).start()
        copy(s).wait()
        dst_ref[pl.ds(s * slab_rows, slab_rows), :] = stage_ref[s % 2, pl.ds(0, slab_rows), :].astype(BF16)


def _ffn_kernel(x_ref, *refs, n_y, layer, out_layer, final_norm, tf):
    y_refs = refs[:n_y]
    has_out = n_y > 0
    (wo_hbm,) = refs[n_y:n_y + 1] if has_out else (None,)
    g_ref, wg_hbm, wu_hbm, wd_hbm, fg_ref, o_ref = refs[n_y + has_out:n_y + has_out + 6]
    scratch = refs[n_y + has_out + 6:]
    wg_ref, wu_ref, wd_ref, wide_stage, tall_stage, sem, wo_sem = scratch[:7]
    wo_ref = scratch[7] if has_out else None
    n_slabs = wg_ref.shape[1] // tf

    def slab_copies(j):
        slot, cols = j % 2, pl.ds(j * tf, tf)
        return (pltpu.make_async_copy(wg_hbm.at[layer, :, cols], wide_stage.at[slot, 0], sem.at[slot, 0]),
                pltpu.make_async_copy(wu_hbm.at[layer, :, cols], wide_stage.at[slot, 1], sem.at[slot, 1]),
                pltpu.make_async_copy(wd_hbm.at[layer, cols, :], tall_stage.at[slot], sem.at[slot, 2]))

    def body(first_step):
        if first_step:
            for j in range(min(2, n_slabs)):
                for cp in slab_copies(j):
                    cp.start()
            if has_out:
                _load_weight(wo_hbm, out_layer, wo_ref, wo_ref_stage, wo_sem, wo_ref.shape[0] // 4)
        x = x_ref[...]
        row0 = 0
        for y_ref in y_refs:
            rows = y_ref.shape[1]
            x = x + _dot(y_ref[...].astype(BF16), wo_ref[row0:row0 + rows, :])
            row0 += rows
        h = _rmsnorm(x, g_ref[...]).astype(BF16)
        acc = None
        for j in range(n_slabs):
            c0 = j * tf
            if first_step:
                for cp in slab_copies(j):
                    cp.wait()
                wg_ref[:, c0:c0 + tf] = wide_stage[j % 2, 0].astype(BF16)
                wu_ref[:, c0:c0 + tf] = wide_stage[j % 2, 1].astype(BF16)
                wd_ref[c0:c0 + tf, :] = tall_stage[j % 2].astype(BF16)
                if j + 2 < n_slabs:
                    for cp in slab_copies(j + 2):
                        cp.start()
            gate = _dot(h, wg_ref[:, c0:c0 + tf])
            up = _dot(h, wu_ref[:, c0:c0 + tf])
            act = (gate * jax.nn.sigmoid(gate) * up).astype(BF16)
            part = _dot(act, wd_ref[c0:c0 + tf, :])
            acc = part if acc is None else acc + part
        out = x + 0.5 * acc
        if final_norm:
            out = _rmsnorm(out, fg_ref[...])
        o_ref[...] = out

    wo_ref_stage = scratch[8] if has_out else None
    pl.when(pl.program_id(0) == 0)(lambda: body(True))
    pl.when(pl.program_id(0) != 0)(lambda: body(False))


def _resident(shape):
    return pl.BlockSpec(shape, lambda i: (0,) * len(shape), pipeline_mode=pl.Buffered(1))


def _ffn(x, ys, wo, out_layer, g, wg, wu, wd, layer, final_g, *, final_norm, tm=512, tf=256):
    m, d = x.shape
    f = wg.shape[2]
    tm = min(tm, m)
    assert f % tf == 0
    hbm = pl.BlockSpec(memory_space=pl.ANY)
    has_out = len(ys) > 0
    scratch = [pltpu.VMEM((d, f), BF16), pltpu.VMEM((d, f), BF16), pltpu.VMEM((f, d), BF16),
               pltpu.VMEM((2, 2, d, tf), F32), pltpu.VMEM((2, tf, d), F32),
               pltpu.SemaphoreType.DMA((2, 3)), pltpu.SemaphoreType.DMA((2,))]
    if has_out:
        assert sum(y.shape[1] for y in ys) == wo.shape[1] and wo.shape[1] % 4 == 0
        scratch += [pltpu.VMEM(wo.shape[1:], BF16), pltpu.VMEM((2, wo.shape[1] // 4, wo.shape[2]), F32)]
    return pl.pallas_call(
        functools.partial(_ffn_kernel, n_y=len(ys), layer=layer, out_layer=out_layer, final_norm=final_norm,
                          tf=tf),
        grid=(m // tm,),
        in_specs=[pl.BlockSpec((tm, d), lambda i: (i, 0))]
        + [pl.BlockSpec((tm, y.shape[1]), lambda i: (i, 0)) for y in ys]
        + ([hbm] if has_out else [])
        + [_resident((1, d)), hbm, hbm, hbm, _resident((1, d))],
        out_specs=pl.BlockSpec((tm, d), lambda i: (i, 0)),
        out_shape=jax.ShapeDtypeStruct((m, d), F32),
        scratch_shapes=scratch,
        compiler_params=_params("arbitrary"),
        name="ffn",
    )(x, *ys, *([wo] if has_out else []), g.reshape(1, d), wg, wu, wd, final_g.reshape(1, d))


def _bdot(a, b, dims):
    return lax.dot_general(a.astype(BF16), b.astype(BF16), dims, preferred_element_type=F32)


def _block_diag(x, half):
    lo = _iota2((1, 1, 2 * half), 2) < half
    zero = jnp.zeros((), x.dtype)
    return jnp.concatenate([jnp.where(lo, x, zero), jnp.where(lo, zero, x)], axis=1)


def _unit_lower_inverse(a_strict, c):
    row = _iota2((1, c, 2 * c), 1)
    col = _iota2((1, c, 2 * c), 2) & (c - 1)
    eye = (row == col).astype(F32)
    t = None
    m = 1
    while m < c:
        mask = ((row // (2 * m)) == (col // (2 * m))) & ((row & m) != 0) & ((col & m) == 0)
        lm = jnp.where(mask, a_strict, 0.0)
        if t is None:
            t = eye - lm
        else:
            t = t - _bdot(_bdot(t, _block_diag(lm.astype(BF16), c), BNN), _block_diag(t.astype(BF16), c), BNN)
            yield
        m *= 2
    return t


def _head_sum(t, hd):
    tile_ones = _block_ones(2 * hd, hd)
    return jnp.concatenate([_bdot(t[:, i:i + 2 * hd], tile_ones, NN) for i in range(0, t.shape[1], 2 * hd)], axis=1)


def _rwkv_front(xs, w0_ref, w2_ref, a0_ref, a2_ref, g2_ref, kk_ref, ka_ref, *, c, n_chunks, heads, hd):
    rows = n_chunks * c
    dim = heads * hd
    r = xs[:, 0:dim]
    k = xs[:, dim:2 * dim]
    v = xs[:, 2 * dim:3 * dim]
    o1 = 3 * dim
    g_lr = xs[:, o1 + LORA_W + LORA_A:o1 + LORA_W + LORA_A + LORA_G]

    wa = xs[:, o1:o1 + LORA_W + LORA_A]
    wa = jnp.where(_iota2(wa.shape, 1) < LORA_W, jnp.tanh(wa), wa)
    w2a2 = jnp.concatenate(
        [jnp.concatenate([w2_ref[...], jnp.zeros((LORA_W, dim), F32)], axis=1),
         jnp.concatenate([jnp.zeros((LORA_A, dim), F32), a2_ref[...]], axis=1)], axis=0)
    za = _mm_fused(wa, w2a2)
    z = w0_ref[...] + za[:, :dim]
    softplus = jnp.maximum(-z, 0.0) + jnp.log(1.0 + jnp.exp(-jnp.abs(z)))
    w_raw = -softplus - 0.5
    lw = -jnp.exp(w_raw)
    yield
    a = jax.nn.sigmoid(a0_ref[...] + za[:, dim:])
    g = _mm_fused(jax.nn.sigmoid(g_lr), g2_ref[...])
    yield
    kk = k * kk_ref[...]
    kk = kk * lax.rsqrt(jnp.maximum(_head_sum(kk * kk, hd), 1e-24))
    k2 = k * (1.0 + (a - 1.0) * ka_ref[...])
    bb = kk * a
    yield

    r_i, c_i = _iota2((rows, rows), 0), _iota2((rows, rows), 1)
    tri_incl = (r_i >= c_i) & (r_i // c == c_i // c)
    cum = _cumsum_matmul(tri_incl, lw)
    ends = [cum[(i + 1) * c - 1:(i + 1) * c, :] for i in range(n_chunks)]
    cum_last = jnp.concatenate([jnp.broadcast_to(e, (c, dim)) for e in ends], axis=0)
    yield
    w_incl = jnp.exp(cum)
    w_excl = jnp.exp(cum - lw)
    yield
    w_inv = jnp.exp(-cum)
    w_tail = jnp.exp(cum_last - cum)
    w_all = jnp.exp(jnp.concatenate(ends, axis=0))
    yield
    ops = dict(kt=kk * w_excl, rt=r * w_incl, bt=bb * w_inv)
    yield
    ops.update(kd=k2 * w_inv, bw=bb * w_tail, kw=k2 * w_tail)
    yield
    ops.update(v=v, w_all=w_all, r=r, k2=k2, g=g)
    return ops


def _rwkv_back(ops, s, rk_ref, lnw_ref, lnb_ref, *, c, n_chunks, heads, hd):
    pw = 2 * hd
    pairs = heads // 2
    def by_pair(t, n_rows=c):
        return jnp.stack([t[i * n_rows:(i + 1) * n_rows, j * pw:(j + 1) * pw]
                          for i in range(n_chunks) for j in range(pairs)], axis=0)
    bd_f = lambda t: _block_diag(t.astype(BF16), hd)
    kt_p, rt_p, bt_p, kd_p, kw_p, bw_p, v_p = (by_pair(ops[n]) for n in ("kt", "rt", "bt", "kd", "kw", "bw", "v"))
    w_all_p = by_pair(ops["w_all"], 1)
    row_t = _iota2((1, c, 2 * c), 1)
    col_t = _iota2((1, c, 2 * c), 2) & (c - 1)
    strict_p, incl_p = row_t > col_t, row_t >= col_t
    same_head = (_iota2((1, pw, pw), 1) // hd) == (_iota2((1, pw, pw), 2) // hd)

    kr = jnp.concatenate([kt_p, rt_p], axis=1)
    g_all = _bdot(kr, jnp.concatenate([bd_f(bt_p), bd_f(kd_p)], axis=1), BNT)
    yield
    gb, gk = g_all[:, :, :2 * c], g_all[:, :, 2 * c:]
    a_b = jnp.where(strict_p, gb[:, :c], 0.0)
    a_k = jnp.where(strict_p, gk[:, :c], 0.0)
    p_b = jnp.where(incl_p, gb[:, c:], 0.0)
    p_k = jnp.where(incl_p, gk[:, c:], 0.0)
    akpk = _bdot(jnp.concatenate([a_k, p_k], axis=1), bd_f(v_p), BNN)
    yield
    t_inv = yield from _unit_lower_inverse(a_b, c)
    kv1 = _bdot(t_inv, jnp.concatenate([bd_f(kt_p), bd_f(akpk[:, :c])], axis=2), BNN)
    yield
    kt1, v1 = kv1[:, :, :pw], kv1[:, :, pw:]
    pbk = _bdot(p_b, jnp.concatenate([bd_f(kt1), bd_f(v1)], axis=2), BNN)
    yield
    q_mat = rt_p - pbk[:, :, :pw]
    z_mat = akpk[:, c:] - pbk[:, :, pw:]
    x_mat = jnp.where(same_head, _bdot(bw_p, kt1, BTN), 0.0)
    yield
    n_mat = jnp.where(same_head, _bdot(jnp.concatenate([v_p, v1], axis=1),
                                       jnp.concatenate([kw_p, -bw_p], axis=1), BTN), 0.0)
    yield
    y_rows = []
    for i in range(n_chunks):
        sl = slice(i * pairs, (i + 1) * pairs)
        y_p = _bdot(q_mat[sl], s, BNT) + z_mat[sl]
        s = s * w_all_p[sl] - _bdot(s, x_mat[sl], BNT) + n_mat[sl]
        y_rows.append(jnp.concatenate([y_p[j] for j in range(pairs)], axis=1))
        yield
    y = jnp.concatenate(y_rows, axis=0)

    inv_hd = 1.0 / hd
    mean = _head_sum(y, hd) * inv_hd
    yc = y - mean
    var = _head_sum(yc * yc, hd) * inv_hd
    yield
    yn = yc * lax.rsqrt(var + LNX_EPS) * lnw_ref[...] + lnb_ref[...]
    bonus = _head_sum(ops["r"] * ops["k2"] * rk_ref[...], hd) * ops["v"]
    return (yn + bonus) * ops["g"], s


def _proj_rwkv_kernel(x_ref, g_ref, w_hbm, mu_ref, w0_ref, w2_ref, a0_ref, a2_ref, g2_ref, kk_ref, ka_ref, rk_ref,
                      lnw_ref, lnb_ref, y_ref, q_ref, k_ref, v_ref, w_ref, stage, sem, carry_ref, st_ref, *,
                      w_layer, tiles_per_seq, tn, chunk, n_chunks, heads, hd):
    c = chunk
    rows = n_chunks * c
    tm = x_ref.shape[0]
    n_groups = tm // rows
    cols = mu_ref.shape[1]
    step = pl.program_id(0)

    @pl.when(step == 0)
    def _():
        _load_weight(w_hbm, w_layer, w_ref, stage, sem, stage.shape[1])
        carry_ref[...] = jnp.zeros_like(carry_ref)
        st_ref[...] = jnp.zeros_like(st_ref)

    h = _rmsnorm(x_ref[...], g_ref[...]).astype(BF16)
    p = jnp.concatenate([_dot(h, w_ref[:, c0:min(c0 + tn, cols)]) for c0 in range(0, cols, tn)], axis=1)
    first = step % tiles_per_seq == 0
    prev_last = jnp.where(first, 0.0, carry_ref[...])
    prev = jnp.where(_iota2((tm, 1), 0) == 0, prev_last, pltpu.roll(p, 1, axis=0))
    carry_ref[...] = p[tm - 1:tm, :]
    xs = p + (prev - p) * mu_ref[...]

    def attention_groups():
        col0 = cols
        for o_ref in (q_ref, k_ref, v_ref):
            n = o_ref.shape[1]
            for c0 in range(0, n, tn):
                o_ref[:, c0:min(c0 + tn, n)] = _dot(h, w_ref[:, col0 + c0:col0 + min(c0 + tn, n)])
                yield
            col0 += n

    front_refs = (w0_ref, w2_ref, a0_ref, a2_ref, g2_ref, kk_ref, ka_ref)
    dims = dict(c=c, n_chunks=n_chunks, heads=heads, hd=hd)
    front = lambda gi: _rwkv_front(xs[gi * rows:(gi + 1) * rows], *front_refs, **dims)
    ready, _ = _interleave(front(0), attention_groups())
    s = jnp.where(first, 0.0, st_ref[...])
    for gi in range(n_groups):
        back = _rwkv_back(ready, s, rk_ref, lnw_ref, lnb_ref, **dims)
        if gi + 1 < n_groups:
            (out, s), ready = _interleave(back, front(gi + 1))
        else:
            ((out, s),) = _interleave(back)
        y_ref[gi * rows:(gi + 1) * rows, :] = out.astype(y_ref.dtype)
    st_ref[...] = s


def _proj_rwkv(x, g, w, w_layer, seq, mu, w0, w2, a0, a2, g2, k_k, k_a, r_k, lnx_w, lnx_b, attn_widths, *, tm=512,
               tn=512, n_slabs=8, chunk=RWKV_CHUNK, n_chunks=RWKV_CHUNKS_PER_GROUP):
    m, d = x.shape
    heads, hd = RWKV_HEADS, RWKV_HD
    dim = heads * hd
    cols = mu.shape[-1]
    n_total = w.shape[2]
    assert n_total == cols + sum(attn_widths) and len(attn_widths) == 3 and d % n_slabs == 0
    assert chunk & (chunk - 1) == 0
    assert seq % tm == 0 and tm % (chunk * n_chunks) == 0
    row = lambda t: t.reshape(1, -1)
    vecs = [row(mu), row(w0), w2, row(a0), a2, g2, row(k_k), row(k_a), row(r_k), row(lnx_w), row(lnx_b)]
    widths = [dim] + list(attn_widths)
    return pl.pallas_call(
        functools.partial(_proj_rwkv_kernel, w_layer=w_layer, tiles_per_seq=seq // tm, tn=tn, chunk=chunk,
                          n_chunks=n_chunks, heads=heads, hd=hd),
        grid=(m // tm,),
        in_specs=[pl.BlockSpec((tm, d), lambda i: (i, 0)), _resident((1, d)), pl.BlockSpec(memory_space=pl.ANY)]
        + [_resident(t.shape) for t in vecs],
        out_specs=[pl.BlockSpec((tm, n), lambda i: (i, 0)) for n in widths],
        out_shape=[jax.ShapeDtypeStruct((m, n), BF16 if i == 0 else F32) for i, n in enumerate(widths)],
        scratch_shapes=[pltpu.VMEM((d, n_total), BF16), pltpu.VMEM((2, d // n_slabs, n_total), F32),
                        pltpu.SemaphoreType.DMA((2,)), pltpu.VMEM((1, cols), F32),
                        pltpu.VMEM((heads // 2, 2 * hd, 2 * hd), F32)],
        compiler_params=_params("arbitrary"),
        name="proj_rwkv",
    )(x, g.reshape(1, d), w, *vecs)


def _moba_kernel(q_ref, k_ref, v_ref, o_ref, km_ref, ka_ref, vat_ref, *, nb, blk, n_sel, heads, hd):
    j = pl.program_id(1)
    nbp = km_ref.shape[0]
    scale = hd ** -0.5
    masked = -1e30
    neg_inf = float("-inf")
    slopes = [2.0 ** (-8.0 * (h + 1) / heads) for h in range(heads)]

    @pl.when(j == 0)
    def _():
        km_ref[...] = jnp.zeros_like(km_ref)
        for n in range(nb):
            km_ref[n:n + 1, :] = jnp.mean(k_ref[0, n * blk:(n + 1) * blk, :], axis=0, keepdims=True)
        col = _iota2((blk, 2 * hd), 0).astype(F32)
        lane = _iota2((blk, 2 * hd), 1)
        feat_lane = lane & (hd - 1)
        for n in range(nb):
            rows = slice(n * blk, (n + 1) * blk)
            one_hot = jnp.where(feat_lane == n, 1.0, 0.0)
            for h in range(heads):
                k_feat = (one_hot + jnp.where(feat_lane == nbp, slopes[h] * col, 0.0)
                          + jnp.where(feat_lane == nbp + 1, slopes[h] * blk * n, 0.0))
                k_tile = k_ref[0, rows, (h // 2) * 2 * hd:(h // 2 + 1) * 2 * hd]
                own_lanes = (lane < hd) if h % 2 == 0 else (lane >= hd)
                ka_ref[h, rows, :] = jnp.where(own_lanes, k_tile, k_feat).astype(BF16)
        ones_row = jnp.where(_iota2((vat_ref.shape[2] - hd, blk), 0) == 0, 1.0, 0.0)
        for n in range(nb):
            vt = v_ref[0, n * blk:(n + 1) * blk, :].T
            for h in range(heads):
                vat_ref[n, h] = jnp.concatenate([vt[h * hd:(h + 1) * hd], ones_row], axis=0).astype(BF16)

    qt = (q_ref[0] * scale).T
    sub = _iota2((nbp, blk), 0)
    const_rows = jnp.where(_iota2((hd - nbp, blk), 0) < 2, 1.0, 0.0)
    q_aug = []
    for h in range(heads):
        sl = slice(h * hd, (h + 1) * hd)
        qh = qt[sl]
        gate = _mm(km_ref[:, sl], qh)
        gate = jnp.where(sub < j, gate, neg_inf)
        cnt = jnp.zeros((nbp, blk), jnp.int32)
        for m in range(nb):
            other = gate[m:m + 1, :]
            beats = (other > gate) | ((other == gate) & (m < sub))
            cnt = cnt + beats.astype(jnp.int32)
        keep = ((sub < j) & (cnt < n_sel)) | (sub == j) | (sub >= nb)
        bias = jnp.where(keep, 0.0, masked)
        parts = [qh, bias, const_rows] if h % 2 == 0 else [bias, const_rows, qh]
        q_aug.append(jnp.concatenate(parts, axis=0).astype(BF16))
    q_aug = jnp.stack(q_aug, axis=0)

    half = blk // 2
    units = [(h, qh * half) for h in range(heads) for qh in range(2)]
    q_units = [q_aug[h][:, q0:q0 + half] for h, q0 in units]

    def key_block(n, carry, own):
        start = pl.multiple_of(n * blk, blk)
        scores_of, soft_of, out = {}, {}, []
        for i in range(len(units) + MOBA_VALUE_LAG):
            if i < len(units):
                scores_of[i] = _dot(ka_ref[units[i][0], pl.ds(start, blk), :], q_units[i])
            if 0 <= i - MOBA_SOFTMAX_LAG < len(units):
                u = i - MOBA_SOFTMAX_LAG
                sc, m_old = scores_of.pop(u), carry[u][0]
                if own:
                    sc = jnp.where(_iota2((blk, half), 0) <= _iota2((blk, half), 1) + units[u][1], sc, masked)
                m_new = jnp.maximum(m_old, jnp.max(sc, axis=0, keepdims=True))
                soft_of[u] = (m_new, jnp.exp(m_old - m_new), jnp.exp(sc - m_new).astype(BF16))
            if 0 <= i - MOBA_VALUE_LAG < len(units):
                u = i - MOBA_VALUE_LAG
                m_new, alpha, pr = soft_of.pop(u)
                out.append((m_new, alpha * carry[u][1] + _dot(vat_ref[n, units[u][0]], pr)))
        return tuple(out)

    empty = tuple((jnp.full((1, half), masked, F32), jnp.zeros((vat_ref.shape[2], half), F32)) for _ in units)
    init = key_block(j, empty, own=True)
    final = lax.fori_loop(0, j, lambda n, carry: key_block(n, carry, own=False), init)
    out_t = jnp.concatenate(
        [jnp.concatenate([final[2 * h + qh][1][:hd] / final[2 * h + qh][1][hd:hd + 1] for qh in range(2)], axis=1)
         for h in range(heads)], axis=0)
    o_ref[0] = out_t.T.astype(o_ref.dtype)


def _moba(q, k, v):
    b, s, dim = q.shape
    heads, hd, blk = MOBA_HEADS, MOBA_HD, MOBA_BLOCK
    assert s % blk == 0
    nb = s // blk
    nbp = -(-nb // SUBLANES) * SUBLANES
    assert nbp + 2 <= hd
    n_sel = min(MOBA_TOPK, nb - 1)
    return pl.pallas_call(
        functools.partial(_moba_kernel, nb=nb, blk=blk, n_sel=n_sel, heads=heads, hd=hd),
        grid=(b, nb),
        in_specs=[
            pl.BlockSpec((1, blk, dim), lambda i, j: (i, j, 0)),
            pl.BlockSpec((1, s, dim), lambda i, j: (i, 0, 0)),
            pl.BlockSpec((1, s, dim), lambda i, j: (i, 0, 0)),
        ],
        out_specs=pl.BlockSpec((1, blk, dim), lambda i, j: (i, j, 0)),
        out_shape=jax.ShapeDtypeStruct((b, s, dim), BF16),
        scratch_shapes=[pltpu.VMEM((nbp, dim), F32),
                        pltpu.VMEM((heads, s, 2 * hd), BF16),
                        pltpu.VMEM((nb, heads, hd + BF16_SUBLANES, blk), BF16)],
        compiler_params=_params("parallel", "arbitrary"),
        name="moba",
    )(q, k, v)


def _hgrn_front(q, fr, v, lb, *, c, n_chunks, heads, dk):
    rows = n_chunks * c
    dim = heads * dk
    sig = jax.nn.sigmoid(fr)
    lf = jnp.log(lb + (1.0 - lb) * sig)
    kf = (1.0 - lb) * (1.0 - sig)
    yield
    r_i, c_i = _iota2((rows, rows), 0), _iota2((rows, rows), 1)
    tri_all = (r_i >= c_i) & (r_i // c == c_i // c)
    b = _cumsum_matmul(tri_all, lf)
    per_chunk = lambda r: jnp.concatenate(
        [jnp.broadcast_to(b[i * c + r:i * c + r + 1, :], (c, dim)) for i in range(n_chunks)], axis=0)
    ends = [b[(i + 1) * c - 1:(i + 1) * c, :] for i in range(n_chunks)]
    b_last = per_chunk(c - 1)
    yield
    b_mid = per_chunk(c // 2 - 1)
    rel = b - b_mid
    qd = q * jnp.exp(rel)
    yield
    kd = kf * jnp.exp(-rel)
    yield
    q_in = qd * jnp.exp(b_mid)
    yield
    kw = kf * jnp.exp(b_last - b)
    w_all = jnp.exp(jnp.concatenate(ends, axis=0))
    return dict(qd=qd, kd=kd, q_in=q_in, kw=kw, w_all=w_all, v=v)


def _hgrn_back(ops, s, g, norm_w, *, c, n_chunks, heads, dk):
    def by_head(t, n_rows=c):
        return jnp.stack([t[i * n_rows:(i + 1) * n_rows, h * dk:(h + 1) * dk]
                          for i in range(n_chunks) for h in range(heads)], axis=0)
    hp, pw = heads // 2, 2 * dk
    by_pair = lambda t: jnp.stack([t[i * c:(i + 1) * c, j * pw:(j + 1) * pw]
                                   for i in range(n_chunks) for j in range(hp)], axis=0)
    tri_pair = _iota2((1, c, 2 * c), 1) >= (_iota2((1, c, 2 * c), 2) & (c - 1))
    q_hi, q_lo = _pieces(by_pair(ops["qd"]), 2)
    k_hi, k_lo = _pieces(_block_diag(by_pair(ops["kd"]), dk), 2)
    sc = lax.dot_general(jnp.concatenate([q_hi, q_hi, q_lo], axis=2), jnp.concatenate([k_hi, k_lo, k_hi], axis=2),
                         BNT, preferred_element_type=F32)
    sc = jnp.where(tri_pair, sc, 0.0)
    yield
    o_pair = _bdot(sc, _block_diag(by_pair(ops["v"]).astype(BF16), dk), BNN)
    yield
    qd_h, v_h = by_head(ops["q_in"]), by_head(ops["v"])
    s_add = _bdot(v_h, by_head(ops["kw"]), BTN)
    yield
    w_all_h = by_head(ops["w_all"], 1)
    o_rows = []
    for i in range(n_chunks):
        sl = slice(i * heads, (i + 1) * heads)
        o_intra = jnp.stack([o_pair[i * hp + h // 2][:, (h % 2) * dk:(h % 2 + 1) * dk]
                             for h in range(heads)], axis=0)
        o_h = o_intra + _bdot(qd_h[sl], s, BNT)
        s = s * w_all_h[sl] + s_add[sl]
        o_h = o_h * lax.rsqrt(jnp.mean(o_h * o_h, axis=-1, keepdims=True) + EPS)
        o_rows.append(jnp.concatenate([o_h[h] for h in range(heads)], axis=1))
        yield
    o = jnp.concatenate(o_rows, axis=0)
    return o * norm_w * jax.nn.sigmoid(g), s


def _proj_hgrn_kernel(x_ref, g_ref, w_hbm, lbl_ref, nw_ref, o_ref, w_ref, stage, sem, st_ref, *, w_layer, layer,
                      tiles_per_seq, tn, chunk, n_chunks, heads, dk):
    c = chunk
    dim = heads * dk
    rows = n_chunks * c
    n_groups = x_ref.shape[0] // rows
    step = pl.program_id(0)

    @pl.when(step == 0)
    def _():
        _load_weight(w_hbm, w_layer, w_ref, stage, sem, stage.shape[1])
        st_ref[...] = jnp.zeros_like(st_ref)

    logits = lbl_ref[...]
    e = jnp.exp(logits - jnp.max(logits, axis=0, keepdims=True))
    sm = e / jnp.sum(e, axis=0, keepdims=True)
    lb = jnp.sum(sm[0:layer + 1, :], axis=0, keepdims=True) - sm[0:1, :]

    h = _rmsnorm(x_ref[...], g_ref[...]).astype(BF16)
    quarter = lambda k: jnp.concatenate([_dot(h, w_ref[:, c0:c0 + tn]) for c0 in range(k * dim, (k + 1) * dim, tn)],
                                        axis=1)
    f_raw, q, v = quarter(1), quarter(0), quarter(2)

    def gate_quarter():
        parts = []
        for c0 in range(3 * dim, 4 * dim, tn):
            parts.append(_dot(h, w_ref[:, c0:c0 + tn]))
            yield
        return jnp.concatenate(parts, axis=1)

    dims = dict(c=c, n_chunks=n_chunks, heads=heads, dk=dk)
    grp = lambda t, gi: t[gi * rows:(gi + 1) * rows, :]
    front = lambda gi: _hgrn_front(grp(q, gi), grp(f_raw, gi), grp(v, gi), lb, **dims)
    ready, g_raw = _interleave(front(0), gate_quarter())
    s = jnp.where(step % tiles_per_seq == 0, 0.0, st_ref[...])
    for gi in range(n_groups):
        back = _hgrn_back(ready, s, grp(g_raw, gi), nw_ref[...], **dims)
        if gi + 1 < n_groups:
            (out, s), ready = _interleave(back, front(gi + 1))
        else:
            ((out, s),) = _interleave(back)
        o_ref[gi * rows:(gi + 1) * rows, :] = out.astype(o_ref.dtype)
    st_ref[...] = s


def _proj_hgrn(x, g, w, w_layer, lb_logits, layer, norm_w, seq, *, tm=512, tn=512, n_slabs=8, chunk=HG_CHUNK,
               n_chunks=HG_CHUNKS_PER_GROUP):
    m, d = x.shape
    heads, dk = HG_HEADS, HG_DK
    dim = heads * dk
    n_total = w.shape[2]
    assert n_total == 4 * dim and dim % tn == 0 and d % n_slabs == 0
    assert chunk & (chunk - 1) == 0
    assert seq % tm == 0 and tm % (chunk * n_chunks) == 0
    return pl.pallas_call(
        functools.partial(_proj_hgrn_kernel, w_layer=w_layer, layer=layer, tiles_per_seq=seq // tm, tn=tn,
                          chunk=chunk, n_chunks=n_chunks, heads=heads, dk=dk),
        grid=(m // tm,),
        in_specs=[pl.BlockSpec((tm, d), lambda i: (i, 0)), _resident((1, d)), pl.BlockSpec(memory_space=pl.ANY),
                  _resident(lb_logits.shape), _resident((1, dim))],
        out_specs=pl.BlockSpec((tm, dim), lambda i: (i, 0)),
        out_shape=jax.ShapeDtypeStruct((m, dim), BF16),
        scratch_shapes=[pltpu.VMEM((d, n_total), BF16), pltpu.VMEM((2, d // n_slabs, n_total), F32),
                        pltpu.SemaphoreType.DMA((2,)), pltpu.VMEM((heads, dk, dk), F32)],
        compiler_params=_params("arbitrary"),
        name="proj_hgrn",
    )(x, g.reshape(1, d), w, lb_logits, norm_w.reshape(1, dim))


def kernel(x, norm_g, ffn1_wg, ffn1_wu, ffn1_wd, ffn2_wg, ffn2_wu, ffn2_wd, ev_w_in, ev_w_out, rw_mu, rw_w0, rw_w2, rw_a0, rw_a2, rw_g2, rw_k_k, rw_k_a, rw_r_k, rw_lnx_w, rw_lnx_b, od_w_in, od_w_out, hg_norm_w, hg_lb_logits, final_g):
    bsz, seq, d = x.shape
    depth = norm_g.shape[0]
    rwkv_dim = RWKV_HEADS * RWKV_HD
    rwkv_cols = 3 * rwkv_dim + LORA_W + LORA_A + LORA_G
    moba_dim = MOBA_HEADS * MOBA_HD
    xf = x.reshape(bsz * seq, d)
    for l in range(depth):
        xf = _ffn(xf, [], None, 0, norm_g[l, 0], ffn1_wg, ffn1_wu, ffn1_wd, l, final_g, final_norm=False)
        if l % 2 == 0:
            e = l // 2
            assert ev_w_in.shape[2] == rwkv_cols + 3 * moba_dim
            y_a, q, k, v = _proj_rwkv(xf, norm_g[l, 1], ev_w_in, e, seq, rw_mu[e], rw_w0[e], rw_w2[e], rw_a0[e],
                                      rw_a2[e], rw_g2[e], rw_k_k[e], rw_k_a[e], rw_r_k[e], rw_lnx_w[e],
                                      rw_lnx_b[e], [moba_dim, moba_dim, moba_dim])
            y_b = _moba(q.reshape(bsz, seq, moba_dim), k.reshape(bsz, seq, moba_dim),
                        v.reshape(bsz, seq, moba_dim))
            ys, wo, wo_layer = [y_a, y_b.reshape(-1, moba_dim)], ev_w_out, e
        else:
            o = l // 2
            y = _proj_hgrn(xf, norm_g[l, 1], od_w_in, o, hg_lb_logits, l, hg_norm_w[o], seq)
            ys, wo, wo_layer = [y], od_w_out, o
        xf = _ffn(xf, ys, wo, wo_layer, norm_g[l, 2], ffn2_wg, ffn2_wu, ffn2_wd, l, final_g,
                  final_norm=(l == depth - 1))
    return xf.reshape(bsz, seq, d)
```

```python
import functools

import jax
import jax.numpy as jnp
from jax import lax
from jax.experimental import pallas as pl
from jax.experimental.pallas import tpu as pltpu

F32 = jnp.float32
BF16 = jnp.bfloat16
NN = (((1,), (0,)), ((), ()))
BNN = (((2,), (1,)), ((0,), (0,)))
BNT = (((2,), (2,)), ((0,), (0,)))
BTN = (((1,), (1,)), ((0,), (0,)))
MIX_PIECES = 2

EPS = 1e-6
LNX_EPS = 64e-5
RWKV_HEADS = 8
RWKV_HD = 64
LORA_W = 64
LORA_A = 64
LORA_G = 128
MOBA_HEADS = 8
MOBA_HD = 64
MOBA_BLOCK = 256
MOBA_TOPK = 3
MOBA_SOFTMAX_LAG = 5
MOBA_VALUE_LAG = 8
HG_HEADS = 8
HG_DK = 128
RWKV_CHUNK = 64
RWKV_CHUNKS_PER_GROUP = 4
HG_CHUNK = 64
HG_CHUNKS_PER_GROUP = 2
SUBLANES = 8
BF16_SUBLANES = 16
V7X_VMEM_BYTES = 64 * 1024 * 1024
VMEM_LIMIT_BYTES = V7X_VMEM_BYTES - 8 * 1024 * 1024


def _params(*semantics):
    return pltpu.CompilerParams(dimension_semantics=semantics, vmem_limit_bytes=VMEM_LIMIT_BYTES)


def _dot(a, b):
    return jnp.dot(a, b, preferred_element_type=F32)


def _pieces(a, n):
    if isinstance(a, tuple):
        return a
    out = []
    for i in range(n):
        hi = a.astype(BF16)
        out.append(hi)
        if i + 1 < n:
            a = a - hi.astype(F32)
    return tuple(out)


def _mm(a, b, dims=NN, n=MIX_PIECES):
    a = _pieces(a, n)
    b = _pieces(b, n)
    order = max(len(a), len(b)) - 1
    out = None
    for i, ai in enumerate(a):
        for j, bj in enumerate(b):
            if i + j <= order:
                t = lax.dot_general(ai, bj, dims, preferred_element_type=F32)
                out = t if out is None else out + t
    return out


def _mm_fused(a, b):
    a_hi, a_lo = _pieces(a, 2)
    b_hi, b_lo = _pieces(b, 2)
    return _dot(jnp.concatenate([a_hi, a_hi, a_lo], axis=1), jnp.concatenate([b_hi, b_lo, b_hi], axis=0))


def _cumsum_matmul(mask, x):
    m = mask.astype(BF16)
    hi, lo = _pieces(x, 2)
    return _dot(jnp.concatenate([m, m], axis=1), jnp.concatenate([hi, lo], axis=0))


def _rmsnorm(x, g):
    return x * lax.rsqrt(jnp.mean(x * x, axis=-1, keepdims=True) + EPS) * g


def _iota2(shape, dim):
    return lax.broadcasted_iota(jnp.int32, shape, dim)


def _block_ones(n, width):
    return (_iota2((n, n), 0) // width == _iota2((n, n), 1) // width).astype(F32)


def _interleave(*gens):
    results = [None] * len(gens)
    live = list(range(len(gens)))
    while live:
        for i in list(live):
            try:
                next(gens[i])
            except StopIteration as stop:
                results[i] = stop.value
                live.remove(i)
    return results


def _load_weight(w_hbm, layer, dst_ref, stage_ref, sem_ref, slab_rows):
    n_rows = dst_ref.shape[0]
    assert n_rows % slab_rows == 0 and slab_rows <= stage_ref.shape[1]
    n_slabs = n_rows // slab_rows

    def copy(s):
        return pltpu.make_async_copy(w_hbm.at[layer, pl.ds(s * slab_rows, slab_rows), :],
                                     stage_ref.at[s % 2, pl.ds(0, slab_rows), :], sem_ref.at[s % 2])

    copy(0).start()
    for s in range(n_slabs):
        if s + 1 < n_slabs:
            copy(s + 1).start()
        copy(s).wait()
        dst_ref[pl.ds(s * slab_rows, slab_rows), :] = stage_ref[s % 2, pl.ds(0, slab_rows), :].astype(BF16)


def _ffn_kernel(x_ref, *refs, n_y, layer, out_layer, final_norm, tf):
    y_refs = refs[:n_y]
    has_out = n_y > 0
    (wo_hbm,) = refs[n_y:n_y + 1] if has_out else (None,)
    g_ref, wg_hbm, wu_hbm, wd_hbm, fg_ref, o_ref = refs[n_y + has_out:n_y + has_out + 6]
    scratch = refs[n_y + has_out + 6:]
    wg_ref, wu_ref, wd_ref, wide_stage, tall_stage, sem, wo_sem = scratch[:7]
    wo_ref = scratch[7] if has_out else None
    n_slabs = wg_ref.shape[1] // tf

    def slab_copies(j):
        slot, cols = j % 2, pl.ds(j * tf, tf)
        return (pltpu.make_async_copy(wg_hbm.at[layer, :, cols], wide_stage.at[slot, 0], sem.at[slot, 0]),
                pltpu.make_async_copy(wu_hbm.at[layer, :, cols], wide_stage.at[slot, 1], sem.at[slot, 1]),
                pltpu.make_async_copy(wd_hbm.at[layer, cols, :], tall_stage.at[slot], sem.at[slot, 2]))

    def body(first_step):
        if first_step:
            for j in range(min(2, n_slabs)):
                for cp in slab_copies(j):
                    cp.start()
            if has_out:
                _load_weight(wo_hbm, out_layer, wo_ref, wo_ref_stage, wo_sem, wo_ref.shape[0] // 4)
        x = x_ref[...]
        row0 = 0
        for y_ref in y_refs:
            rows = y_ref.shape[1]
            x = x + _dot(y_ref[...].astype(BF16), wo_ref[row0:row0 + rows, :])
            row0 += rows
        h = _rmsnorm(x, g_ref[...]).astype(BF16)
        acc = None
        for j in range(n_slabs):
            c0 = j * tf
            if first_step:
                for cp in slab_copies(j):
                    cp.wait()
                wg_ref[:, c0:c0 + tf] = wide_stage[j % 2, 0].astype(BF16)
                wu_ref[:, c0:c0 + tf] = wide_stage[j % 2, 1].astype(BF16)
                wd_ref[c0:c0 + tf, :] = tall_stage[j % 2].astype(BF16)
                if j + 2 < n_slabs:
                    for cp in slab_copies(j + 2):
                        cp.start()
            gate = _dot(h, wg_ref[:, c0:c0 + tf])
            up = _dot(h, wu_ref[:, c0:c0 + tf])
            act = (gate * jax.nn.sigmoid(gate) * up).astype(BF16)
            part = _dot(act, wd_ref[c0:c0 + tf, :])
            acc = part if acc is None else acc + part
        out = x + 0.5 * acc
        if final_norm:
            out = _rmsnorm(out, fg_ref[...])
        o_ref[...] = out

    wo_ref_stage = scratch[8] if has_out else None
    pl.when(pl.program_id(0) == 0)(lambda: body(True))
    pl.when(pl.program_id(0) != 0)(lambda: body(False))


def _resident(shape):
    return pl.BlockSpec(shape, lambda i: (0,) * len(shape), pipeline_mode=pl.Buffered(1))


def _ffn(x, ys, wo, out_layer, g, wg, wu, wd, layer, final_g, *, final_norm, tm=512, tf=256):
    m, d = x.shape
    f = wg.shape[2]
    tm = min(tm, m)
    assert f % tf == 0
    hbm = pl.BlockSpec(memory_space=pl.ANY)
    has_out = len(ys) > 0
    scratch = [pltpu.VMEM((d, f), BF16), pltpu.VMEM((d, f), BF16), pltpu.VMEM((f, d), BF16),
               pltpu.VMEM((2, 2, d, tf), F32), pltpu.VMEM((2, tf, d), F32),
               pltpu.SemaphoreType.DMA((2, 3)), pltpu.SemaphoreType.DMA((2,))]
    if has_out:
        assert sum(y.shape[1] for y in ys) == wo.shape[1] and wo.shape[1] % 4 == 0
        scratch += [pltpu.VMEM(wo.shape[1:], BF16), pltpu.VMEM((2, wo.shape[1] // 4, wo.shape[2]), F32)]
    return pl.pallas_call(
        functools.partial(_ffn_kernel, n_y=len(ys), layer=layer, out_layer=out_layer, final_norm=final_norm,
                          tf=tf),
        grid=(m // tm,),
        in_specs=[pl.BlockSpec((tm, d), lambda i: (i, 0))]
        + [pl.BlockSpec((tm, y.shape[1]), lambda i: (i, 0)) for y in ys]
        + ([hbm] if has_out else [])
        + [_resident((1, d)), hbm, hbm, hbm, _resident((1, d))],
        out_specs=pl.BlockSpec((tm, d), lambda i: (i, 0)),
        out_shape=jax.ShapeDtypeStruct((m, d), F32),
        scratch_shapes=scratch,
        compiler_params=_params("arbitrary"),
        name="ffn",
    )(x, *ys, *([wo] if has_out else []), g.reshape(1, d), wg, wu, wd, final_g.reshape(1, d))


def _bdot(a, b, dims):
    return lax.dot_general(a.astype(BF16), b.astype(BF16), dims, preferred_element_type=F32)


def _block_diag(x, half):
    lo = _iota2((1, 1, 2 * half), 2) < half
    zero = jnp.zeros((), x.dtype)
    return jnp.concatenate([jnp.where(lo, x, zero), jnp.where(lo, zero, x)], axis=1)


def _unit_lower_inverse(a_strict, c):
    row = _iota2((1, c, 2 * c), 1)
    col = _iota2((1, c, 2 * c), 2) & (c - 1)
    eye = (row == col).astype(F32)
    t = None
    m = 1
    while m < c:
        mask = ((row // (2 * m)) == (col // (2 * m))) & ((row & m) != 0) & ((col & m) == 0)
        lm = jnp.where(mask, a_strict, 0.0)
        if t is None:
            t = eye - lm
        else:
            t = t - _bdot(_bdot(t, _block_diag(lm.astype(BF16), c), BNN), _block_diag(t.astype(BF16), c), BNN)
            yield
        m *= 2
    return t


def _head_sum(t, hd):
    tile_ones = _block_ones(2 * hd, hd)
    return jnp.concatenate([_bdot(t[:, i:i + 2 * hd], tile_ones, NN) for i in range(0, t.shape[1], 2 * hd)], axis=1)


def _rwkv_front(xs, w0_ref, w2_ref, a0_ref, a2_ref, g2_ref, kk_ref, ka_ref, *, c, n_chunks, heads, hd):
    rows = n_chunks * c
    dim = heads * hd
    r = xs[:, 0:dim]
    k = xs[:, dim:2 * dim]
    v = xs[:, 2 * dim:3 * dim]
    o1 = 3 * dim
    g_lr = xs[:, o1 + LORA_W + LORA_A:o1 + LORA_W + LORA_A + LORA_G]

    wa = xs[:, o1:o1 + LORA_W + LORA_A]
    wa = jnp.where(_iota2(wa.shape, 1) < LORA_W, jnp.tanh(wa), wa)
    w2a2 = jnp.concatenate(
        [jnp.concatenate([w2_ref[...], jnp.zeros((LORA_W, dim), F32)], axis=1),
         jnp.concatenate([jnp.zeros((LORA_A, dim), F32), a2_ref[...]], axis=1)], axis=0)
    za = _mm_fused(wa, w2a2)
    z = w0_ref[...] + za[:, :dim]
    softplus = jnp.maximum(-z, 0.0) + jnp.log(1.0 + jnp.exp(-jnp.abs(z)))
    w_raw = -softplus - 0.5
    lw = -jnp.exp(w_raw)
    yield
    a = jax.nn.sigmoid(a0_ref[...] + za[:, dim:])
    g = _mm_fused(jax.nn.sigmoid(g_lr), g2_ref[...])
    yield
    kk = k * kk_ref[...]
    kk = kk * lax.rsqrt(jnp.maximum(_head_sum(kk * kk, hd), 1e-24))
    k2 = k * (1.0 + (a - 1.0) * ka_ref[...])
    bb = kk * a
    yield

    r_i, c_i = _iota2((rows, rows), 0), _iota2((rows, rows), 1)
    tri_incl = (r_i >= c_i) & (r_i // c == c_i // c)
    cum = _cumsum_matmul(tri_incl, lw)
    ends = [cum[(i + 1) * c - 1:(i + 1) * c, :] for i in range(n_chunks)]
    cum_last = jnp.concatenate([jnp.broadcast_to(e, (c, dim)) for e in ends], axis=0)
    yield
    w_incl = jnp.exp(cum)
    w_excl = jnp.exp(cum - lw)
    yield
    w_inv = jnp.exp(-cum)
    w_tail = jnp.exp(cum_last - cum)
    w_all = jnp.exp(jnp.concatenate(ends, axis=0))
    yield
    ops = dict(kt=kk * w_excl, rt=r * w_incl, bt=bb * w_inv)
    yield
    ops.update(kd=k2 * w_inv, bw=bb * w_tail, kw=k2 * w_tail)
    yield
    ops.update(v=v, w_all=w_all, r=r, k2=k2, g=g)
    return ops


def _rwkv_back(ops, s, rk_ref, lnw_ref, lnb_ref, *, c, n_chunks, heads, hd):
    pw = 2 * hd
    pairs = heads // 2
    def by_pair(t, n_rows=c):
        return jnp.stack([t[i * n_rows:(i + 1) * n_rows, j * pw:(j + 1) * pw]
                          for i in range(n_chunks) for j in range(pairs)], axis=0)
    bd_f = lambda t: _block_diag(t.astype(BF16), hd)
    kt_p, rt_p, bt_p, kd_p, kw_p, bw_p, v_p = (by_pair(ops[n]) for n in ("kt", "rt", "bt", "kd", "kw", "bw", "v"))
    w_all_p = by_pair(ops["w_all"], 1)
    row_t = _iota2((1, c, 2 * c), 1)
    col_t = _iota2((1, c, 2 * c), 2) & (c - 1)
    strict_p, incl_p = row_t > col_t, row_t >= col_t
    same_head = (_iota2((1, pw, pw), 1) // hd) == (_iota2((1, pw, pw), 2) // hd)

    kr = jnp.concatenate([kt_p, rt_p], axis=1)
    g_all = _bdot(kr, jnp.concatenate([bd_f(bt_p), bd_f(kd_p)], axis=1), BNT)
    yield
    gb, gk = g_all[:, :, :2 * c], g_all[:, :, 2 * c:]
    a_b = jnp.where(strict_p, gb[:, :c], 0.0)
    a_k = jnp.where(strict_p, gk[:, :c], 0.0)
    p_b = jnp.where(incl_p, gb[:, c:], 0.0)
    p_k = jnp.where(incl_p, gk[:, c:], 0.0)
    akpk = _bdot(jnp.concatenate([a_k, p_k], axis=1), bd_f(v_p), BNN)
    yield
    t_inv = yield from _unit_lower_inverse(a_b, c)
    kv1 = _bdot(t_inv, jnp.concatenate([bd_f(kt_p), bd_f(akpk[:, :c])], axis=2), BNN)
    yield
    kt1, v1 = kv1[:, :, :pw], kv1[:, :, pw:]
    pbk = _bdot(p_b, jnp.concatenate([bd_f(kt1), bd_f(v1)], axis=2), BNN)
    yield
    q_mat = rt_p - pbk[:, :, :pw]
    z_mat = akpk[:, c:] - pbk[:, :, pw:]
    x_mat = jnp.where(same_head, _bdot(bw_p, kt1, BTN), 0.0)
    yield
    n_mat = jnp.where(same_head, _bdot(jnp.concatenate([v_p, v1], axis=1),
                                       jnp.concatenate([kw_p, -bw_p], axis=1), BTN), 0.0)
    yield
    y_rows = []
    for i in range(n_chunks):
        sl = slice(i * pairs, (i + 1) * pairs)
        y_p = _bdot(q_mat[sl], s, BNT) + z_mat[sl]
        s = s * w_all_p[sl] - _bdot(s, x_mat[sl], BNT) + n_mat[sl]
        y_rows.append(jnp.concatenate([y_p[j] for j in range(pairs)], axis=1))
        yield
    y = jnp.concatenate(y_rows, axis=0)

    inv_hd = 1.0 / hd
    mean = _head_sum(y, hd) * inv_hd
    yc = y - mean
    var = _head_sum(yc * yc, hd) * inv_hd
    yield
    yn = yc * lax.rsqrt(var + LNX_EPS) * lnw_ref[...] + lnb_ref[...]
    bonus = _head_sum(ops["r"] * ops["k2"] * rk_ref[...], hd) * ops["v"]
    return (yn + bonus) * ops["g"], s


def _proj_rwkv_kernel(x_ref, g_ref, w_hbm, mu_ref, w0_ref, w2_ref, a0_ref, a2_ref, g2_ref, kk_ref, ka_ref, rk_ref,
                      lnw_ref, lnb_ref, y_ref, q_ref, k_ref, v_ref, w_ref, stage, sem, carry_ref, st_ref, *,
                      w_layer, tiles_per_seq, tn, chunk, n_chunks, heads, hd):
    c = chunk
    rows = n_chunks * c
    tm = x_ref.shape[0]
    n_groups = tm // rows
    cols = mu_ref.shape[1]
    step = pl.program_id(0)

    @pl.when(step == 0)
    def _():
        _load_weight(w_hbm, w_layer, w_ref, stage, sem, stage.shape[1])
        carry_ref[...] = jnp.zeros_like(carry_ref)
        st_ref[...] = jnp.zeros_like(st_ref)

    h = _rmsnorm(x_ref[...], g_ref[...]).astype(BF16)
    p = jnp.concatenate([_dot(h, w_ref[:, c0:min(c0 + tn, cols)]) for c0 in range(0, cols, tn)], axis=1)
    first = step % tiles_per_seq == 0
    prev_last = jnp.where(first, 0.0, carry_ref[...])
    prev = jnp.where(_iota2((tm, 1), 0) == 0, prev_last, pltpu.roll(p, 1, axis=0))
    carry_ref[...] = p[tm - 1:tm, :]
    xs = p + (prev - p) * mu_ref[...]

    def attention_groups():
        col0 = cols
        for o_ref in (q_ref, k_ref, v_ref):
            n = o_ref.shape[1]
            for c0 in range(0, n, tn):
                o_ref[:, c0:min(c0 + tn, n)] = _dot(h, w_ref[:, col0 + c0:col0 + min(c0 + tn, n)])
                yield
            col0 += n

    front_refs = (w0_ref, w2_ref, a0_ref, a2_ref, g2_ref, kk_ref, ka_ref)
    dims = dict(c=c, n_chunks=n_chunks, heads=heads, hd=hd)
    front = lambda gi: _rwkv_front(xs[gi * rows:(gi + 1) * rows], *front_refs, **dims)
    ready, _ = _interleave(front(0), attention_groups())
    s = jnp.where(first, 0.0, st_ref[...])
    for gi in range(n_groups):
        back = _rwkv_back(ready, s, rk_ref, lnw_ref, lnb_ref, **dims)
        if gi + 1 < n_groups:
            (out, s), ready = _interleave(back, front(gi + 1))
        else:
            ((out, s),) = _interleave(back)
        y_ref[gi * rows:(gi + 1) * rows, :] = out.astype(y_ref.dtype)
    st_ref[...] = s


def _proj_rwkv(x, g, w, w_layer, seq, mu, w0, w2, a0, a2, g2, k_k, k_a, r_k, lnx_w, lnx_b, attn_widths, *, tm=512,
               tn=512, n_slabs=8, chunk=RWKV_CHUNK, n_chunks=RWKV_CHUNKS_PER_GROUP):
    m, d = x.shape
    heads, hd = RWKV_HEADS, RWKV_HD
    dim = heads * hd
    cols = mu.shape[-1]
    n_total = w.shape[2]
    assert n_total == cols + sum(attn_widths) and len(attn_widths) == 3 and d % n_slabs == 0
    assert chunk & (chunk - 1) == 0
    assert seq % tm == 0 and tm % (chunk * n_chunks) == 0
    row = lambda t: t.reshape(1, -1)
    vecs = [row(mu), row(w0), w2, row(a0), a2, g2, row(k_k), row(k_a), row(r_k), row(lnx_w), row(lnx_b)]
    widths = [dim] + list(attn_widths)
    return pl.pallas_call(
        functools.partial(_proj_rwkv_kernel, w_layer=w_layer, tiles_per_seq=seq // tm, tn=tn, chunk=chunk,
                          n_chunks=n_chunks, heads=heads, hd=hd),
        grid=(m // tm,),
        in_specs=[pl.BlockSpec((tm, d), lambda i: (i, 0)), _resident((1, d)), pl.BlockSpec(memory_space=pl.ANY)]
        + [_resident(t.shape) for t in vecs],
        out_specs=[pl.BlockSpec((tm, n), lambda i: (i, 0)) for n in widths],
        out_shape=[jax.ShapeDtypeStruct((m, n), BF16 if i == 0 else F32) for i, n in enumerate(widths)],
        scratch_shapes=[pltpu.VMEM((d, n_total), BF16), pltpu.VMEM((2, d // n_slabs, n_total), F32),
                        pltpu.SemaphoreType.DMA((2,)), pltpu.VMEM((1, cols), F32),
                        pltpu.VMEM((heads // 2, 2 * hd, 2 * hd), F32)],
        compiler_params=_params("arbitrary"),
        name="proj_rwkv",
    )(x, g.reshape(1, d), w, *vecs)


def _moba_kernel(q_ref, k_ref, v_ref, o_ref, km_ref, ka_ref, vat_ref, *, nb, blk, n_sel, heads, hd):
    j = pl.program_id(1)
    nbp = km_ref.shape[0]
    scale = hd ** -0.5
    masked = -1e30
    neg_inf = float("-inf")
    slopes = [2.0 ** (-8.0 * (h + 1) / heads) for h in range(heads)]

    @pl.when(j == 0)
    def _():
        km_ref[...] = jnp.zeros_like(km_ref)
        for n in range(nb):
            km_ref[n:n + 1, :] = jnp.mean(k_ref[0, n * blk:(n + 1) * blk, :], axis=0, keepdims=True)
        col = _iota2((blk, 2 * hd), 0).astype(F32)
        lane = _iota2((blk, 2 * hd), 1)
        feat_lane = lane & (hd - 1)
        for n in range(nb):
            rows = slice(n * blk, (n + 1) * blk)
            one_hot = jnp.where(feat_lane == n, 1.0, 0.0)
            for h in range(heads):
                k_feat = (one_hot + jnp.where(feat_lane == nbp, slopes[h] * col, 0.0)
                          + jnp.where(feat_lane == nbp + 1, slopes[h] * blk * n, 0.0))
                k_tile = k_ref[0, rows, (h // 2) * 2 * hd:(h // 2 + 1) * 2 * hd]
                own_lanes = (lane < hd) if h % 2 == 0 else (lane >= hd)
                ka_ref[h, rows, :] = jnp.where(own_lanes, k_tile, k_feat).astype(BF16)
        ones_row = jnp.where(_iota2((vat_ref.shape[2] - hd, blk), 0) == 0, 1.0, 0.0)
        for n in range(nb):
            vt = v_ref[0, n * blk:(n + 1) * blk, :].T
            for h in range(heads):
                vat_ref[n, h] = jnp.concatenate([vt[h * hd:(h + 1) * hd], ones_row], axis=0).astype(BF16)

    qt = (q_ref[0] * scale).T
    sub = _iota2((nbp, blk), 0)
    const_rows = jnp.where(_iota2((hd - nbp, blk), 0) < 2, 1.0, 0.0)
    q_aug = []
    for h in range(heads):
        sl = slice(h * hd, (h + 1) * hd)
        qh = qt[sl]
        gate = _mm(km_ref[:, sl], qh)
        gate = jnp.where(sub < j, gate, neg_inf)
        cnt = jnp.zeros((nbp, blk), jnp.int32)
        for m in range(nb):
            other = gate[m:m + 1, :]
            beats = (other > gate) | ((other == gate) & (m < sub))
            cnt = cnt + beats.astype(jnp.int32)
        keep = ((sub < j) & (cnt < n_sel)) | (sub == j) | (sub >= nb)
        bias = jnp.where(keep, 0.0, masked)
        parts = [qh, bias, const_rows] if h % 2 == 0 else [bias, const_rows, qh]
        q_aug.append(jnp.concatenate(parts, axis=0).astype(BF16))
    q_aug = jnp.stack(q_aug, axis=0)

    half = blk // 2
    units = [(h, qh * half) for h in range(heads) for qh in range(2)]
    q_units = [q_aug[h][:, q0:q0 + half] for h, q0 in units]

    def key_block(n, carry, own):
        start = pl.multiple_of(n * blk, blk)
        scores_of, soft_of, out = {}, {}, []
        for i in range(len(units) + MOBA_VALUE_LAG):
            if i < len(units):
                scores_of[i] = _dot(ka_ref[units[i][0], pl.ds(start, blk), :], q_units[i])
            if 0 <= i - MOBA_SOFTMAX_LAG < len(units):
                u = i - MOBA_SOFTMAX_LAG
                sc, m_old = scores_of.pop(u), carry[u][0]
                if own:
                    sc = jnp.where(_iota2((blk, half), 0) <= _iota2((blk, half), 1) + units[u][1], sc, masked)
                m_new = jnp.maximum(m_old, jnp.max(sc, axis=0, keepdims=True))
                soft_of[u] = (m_new, jnp.exp(m_old - m_new), jnp.exp(sc - m_new).astype(BF16))
            if 0 <= i - MOBA_VALUE_LAG < len(units):
                u = i - MOBA_VALUE_LAG
                m_new, alpha, pr = soft_of.pop(u)
                out.append((m_new, alpha * carry[u][1] + _dot(vat_ref[n, units[u][0]], pr)))
        return tuple(out)

    empty = tuple((jnp.full((1, half), masked, F32), jnp.zeros((vat_ref.shape[2], half), F32)) for _ in units)
    init = key_block(j, empty, own=True)
    final = lax.fori_loop(0, j, lambda n, carry: key_block(n, carry, own=False), init)
    out_t = jnp.concatenate(
        [jnp.concatenate([final[2 * h + qh][1][:hd] / final[2 * h + qh][1][hd:hd + 1] for qh in range(2)], axis=1)
         for h in range(heads)], axis=0)
    o_ref[0] = out_t.T.astype(o_ref.dtype)


def _moba(q, k, v):
    b, s, dim = q.shape
    heads, hd, blk = MOBA_HEADS, MOBA_HD, MOBA_BLOCK
    assert s % blk == 0
    nb = s // blk
    nbp = -(-nb // SUBLANES) * SUBLANES
    assert nbp + 2 <= hd
    n_sel = min(MOBA_TOPK, nb - 1)
    return pl.pallas_call(
        functools.partial(_moba_kernel, nb=nb, blk=blk, n_sel=n_sel, heads=heads, hd=hd),
        grid=(b, nb),
        in_specs=[
            pl.BlockSpec((1, blk, dim), lambda i, j: (i, j, 0)),
            pl.BlockSpec((1, s, dim), lambda i, j: (i, 0, 0)),
            pl.BlockSpec((1, s, dim), lambda i, j: (i, 0, 0)),
        ],
        out_specs=pl.BlockSpec((1, blk, dim), lambda i, j: (i, j, 0)),
        out_shape=jax.ShapeDtypeStruct((b, s, dim), BF16),
        scratch_shapes=[pltpu.VMEM((nbp, dim), F32),
                        pltpu.VMEM((heads, s, 2 * hd), BF16),
                        pltpu.VMEM((nb, heads, hd + BF16_SUBLANES, blk), BF16)],
        compiler_params=_params("parallel", "arbitrary"),
        name="moba",
    )(q, k, v)


def _hgrn_front(q, fr, v, lb, *, c, n_chunks, heads, dk):
    rows = n_chunks * c
    dim = heads * dk
    sig = jax.nn.sigmoid(fr)
    lf = jnp.log(lb + (1.0 - lb) * sig)
    kf = (1.0 - lb) * (1.0 - sig)
    yield
    r_i, c_i = _iota2((rows, rows), 0), _iota2((rows, rows), 1)
    tri_all = (r_i >= c_i) & (r_i // c == c_i // c)
    b = _cumsum_matmul(tri_all, lf)
    per_chunk = lambda r: jnp.concatenate(
        [jnp.broadcast_to(b[i * c + r:i * c + r + 1, :], (c, dim)) for i in range(n_chunks)], axis=0)
    ends = [b[(i + 1) * c - 1:(i + 1) * c, :] for i in range(n_chunks)]
    b_last = per_chunk(c - 1)
    yield
    b_mid = per_chunk(c // 2 - 1)
    rel = b - b_mid
    qd = q * jnp.exp(rel)
    yield
    kd = kf * jnp.exp(-rel)
    yield
    q_in = qd * jnp.exp(b_mid)
    yield
    kw = kf * jnp.exp(b_last - b)
    w_all = jnp.exp(jnp.concatenate(ends, axis=0))
    return dict(qd=qd, kd=kd, q_in=q_in, kw=kw, w_all=w_all, v=v)


def _hgrn_back(ops, s, g, norm_w, *, c, n_chunks, heads, dk):
    def by_head(t, n_rows=c):
        return jnp.stack([t[i * n_rows:(i + 1) * n_rows, h * dk:(h + 1) * dk]
                          for i in range(n_chunks) for h in range(heads)], axis=0)
    hp, pw = heads // 2, 2 * dk
    by_pair = lambda t: jnp.stack([t[i * c:(i + 1) * c, j * pw:(j + 1) * pw]
                                   for i in range(n_chunks) for j in range(hp)], axis=0)
    tri_pair = _iota2((1, c, 2 * c), 1) >= (_iota2((1, c, 2 * c), 2) & (c - 1))
    q_hi, q_lo = _pieces(by_pair(ops["qd"]), 2)
    k_hi, k_lo = _pieces(_block_diag(by_pair(ops["kd"]), dk), 2)
    sc = lax.dot_general(jnp.concatenate([q_hi, q_hi, q_lo], axis=2), jnp.concatenate([k_hi, k_lo, k_hi], axis=2),
                         BNT, preferred_element_type=F32)
    sc = jnp.where(tri_pair, sc, 0.0)
    yield
    o_pair = _bdot(sc, _block_diag(by_pair(ops["v"]).astype(BF16), dk), BNN)
    yield
    qd_h, v_h = by_head(ops["q_in"]), by_head(ops["v"])
    s_add = _bdot(v_h, by_head(ops["kw"]), BTN)
    yield
    w_all_h = by_head(ops["w_all"], 1)
    o_rows = []
    for i in range(n_chunks):
        sl = slice(i * heads, (i + 1) * heads)
        o_intra = jnp.stack([o_pair[i * hp + h // 2][:, (h % 2) * dk:(h % 2 + 1) * dk]
                             for h in range(heads)], axis=0)
        o_h = o_intra + _bdot(qd_h[sl], s, BNT)
        s = s * w_all_h[sl] + s_add[sl]
        o_h = o_h * lax.rsqrt(jnp.mean(o_h * o_h, axis=-1, keepdims=True) + EPS)
        o_rows.append(jnp.concatenate([o_h[h] for h in range(heads)], axis=1))
        yield
    o = jnp.concatenate(o_rows, axis=0)
    return o * norm_w * jax.nn.sigmoid(g), s


def _proj_hgrn_kernel(x_ref, g_ref, w_hbm, lbl_ref, nw_ref, o_ref, w_ref, stage, sem, st_ref, *, w_layer, layer,
                      tiles_per_seq, tn, chunk, n_chunks, heads, dk):
    c = chunk
    dim = heads * dk
    rows = n_chunks * c
    n_groups = x_ref.shape[0] // rows
    step = pl.program_id(0)

    @pl.when(step == 0)
    def _():
        _load_weight(w_hbm, w_layer, w_ref, stage, sem, stage.shape[1])
        st_ref[...] = jnp.zeros_like(st_ref)

    logits = lbl_ref[...]
    e = jnp.exp(logits - jnp.max(logits, axis=0, keepdims=True))
    sm = e / jnp.sum(e, axis=0, keepdims=True)
    lb = jnp.sum(sm[0:layer + 1, :], axis=0, keepdims=True) - sm[0:1, :]

    h = _rmsnorm(x_ref[...], g_ref[...]).astype(BF16)
    quarter = lambda k: jnp.concatenate([_dot(h, w_ref[:, c0:c0 + tn]) for c0 in range(k * dim, (k + 1) * dim, tn)],
                                        axis=1)
    f_raw, q, v = quarter(1), quarter(0), quarter(2)

    def gate_quarter():
        parts = []
        for c0 in range(3 * dim, 4 * dim, tn):
            parts.append(_dot(h, w_ref[:, c0:c0 + tn]))
            yield
        return jnp.concatenate(parts, axis=1)

    dims = dict(c=c, n_chunks=n_chunks, heads=heads, dk=dk)
    grp = lambda t, gi: t[gi * rows:(gi + 1) * rows, :]
    front = lambda gi: _hgrn_front(grp(q, gi), grp(f_raw, gi), grp(v, gi), lb, **dims)
    ready, g_raw = _interleave(front(0), gate_quarter())
    s = jnp.where(step % tiles_per_seq == 0, 0.0, st_ref[...])
    for gi in range(n_groups):
        back = _hgrn_back(ready, s, grp(g_raw, gi), nw_ref[...], **dims)
        if gi + 1 < n_groups:
            (out, s), ready = _interleave(back, front(gi + 1))
        else:
            ((out, s),) = _interleave(back)
        o_ref[gi * rows:(gi + 1) * rows, :] = out.astype(o_ref.dtype)
    st_ref[...] = s


def _proj_hgrn(x, g, w, w_layer, lb_logits, layer, norm_w, seq, *, tm=512, tn=512, n_slabs=8, chunk=HG_CHUNK,
               n_chunks=HG_CHUNKS_PER_GROUP):
    m, d = x.shape
    heads, dk = HG_HEADS, HG_DK
    dim = heads * dk
    n_total = w.shape[2]
    assert n_total == 4 * dim and dim % tn == 0 and d % n_slabs == 0
    assert chunk & (chunk - 1) == 0
    assert seq % tm == 0 and tm % (chunk * n_chunks) == 0
    return pl.pallas_call(
        functools.partial(_proj_hgrn_kernel, w_layer=w_layer, layer=layer, tiles_per_seq=seq // tm, tn=tn,
                          chunk=chunk, n_chunks=n_chunks, heads=heads, dk=dk),
        grid=(m // tm,),
        in_specs=[pl.BlockSpec((tm, d), lambda i: (i, 0)), _resident((1, d)), pl.BlockSpec(memory_space=pl.ANY),
                  _resident(lb_logits.shape), _resident((1, dim))],
        out_specs=pl.BlockSpec((tm, dim), lambda i: (i, 0)),
        out_shape=jax.ShapeDtypeStruct((m, dim), BF16),
        scratch_shapes=[pltpu.VMEM((d, n_total), BF16), pltpu.VMEM((2, d // n_slabs, n_total), F32),
                        pltpu.SemaphoreType.DMA((2,)), pltpu.VMEM((heads, dk, dk), F32)],
        compiler_params=_params("arbitrary"),
        name="proj_hgrn",
    )(x, g.reshape(1, d), w, lb_logits, norm_w.reshape(1, dim))


def kernel(x, norm_g, ffn1_wg, ffn1_wu, ffn1_wd, ffn2_wg, ffn2_wu, ffn2_wd, ev_w_in, ev_w_out, rw_mu, rw_w0, rw_w2, rw_a0, rw_a2, rw_g2, rw_k_k, rw_k_a, rw_r_k, rw_lnx_w, rw_lnx_b, od_w_in, od_w_out, hg_norm_w, hg_lb_logits, final_g):
    bsz, seq, d = x.shape
    depth = norm_g.shape[0]
    rwkv_dim = RWKV_HEADS * RWKV_HD
    rwkv_cols = 3 * rwkv_dim + LORA_W + LORA_A + LORA_G
    moba_dim = MOBA_HEADS * MOBA_HD
    xf = x.reshape(bsz * seq, d)
    for l in range(depth):
        xf = _ffn(xf, [], None, 0, norm_g[l, 0], ffn1_wg, ffn1_wu, ffn1_wd, l, final_g, final_norm=False)
        if l % 2 == 0:
            e = l // 2
            assert ev_w_in.shape[2] == rwkv_cols + 3 * moba_dim
            y_a, q, k, v = _proj_rwkv(xf, norm_g[l, 1], ev_w_in, e, seq, rw_mu[e], rw_w0[e], rw_w2[e], rw_a0[e],
                                      rw_a2[e], rw_g2[e], rw_k_k[e], rw_k_a[e], rw_r_k[e], rw_lnx_w[e],
                                      rw_lnx_b[e], [moba_dim, moba_dim, moba_dim])
            y_b = _moba(q.reshape(bsz, seq, moba_dim), k.reshape(bsz, seq, moba_dim),
                        v.reshape(bsz, seq, moba_dim))
            ys, wo, wo_layer = [y_a, y_b.reshape(-1, moba_dim)], ev_w_out, e
        else:
            o = l // 2
            y = _proj_hgrn(xf, norm_g[l, 1], od_w_in, o, hg_lb_logits, l, hg_norm_w[o], seq)
            ys, wo, wo_layer = [y], od_w_out, o
        xf = _ffn(xf, ys, wo, wo_layer, norm_g[l, 2], ffn2_wg, ffn2_wu, ffn2_wd, l, final_g,
                  final_norm=(l == depth - 1))
    return xf.reshape(bsz, seq, d)
```
